```python
import jax
import jax.numpy as jnp
from jax import lax
import numpy as np

D_MODEL = 1024
BATCH = 8
SEQ = 2048
DEPTH = 1
DEC_BATCH = 128
DEC_SEQ = 8
PAST_LEN = 16384
PAGE_SIZE = 128

HEAD_SIZE = 64
RWKV_WIDTH = D_MODEL // 2
RWKV_HEADS = RWKV_WIDTH // HEAD_SIZE
POOL_WIDTH = D_MODEL // 4
POOL_WINDOWS = (2, 4, 8, 16)
POOL_GROUP = POOL_WIDTH // len(POOL_WINDOWS)
POOL_BUF = max(POOL_WINDOWS) - 1
MEM_WIDTH = D_MODEL - RWKV_WIDTH - POOL_WIDTH
MEM_HEADS = 4
MEM_HEAD_DIM = MEM_WIDTH // MEM_HEADS
MEM_TOKENS = 256
DECAY_LORA = max(32, int(round(1.8 * D_MODEL ** 0.5 / 32)) * 32)
AAA_LORA = max(32, int(round(1.8 * D_MODEL ** 0.5 / 32)) * 32)
GATE_LORA = max(32, int(round(0.6 * D_MODEL ** 0.8 / 32)) * 32)
SHIFT_WIDTH = 3 * RWKV_WIDTH + DECAY_LORA + AAA_LORA + GATE_LORA
RWKV_SPLITS = (RWKV_WIDTH, 2 * RWKV_WIDTH, 3 * RWKV_WIDTH,
               3 * RWKV_WIDTH + DECAY_LORA, 3 * RWKV_WIDTH + DECAY_LORA + AAA_LORA)
IN_WIDTH = SHIFT_WIDTH + POOL_WIDTH + MEM_WIDTH
N_EXPERTS = 256
TOP_K = 8
N_GROUPS = 8
TOPK_GROUPS = 4
EXPERT_FF = 256
ROUTED_SCALE = 2.5
EXPERT_BLOCK = 128
ALPHA = (2.0 * DEPTH) ** 0.25
BETA = (8.0 * DEPTH) ** -0.25
LN_EPS = 1e-5
GN_EPS = 64e-5

kernel_name = 'hymba_rwkv7_pool_memattn_moe_step'


def _layer_norm(x, g, b):
    x32 = x.astype(jnp.float32)
    mu = x32.mean(-1, keepdims=True)
    var = jnp.square(x32 - mu).mean(-1, keepdims=True)
    out = (x32 - mu) * lax.rsqrt(var + LN_EPS) * g.astype(jnp.float32) + b.astype(jnp.float32)
    return out.astype(x.dtype)


def _wkv7_scan(r, w, k, v, kk, a, s0):
    def step(s, inp):
        r_t, w_t, k_t, v_t, kk_t, a_t = inp
        sa = jnp.einsum('bhvk,bhk->bhv', s, -kk_t)
        s = (s * w_t[:, :, None, :] + sa[..., None] * (kk_t * a_t)[:, :, None, :]
             + v_t[..., None] * k_t[:, :, None, :])
        return s, jnp.einsum('bhvk,bhk->bhv', s, r_t)
    xs = tuple(jnp.swapaxes(t, 0, 1) for t in (r, w, k, v, kk, a))
    s, ys = lax.scan(step, s0.astype(jnp.float32), xs)
    return jnp.swapaxes(ys, 0, 1), s


def _rwkv7_group(zs, z_prev, s0, p):
    b, t, _ = zs.shape
    shifted = jnp.concatenate([z_prev[:, None].astype(zs.dtype), zs[:, :-1]], axis=1)
    zm = (zs + p['mu_shift'] * (shifted - zs)).astype(jnp.float32)
    r, k, v, xw, xa, xg = jnp.split(zm, RWKV_SPLITS, axis=-1)
    f32 = lambda u: u.astype(jnp.float32)
    w_log = -jax.nn.softplus(-(f32(p['w0']) + jnp.tanh(xw) @ f32(p['w_up_decay']))) - 0.5
    decay = jnp.exp(-jnp.exp(w_log))
    a = jax.nn.sigmoid(f32(p['a0']) + xa @ f32(p['w_up_aaa']))
    g = jax.nn.sigmoid(xg) @ f32(p['w_up_gate'])
    hs = lambda u: u.reshape(b, t, RWKV_HEADS, HEAD_SIZE)
    kk = hs(k * f32(p['k_k']))
    kk = kk / jnp.maximum(jnp.sqrt(jnp.sum(kk * kk, -1, keepdims=True)), 1e-12)
    k = k * (1.0 + (a - 1.0) * f32(p['k_a']))
    y, s_new = _wkv7_scan(hs(r), hs(decay), hs(k), hs(v), kk, hs(a), s0)
    mu = y.mean(-1, keepdims=True)
    var = jnp.square(y - mu).mean(-1, keepdims=True)
    yn = ((y - mu) * lax.rsqrt(var + GN_EPS)).reshape(b, t, RWKV_WIDTH)
    yn = yn * f32(p['ln_x_w']) + f32(p['ln_x_b'])
    bonus = (jnp.sum(hs(r) * hs(k) * f32(p['r_k']), -1, keepdims=True) * hs(v)).reshape(b, t, RWKV_WIDTH)
    return (yn + bonus) * g, s_new


def _pool_group(zp, pool_prev, pos0, pool_w, pool_scale):
    t = zp.shape[1]
    full = jnp.concatenate([pool_prev.astype(zp.dtype), zp], axis=1)
    full32 = full.astype(jnp.float32)
    cs = jnp.concatenate([jnp.zeros_like(full32[:, :1]), jnp.cumsum(full32, axis=1)], axis=1)
    pos = pos0 + jnp.arange(t) + 1
    outs = []
    for gi, w in enumerate(POOL_WINDOWS):
        lo, hi = gi * POOL_GROUP, (gi + 1) * POOL_GROUP
        wsum = cs[:, POOL_BUF + 1:POOL_BUF + 1 + t, lo:hi] - cs[:, POOL_BUF + 1 - w:POOL_BUF + 1 - w + t, lo:hi]
        cnt = jnp.minimum(pos, w).astype(jnp.float32)[None, :, None]
        diff = wsum / cnt - full32[:, POOL_BUF:, lo:hi]
        outs.append(jnp.einsum('btc,cd->btd', diff, pool_w[gi].astype(jnp.float32)))
    out = jnp.concatenate(outs, axis=-1) * pool_scale.astype(jnp.float32)
    return out, full[:, -POOL_BUF:]


def _memory_group(zq, mem_k, mem_v):
    b, t, _ = zq.shape
    q = zq.reshape(b, t, MEM_HEADS, MEM_HEAD_DIM)
    s = jnp.einsum('bthd,bmhd->bhtm', q, mem_k).astype(jnp.float32) * (MEM_HEAD_DIM ** -0.5)
    prob = jax.nn.softmax(s, axis=-1)
    o = jnp.einsum('bhtm,bmhd->bthd', prob.astype(mem_v.dtype), mem_v)
    return o.reshape(b, t, MEM_WIDTH)


def _moe(h, router_w, router_b, exp_gate, exp_up, exp_down, sh_gate, sh_up, sh_down):
    n = h.shape[0]
    scores = jax.nn.sigmoid(h.astype(jnp.float32) @ router_w.astype(jnp.float32))
    sel = scores + router_b.astype(jnp.float32)
    grp = sel.reshape(n, N_GROUPS, N_EXPERTS // N_GROUPS)
    grp_score = lax.top_k(grp, 2)[0].sum(-1)
    _, grp_idx = lax.top_k(grp_score, TOPK_GROUPS)
    grp_mask = jax.nn.one_hot(grp_idx, N_GROUPS, dtype=jnp.float32).sum(1)
    exp_mask = jnp.repeat(grp_mask, N_EXPERTS // N_GROUPS, axis=1)
    _, idx = lax.top_k(jnp.where(exp_mask > 0, sel, -jnp.inf), TOP_K)
    gate = jnp.take_along_axis(scores, idx, axis=1)
    gate = gate / gate.sum(-1, keepdims=True) * ROUTED_SCALE
    a_tot = n * TOP_K
    flat_e = idx.reshape(-1).astype(jnp.int32)
    flat_tok = jnp.repeat(jnp.arange(n, dtype=jnp.int32), TOP_K)
    order = jnp.argsort(flat_e)
    se, stok, sw = flat_e[order], flat_tok[order], gate.reshape(-1)[order]
    counts = jnp.bincount(flat_e, length=N_EXPERTS).astype(jnp.int32)
    padded = (counts + EXPERT_BLOCK - 1) // EXPERT_BLOCK * EXPERT_BLOCK
    pad_end = jnp.cumsum(padded)
    pad_start = pad_end - padded
    start = jnp.cumsum(counts) - counts
    dest = pad_start[se] + jnp.arange(a_tot, dtype=jnp.int32) - start[se]
    n_blocks = (a_tot + N_EXPERTS * (EXPERT_BLOCK - 1) + EXPERT_BLOCK - 1) // EXPERT_BLOCK
    total = n_blocks * EXPERT_BLOCK
    tok_buf = jnp.full((total,), n, jnp.int32).at[dest].set(stok)
    w_buf = jnp.zeros((total,), jnp.float32).at[dest].set(sw)
    blk_exp = jnp.minimum(jnp.searchsorted(pad_end, jnp.arange(n_blocks, dtype=jnp.int32) * EXPERT_BLOCK,
                                           side='right'), N_EXPERTS - 1)
    h_pad = jnp.concatenate([h, jnp.zeros((1, h.shape[1]), h.dtype)], axis=0)

    def body(acc, blk):
        tok, wt, e = blk
        xb = h_pad[tok]
        u = jax.nn.silu(xb @ exp_gate[e]) * (xb @ exp_up[e])
        acc = acc.at[tok].add((u @ exp_down[e]).astype(jnp.float32) * wt[:, None])
        return acc, None

    acc, _ = lax.scan(body, jnp.zeros((n + 1, h.shape[1]), jnp.float32),
                      (tok_buf.reshape(n_blocks, EXPERT_BLOCK), w_buf.reshape(n_blocks, EXPERT_BLOCK), blk_exp))
    shared = (jax.nn.silu(h @ sh_gate) * (h @ sh_up)) @ sh_down
    return (acc[:n] + shared.astype(jnp.float32)).astype(h.dtype)


def _decoder_layer(x, shift_prev, pool_prev, wkv_prev, mem_k, mem_v, pos0, p):
    b, t, _ = x.shape
    z = jnp.einsum('btd,de->bte', x, p['w_in'])
    zs = z[..., :SHIFT_WIDTH]
    zp = z[..., SHIFT_WIDTH:SHIFT_WIDTH + POOL_WIDTH]
    zq = z[..., SHIFT_WIDTH + POOL_WIDTH:]
    z_prev = shift_prev @ p['w_in'][:, :SHIFT_WIDTH]
    o_rwkv, wkv_new = _rwkv7_group(zs, z_prev, wkv_prev, p)
    o_pool, pool_new = _pool_group(zp, pool_prev, pos0, p['pool_w'], p['pool_scale'])
    o_mem = _memory_group(zq, mem_k, mem_v)
    heads = jnp.concatenate([o_rwkv.astype(x.dtype), o_pool.astype(x.dtype), o_mem.astype(x.dtype)], axis=-1)
    mixed = jnp.einsum('bte,ed->btd', heads, p['w_out'])
    h = _layer_norm(ALPHA * x + mixed, p['ln1_g'], p['ln1_b'])
    f = _moe(h.reshape(b * t, D_MODEL), p['router_w'], p['router_b'], p['exp_gate'], p['exp_up'],
             p['exp_down'], p['sh_gate'], p['sh_up'], p['sh_down']).reshape(b, t, D_MODEL)
    y = _layer_norm(ALPHA * h + f, p['ln2_g'], p['ln2_b'])
    return y, wkv_new, x[:, -1], pool_new


def setup_inputs(seed: int = 0) -> dict:
    key = jax.random.key(seed)
    ks = iter(jax.random.split(key, 48))

    def nrm(shape, scale):
        return jax.random.normal(next(ks), shape, jnp.float32) * scale

    L, D = DEPTH, D_MODEL
    inp = {}
    inp['x_prompt'] = nrm((BATCH, SEQ, D), 1.0)
    inp['x_sample'] = nrm((DEC_BATCH, DEC_SEQ, D), 1.0)
    inp['mem_prompt'] = nrm((BATCH, MEM_TOKENS, D), 1.0)
    inp['state_wkv'] = nrm((L, DEC_BATCH, RWKV_HEADS, HEAD_SIZE, HEAD_SIZE), 0.5)
    inp['state_shift'] = nrm((L, DEC_BATCH, D), 1.0)
    inp['state_pool'] = nrm((L, DEC_BATCH, POOL_BUF, POOL_WIDTH), 1.0)
    inp['cache_mem_k'] = nrm((L, DEC_BATCH, MEM_TOKENS, MEM_HEADS, MEM_HEAD_DIM), 1.0)
    inp['cache_mem_v'] = nrm((L, DEC_BATCH, MEM_TOKENS, MEM_HEADS, MEM_HEAD_DIM), 1.0)
    inp['w_in'] = nrm((L, D, IN_WIDTH), D ** -0.5)
    inp['mu_shift'] = jax.random.uniform(next(ks), (L, SHIFT_WIDTH), jnp.float32)
    inp['w0'] = jax.random.uniform(next(ks), (L, RWKV_WIDTH), jnp.float32, minval=-6.0, maxval=-1.0)
    inp['w_up_decay'] = nrm((L, DECAY_LORA, RWKV_WIDTH), 0.1)
    inp['a0'] = nrm((L, RWKV_WIDTH), 0.1)
    inp['w_up_aaa'] = nrm((L, AAA_LORA, RWKV_WIDTH), AAA_LORA ** -0.5)
    inp['w_up_gate'] = nrm((L, GATE_LORA, RWKV_WIDTH), GATE_LORA ** -0.5)
    inp['k_k'] = 0.85 + nrm((L, RWKV_WIDTH), 0.05)
    inp['k_a'] = 1.0 + nrm((L, RWKV_WIDTH), 0.05)
    inp['r_k'] = nrm((L, RWKV_HEADS, HEAD_SIZE), 0.1)
    inp['ln_x_w'] = 1.0 + nrm((L, RWKV_WIDTH), 0.05)
    inp['ln_x_b'] = nrm((L, RWKV_WIDTH), 0.02)
    inp['pool_w'] = nrm((L, len(POOL_WINDOWS), POOL_GROUP, POOL_GROUP), POOL_GROUP ** -0.5)
    inp['pool_scale'] = 1.0 + nrm((L, POOL_WIDTH), 0.1)
    inp['mem_wk'] = nrm((L, D, MEM_WIDTH), D ** -0.5)
    inp['mem_wv'] = nrm((L, D, MEM_WIDTH), D ** -0.5)
    inp['w_out'] = nrm((L, D, D), D ** -0.5 * BETA)
    inp['ln1_g'] = 1.0 + nrm((L, D), 0.05)
    inp['ln1_b'] = nrm((L, D), 0.02)
    inp['router_w'] = nrm((L, D, N_EXPERTS), D ** -0.5)
    inp['router_b'] = nrm((L, N_EXPERTS), 0.01)
    inp['exp_gate'] = nrm((L, N_EXPERTS, D, EXPERT_FF), D ** -0.5)
    inp['exp_up'] = nrm((L, N_EXPERTS, D, EXPERT_FF), D ** -0.5)
    inp['exp_down'] = nrm((L, N_EXPERTS, EXPERT_FF, D), EXPERT_FF ** -0.5 * BETA)
    inp['sh_gate'] = nrm((L, D, EXPERT_FF), D ** -0.5)
    inp['sh_up'] = nrm((L, D, EXPERT_FF), D ** -0.5)
    inp['sh_down'] = nrm((L, EXPERT_FF, D), EXPERT_FF ** -0.5 * BETA)
    inp['ln2_g'] = 1.0 + nrm((L, D), 0.05)
    inp['ln2_b'] = nrm((L, D), 0.02)
    return inp


def reference(x_prompt, x_sample, mem_prompt, state_wkv, state_shift, state_pool, cache_mem_k, cache_mem_v,
              w_in, mu_shift, w0, w_up_decay, a0, w_up_aaa, w_up_gate, k_k, k_a, r_k, ln_x_w, ln_x_b,
              pool_w, pool_scale, mem_wk, mem_wv, w_out, ln1_g, ln1_b, router_w, router_b,
              exp_gate, exp_up, exp_down, sh_gate, sh_up, sh_down, ln2_g, ln2_b):
    b = x_prompt.shape[0]
    hp, hs_ = x_prompt, x_sample
    wkv_p, shift_p, pool_p, mk_p, mv_p = [], [], [], [], []
    wkv_s, shift_s, pool_s = [], [], []
    for l in range(DEPTH):
        p = dict(w_in=w_in[l], mu_shift=mu_shift[l], w0=w0[l], w_up_decay=w_up_decay[l], a0=a0[l],
                 w_up_aaa=w_up_aaa[l], w_up_gate=w_up_gate[l], k_k=k_k[l], k_a=k_a[l], r_k=r_k[l],
                 ln_x_w=ln_x_w[l], ln_x_b=ln_x_b[l], pool_w=pool_w[l], pool_scale=pool_scale[l],
                 w_out=w_out[l], ln1_g=ln1_g[l], ln1_b=ln1_b[l], router_w=router_w[l], router_b=router_b[l],
                 exp_gate=exp_gate[l], exp_up=exp_up[l], exp_down=exp_down[l], sh_gate=sh_gate[l],
                 sh_up=sh_up[l], sh_down=sh_down[l], ln2_g=ln2_g[l], ln2_b=ln2_b[l])
        mk = jnp.einsum('bmd,de->bme', mem_prompt, mem_wk[l]).reshape(b, MEM_TOKENS, MEM_HEADS, MEM_HEAD_DIM)
        mv = jnp.einsum('bmd,de->bme', mem_prompt, mem_wv[l]).reshape(b, MEM_TOKENS, MEM_HEADS, MEM_HEAD_DIM)
        hp, s_p, sh_p, pl_p = _decoder_layer(
            hp, jnp.zeros((b, D_MODEL), hp.dtype), jnp.zeros((b, POOL_BUF, POOL_WIDTH), hp.dtype),
            jnp.zeros((b, RWKV_HEADS, HEAD_SIZE, HEAD_SIZE), jnp.float32), mk, mv, 0, p)
        wkv_p.append(s_p); shift_p.append(sh_p); pool_p.append(pl_p); mk_p.append(mk); mv_p.append(mv)
        hs_, s_s, sh_s, pl_s = _decoder_layer(hs_, state_shift[l], state_pool[l], state_wkv[l],
                                              cache_mem_k[l], cache_mem_v[l], PAST_LEN, p)
        wkv_s.append(s_s); shift_s.append(sh_s); pool_s.append(pl_s)
    return (hp, hs_, jnp.stack(wkv_p), jnp.stack(shift_p), jnp.stack(pool_p), jnp.stack(mk_p), jnp.stack(mv_p),
            jnp.stack(wkv_s), jnp.stack(shift_s), jnp.stack(pool_s))
```

```python
import functools

import jax
import jax.numpy as jnp
from jax import lax
from jax.experimental import pallas as pl
from jax.experimental.pallas import tpu as pltpu

F32 = jnp.float32
BF16 = jnp.bfloat16
I32 = jnp.int32
HIGHEST = lax.Precision.HIGHEST

D_MODEL = 1024
HEAD = 64
N_HEADS = 8
RWKV_W = 512
POOL_W = 256
MEM_W = 256
MEM_HEADS = 4
MEM_TOKENS = 256
POOL_BUF = 15
POOL_PAD = 16
DECAY_LORA = 64
AAA_LORA = 64
GATE_LORA = 160
LORA_PAD = 128
GATE_PAD = 256
ZS_W = 3 * RWKV_W + 2 * LORA_PAD + GATE_PAD
Z_W = ZS_W + POOL_W + MEM_W
N_EXPERTS = 256
TOP_K = 8
N_GROUPS = 8
GROUP_SIZE = N_EXPERTS // N_GROUPS
TOPK_GROUPS = 4
EXPERT_FF = 256
ROUTED_SCALE = 2.5
CHUNK = 256
PAST_LEN = 16384
LN_EPS = 1e-5
GN_EPS = 64e-5
LANES = 128
VMEM_LIMIT = 48 * 1024 * 1024


def _cparams(n_axes):
    return pltpu.CompilerParams(dimension_semantics=("arbitrary",) * n_axes, vmem_limit_bytes=VMEM_LIMIT)


def _tile(n, preferred, multiple=8):
    best = None
    for c in range(multiple, min(n, preferred) + 1, multiple):
        if n % c == 0:
            best = c
    assert best is not None, (n, preferred, multiple)
    return best


def _sigmoid(x):
    return 1.0 / (1.0 + jnp.exp(-x))


def _matmul_kernel(x_ref, w_ref, o_ref):
    o_ref[...] = jnp.dot(x_ref[...].astype(BF16), w_ref[...], preferred_element_type=F32)


def _matmul(x, w, tm):
    m, k = x.shape
    n = w.shape[1]
    assert m % tm == 0
    return pl.pallas_call(
        _matmul_kernel,
        grid=(m // tm,),
        in_specs=[pl.BlockSpec((tm, k), lambda i: (i, 0)), pl.BlockSpec((k, n), lambda i: (0, 0))],
        out_specs=pl.BlockSpec((tm, n), lambda i: (i, 0)),
        out_shape=jax.ShapeDtypeStruct((m, n), F32),
        compiler_params=_cparams(1),
        name="matmul",
    )(x, w)


def _segsum(x, bd):
    hi = x.astype(BF16)
    lo = (x - hi.astype(F32)).astype(BF16)
    return jnp.dot(hi, bd, preferred_element_type=F32) + jnp.dot(lo, bd, preferred_element_type=F32)


def _prep_kernel(zs_ref, sh_ref, mu_ref, w0_ref, a0_ref, wd_ref, wa_ref, wg_ref, kk_ref, ka_ref, rk_ref, bd_ref,
                 r_o, w_o, k_o, v_o, kk_o, nb_o, g_o, bonus_o):
    zs = zs_ref[...]
    zm = zs + mu_ref[...] * (sh_ref[...] - zs)
    r = zm[:, 0:RWKV_W]
    k = zm[:, RWKV_W:2 * RWKV_W]
    v = zm[:, 2 * RWKV_W:3 * RWKV_W]
    c0 = 3 * RWKV_W
    xw = zm[:, c0:c0 + LORA_PAD]
    xa = zm[:, c0 + LORA_PAD:c0 + 2 * LORA_PAD]
    xg = zm[:, c0 + 2 * LORA_PAD:c0 + 2 * LORA_PAD + GATE_PAD]
    bd = bd_ref[...]
    u = -(w0_ref[...] + jnp.dot(jnp.tanh(xw), wd_ref[...], precision=HIGHEST, preferred_element_type=F32))
    softplus = jnp.maximum(u, 0.0) + jnp.log(1.0 + jnp.exp(-jnp.abs(u)))
    decay = jnp.exp(-jnp.exp(-softplus - 0.5))
    a = _sigmoid(a0_ref[...] + jnp.dot(xa, wa_ref[...], precision=HIGHEST, preferred_element_type=F32))
    g = jnp.dot(_sigmoid(xg), wg_ref[...], precision=HIGHEST, preferred_element_type=F32)
    kk = k * kk_ref[...]
    kk = kk / jnp.maximum(jnp.sqrt(_segsum(kk * kk, bd)), 1e-12)
    kp = k * (1.0 + (a - 1.0) * ka_ref[...])
    r_o[...] = r
    w_o[...] = decay
    k_o[...] = kp
    v_o[...] = v
    kk_o[...] = kk
    nb_o[...] = -(kk * a)
    g_o[...] = g
    bonus_o[...] = _segsum(r * kp * rk_ref[...], bd) * v


def _rwkv_prep(z, shifted, n_tok, tt, params):
    assert n_tok % tt == 0
    row = lambda w: pl.BlockSpec((1, w), lambda i: (0, 0))
    full = lambda a: pl.BlockSpec(a.shape, lambda i: (0, 0))
    tok = pl.BlockSpec((tt, RWKV_W), lambda i: (i, 0))
    mu, w0, a0, wd, wa, wg, k_k, k_a, r_k, bd = params
    return pl.pallas_call(
        _prep_kernel,
        grid=(n_tok // tt,),
        in_specs=[pl.BlockSpec((tt, ZS_W), lambda i: (i, 0)), pl.BlockSpec((tt, ZS_W), lambda i: (i, 0)),
                  row(ZS_W), row(RWKV_W), row(RWKV_W), full(wd), full(wa), full(wg),
                  row(RWKV_W), row(RWKV_W), row(RWKV_W), full(bd)],
        out_specs=[tok] * 8,
        out_shape=[jax.ShapeDtypeStruct((n_tok, RWKV_W), F32)] * 8,
        compiler_params=_cparams(1),
        name="rwkv_prep",
    )(z, shifted, mu, w0, a0, wd, wa, wg, k_k, k_a, r_k, bd)


def _wkv_kernel(r_ref, w_ref, k_ref, kk_ref, nb_ref, v_ref, s0_ref, y_ref, s_ref, *, tb):
    @pl.when(pl.program_id(1) == 0)
    def _():
        s_ref[...] = s0_ref[...]

    n_acc = 4

    def tree(parts):
        parts = [p for p in parts if p is not None]
        while len(parts) > 1:
            parts = [parts[i] + parts[i + 1] if i + 1 < len(parts) else parts[i] for i in range(0, len(parts), 2)]
        return parts[0]

    def step(t, carry):
        acc = [None] * n_acc
        for k in range(HEAD):
            p = s_ref[k] * kk_ref[t, k:k + 1, :]
            acc[k % n_acc] = p if acc[k % n_acc] is None else acc[k % n_acc] + p
        sa = tree(acc)
        vt = v_ref[t]
        acc = [None] * n_acc
        for k in range(HEAD):
            sn = s_ref[k] * w_ref[t, k:k + 1, :] + sa * nb_ref[t, k:k + 1, :] + vt * k_ref[t, k:k + 1, :]
            s_ref[k] = sn
            p = sn * r_ref[t, k:k + 1, :]
            acc[k % n_acc] = p if acc[k % n_acc] is None else acc[k % n_acc] + p
        y_ref[t] = tree(acc)
        return carry

    lax.fori_loop(0, tb, step, 0)


def _wkv(r, w, k, kk, nb, v, s0, tb):
    t, _, l = r.shape
    vr = v.shape[1]
    assert t % tb == 0 and l % LANES == 0
    vec = pl.BlockSpec((tb, HEAD, LANES), lambda g, i: (i, 0, g))
    val = pl.BlockSpec((tb, vr, LANES), lambda g, i: (i, 0, g))
    st = pl.BlockSpec((HEAD, vr, LANES), lambda g, i: (0, 0, g))
    return pl.pallas_call(
        functools.partial(_wkv_kernel, tb=tb),
        grid=(l // LANES, t // tb),
        in_specs=[vec, vec, vec, vec, vec, val, st],
        out_specs=[val, st],
        out_shape=[jax.ShapeDtypeStruct((t, vr, l), F32), jax.ShapeDtypeStruct((HEAD, vr, l), F32)],
        compiler_params=_cparams(2),
        name="wkv",
    )(r, w, k, kk, nb, v, s0)


def _pool_kernel(full_ref, pw_ref, ps_ref, o_ref, *, t, pos0):
    bb = full_ref.shape[0]
    lane = lax.broadcasted_iota(I32, (t, POOL_W), 1)
    window = jnp.where(lane < 64, 2, jnp.where(lane < 128, 4, jnp.where(lane < 192, 8, 16)))
    pos = lax.broadcasted_iota(I32, (t, POOL_W), 0) + (pos0 + 1)
    cnt = jnp.minimum(pos, window).astype(F32)
    diffs = []
    for i in range(bb):
        f = full_ref[i]
        s2 = f + pltpu.roll(f, 1, 0)
        s4 = s2 + pltpu.roll(s2, 2, 0)
        s8 = s4 + pltpu.roll(s4, 4, 0)
        s16 = s8 + pltpu.roll(s8, 8, 0)
        wsum = jnp.where(lane < 64, s2[POOL_PAD:], jnp.where(lane < 128, s4[POOL_PAD:],
                                                            jnp.where(lane < 192, s8[POOL_PAD:], s16[POOL_PAD:])))
        diffs.append(wsum / cnt - f[POOL_PAD:])
    diff = diffs[0] if bb == 1 else jnp.concatenate(diffs, axis=0)
    out = jnp.dot(diff.astype(BF16), pw_ref[...], preferred_element_type=F32) * ps_ref[...]
    for i in range(bb):
        o_ref[i] = out[i * t:(i + 1) * t]


def _pool(full, pw_bd, pscale, bb, pos0):
    b, rows, _ = full.shape
    t = rows - POOL_PAD
    assert b % bb == 0
    return pl.pallas_call(
        functools.partial(_pool_kernel, t=t, pos0=pos0),
        grid=(b // bb,),
        in_specs=[pl.BlockSpec((bb, rows, POOL_W), lambda i: (i, 0, 0)),
                  pl.BlockSpec((POOL_W, POOL_W), lambda i: (0, 0)), pl.BlockSpec((1, POOL_W), lambda i: (0, 0))],
        out_specs=pl.BlockSpec((bb, t, POOL_W), lambda i: (i, 0, 0)),
        out_shape=jax.ShapeDtypeStruct((b, t, POOL_W), F32),
        compiler_params=_cparams(1),
        name="pool",
    )(full, pw_bd, pscale)


def _memattn_kernel(q_ref, k_ref, v_ref, o_ref):
    q = q_ref[0].astype(BF16)
    kf = k_ref[0]
    vf = v_ref[0]
    head_of_lane = lax.broadcasted_iota(I32, kf.shape, 1) // (MEM_W // MEM_HEADS)
    out = None
    for h in range(MEM_HEADS):
        kh = jnp.where(head_of_lane == h, kf, 0.0).astype(BF16)
        vh = jnp.where(head_of_lane == h, vf, 0.0).astype(BF16)
        s = lax.dot_general(q, kh, (((1,), (1,)), ((), ())), preferred_element_type=F32) * (64 ** -0.5)
        e = jnp.exp(s - jnp.max(s, axis=-1, keepdims=True))
        p = e / jnp.sum(e, axis=-1, keepdims=True)
        o = jnp.dot(p.astype(BF16), vh, preferred_element_type=F32)
        out = o if out is None else out + o
    o_ref[0] = out


def _memattn(z3, mk, mv, tt):
    b, t, _ = z3.shape
    assert t % tt == 0
    qcol = (Z_W - MEM_W) // MEM_W
    kv = pl.BlockSpec((1, MEM_TOKENS, MEM_W), lambda i, j: (i, 0, 0))
    return pl.pallas_call(
        _memattn_kernel,
        grid=(b, t // tt),
        in_specs=[pl.BlockSpec((1, tt, MEM_W), lambda i, j: (i, j, qcol)), kv, kv],
        out_specs=pl.BlockSpec((1, tt, MEM_W), lambda i, j: (i, j, 0)),
        out_shape=jax.ShapeDtypeStruct((b, t, MEM_W), F32),
        compiler_params=_cparams(2),
        name="mem_attn",
    )(z3, mk, mv)


def _layer_norm(x, g, b):
    mu = jnp.mean(x, axis=-1, keepdims=True)
    xc = x - mu
    var = jnp.mean(xc * xc, axis=-1, keepdims=True)
    return xc * lax.rsqrt(var + LN_EPS) * g + b


def _post_kernel(y_ref, bonus_ref, g_ref, op_ref, om_ref, x_ref, wo_ref, lxw_ref, lxb_ref, l1g_ref, l1b_ref, bd_ref,
                 h_ref, *, alpha):
    y = y_ref[...]
    bd = bd_ref[...]
    mu = _segsum(y, bd) * (1.0 / HEAD)
    yc = y - mu
    var = _segsum(yc * yc, bd) * (1.0 / HEAD)
    yn = yc * lax.rsqrt(var + GN_EPS) * lxw_ref[...] + lxb_ref[...]
    o_rwkv = (yn + bonus_ref[...]) * g_ref[...]
    mixed = (jnp.dot(o_rwkv.astype(BF16), wo_ref[0:RWKV_W, :], preferred_element_type=F32)
             + jnp.dot(op_ref[...].astype(BF16), wo_ref[RWKV_W:RWKV_W + POOL_W, :], preferred_element_type=F32)
             + jnp.dot(om_ref[...].astype(BF16), wo_ref[RWKV_W + POOL_W:, :], preferred_element_type=F32))
    h_ref[...] = _layer_norm(alpha * x_ref[...] + mixed, l1g_ref[...], l1b_ref[...])


def _post(y, bonus, g, o_pool, o_mem, x, wo, lxw, lxb, l1g, l1b, bd, tt, alpha):
    n = y.shape[0]
    assert n % tt == 0
    tok = lambda w: pl.BlockSpec((tt, w), lambda i: (i, 0))
    row = lambda w: pl.BlockSpec((1, w), lambda i: (0, 0))
    full = lambda a: pl.BlockSpec(a.shape, lambda i: (0, 0))
    return pl.pallas_call(
        functools.partial(_post_kernel, alpha=alpha),
        grid=(n // tt,),
        in_specs=[tok(RWKV_W), tok(RWKV_W), tok(RWKV_W), tok(POOL_W), tok(MEM_W), tok(D_MODEL), full(wo),
                  row(RWKV_W), row(RWKV_W), row(D_MODEL), row(D_MODEL), full(bd)],
        out_specs=tok(D_MODEL),
        out_shape=jax.ShapeDtypeStruct((n, D_MODEL), F32),
        compiler_params=_cparams(1),
        name="post",
    )(y, bonus, g, o_pool, o_mem, x, wo, lxw, lxb, l1g, l1b, bd)


def _router_kernel(h_ref, rwt_ref, bias_ref, tri_ref, idx_o, gate_o, rank_o, cnt_o, carry_ref):
    @pl.when(pl.program_id(0) == 0)
    def _():
        carry_ref[...] = jnp.zeros_like(carry_ref)

    neg = -jnp.inf
    logits = lax.dot_general(rwt_ref[...], h_ref[...], (((1,), (1,)), ((), ())),
                             precision=HIGHEST, preferred_element_type=F32)
    scores = _sigmoid(logits)
    sel = scores + bias_ref[...]
    tt = sel.shape[1]
    gio = lax.broadcasted_iota(I32, (GROUP_SIZE, tt), 0).astype(F32)
    blocks, gscore = [], []
    for g in range(N_GROUPS):
        blk = sel[g * GROUP_SIZE:(g + 1) * GROUP_SIZE, :]
        m1 = jnp.max(blk, axis=0, keepdims=True)
        first = jnp.min(jnp.where(blk == m1, gio, float(GROUP_SIZE)), axis=0, keepdims=True)
        m2 = jnp.max(jnp.where(gio == first, neg, blk), axis=0, keepdims=True)
        blocks.append(blk)
        gscore.append(m1 + m2)
    masked = []
    for g in range(N_GROUPS):
        beaten_by = jnp.zeros((1, tt), F32)
        for g2 in range(N_GROUPS):
            if g2 != g:
                wins = (gscore[g2] >= gscore[g]) if g2 < g else (gscore[g2] > gscore[g])
                beaten_by = beaten_by + jnp.where(wins, 1.0, 0.0)
        masked.append(jnp.where(beaten_by < TOPK_GROUPS, blocks[g], neg))
    msel = jnp.concatenate(masked, axis=0)
    eio = lax.broadcasted_iota(I32, msel.shape, 0).astype(F32)
    chosen = jnp.zeros(msel.shape, F32)
    idxs, scs = [], []
    for _ in range(TOP_K):
        m = jnp.max(msel, axis=0, keepdims=True)
        first = jnp.min(jnp.where(msel == m, eio, float(N_EXPERTS)), axis=0, keepdims=True)
        hit = eio == first
        scs.append(jnp.sum(jnp.where(hit, scores, 0.0), axis=0, keepdims=True))
        msel = jnp.where(hit, neg, msel)
        chosen = jnp.where(hit, 1.0, chosen)
        idxs.append(first)
    total = scs[0]
    for s in scs[1:]:
        total = total + s
    before = jnp.dot(chosen.astype(BF16), tri_ref[...], preferred_element_type=F32) + carry_ref[...]
    ranks = [jnp.sum(jnp.where(eio == i, before, 0.0), axis=0, keepdims=True) for i in idxs]
    carry_ref[...] = carry_ref[...] + jnp.sum(chosen, axis=1, keepdims=True)
    idx_o[...] = jnp.concatenate(idxs, axis=0).astype(I32)
    gate_o[...] = jnp.concatenate([s / total * ROUTED_SCALE for s in scs], axis=0)
    rank_o[...] = jnp.concatenate(ranks, axis=0).astype(I32)
    cnt_o[...] = jnp.broadcast_to(carry_ref[...], cnt_o.shape)


def _router(h, rwt, bias, tt):
    n = h.shape[0]
    assert n % tt == 0
    tri = (lax.broadcasted_iota(I32, (tt, tt), 0) < lax.broadcasted_iota(I32, (tt, tt), 1)).astype(BF16)
    tokT = pl.BlockSpec((TOP_K, tt), lambda i: (0, i))
    return pl.pallas_call(
        _router_kernel,
        grid=(n // tt,),
        in_specs=[pl.BlockSpec((tt, D_MODEL), lambda i: (i, 0)), pl.BlockSpec((N_EXPERTS, D_MODEL), lambda i: (0, 0)),
                  pl.BlockSpec((N_EXPERTS, 1), lambda i: (0, 0)), pl.BlockSpec((tt, tt), lambda i: (0, 0))],
        out_specs=[tokT, tokT, tokT, pl.BlockSpec((N_EXPERTS, LANES), lambda i: (0, 0))],
        out_shape=[jax.ShapeDtypeStruct((TOP_K, n), I32), jax.ShapeDtypeStruct((TOP_K, n), F32),
                   jax.ShapeDtypeStruct((TOP_K, n), I32), jax.ShapeDtypeStruct((N_EXPERTS, LANES), F32)],
        scratch_shapes=[pltpu.VMEM((N_EXPERTS, 1), F32)],
        compiler_params=_cparams(1),
        name="router",
    )(h, rwt, bias, tri)


def _row_copy(src_ref, src_row, dst_ref, dst_row, sem):
    return pltpu.make_async_copy(src_ref.at[pl.ds(src_row, 1), :], dst_ref.at[pl.ds(dst_row, 1), :], sem)


TOKENS_PER_IDX_ROW = LANES // TOP_K


def _dest_of(dest_smem, r, j):
    return dest_smem[r // TOKENS_PER_IDX_ROW, (r % TOKENS_PER_IDX_ROW) * TOP_K + j]


def _dispatch_kernel(zpos_ref, dest_hbm, h_ref, xs_hbm, dest_smem, zero_buf, sem_idx, sem_zero, sem_rows, *, td):
    i = pl.program_id(0)

    def zero_copy(e):
        return pltpu.make_async_copy(zero_buf, xs_hbm.at[pl.ds(pl.multiple_of(zpos_ref[e], CHUNK), CHUNK), :], sem_zero)

    @pl.when(i == 0)
    def _():
        zero_buf[...] = jnp.zeros_like(zero_buf)

        def start(e, c):
            @pl.when(zpos_ref[e] >= 0)
            def _():
                zero_copy(e).start()
            return c

        def wait(e, c):
            @pl.when(zpos_ref[e] >= 0)
            def _():
                zero_copy(e).wait()
            return c

        lax.fori_loop(0, N_EXPERTS, start, 0)
        lax.fori_loop(0, N_EXPERTS, wait, 0)

    idx_copy = pltpu.make_async_copy(dest_hbm.at[i], dest_smem, sem_idx)
    idx_copy.start()
    idx_copy.wait()

    def start_rows(r, c):
        for j in range(TOP_K):
            _row_copy(h_ref, r, xs_hbm, _dest_of(dest_smem, r, j), sem_rows).start()
        return c

    def wait_rows(r, c):
        for j in range(TOP_K):
            _row_copy(h_ref, r, xs_hbm, _dest_of(dest_smem, r, j), sem_rows).wait()
        return c

    lax.fori_loop(0, td, start_rows, 0)
    lax.fori_loop(0, td, wait_rows, 0)


def _dispatch(zpos, dest_tiles, h, n_rows, td):
    n = h.shape[0]
    assert n % td == 0
    return pl.pallas_call(
        functools.partial(_dispatch_kernel, td=td),
        grid_spec=pltpu.PrefetchScalarGridSpec(
            num_scalar_prefetch=1,
            grid=(n // td,),
            in_specs=[pl.BlockSpec(memory_space=pl.ANY), pl.BlockSpec((td, D_MODEL), lambda i, z: (i, 0))],
            out_specs=pl.BlockSpec(memory_space=pl.ANY),
            scratch_shapes=[pltpu.SMEM((td // TOKENS_PER_IDX_ROW, LANES), I32), pltpu.VMEM((CHUNK, D_MODEL), F32),
                            pltpu.SemaphoreType.DMA, pltpu.SemaphoreType.DMA, pltpu.SemaphoreType.DMA],
        ),
        out_shape=jax.ShapeDtypeStruct((n_rows, D_MODEL), F32),
        compiler_params=_cparams(1),
        name="dispatch",
    )(zpos, dest_tiles, h)


def _ffn_kernel(be_ref, nu_ref, x_ref, wg_ref, wu_ref, wd_ref, o_ref):
    @pl.when(pl.program_id(0) < nu_ref[0])
    def _():
        x = x_ref[...].astype(BF16)
        gate = jnp.dot(x, wg_ref[0].astype(BF16), preferred_element_type=F32)
        up = jnp.dot(x, wu_ref[0].astype(BF16), preferred_element_type=F32)
        act = (gate * _sigmoid(gate)) * up
        o_ref[...] = jnp.dot(act.astype(BF16), wd_ref[0].astype(BF16), preferred_element_type=F32)


def _ffn(blk_exp, n_used, xs, wg, wu, wd):
    n_chunks = xs.shape[0] // CHUNK
    rows = pl.BlockSpec((CHUNK, D_MODEL), lambda c, be, nu: (jnp.minimum(c, nu[0] - 1), 0))
    return pl.pallas_call(
        _ffn_kernel,
        grid_spec=pltpu.PrefetchScalarGridSpec(
            num_scalar_prefetch=2,
            grid=(n_chunks,),
            in_specs=[rows,
                      pl.BlockSpec((1, D_MODEL, EXPERT_FF), lambda c, be, nu: (be[c], 0, 0)),
                      pl.BlockSpec((1, D_MODEL, EXPERT_FF), lambda c, be, nu: (be[c], 0, 0)),
                      pl.BlockSpec((1, EXPERT_FF, D_MODEL), lambda c, be, nu: (be[c], 0, 0))],
            out_specs=rows,
        ),
        out_shape=jax.ShapeDtypeStruct(xs.shape, F32),
        compiler_params=_cparams(1),
        name="expert_ffn",
    )(blk_exp, n_used, xs, wg, wu, wd)


def _combine_kernel(dest_hbm, gate_ref, h_ref, ys_hbm, sg_ref, su_ref, sd_ref, l2g_ref, l2b_ref, o_ref,
                    dest_smem, buf, sem_idx, sem_rows, *, tc, alpha):
    i = pl.program_id(0)
    idx_copy = pltpu.make_async_copy(dest_hbm.at[i], dest_smem, sem_idx)
    idx_copy.start()
    idx_copy.wait()

    def start_rows(r, c):
        for j in range(TOP_K):
            _row_copy(ys_hbm, _dest_of(dest_smem, r, j), buf.at[j], r, sem_rows).start()
        return c

    def wait_rows(r, c):
        for j in range(TOP_K):
            _row_copy(ys_hbm, _dest_of(dest_smem, r, j), buf.at[j], r, sem_rows).wait()
        return c

    lax.fori_loop(0, tc, start_rows, 0)
    h = h_ref[...]
    hb = h.astype(BF16)
    sgate = jnp.dot(hb, sg_ref[...], preferred_element_type=F32)
    sup = jnp.dot(hb, su_ref[...], preferred_element_type=F32)
    shared = jnp.dot(((sgate * _sigmoid(sgate)) * sup).astype(BF16), sd_ref[...], preferred_element_type=F32)
    lax.fori_loop(0, tc, wait_rows, 0)
    gate = gate_ref[...]
    routed = buf[0] * gate[:, 0:1]
    for j in range(1, TOP_K):
        routed = routed + buf[j] * gate[:, j:j + 1]
    o_ref[...] = _layer_norm(alpha * h + (routed + shared), l2g_ref[...], l2b_ref[...])


def _combine(dest_tiles, gate, h, ys, sg, su, sd, l2g, l2b, tc, alpha):
    n = h.shape[0]
    assert n % tc == 0
    tok = lambda w: pl.BlockSpec((tc, w), lambda i: (i, 0))
    full = lambda a: pl.BlockSpec(a.shape, lambda i: (0, 0))
    anyspec = pl.BlockSpec(memory_space=pl.ANY)
    return pl.pallas_call(
        functools.partial(_combine_kernel, tc=tc, alpha=alpha),
        grid=(n // tc,),
        in_specs=[anyspec, tok(TOP_K), tok(D_MODEL), anyspec, full(sg), full(su), full(sd), full(l2g), full(l2b)],
        out_specs=tok(D_MODEL),
        out_shape=jax.ShapeDtypeStruct((n, D_MODEL), F32),
        scratch_shapes=[pltpu.SMEM((tc // TOKENS_PER_IDX_ROW, LANES), I32), pltpu.VMEM((TOP_K, tc, D_MODEL), F32),
                        pltpu.SemaphoreType.DMA, pltpu.SemaphoreType.DMA],
        compiler_params=_cparams(1),
        name="combine",
    )(dest_tiles, gate, h, ys, sg, su, sd, l2g, l2b)


def _pad_cols(a, width):
    return jnp.pad(a, ((0, 0), (0, width - a.shape[1])))


def _pack_shift_cols(a):
    c = 3 * RWKV_W
    return jnp.concatenate([a[:, :c], _pad_cols(a[:, c:c + DECAY_LORA], LORA_PAD),
                            _pad_cols(a[:, c + DECAY_LORA:c + DECAY_LORA + AAA_LORA], LORA_PAD),
                            _pad_cols(a[:, c + DECAY_LORA + AAA_LORA:], GATE_PAD)], axis=1)


def _chains_to_lanes(a, b, t, dup):
    a = a.reshape(b, t, N_HEADS, HEAD).transpose(1, 3, 0, 2).reshape(t, HEAD, b * N_HEADS)
    return jnp.concatenate([a, a], axis=-1) if dup else a


def _values_to_lanes(a, b, t, split):
    a = a.reshape(b, t, N_HEADS, HEAD).transpose(1, 3, 0, 2).reshape(t, HEAD, b * N_HEADS)
    if split:
        a = a.reshape(t, 2, HEAD // 2, b * N_HEADS).transpose(0, 2, 1, 3).reshape(t, HEAD // 2, 2 * b * N_HEADS)
    return a


def _values_from_lanes(y, b, t, split):
    if split:
        y = y.reshape(t, HEAD // 2, 2, b * N_HEADS).transpose(0, 2, 1, 3).reshape(t, HEAD, b * N_HEADS)
    return y.reshape(t, HEAD, b, N_HEADS).transpose(2, 0, 3, 1).reshape(b * t, RWKV_W)


def _state_to_lanes(s, b, split):
    s = s.transpose(3, 2, 0, 1).reshape(HEAD, HEAD, b * N_HEADS)
    if split:
        s = s.reshape(HEAD, 2, HEAD // 2, b * N_HEADS).transpose(0, 2, 1, 3).reshape(HEAD, HEAD // 2, 2 * b * N_HEADS)
    return s


def _state_from_lanes(s, b, split):
    if split:
        s = s.reshape(HEAD, HEAD // 2, 2, b * N_HEADS).transpose(0, 2, 1, 3).reshape(HEAD, HEAD, b * N_HEADS)
    return s.reshape(HEAD, HEAD, b, N_HEADS).transpose(2, 3, 1, 0)


def _mixer(z, zprev, x2, b, t, wkv0, pool_prev, mk, mv, pos0, wts, tiles):
    n = b * t
    tt, tb, pool_bb, att_tt = tiles
    z3 = z[:n].reshape(b, t, Z_W)
    shifted = jnp.concatenate([zprev[:, None, :], z3[:, :-1, :ZS_W]], axis=1).reshape(n, ZS_W)
    r, w, kp, v, kk, nb, g, bonus = _rwkv_prep(z, shifted, n, tt, wts["prep"])

    split = (b * N_HEADS) % LANES != 0
    lay = lambda a: _chains_to_lanes(a, b, t, split)
    y_l, s_l = _wkv(lay(r), lay(w), lay(kp), lay(kk), lay(nb), _values_to_lanes(v, b, t, split),
                    _state_to_lanes(wkv0, b, split), tb)
    y = _values_from_lanes(y_l, b, t, split)
    wkv_new = _state_from_lanes(s_l, b, split)

    zp = z3[:, :, ZS_W:ZS_W + POOL_W]
    full = jnp.concatenate([jnp.zeros((b, POOL_PAD - POOL_BUF, POOL_W), F32), pool_prev, zp], axis=1)
    o_pool = _pool(full, wts["pool_w"], wts["pool_scale"], pool_bb, pos0).reshape(n, POOL_W)
    pool_new = full[:, -POOL_BUF:]

    o_mem = _memattn(z3, mk, mv, att_tt).reshape(n, MEM_W)
    h = _post(y, bonus, g, o_pool, o_mem, x2, *wts["post"], tt, wts["alpha"])
    return h, wkv_new, pool_new


def kernel(x_prompt, x_sample, mem_prompt, state_wkv, state_shift, state_pool, cache_mem_k, cache_mem_v, w_in, mu_shift, w0, w_up_decay, a0, w_up_aaa, w_up_gate, k_k, k_a, r_k, ln_x_w, ln_x_b, pool_w, pool_scale, mem_wk, mem_wv, w_out, ln1_g, ln1_b, router_w, router_b, exp_gate, exp_up, exp_down, sh_gate, sh_up, sh_down, ln2_g, ln2_b):
    depth = w_in.shape[0]
    assert depth == 1
    l = 0
    alpha = (2.0 * depth) ** 0.25
    bp, tp, d = x_prompt.shape
    bs, ts, _ = x_sample.shape
    n_p, n_s = bp * tp, bs * ts
    n = n_p + n_s

    w_in_p = jnp.concatenate([_pack_shift_cols(w_in[l][:, :1824]), w_in[l][:, 1824:]], axis=1).astype(BF16)
    row = lambda a: a.reshape(1, -1)
    pad_rows = lambda a, rows: jnp.pad(a, ((0, rows - a.shape[0]), (0, 0)))
    bd = jnp.kron(jnp.eye(N_HEADS, dtype=F32), jnp.ones((HEAD, HEAD), F32)).astype(BF16)
    pw = pool_w[l]
    pw_bd = jnp.zeros((POOL_W, POOL_W), F32)
    for gi in range(4):
        pw_bd = pw_bd.at[gi * 64:(gi + 1) * 64, gi * 64:(gi + 1) * 64].set(pw[gi])
    wts = {
        "prep": (_pack_shift_cols(row(mu_shift[l])), row(w0[l]), row(a0[l]), pad_rows(w_up_decay[l], LORA_PAD),
                 pad_rows(w_up_aaa[l], LORA_PAD), pad_rows(w_up_gate[l], GATE_PAD), row(k_k[l]), row(k_a[l]),
                 row(r_k[l]), bd),
        "pool_w": pw_bd.astype(BF16), "pool_scale": row(pool_scale[l]),
        "post": (w_out[l].astype(BF16), row(ln_x_w[l]), row(ln_x_b[l]), row(ln1_g[l]), row(ln1_b[l]), bd),
        "alpha": alpha,
    }

    xp2 = x_prompt.reshape(n_p, d)
    xs2 = x_sample.reshape(n_s, d)
    z_p = _matmul(xp2, w_in_p, _tile(n_p, 512))
    z_s = _matmul(jnp.concatenate([xs2, state_shift[l]], axis=0), w_in_p, _tile(n_s + bs, 512))
    mkv = _matmul(mem_prompt.reshape(bp * MEM_TOKENS, d),
                  jnp.concatenate([mem_wk[l], mem_wv[l]], axis=1).astype(BF16), _tile(bp * MEM_TOKENS, 512))
    mk_p = mkv[:, :MEM_W].reshape(bp, MEM_TOKENS, MEM_W)
    mv_p = mkv[:, MEM_W:].reshape(bp, MEM_TOKENS, MEM_W)

    h_p, wkv_p, pool_p = _mixer(z_p, jnp.zeros((bp, ZS_W), F32), xp2, bp, tp,
                                jnp.zeros((bp, N_HEADS, HEAD, HEAD), F32), jnp.zeros((bp, POOL_BUF, POOL_W), F32),
                                mk_p, mv_p, 0, wts,
                                (_tile(n_p, 256), _tile(tp, 32, 1), 1, _tile(tp, 256)))
    h_s, wkv_s, pool_s = _mixer(z_s, z_s[n_s:, :ZS_W], xs2, bs, ts, state_wkv[l], state_pool[l],
                                cache_mem_k[l].reshape(bs, MEM_TOKENS, MEM_W),
                                cache_mem_v[l].reshape(bs, MEM_TOKENS, MEM_W), PAST_LEN, wts,
                                (_tile(n_s, 256), _tile(ts, 32, 1), _tile(bs, 16, 1), _tile(ts, 256)))

    h = jnp.concatenate([h_p, h_s], axis=0)
    idx, gate, rank, cnt = _router(h, router_w[l].T, router_b[l].reshape(N_EXPERTS, 1), _tile(n, 256, LANES))
    counts = cnt[:, 0].astype(I32)
    padded = (counts + CHUNK - 1) // CHUNK * CHUNK
    pad_end = jnp.cumsum(padded)
    pad_start = pad_end - padded
    n_chunks = (n * TOP_K + N_EXPERTS * (CHUNK - 1) + CHUNK - 1) // CHUNK
    dest = pad_start[idx] + rank
    td = _tile(n, 256, LANES)
    dest_tiles = dest.T.reshape(n // td, td // TOKENS_PER_IDX_ROW, LANES)
    zpos = jnp.where(padded > 0, pad_end - CHUNK, -1).astype(I32)
    blk_exp = jnp.minimum(jnp.searchsorted(pad_end, jnp.arange(n_chunks, dtype=I32) * CHUNK, side="right"),
                          N_EXPERTS - 1).astype(I32)
    n_used = (pad_end[-1:] // CHUNK).astype(I32)
    xs_sorted = _dispatch(zpos, dest_tiles, h, n_chunks * CHUNK, td)
    ys_sorted = _ffn(blk_exp, n_used, xs_sorted, exp_gate[l], exp_up[l], exp_down[l])
    y = _combine(dest_tiles, gate.T, h, ys_sorted, sh_gate[l].astype(BF16), sh_up[l].astype(BF16),
                 sh_down[l].astype(BF16), row(ln2_g[l]), row(ln2_b[l]), td, alpha)

    return (y[:n_p].reshape(bp, tp, d), y[n_p:].reshape(bs, ts, d),
            wkv_p[None], x_prompt[:, -1][None], pool_p[None],
            mk_p.reshape(bp, MEM_TOKENS, MEM_HEADS, HEAD)[None], mv_p.reshape(bp, MEM_TOKENS, MEM_HEADS, HEAD)[None],
            wkv_s[None], x_sample[:, -1][None], pool_s[None])
```

```python
import functools

import jax
import jax.numpy as jnp
from jax import lax
from jax.experimental import pallas as pl
from jax.experimental.pallas import tpu as pltpu

F32 = jnp.float32
BF16 = jnp.bfloat16
I32 = jnp.int32
HIGHEST = lax.Precision.HIGHEST

D_MODEL = 1024
HEAD = 64
N_HEADS = 8
RWKV_W = 512
POOL_W = 256
MEM_W = 256
MEM_HEADS = 4
MEM_TOKENS = 256
POOL_BUF = 15
POOL_PAD = 16
DECAY_LORA = 64
AAA_LORA = 64
GATE_LORA = 160
LORA_PAD = 128
GATE_PAD = 256
ZS_W = 3 * RWKV_W + 2 * LORA_PAD + GATE_PAD
Z_W = ZS_W + POOL_W + MEM_W
N_EXPERTS = 256
TOP_K = 8
N_GROUPS = 8
GROUP_SIZE = N_EXPERTS // N_GROUPS
TOPK_GROUPS = 4
EXPERT_FF = 256
ROUTED_SCALE = 2.5
CHUNK = 256
PAST_LEN = 16384
LN_EPS = 1e-5
GN_EPS = 64e-5
LANES = 128
VMEM_LIMIT = 48 * 1024 * 1024


def _cparams(n_axes):
    return pltpu.CompilerParams(dimension_semantics=("arbitrary",) * n_axes, vmem_limit_bytes=VMEM_LIMIT)


def _tile(n, preferred, multiple=8):
    best = None
    for c in range(multiple, min(n, preferred) + 1, multiple):
        if n % c == 0:
            best = c
    assert best is not None, (n, preferred, multiple)
    return best


def _sigmoid(x):
    return 1.0 / (1.0 + jnp.exp(-x))


def _matmul_kernel(x_ref, w_ref, o_ref):
    o_ref[...] = jnp.dot(x_ref[...].astype(BF16), w_ref[...], preferred_element_type=F32)


def _matmul(x, w, tm):
    m, k = x.shape
    n = w.shape[1]
    assert m % tm == 0
    return pl.pallas_call(
        _matmul_kernel,
        grid=(m // tm,),
        in_specs=[pl.BlockSpec((tm, k), lambda i: (i, 0)), pl.BlockSpec((k, n), lambda i: (0, 0))],
        out_specs=pl.BlockSpec((tm, n), lambda i: (i, 0)),
        out_shape=jax.ShapeDtypeStruct((m, n), F32),
        compiler_params=_cparams(1),
        name="matmul",
    )(x, w)


def _segsum(x, bd):
    hi = x.astype(BF16)
    lo = (x - hi.astype(F32)).astype(BF16)
    return jnp.dot(hi, bd, preferred_element_type=F32) + jnp.dot(lo, bd, preferred_element_type=F32)


def _prep_kernel(zs_ref, sh_ref, mu_ref, w0_ref, a0_ref, wd_ref, wa_ref, wg_ref, kk_ref, ka_ref, rk_ref, bd_ref,
                 r_o, w_o, k_o, v_o, kk_o, nb_o, g_o, bonus_o):
    zs = zs_ref[...]
    zm = zs + mu_ref[...] * (sh_ref[...] - zs)
    r = zm[:, 0:RWKV_W]
    k = zm[:, RWKV_W:2 * RWKV_W]
    v = zm[:, 2 * RWKV_W:3 * RWKV_W]
    c0 = 3 * RWKV_W
    xw = zm[:, c0:c0 + LORA_PAD]
    xa = zm[:, c0 + LORA_PAD:c0 + 2 * LORA_PAD]
    xg = zm[:, c0 + 2 * LORA_PAD:c0 + 2 * LORA_PAD + GATE_PAD]
    bd = bd_ref[...]
    u = -(w0_ref[...] + jnp.dot(jnp.tanh(xw), wd_ref[...], precision=HIGHEST, preferred_element_type=F32))
    softplus = jnp.maximum(u, 0.0) + jnp.log(1.0 + jnp.exp(-jnp.abs(u)))
    decay = jnp.exp(-jnp.exp(-softplus - 0.5))
    a = _sigmoid(a0_ref[...] + jnp.dot(xa, wa_ref[...], precision=HIGHEST, preferred_element_type=F32))
    g = jnp.dot(_sigmoid(xg), wg_ref[...], precision=HIGHEST, preferred_element_type=F32)
    kk = k * kk_ref[...]
    kk = kk / jnp.maximum(jnp.sqrt(_segsum(kk * kk, bd)), 1e-12)
    kp = k * (1.0 + (a - 1.0) * ka_ref[...])
    r_o[...] = r
    w_o[...] = decay
    k_o[...] = kp
    v_o[...] = v
    kk_o[...] = kk
    nb_o[...] = -(kk * a)
    g_o[...] = g
    bonus_o[...] = _segsum(r * kp * rk_ref[...], bd) * v


def _rwkv_prep(z, shifted, n_tok, tt, params):
    assert n_tok % tt == 0
    row = lambda w: pl.BlockSpec((1, w), lambda i: (0, 0))
    full = lambda a: pl.BlockSpec(a.shape, lambda i: (0, 0))
    tok = pl.BlockSpec((tt, RWKV_W), lambda i: (i, 0))
    mu, w0, a0, wd, wa, wg, k_k, k_a, r_k, bd = params
    return pl.pallas_call(
        _prep_kernel,
        grid=(n_tok // tt,),
        in_specs=[pl.BlockSpec((tt, ZS_W), lambda i: (i, 0)), pl.BlockSpec((tt, ZS_W), lambda i: (i, 0)),
                  row(ZS_W), row(RWKV_W), row(RWKV_W), full(wd), full(wa), full(wg),
                  row(RWKV_W), row(RWKV_W), row(RWKV_W), full(bd)],
        out_specs=[tok] * 8,
        out_shape=[jax.ShapeDtypeStruct((n_tok, RWKV_W), F32)] * 8,
        compiler_params=_cparams(1),
        name="rwkv_prep",
    )(z, shifted, mu, w0, a0, wd, wa, wg, k_k, k_a, r_k, bd)


def _wkv_kernel(r_ref, w_ref, k_ref, kk_ref, nb_ref, v_ref, s0_ref, y_ref, s_ref, *, tb, halves):
    n_k = HEAD // halves

    @pl.when(pl.program_id(1) == 0)
    def _():
        s_ref[...] = s0_ref[...]

    n_acc = 4

    def total(parts):
        while len(parts) > 1:
            parts = [parts[i] + parts[i + 1] if i + 1 < len(parts) else parts[i] for i in range(0, len(parts), 2)]
        out = parts[0]
        return out + pltpu.roll(out, LANES // 2, 1) if halves == 2 else out

    def step(t, carry):
        acc = [None] * n_acc
        for k in range(n_k):
            p = s_ref[k] * kk_ref[t, k:k + 1, :]
            acc[k % n_acc] = p if acc[k % n_acc] is None else acc[k % n_acc] + p
        sa = total(acc)
        vt = v_ref[t]
        if halves == 2:
            vt = jnp.concatenate([vt, vt], axis=1)
        acc = [None] * n_acc
        for k in range(n_k):
            sn = s_ref[k] * w_ref[t, k:k + 1, :] + sa * nb_ref[t, k:k + 1, :] + vt * k_ref[t, k:k + 1, :]
            s_ref[k] = sn
            p = sn * r_ref[t, k:k + 1, :]
            acc[k % n_acc] = p if acc[k % n_acc] is None else acc[k % n_acc] + p
        y_ref[t] = total(acc)[:, :LANES // halves]
        return carry

    lax.fori_loop(0, tb, step, 0)


def _wkv(r, w, k, kk, nb, v, s0, tb):
    t, n_k, l = r.shape
    halves = HEAD // n_k
    assert t % tb == 0 and l % LANES == 0 and v.shape[2] * halves == l and (halves == 1 or l == LANES)
    vec = pl.BlockSpec((tb, n_k, LANES), lambda g, i: (i, 0, g))
    val = pl.BlockSpec((tb, HEAD, LANES // halves), lambda g, i: (i, 0, g))
    st = pl.BlockSpec((n_k, HEAD, LANES), lambda g, i: (0, 0, g))
    return pl.pallas_call(
        functools.partial(_wkv_kernel, tb=tb, halves=halves),
        grid=(l // LANES, t // tb),
        in_specs=[vec, vec, vec, vec, vec, val, st],
        out_specs=[val, st],
        out_shape=[jax.ShapeDtypeStruct(v.shape, F32), jax.ShapeDtypeStruct(s0.shape, F32)],
        compiler_params=_cparams(2),
        name="wkv",
    )(r, w, k, kk, nb, v, s0)


def _pool_kernel(full_ref, pw_ref, ps_ref, o_ref, *, t, pos0):
    bb = full_ref.shape[0]
    lane = lax.broadcasted_iota(I32, (t, POOL_W), 1)
    window = jnp.where(lane < 64, 2, jnp.where(lane < 128, 4, jnp.where(lane < 192, 8, 16)))
    pos = lax.broadcasted_iota(I32, (t, POOL_W), 0) + (pos0 + 1)
    cnt = jnp.minimum(pos, window).astype(F32)
    diffs = []
    for i in range(bb):
        f = full_ref[i]
        s2 = f + pltpu.roll(f, 1, 0)
        s4 = s2 + pltpu.roll(s2, 2, 0)
        s8 = s4 + pltpu.roll(s4, 4, 0)
        s16 = s8 + pltpu.roll(s8, 8, 0)
        wsum = jnp.where(lane < 64, s2[POOL_PAD:], jnp.where(lane < 128, s4[POOL_PAD:],
                                                            jnp.where(lane < 192, s8[POOL_PAD:], s16[POOL_PAD:])))
        diffs.append(wsum / cnt - f[POOL_PAD:])
    diff = diffs[0] if bb == 1 else jnp.concatenate(diffs, axis=0)
    out = jnp.dot(diff.astype(BF16), pw_ref[...], preferred_element_type=F32) * ps_ref[...]
    for i in range(bb):
        o_ref[i] = out[i * t:(i + 1) * t]


def _pool(full, pw_bd, pscale, bb, pos0):
    b, rows, _ = full.shape
    t = rows - POOL_PAD
    assert b % bb == 0
    return pl.pallas_call(
        functools.partial(_pool_kernel, t=t, pos0=pos0),
        grid=(b // bb,),
        in_specs=[pl.BlockSpec((bb, rows, POOL_W), lambda i: (i, 0, 0)),
                  pl.BlockSpec((POOL_W, POOL_W), lambda i: (0, 0)), pl.BlockSpec((1, POOL_W), lambda i: (0, 0))],
        out_specs=pl.BlockSpec((bb, t, POOL_W), lambda i: (i, 0, 0)),
        out_shape=jax.ShapeDtypeStruct((b, t, POOL_W), F32),
        compiler_params=_cparams(1),
        name="pool",
    )(full, pw_bd, pscale)


def _memattn_kernel(q_ref, k_ref, v_ref, o_ref):
    q = q_ref[0].astype(BF16)
    kf = k_ref[0]
    vf = v_ref[0]
    head_of_lane = lax.broadcasted_iota(I32, kf.shape, 1) // (MEM_W // MEM_HEADS)
    out = None
    for h in range(MEM_HEADS):
        kh = jnp.where(head_of_lane == h, kf, 0.0).astype(BF16)
        vh = jnp.where(head_of_lane == h, vf, 0.0).astype(BF16)
        s = lax.dot_general(q, kh, (((1,), (1,)), ((), ())), preferred_element_type=F32) * (64 ** -0.5)
        e = jnp.exp(s - jnp.max(s, axis=-1, keepdims=True))
        p = e / jnp.sum(e, axis=-1, keepdims=True)
        o = jnp.dot(p.astype(BF16), vh, preferred_element_type=F32)
        out = o if out is None else out + o
    o_ref[0] = out


def _memattn(z3, mk, mv, tt):
    b, t, _ = z3.shape
    assert t % tt == 0
    qcol = (Z_W - MEM_W) // MEM_W
    kv = pl.BlockSpec((1, MEM_TOKENS, MEM_W), lambda i, j: (i, 0, 0))
    return pl.pallas_call(
        _memattn_kernel,
        grid=(b, t // tt),
        in_specs=[pl.BlockSpec((1, tt, MEM_W), lambda i, j: (i, j, qcol)), kv, kv],
        out_specs=pl.BlockSpec((1, tt, MEM_W), lambda i, j: (i, j, 0)),
        out_shape=jax.ShapeDtypeStruct((b, t, MEM_W), F32),
        compiler_params=_cparams(2),
        name="mem_attn",
    )(z3, mk, mv)


def _layer_norm(x, g, b):
    mu = jnp.mean(x, axis=-1, keepdims=True)
    xc = x - mu
    var = jnp.mean(xc * xc, axis=-1, keepdims=True)
    return xc * lax.rsqrt(var + LN_EPS) * g + b


def _post_kernel(y_ref, bonus_ref, g_ref, op_ref, om_ref, x_ref, wo_ref, lxw_ref, lxb_ref, l1g_ref, l1b_ref, bd_ref,
                 h_ref, *, alpha):
    y = y_ref[...]
    bd = bd_ref[...]
    mu = _segsum(y, bd) * (1.0 / HEAD)
    yc = y - mu
    var = _segsum(yc * yc, bd) * (1.0 / HEAD)
    yn = yc * lax.rsqrt(var + GN_EPS) * lxw_ref[...] + lxb_ref[...]
    o_rwkv = (yn + bonus_ref[...]) * g_ref[...]
    mixed = (jnp.dot(o_rwkv.astype(BF16), wo_ref[0:RWKV_W, :], preferred_element_type=F32)
             + jnp.dot(op_ref[...].astype(BF16), wo_ref[RWKV_W:RWKV_W + POOL_W, :], preferred_element_type=F32)
             + jnp.dot(om_ref[...].astype(BF16), wo_ref[RWKV_W + POOL_W:, :], preferred_element_type=F32))
    h_ref[...] = _layer_norm(alpha * x_ref[...] + mixed, l1g_ref[...], l1b_ref[...])


def _post(y, bonus, g, o_pool, o_mem, x, wo, lxw, lxb, l1g, l1b, bd, tt, alpha):
    n = y.shape[0]
    assert n % tt == 0
    tok = lambda w: pl.BlockSpec((tt, w), lambda i: (i, 0))
    row = lambda w: pl.BlockSpec((1, w), lambda i: (0, 0))
    full = lambda a: pl.BlockSpec(a.shape, lambda i: (0, 0))
    return pl.pallas_call(
        functools.partial(_post_kernel, alpha=alpha),
        grid=(n // tt,),
        in_specs=[tok(RWKV_W), tok(RWKV_W), tok(RWKV_W), tok(POOL_W), tok(MEM_W), tok(D_MODEL), full(wo),
                  row(RWKV_W), row(RWKV_W), row(D_MODEL), row(D_MODEL), full(bd)],
        out_specs=tok(D_MODEL),
        out_shape=jax.ShapeDtypeStruct((n, D_MODEL), F32),
        compiler_params=_cparams(1),
        name="post",
    )(y, bonus, g, o_pool, o_mem, x, wo, lxw, lxb, l1g, l1b, bd)


def _router_kernel(h_ref, rwt_ref, bias_ref, tri_ref, idx_o, gate_o, rank_o, cnt_o, carry_ref):
    @pl.when(pl.program_id(0) == 0)
    def _():
        carry_ref[...] = jnp.zeros_like(carry_ref)

    neg = -jnp.inf
    logits = lax.dot_general(rwt_ref[...], h_ref[...], (((1,), (1,)), ((), ())),
                             precision=HIGHEST, preferred_element_type=F32)
    scores = _sigmoid(logits)
    sel = scores + bias_ref[...]
    tt = sel.shape[1]
    gio = lax.broadcasted_iota(I32, (GROUP_SIZE, tt), 0).astype(F32)
    blocks, gscore = [], []
    for g in range(N_GROUPS):
        blk = sel[g * GROUP_SIZE:(g + 1) * GROUP_SIZE, :]
        m1 = jnp.max(blk, axis=0, keepdims=True)
        first = jnp.min(jnp.where(blk == m1, gio, float(GROUP_SIZE)), axis=0, keepdims=True)
        m2 = jnp.max(jnp.where(gio == first, neg, blk), axis=0, keepdims=True)
        blocks.append(blk)
        gscore.append(m1 + m2)
    masked = []
    for g in range(N_GROUPS):
        beaten_by = jnp.zeros((1, tt), F32)
        for g2 in range(N_GROUPS):
            if g2 != g:
                wins = (gscore[g2] >= gscore[g]) if g2 < g else (gscore[g2] > gscore[g])
                beaten_by = beaten_by + jnp.where(wins, 1.0, 0.0)
        masked.append(jnp.where(beaten_by < TOPK_GROUPS, blocks[g], neg))
    msel = jnp.concatenate(masked, axis=0)
    eio = lax.broadcasted_iota(I32, msel.shape, 0).astype(F32)
    chosen = jnp.zeros(msel.shape, F32)
    idxs, scs = [], []
    for _ in range(TOP_K):
        m = jnp.max(msel, axis=0, keepdims=True)
        first = jnp.min(jnp.where(msel == m, eio, float(N_EXPERTS)), axis=0, keepdims=True)
        hit = eio == first
        scs.append(jnp.sum(jnp.where(hit, scores, 0.0), axis=0, keepdims=True))
        msel = jnp.where(hit, neg, msel)
        chosen = jnp.where(hit, 1.0, chosen)
        idxs.append(first)
    total = scs[0]
    for s in scs[1:]:
        total = total + s
    before = jnp.dot(chosen.astype(BF16), tri_ref[...], preferred_element_type=F32) + carry_ref[...]
    ranks = [jnp.sum(jnp.where(eio == i, before, 0.0), axis=0, keepdims=True) for i in idxs]
    carry_ref[...] = carry_ref[...] + jnp.sum(chosen, axis=1, keepdims=True)
    idx_o[...] = jnp.concatenate(idxs, axis=0).astype(I32)
    gate_o[...] = jnp.concatenate([s / total * ROUTED_SCALE for s in scs], axis=0)
    rank_o[...] = jnp.concatenate(ranks, axis=0).astype(I32)
    cnt_o[...] = jnp.broadcast_to(carry_ref[...], cnt_o.shape)


def _router(h, rwt, bias, tt):
    n = h.shape[0]
    assert n % tt == 0
    tri = (lax.broadcasted_iota(I32, (tt, tt), 0) < lax.broadcasted_iota(I32, (tt, tt), 1)).astype(BF16)
    tokT = pl.BlockSpec((TOP_K, tt), lambda i: (0, i))
    return pl.pallas_call(
        _router_kernel,
        grid=(n // tt,),
        in_specs=[pl.BlockSpec((tt, D_MODEL), lambda i: (i, 0)), pl.BlockSpec((N_EXPERTS, D_MODEL), lambda i: (0, 0)),
                  pl.BlockSpec((N_EXPERTS, 1), lambda i: (0, 0)), pl.BlockSpec((tt, tt), lambda i: (0, 0))],
        out_specs=[tokT, tokT, tokT, pl.BlockSpec((N_EXPERTS, LANES), lambda i: (0, 0))],
        out_shape=[jax.ShapeDtypeStruct((TOP_K, n), I32), jax.ShapeDtypeStruct((TOP_K, n), F32),
                   jax.ShapeDtypeStruct((TOP_K, n), I32), jax.ShapeDtypeStruct((N_EXPERTS, LANES), F32)],
        scratch_shapes=[pltpu.VMEM((N_EXPERTS, 1), F32)],
        compiler_params=_cparams(1),
        name="router",
    )(h, rwt, bias, tri)


ROW_TILE = (D_MODEL // LANES, LANES)


def _row_copy(src_ref, src_row, dst_ref, dst_row, sem):
    return pltpu.make_async_copy(src_ref.at[pl.ds(src_row, 1)], dst_ref.at[pl.ds(dst_row, 1)], sem)


def _store_row_tiles(ref, x):
    for c in range(ROW_TILE[0]):
        ref[:, c, :] = x[:, c * LANES:(c + 1) * LANES]


def _load_row_tiles(ref):
    return jnp.concatenate([ref[:, c, :] for c in range(ROW_TILE[0])], axis=-1)


TOKENS_PER_IDX_ROW = LANES // TOP_K


def _dest_of(dest_smem, r, j):
    return dest_smem[r // TOKENS_PER_IDX_ROW, (r % TOKENS_PER_IDX_ROW) * TOP_K + j]


def _dispatch_kernel(zpos_ref, dest_hbm, h_ref, xs_hbm, dest_smem, zero_buf, rows_buf, sem_idx, sem_zero, sem_rows,
                     *, td):
    i = pl.program_id(0)

    def zero_copy(e):
        return pltpu.make_async_copy(zero_buf, xs_hbm.at[pl.ds(pl.multiple_of(zpos_ref[e], CHUNK), CHUNK)], sem_zero)

    @pl.when(i == 0)
    def _():
        zero_buf[...] = jnp.zeros_like(zero_buf)

        def start(e, c):
            @pl.when(zpos_ref[e] >= 0)
            def _():
                zero_copy(e).start()
            return c

        def wait(e, c):
            @pl.when(zpos_ref[e] >= 0)
            def _():
                zero_copy(e).wait()
            return c

        lax.fori_loop(0, N_EXPERTS, start, 0)
        lax.fori_loop(0, N_EXPERTS, wait, 0)

    idx_copy = pltpu.make_async_copy(dest_hbm.at[i], dest_smem, sem_idx)
    idx_copy.start()
    _store_row_tiles(rows_buf, h_ref[...])
    idx_copy.wait()

    def start_rows(r, c):
        for j in range(TOP_K):
            _row_copy(rows_buf, r, xs_hbm, _dest_of(dest_smem, r, j), sem_rows).start(priority=j % 2)
        return c

    def wait_rows(r, c):
        for j in range(TOP_K):
            _row_copy(rows_buf, r, xs_hbm, _dest_of(dest_smem, r, j), sem_rows).wait()
        return c

    lax.fori_loop(0, td, start_rows, 0)
    lax.fori_loop(0, td, wait_rows, 0)


def _dispatch(zpos, dest_tiles, h, n_rows, td):
    n = h.shape[0]
    assert n % td == 0
    return pl.pallas_call(
        functools.partial(_dispatch_kernel, td=td),
        grid_spec=pltpu.PrefetchScalarGridSpec(
            num_scalar_prefetch=1,
            grid=(n // td,),
            in_specs=[pl.BlockSpec(memory_space=pl.ANY), pl.BlockSpec((td, D_MODEL), lambda i, z: (i, 0))],
            out_specs=pl.BlockSpec(memory_space=pl.ANY),
            scratch_shapes=[pltpu.SMEM((td // TOKENS_PER_IDX_ROW, LANES), I32), pltpu.VMEM((CHUNK,) + ROW_TILE, F32),
                            pltpu.VMEM((td,) + ROW_TILE, F32),
                            pltpu.SemaphoreType.DMA, pltpu.SemaphoreType.DMA, pltpu.SemaphoreType.DMA],
        ),
        out_shape=jax.ShapeDtypeStruct((n_rows,) + ROW_TILE, F32),
        compiler_params=_cparams(1),
        name="dispatch",
    )(zpos, dest_tiles, h)


def _ffn_kernel(be_ref, nu_ref, x_ref, wg_ref, wu_ref, wd_ref, o_ref):
    @pl.when(pl.program_id(0) < nu_ref[0])
    def _():
        x = _load_row_tiles(x_ref).astype(BF16)
        gate = jnp.dot(x, wg_ref[0].astype(BF16), preferred_element_type=F32)
        up = jnp.dot(x, wu_ref[0].astype(BF16), preferred_element_type=F32)
        act = (gate * _sigmoid(gate)) * up
        _store_row_tiles(o_ref, jnp.dot(act.astype(BF16), wd_ref[0].astype(BF16), preferred_element_type=F32))


def _ffn(blk_exp, n_used, xs, wg, wu, wd):
    n_chunks = xs.shape[0] // CHUNK
    rows = pl.BlockSpec((CHUNK,) + ROW_TILE, lambda c, be, nu: (jnp.minimum(c, nu[0] - 1), 0, 0))
    return pl.pallas_call(
        _ffn_kernel,
        grid_spec=pltpu.PrefetchScalarGridSpec(
            num_scalar_prefetch=2,
            grid=(n_chunks,),
            in_specs=[rows,
                      pl.BlockSpec((1, D_MODEL, EXPERT_FF), lambda c, be, nu: (be[c], 0, 0)),
                      pl.BlockSpec((1, D_MODEL, EXPERT_FF), lambda c, be, nu: (be[c], 0, 0)),
                      pl.BlockSpec((1, EXPERT_FF, D_MODEL), lambda c, be, nu: (be[c], 0, 0))],
            out_specs=rows,
        ),
        out_shape=jax.ShapeDtypeStruct(xs.shape, F32),
        compiler_params=_cparams(1),
        name="expert_ffn",
    )(blk_exp, n_used, xs, wg, wu, wd)


def _combine_kernel(dest_hbm, gate_ref, h_ref, ys_hbm, sg_ref, su_ref, sd_ref, l2g_ref, l2b_ref, o_ref,
                    dest_smem, buf, sem_idx, sem_rows, *, tc, alpha):
    i = pl.program_id(0)
    idx_copy = pltpu.make_async_copy(dest_hbm.at[i], dest_smem, sem_idx)
    idx_copy.start()
    idx_copy.wait()

    def start_rows(r, c):
        for j in range(TOP_K):
            _row_copy(ys_hbm, _dest_of(dest_smem, r, j), buf.at[j], r, sem_rows).start(priority=j % 2)
        return c

    def wait_rows(r, c):
        for j in range(TOP_K):
            _row_copy(ys_hbm, _dest_of(dest_smem, r, j), buf.at[j], r, sem_rows).wait()
        return c

    lax.fori_loop(0, tc, start_rows, 0)
    h = h_ref[...]
    hb = h.astype(BF16)
    sgate = jnp.dot(hb, sg_ref[...], preferred_element_type=F32)
    sup = jnp.dot(hb, su_ref[...], preferred_element_type=F32)
    shared = jnp.dot(((sgate * _sigmoid(sgate)) * sup).astype(BF16), sd_ref[...], preferred_element_type=F32)
    lax.fori_loop(0, tc, wait_rows, 0)
    gate = gate_ref[...]
    routed = _load_row_tiles(buf.at[0]) * gate[:, 0:1]
    for j in range(1, TOP_K):
        routed = routed + _load_row_tiles(buf.at[j]) * gate[:, j:j + 1]
    o_ref[...] = _layer_norm(alpha * h + (routed + shared), l2g_ref[...], l2b_ref[...])


def _combine(dest_tiles, gate, h, ys, sg, su, sd, l2g, l2b, tc, alpha):
    n = h.shape[0]
    assert n % tc == 0
    tok = lambda w: pl.BlockSpec((tc, w), lambda i: (i, 0))
    full = lambda a: pl.BlockSpec(a.shape, lambda i: (0, 0))
    anyspec = pl.BlockSpec(memory_space=pl.ANY)
    return pl.pallas_call(
        functools.partial(_combine_kernel, tc=tc, alpha=alpha),
        grid=(n // tc,),
        in_specs=[anyspec, tok(TOP_K), tok(D_MODEL), anyspec, full(sg), full(su), full(sd), full(l2g), full(l2b)],
        out_specs=tok(D_MODEL),
        out_shape=jax.ShapeDtypeStruct((n, D_MODEL), F32),
        scratch_shapes=[pltpu.SMEM((tc // TOKENS_PER_IDX_ROW, LANES), I32), pltpu.VMEM((TOP_K, tc) + ROW_TILE, F32),
                        pltpu.SemaphoreType.DMA, pltpu.SemaphoreType.DMA],
        compiler_params=_cparams(1),
        name="combine",
    )(dest_tiles, gate, h, ys, sg, su, sd, l2g, l2b)


def _pad_cols(a, width):
    return jnp.pad(a, ((0, 0), (0, width - a.shape[1])))


def _pack_shift_cols(a):
    c = 3 * RWKV_W
    return jnp.concatenate([a[:, :c], _pad_cols(a[:, c:c + DECAY_LORA], LORA_PAD),
                            _pad_cols(a[:, c + DECAY_LORA:c + DECAY_LORA + AAA_LORA], LORA_PAD),
                            _pad_cols(a[:, c + DECAY_LORA + AAA_LORA:], GATE_PAD)], axis=1)


def _keys_to_lanes(a, b, t, halves):
    a = a.reshape(b, t, N_HEADS, halves, HEAD // halves).transpose(1, 4, 3, 0, 2)
    return a.reshape(t, HEAD // halves, halves * b * N_HEADS)


def _values_to_lanes(a, b, t):
    return a.reshape(b, t, N_HEADS, HEAD).transpose(1, 3, 0, 2).reshape(t, HEAD, b * N_HEADS)


def _values_from_lanes(y, b, t):
    return y.reshape(t, HEAD, b, N_HEADS).transpose(2, 0, 3, 1).reshape(b * t, RWKV_W)


def _state_to_lanes(s, b, halves):
    s = s.reshape(b, N_HEADS, HEAD, halves, HEAD // halves).transpose(4, 2, 3, 0, 1)
    return s.reshape(HEAD // halves, HEAD, halves * b * N_HEADS)


def _state_from_lanes(s, b, halves):
    s = s.reshape(HEAD // halves, HEAD, halves, b, N_HEADS).transpose(3, 4, 1, 2, 0)
    return s.reshape(b, N_HEADS, HEAD, HEAD)


def _mixer(z, zprev, x2, b, t, wkv0, pool_prev, mk, mv, pos0, wts, tiles):
    n = b * t
    tt, tb, pool_bb, att_tt = tiles
    z3 = z[:n].reshape(b, t, Z_W)
    shifted = jnp.concatenate([zprev[:, None, :], z3[:, :-1, :ZS_W]], axis=1).reshape(n, ZS_W)
    r, w, kp, v, kk, nb, g, bonus = _rwkv_prep(z, shifted, n, tt, wts["prep"])

    halves = 1 if (b * N_HEADS) % LANES == 0 else LANES // (b * N_HEADS)
    lay = lambda a: _keys_to_lanes(a, b, t, halves)
    y_l, s_l = _wkv(lay(r), lay(w), lay(kp), lay(kk), lay(nb), _values_to_lanes(v, b, t),
                    _state_to_lanes(wkv0, b, halves), tb)
    y = _values_from_lanes(y_l, b, t)
    wkv_new = _state_from_lanes(s_l, b, halves)

    zp = z3[:, :, ZS_W:ZS_W + POOL_W]
    full = jnp.concatenate([jnp.zeros((b, POOL_PAD - POOL_BUF, POOL_W), F32), pool_prev, zp], axis=1)
    o_pool = _pool(full, wts["pool_w"], wts["pool_scale"], pool_bb, pos0).reshape(n, POOL_W)
    pool_new = full[:, -POOL_BUF:]

    o_mem = _memattn(z3, mk, mv, att_tt).reshape(n, MEM_W)
    h = _post(y, bonus, g, o_pool, o_mem, x2, *wts["post"], tt, wts["alpha"])
    return h, wkv_new, pool_new


def kernel(x_prompt, x_sample, mem_prompt, state_wkv, state_shift, state_pool, cache_mem_k, cache_mem_v, w_in, mu_shift, w0, w_up_decay, a0, w_up_aaa, w_up_gate, k_k, k_a, r_k, ln_x_w, ln_x_b, pool_w, pool_scale, mem_wk, mem_wv, w_out, ln1_g, ln1_b, router_w, router_b, exp_gate, exp_up, exp_down, sh_gate, sh_up, sh_down, ln2_g, ln2_b):
    depth = w_in.shape[0]
    assert depth == 1
    l = 0
    alpha = (2.0 * depth) ** 0.25
    bp, tp, d = x_prompt.shape
    bs, ts, _ = x_sample.shape
    n_p, n_s = bp * tp, bs * ts
    n = n_p + n_s

    w_in_p = jnp.concatenate([_pack_shift_cols(w_in[l][:, :1824]), w_in[l][:, 1824:]], axis=1).astype(BF16)
    row = lambda a: a.reshape(1, -1)
    pad_rows = lambda a, rows: jnp.pad(a, ((0, rows - a.shape[0]), (0, 0)))
    bd = jnp.kron(jnp.eye(N_HEADS, dtype=F32), jnp.ones((HEAD, HEAD), F32)).astype(BF16)
    pw = pool_w[l]
    pw_bd = jnp.zeros((POOL_W, POOL_W), F32)
    for gi in range(4):
        pw_bd = pw_bd.at[gi * 64:(gi + 1) * 64, gi * 64:(gi + 1) * 64].set(pw[gi])
    wts = {
        "prep": (_pack_shift_cols(row(mu_shift[l])), row(w0[l]), row(a0[l]), pad_rows(w_up_decay[l], LORA_PAD),
                 pad_rows(w_up_aaa[l], LORA_PAD), pad_rows(w_up_gate[l], GATE_PAD), row(k_k[l]), row(k_a[l]),
                 row(r_k[l]), bd),
        "pool_w": pw_bd.astype(BF16), "pool_scale": row(pool_scale[l]),
        "post": (w_out[l].astype(BF16), row(ln_x_w[l]), row(ln_x_b[l]), row(ln1_g[l]), row(ln1_b[l]), bd),
        "alpha": alpha,
    }

    xp2 = x_prompt.reshape(n_p, d)
    xs2 = x_sample.reshape(n_s, d)
    z_p = _matmul(xp2, w_in_p, _tile(n_p, 512))
    z_s = _matmul(jnp.concatenate([xs2, state_shift[l]], axis=0), w_in_p, _tile(n_s + bs, 512))
    mkv = _matmul(mem_prompt.reshape(bp * MEM_TOKENS, d),
                  jnp.concatenate([mem_wk[l], mem_wv[l]], axis=1).astype(BF16), _tile(bp * MEM_TOKENS, 512))
    mk_p = mkv[:, :MEM_W].reshape(bp, MEM_TOKENS, MEM_W)
    mv_p = mkv[:, MEM_W:].reshape(bp, MEM_TOKENS, MEM_W)

    h_p, wkv_p, pool_p = _mixer(z_p, jnp.zeros((bp, ZS_W), F32), xp2, bp, tp,
                                jnp.zeros((bp, N_HEADS, HEAD, HEAD), F32), jnp.zeros((bp, POOL_BUF, POOL_W), F32),
                                mk_p, mv_p, 0, wts,
                                (_tile(n_p, 256), _tile(tp, 32, 1), 1, _tile(tp, 256)))
    h_s, wkv_s, pool_s = _mixer(z_s, z_s[n_s:, :ZS_W], xs2, bs, ts, state_wkv[l], state_pool[l],
                                cache_mem_k[l].reshape(bs, MEM_TOKENS, MEM_W),
                                cache_mem_v[l].reshape(bs, MEM_TOKENS, MEM_W), PAST_LEN, wts,
                                (_tile(n_s, 256), _tile(ts, 32, 1), _tile(bs, 16, 1), _tile(ts, 256)))

    h = jnp.concatenate([h_p, h_s], axis=0)
    idx, gate, rank, cnt = _router(h, router_w[l].T, router_b[l].reshape(N_EXPERTS, 1), _tile(n, 256, LANES))
    counts = cnt[:, 0].astype(I32)
    padded = (counts + CHUNK - 1) // CHUNK * CHUNK
    pad_end = jnp.cumsum(padded)
    pad_start = pad_end - padded
    n_chunks = (n * TOP_K + N_EXPERTS * (CHUNK - 1) + CHUNK - 1) // CHUNK
    experts = jnp.arange(N_EXPERTS, dtype=I32)
    dest = rank + jnp.sum(jnp.where(idx[..., None] == experts, pad_start, 0), axis=-1)
    td = _tile(n, 256, LANES)
    dest_tiles = dest.T.reshape(n // td, td // TOKENS_PER_IDX_ROW, LANES)
    zpos = jnp.where(padded > 0, pad_end - CHUNK, -1).astype(I32)
    chunk_start = jnp.arange(n_chunks, dtype=I32) * CHUNK
    blk_exp = jnp.minimum(jnp.sum((pad_end[None, :] <= chunk_start[:, None]).astype(I32), axis=1), N_EXPERTS - 1)
    n_used = (pad_end[-1:] // CHUNK).astype(I32)
    xs_sorted = _dispatch(zpos, dest_tiles, h, n_chunks * CHUNK, td)
    ys_sorted = _ffn(blk_exp, n_used, xs_sorted, exp_gate[l], exp_up[l], exp_down[l])
    y = _combine(dest_tiles, gate.T, h, ys_sorted, sh_gate[l].astype(BF16), sh_up[l].astype(BF16),
                 sh_down[l].astype(BF16), row(ln2_g[l]), row(ln2_b[l]), td, alpha)

    return (y[:n_p].reshape(bp, tp, d), y[n_p:].reshape(bs, ts, d),
            wkv_p[None], x_prompt[:, -1][None], pool_p[None],
            mk_p.reshape(bp, MEM_TOKENS, MEM_HEADS, HEAD)[None], mv_p.reshape(bp, MEM_TOKENS, MEM_HEADS, HEAD)[None],
            wkv_s[None], x_sample[:, -1][None], pool_s[None])
```

```python
import functools

import jax
import jax.numpy as jnp
from jax import lax
from jax.experimental import pallas as pl
from jax.experimental.pallas import tpu as pltpu

F32 = jnp.float32
BF16 = jnp.bfloat16
I32 = jnp.int32
HIGHEST = lax.Precision.HIGHEST

D_MODEL = 1024
HEAD = 64
N_HEADS = 8
RWKV_W = 512
POOL_W = 256
MEM_W = 256
MEM_HEADS = 4
MEM_TOKENS = 256
POOL_BUF = 15
POOL_PAD = 16
DECAY_LORA = 64
AAA_LORA = 64
GATE_LORA = 160
LORA_PAD = 128
GATE_PAD = 256
ZS_W = 3 * RWKV_W + 2 * LORA_PAD + GATE_PAD
Z_W = ZS_W + POOL_W + MEM_W
N_EXPERTS = 256
TOP_K = 8
N_GROUPS = 8
GROUP_SIZE = N_EXPERTS // N_GROUPS
TOPK_GROUPS = 4
EXPERT_FF = 256
ROUTED_SCALE = 2.5
CHUNK = 256
PAST_LEN = 16384
LN_EPS = 1e-5
GN_EPS = 64e-5
LANES = 128
SUBLANES = 8
VMEM_LIMIT = 48 * 1024 * 1024


def _cparams(n_axes):
    return pltpu.CompilerParams(dimension_semantics=("arbitrary",) * n_axes, vmem_limit_bytes=VMEM_LIMIT)


def _tile(n, preferred, multiple=8):
    best = None
    for c in range(multiple, min(n, preferred) + 1, multiple):
        if n % c == 0:
            best = c
    assert best is not None, (n, preferred, multiple)
    return best


def _sigmoid(x):
    return 1.0 / (1.0 + jnp.exp(-x))


def _matmul_kernel(x_ref, w_ref, o_ref):
    o_ref[...] = jnp.dot(x_ref[...].astype(BF16), w_ref[...], preferred_element_type=F32)


def _matmul(x, w, tm):
    m, k = x.shape
    n = w.shape[1]
    assert m % tm == 0
    return pl.pallas_call(
        _matmul_kernel,
        grid=(m // tm,),
        in_specs=[pl.BlockSpec((tm, k), lambda i: (i, 0)), pl.BlockSpec((k, n), lambda i: (0, 0))],
        out_specs=pl.BlockSpec((tm, n), lambda i: (i, 0)),
        out_shape=jax.ShapeDtypeStruct((m, n), F32),
        compiler_params=_cparams(1),
        name="matmul",
    )(x, w)


def _segsum(x, bd):
    hi = x.astype(BF16)
    lo = (x - hi.astype(F32)).astype(BF16)
    return jnp.dot(hi, bd, preferred_element_type=F32) + jnp.dot(lo, bd, preferred_element_type=F32)


def _prep_kernel(zs_ref, zp_ref, mu_ref, w0_ref, a0_ref, wd_ref, wa_ref, wg_ref, kk_ref, ka_ref, rk_ref, bd_ref,
                 r_o, w_o, k_o, v_o, kk_o, nb_o, g_o, bonus_o, carry_ref, *, tiles_per_seq, seq_len):
    zs = zs_ref[...]
    rolled = pltpu.roll(zs, 1, 0)
    row = lax.broadcasted_iota(I32, zs.shape, 0)
    if tiles_per_seq:
        i = pl.program_id(0)

        @pl.when(i == 0)
        def _():
            carry_ref[...] = jnp.zeros_like(carry_ref)

        prev = jnp.where(i % tiles_per_seq == 0, zp_ref[0], carry_ref[...])
        shifted = jnp.where(row == 0, prev, rolled)
        carry_ref[...] = zs[zs.shape[0] - 1:, :]
    else:
        shifted = jnp.where((row & (seq_len - 1)) == 0, zp_ref[...], rolled)
    zm = zs + mu_ref[...] * (shifted - zs)
    r = zm[:, 0:RWKV_W]
    k = zm[:, RWKV_W:2 * RWKV_W]
    v = zm[:, 2 * RWKV_W:3 * RWKV_W]
    c0 = 3 * RWKV_W
    xw = zm[:, c0:c0 + LORA_PAD]
    xa = zm[:, c0 + LORA_PAD:c0 + 2 * LORA_PAD]
    xg = zm[:, c0 + 2 * LORA_PAD:c0 + 2 * LORA_PAD + GATE_PAD]
    bd = bd_ref[...]
    u = -(w0_ref[...] + jnp.dot(jnp.tanh(xw), wd_ref[...], precision=HIGHEST, preferred_element_type=F32))
    softplus = jnp.maximum(u, 0.0) + jnp.log(1.0 + jnp.exp(-jnp.abs(u)))
    decay = jnp.exp(-jnp.exp(-softplus - 0.5))
    a = _sigmoid(a0_ref[...] + jnp.dot(xa, wa_ref[...], precision=HIGHEST, preferred_element_type=F32))
    g = jnp.dot(_sigmoid(xg), wg_ref[...], precision=HIGHEST, preferred_element_type=F32)
    kk = k * kk_ref[...]
    kk = kk / jnp.maximum(jnp.sqrt(_segsum(kk * kk, bd)), 1e-12)
    kp = k * (1.0 + (a - 1.0) * ka_ref[...])
    r_o[...] = r
    w_o[...] = decay
    k_o[...] = kp
    v_o[...] = v
    kk_o[...] = kk
    nb_o[...] = -(kk * a)
    g_o[...] = g
    bonus_o[...] = _segsum(r * kp * rk_ref[...], bd) * v


def _rwkv_prep(z, zprev, b, t, tt, params):
    n_tok = b * t
    assert n_tok % tt == 0
    row = lambda w: pl.BlockSpec((1, w), lambda i: (0, 0))
    full = lambda a: pl.BlockSpec(a.shape, lambda i: (0, 0))
    tok = pl.BlockSpec((tt, RWKV_W), lambda i: (i, 0))
    mu, w0, a0, wd, wa, wg, k_k, k_a, r_k, bd = params
    if t % tt == 0:
        tiles_per_seq = t // tt
        zp = zprev.reshape(b, 1, ZS_W)
        zp_spec = pl.BlockSpec((1, 1, ZS_W), lambda i: (i // tiles_per_seq, 0, 0))
    else:
        assert tt % t == 0 and t & (t - 1) == 0
        tiles_per_seq = 0
        zp = jnp.repeat(zprev, t, axis=0)
        zp_spec = pl.BlockSpec((tt, ZS_W), lambda i: (i, 0))
    return pl.pallas_call(
        functools.partial(_prep_kernel, tiles_per_seq=tiles_per_seq, seq_len=t),
        grid=(n_tok // tt,),
        in_specs=[pl.BlockSpec((tt, ZS_W), lambda i: (i, 0)), zp_spec,
                  row(ZS_W), row(RWKV_W), row(RWKV_W), full(wd), full(wa), full(wg),
                  row(RWKV_W), row(RWKV_W), row(RWKV_W), full(bd)],
        out_specs=[tok] * 8,
        out_shape=[jax.ShapeDtypeStruct((n_tok, RWKV_W), F32)] * 8,
        scratch_shapes=[pltpu.VMEM((1, ZS_W), F32)],
        compiler_params=_cparams(1),
        name="rwkv_prep",
    )(z, zp, mu, w0, a0, wd, wa, wg, k_k, k_a, r_k, bd)


def _wkv_kernel(r_ref, w_ref, k_ref, kk_ref, nb_ref, v_ref, s0_ref, y_ref, s_ref, *, tb, halves):
    n_k = HEAD // halves

    @pl.when(pl.program_id(1) == 0)
    def _():
        s_ref[...] = s0_ref[...]

    slabs = HEAD // SUBLANES
    SLAB_GROUP = 4

    def total(parts):
        return tuple(p + pltpu.roll(p, LANES // 2, 1) if halves == 2 else p for p in parts)

    def rows(ref, t, k):
        return jnp.broadcast_to(ref[t, k:k + 1, :], (SUBLANES, LANES))

    def accumulate(acc, i, p):
        acc[i] = p if acc[i] is None else acc[i] + p

    def step(t, sa, with_next):
        vt = []
        for i in range(slabs):
            vi = v_ref[t, i * SUBLANES:(i + 1) * SUBLANES, :]
            vt.append(jnp.concatenate([vi, vi], axis=1) if halves == 2 else vi)
        y_acc = [None] * slabs
        next_acc = [None] * slabs
        for group in range(0, slabs, SLAB_GROUP):
            for k in range(n_k):
                wb, nbb, kb, rb = rows(w_ref, t, k), rows(nb_ref, t, k), rows(k_ref, t, k), rows(r_ref, t, k)
                kkb = rows(kk_ref, t + 1, k) if with_next else None
                for i in range(group, group + SLAB_GROUP):
                    sn = s_ref[k, i] * wb + sa[i] * nbb + vt[i] * kb
                    s_ref[k, i] = sn
                    accumulate(y_acc, i, sn * rb)
                    if with_next:
                        accumulate(next_acc, i, sn * kkb)
        for i, y in enumerate(total(y_acc)):
            y_ref[t, i * SUBLANES:(i + 1) * SUBLANES, :] = y[:, :LANES // halves]
        return total(next_acc) if with_next else None

    sa0 = [None] * slabs
    for k in range(n_k):
        kkb = rows(kk_ref, 0, k)
        for i in range(slabs):
            accumulate(sa0, i, s_ref[k, i] * kkb)
    sa_last = lax.fori_loop(0, tb - 1, lambda t, sa: step(t, sa, True), total(sa0))
    step(tb - 1, sa_last, False)


def _wkv(r, w, k, kk, nb, v, s0, tb):
    t, n_k, l = r.shape
    halves = HEAD // n_k
    assert t % tb == 0 and l % LANES == 0 and v.shape[2] * halves == l and (halves == 1 or l == LANES)
    vec = pl.BlockSpec((tb, n_k, LANES), lambda g, i: (i, 0, g))
    val = pl.BlockSpec((tb, HEAD, LANES // halves), lambda g, i: (i, 0, g))
    slabs = HEAD // SUBLANES
    st = pl.BlockSpec((n_k, slabs, SUBLANES, LANES), lambda g, i: (0, 0, 0, g))
    s0 = s0.reshape(n_k, slabs, SUBLANES, l)
    y, s_new = pl.pallas_call(
        functools.partial(_wkv_kernel, tb=tb, halves=halves),
        grid=(l // LANES, t // tb),
        in_specs=[vec, vec, vec, vec, vec, val, st],
        out_specs=[val, st],
        out_shape=[jax.ShapeDtypeStruct(v.shape, F32), jax.ShapeDtypeStruct(s0.shape, F32)],
        compiler_params=_cparams(2),
        name="wkv",
    )(r, w, k, kk, nb, v, s0)
    return y, s_new.reshape(n_k, HEAD, l)


def _pool_kernel(full_ref, pw_ref, ps_ref, o_ref, *, t, pos0):
    bb = full_ref.shape[0]
    lane = lax.broadcasted_iota(I32, (t, POOL_W), 1)
    window = jnp.where(lane < 64, 2, jnp.where(lane < 128, 4, jnp.where(lane < 192, 8, 16)))
    pos = lax.broadcasted_iota(I32, (t, POOL_W), 0) + (pos0 + 1)
    cnt = jnp.minimum(pos, window).astype(F32)
    diffs = []
    for i in range(bb):
        f = full_ref[i]
        s2 = f + pltpu.roll(f, 1, 0)
        s4 = s2 + pltpu.roll(s2, 2, 0)
        s8 = s4 + pltpu.roll(s4, 4, 0)
        s16 = s8 + pltpu.roll(s8, 8, 0)
        wsum = jnp.where(lane < 64, s2[POOL_PAD:], jnp.where(lane < 128, s4[POOL_PAD:],
                                                            jnp.where(lane < 192, s8[POOL_PAD:], s16[POOL_PAD:])))
        diffs.append(wsum / cnt - f[POOL_PAD:])
    diff = diffs[0] if bb == 1 else jnp.concatenate(diffs, axis=0)
    out = jnp.dot(diff.astype(BF16), pw_ref[...], preferred_element_type=F32) * ps_ref[...]
    for i in range(bb):
        o_ref[i] = out[i * t:(i + 1) * t]


def _pool(full, pw_bd, pscale, bb, pos0):
    b, rows, _ = full.shape
    t = rows - POOL_PAD
    assert b % bb == 0
    return pl.pallas_call(
        functools.partial(_pool_kernel, t=t, pos0=pos0),
        grid=(b // bb,),
        in_specs=[pl.BlockSpec((bb, rows, POOL_W), lambda i: (i, 0, 0)),
                  pl.BlockSpec((POOL_W, POOL_W), lambda i: (0, 0)), pl.BlockSpec((1, POOL_W), lambda i: (0, 0))],
        out_specs=pl.BlockSpec((bb, t, POOL_W), lambda i: (i, 0, 0)),
        out_shape=jax.ShapeDtypeStruct((b, t, POOL_W), F32),
        compiler_params=_cparams(1),
        name="pool",
    )(full, pw_bd, pscale)


def _memattn_kernel(q_ref, k_ref, v_ref, o_ref):
    q = q_ref[0].astype(BF16)
    kf = k_ref[0]
    vf = v_ref[0]
    head_of_lane = lax.broadcasted_iota(I32, kf.shape, 1) // (MEM_W // MEM_HEADS)
    out = None
    for h in range(MEM_HEADS):
        kh = jnp.where(head_of_lane == h, kf, 0.0).astype(BF16)
        vh = jnp.where(head_of_lane == h, vf, 0.0).astype(BF16)
        s = lax.dot_general(q, kh, (((1,), (1,)), ((), ())), preferred_element_type=F32) * (64 ** -0.5)
        e = jnp.exp(s - jnp.max(s, axis=-1, keepdims=True))
        p = e / jnp.sum(e, axis=-1, keepdims=True)
        o = jnp.dot(p.astype(BF16), vh, preferred_element_type=F32)
        out = o if out is None else out + o
    o_ref[0] = out


def _memattn(z3, mk, mv, tt):
    b, t, _ = z3.shape
    assert t % tt == 0
    qcol = (Z_W - MEM_W) // MEM_W
    kv = pl.BlockSpec((1, MEM_TOKENS, MEM_W), lambda i, j: (i, 0, 0))
    return pl.pallas_call(
        _memattn_kernel,
        grid=(b, t // tt),
        in_specs=[pl.BlockSpec((1, tt, MEM_W), lambda i, j: (i, j, qcol)), kv, kv],
        out_specs=pl.BlockSpec((1, tt, MEM_W), lambda i, j: (i, j, 0)),
        out_shape=jax.ShapeDtypeStruct((b, t, MEM_W), F32),
        compiler_params=_cparams(2),
        name="mem_attn",
    )(z3, mk, mv)


def _layer_norm(x, g, b):
    mu = jnp.mean(x, axis=-1, keepdims=True)
    xc = x - mu
    var = jnp.mean(xc * xc, axis=-1, keepdims=True)
    return xc * lax.rsqrt(var + LN_EPS) * g + b


def _post_kernel(y_ref, bonus_ref, g_ref, op_ref, om_ref, x_ref, wo_ref, lxw_ref, lxb_ref, l1g_ref, l1b_ref, bd_ref,
                 h_ref, *, alpha):
    y = y_ref[...]
    bd = bd_ref[...]
    mu = _segsum(y, bd) * (1.0 / HEAD)
    yc = y - mu
    var = _segsum(yc * yc, bd) * (1.0 / HEAD)
    yn = yc * lax.rsqrt(var + GN_EPS) * lxw_ref[...] + lxb_ref[...]
    o_rwkv = (yn + bonus_ref[...]) * g_ref[...]
    mixed = (jnp.dot(o_rwkv.astype(BF16), wo_ref[0:RWKV_W, :], preferred_element_type=F32)
             + jnp.dot(op_ref[...].astype(BF16), wo_ref[RWKV_W:RWKV_W + POOL_W, :], preferred_element_type=F32)
             + jnp.dot(om_ref[...].astype(BF16), wo_ref[RWKV_W + POOL_W:, :], preferred_element_type=F32))
    h_ref[...] = _layer_norm(alpha * x_ref[...] + mixed, l1g_ref[...], l1b_ref[...])


def _post(y, bonus, g, o_pool, o_mem, x, wo, lxw, lxb, l1g, l1b, bd, tt, alpha):
    n = y.shape[0]
    assert n % tt == 0
    tok = lambda w: pl.BlockSpec((tt, w), lambda i: (i, 0))
    row = lambda w: pl.BlockSpec((1, w), lambda i: (0, 0))
    full = lambda a: pl.BlockSpec(a.shape, lambda i: (0, 0))
    return pl.pallas_call(
        functools.partial(_post_kernel, alpha=alpha),
        grid=(n // tt,),
        in_specs=[tok(RWKV_W), tok(RWKV_W), tok(RWKV_W), tok(POOL_W), tok(MEM_W), tok(D_MODEL), full(wo),
                  row(RWKV_W), row(RWKV_W), row(D_MODEL), row(D_MODEL), full(bd)],
        out_specs=tok(D_MODEL),
        out_shape=jax.ShapeDtypeStruct((n, D_MODEL), F32),
        compiler_params=_cparams(1),
        name="post",
    )(y, bonus, g, o_pool, o_mem, x, wo, lxw, lxb, l1g, l1b, bd)


def _router_kernel(h_ref, rwt_ref, bias_ref, tri_ref, idx_o, gate_o, rank_o, cnt_o, carry_ref):
    @pl.when(pl.program_id(0) == 0)
    def _():
        carry_ref[...] = jnp.zeros_like(carry_ref)

    neg = -jnp.inf
    logits = lax.dot_general(rwt_ref[...], h_ref[...], (((1,), (1,)), ((), ())),
                             precision=HIGHEST, preferred_element_type=F32)
    scores = _sigmoid(logits)
    sel = scores + bias_ref[...]
    tt = sel.shape[1]
    gio = lax.broadcasted_iota(I32, (GROUP_SIZE, tt), 0).astype(F32)
    blocks, gscore = [], []
    for g in range(N_GROUPS):
        blk = sel[g * GROUP_SIZE:(g + 1) * GROUP_SIZE, :]
        m1 = jnp.max(blk, axis=0, keepdims=True)
        first = jnp.min(jnp.where(blk == m1, gio, float(GROUP_SIZE)), axis=0, keepdims=True)
        m2 = jnp.max(jnp.where(gio == first, neg, blk), axis=0, keepdims=True)
        blocks.append(blk)
        gscore.append(m1 + m2)
    masked = []
    for g in range(N_GROUPS):
        beaten_by = jnp.zeros((1, tt), F32)
        for g2 in range(N_GROUPS):
            if g2 != g:
                wins = (gscore[g2] >= gscore[g]) if g2 < g else (gscore[g2] > gscore[g])
                beaten_by = beaten_by + jnp.where(wins, 1.0, 0.0)
        masked.append(jnp.where(beaten_by < TOPK_GROUPS, blocks[g], neg))
    msel = jnp.concatenate(masked, axis=0)
    eio = lax.broadcasted_iota(I32, msel.shape, 0).astype(F32)
    chosen = jnp.zeros(msel.shape, F32)
    idxs, scs = [], []
    for _ in range(TOP_K):
        m = jnp.max(msel, axis=0, keepdims=True)
        first = jnp.min(jnp.where(msel == m, eio, float(N_EXPERTS)), axis=0, keepdims=True)
        hit = eio == first
        scs.append(jnp.sum(jnp.where(hit, scores, 0.0), axis=0, keepdims=True))
        msel = jnp.where(hit, neg, msel)
        chosen = jnp.where(hit, 1.0, chosen)
        idxs.append(first)
    total = scs[0]
    for s in scs[1:]:
        total = total + s
    before = jnp.dot(chosen.astype(BF16), tri_ref[...], preferred_element_type=F32) + carry_ref[...]
    ranks = [jnp.sum(jnp.where(eio == i, before, 0.0), axis=0, keepdims=True) for i in idxs]
    carry_ref[...] = carry_ref[...] + jnp.sum(chosen, axis=1, keepdims=True)
    idx_o[...] = jnp.concatenate(idxs, axis=0).astype(I32)
    gate_o[...] = jnp.concatenate([s / total * ROUTED_SCALE for s in scs], axis=0)
    rank_o[...] = jnp.concatenate(ranks, axis=0).astype(I32)
    cnt_o[...] = jnp.broadcast_to(carry_ref[...], cnt_o.shape)


def _router(h, rwt, bias, tt):
    n = h.shape[0]
    assert n % tt == 0
    tri = (lax.broadcasted_iota(I32, (tt, tt), 0) < lax.broadcasted_iota(I32, (tt, tt), 1)).astype(BF16)
    tokT = pl.BlockSpec((TOP_K, tt), lambda i: (0, i))
    return pl.pallas_call(
        _router_kernel,
        grid=(n // tt,),
        in_specs=[pl.BlockSpec((tt, D_MODEL), lambda i: (i, 0)), pl.BlockSpec((N_EXPERTS, D_MODEL), lambda i: (0, 0)),
                  pl.BlockSpec((N_EXPERTS, 1), lambda i: (0, 0)), pl.BlockSpec((tt, tt), lambda i: (0, 0))],
        out_specs=[tokT, tokT, tokT, pl.BlockSpec((N_EXPERTS, LANES), lambda i: (0, 0))],
        out_shape=[jax.ShapeDtypeStruct((TOP_K, n), I32), jax.ShapeDtypeStruct((TOP_K, n), F32),
                   jax.ShapeDtypeStruct((TOP_K, n), I32), jax.ShapeDtypeStruct((N_EXPERTS, LANES), F32)],
        scratch_shapes=[pltpu.VMEM((N_EXPERTS, 1), F32)],
        compiler_params=_cparams(1),
        name="router",
    )(h, rwt, bias, tri)


ROW_SUB = D_MODEL // LANES


def _row_copy(src_ref, src_row, dst_ref, dst_row, sem):
    src = src_ref.at[pl.ds(pl.multiple_of(src_row * ROW_SUB, ROW_SUB), ROW_SUB), :]
    dst = dst_ref.at[pl.ds(pl.multiple_of(dst_row * ROW_SUB, ROW_SUB), ROW_SUB), :]
    return pltpu.make_async_copy(src, dst, sem)


def _store_row_tiles(ref, x):
    rows = x.shape[0]
    for c in range(ROW_SUB):
        ref[pl.ds(c, rows, stride=ROW_SUB), :] = x[:, c * LANES:(c + 1) * LANES]


def _load_row_tiles(ref):
    rows = ref.shape[0] // ROW_SUB
    return jnp.concatenate([ref[pl.ds(c, rows, stride=ROW_SUB), :] for c in range(ROW_SUB)], axis=-1)


TOKENS_PER_IDX_ROW = LANES // TOP_K


def _dest_of(dest_smem, r, j):
    return dest_smem[r // TOKENS_PER_IDX_ROW, (r % TOKENS_PER_IDX_ROW) * TOP_K + j]


def _dispatch_kernel(zpos_ref, dest_hbm, h_ref, xs_hbm, dest_smem, zero_buf, rows_buf, sem_idx, sem_zero, sem_rows,
                     *, td):
    i = pl.program_id(0)

    def zero_copy(e):
        start = pl.multiple_of(zpos_ref[e] * ROW_SUB, CHUNK * ROW_SUB)
        return pltpu.make_async_copy(zero_buf, xs_hbm.at[pl.ds(start, CHUNK * ROW_SUB), :], sem_zero)

    @pl.when(i == 0)
    def _():
        zero_buf[...] = jnp.zeros_like(zero_buf)

        def start(e, c):
            @pl.when(zpos_ref[e] >= 0)
            def _():
                zero_copy(e).start()
            return c

        def wait(e, c):
            @pl.when(zpos_ref[e] >= 0)
            def _():
                zero_copy(e).wait()
            return c

        lax.fori_loop(0, N_EXPERTS, start, 0)
        lax.fori_loop(0, N_EXPERTS, wait, 0)

    idx_copy = pltpu.make_async_copy(dest_hbm.at[i], dest_smem, sem_idx)
    idx_copy.start()
    _store_row_tiles(rows_buf, h_ref[...])
    idx_copy.wait()

    def start_rows(r, c):
        for j in range(TOP_K):
            _row_copy(rows_buf, r, xs_hbm, _dest_of(dest_smem, r, j), sem_rows).start(priority=j % 2)
        return c

    def wait_rows(r, c):
        for j in range(TOP_K):
            _row_copy(rows_buf, r, xs_hbm, _dest_of(dest_smem, r, j), sem_rows).wait()
        return c

    lax.fori_loop(0, td, start_rows, 0)
    lax.fori_loop(0, td, wait_rows, 0)


def _dispatch(zpos, dest_tiles, h, n_rows, td):
    n = h.shape[0]
    assert n % td == 0
    return pl.pallas_call(
        functools.partial(_dispatch_kernel, td=td),
        grid_spec=pltpu.PrefetchScalarGridSpec(
            num_scalar_prefetch=1,
            grid=(n // td,),
            in_specs=[pl.BlockSpec(memory_space=pl.ANY), pl.BlockSpec((td, D_MODEL), lambda i, z: (i, 0))],
            out_specs=pl.BlockSpec(memory_space=pl.ANY),
            scratch_shapes=[pltpu.SMEM((td // TOKENS_PER_IDX_ROW, LANES), I32), pltpu.VMEM((CHUNK * ROW_SUB, LANES), F32),
                            pltpu.VMEM((td * ROW_SUB, LANES), F32),
                            pltpu.SemaphoreType.DMA, pltpu.SemaphoreType.DMA, pltpu.SemaphoreType.DMA],
        ),
        out_shape=jax.ShapeDtypeStruct((n_rows * ROW_SUB, LANES), F32),
        compiler_params=_cparams(1),
        name="dispatch",
    )(zpos, dest_tiles, h)


def _ffn_kernel(be_ref, nu_ref, x_ref, wg_ref, wu_ref, wd_ref, o_ref):
    @pl.when(pl.program_id(0) < nu_ref[0])
    def _():
        x = _load_row_tiles(x_ref).astype(BF16)
        gate = jnp.dot(x, wg_ref[0].astype(BF16), preferred_element_type=F32)
        up = jnp.dot(x, wu_ref[0].astype(BF16), preferred_element_type=F32)
        act = (gate * _sigmoid(gate)) * up
        _store_row_tiles(o_ref, jnp.dot(act.astype(BF16), wd_ref[0].astype(BF16), preferred_element_type=F32))


def _ffn(blk_exp, n_used, xs, wg, wu, wd):
    n_chunks = xs.shape[0] // (CHUNK * ROW_SUB)
    rows = pl.BlockSpec((CHUNK * ROW_SUB, LANES), lambda c, be, nu: (jnp.minimum(c, nu[0] - 1), 0))
    return pl.pallas_call(
        _ffn_kernel,
        grid_spec=pltpu.PrefetchScalarGridSpec(
            num_scalar_prefetch=2,
            grid=(n_chunks,),
            in_specs=[rows,
                      pl.BlockSpec((1, D_MODEL, EXPERT_FF), lambda c, be, nu: (be[c], 0, 0)),
                      pl.BlockSpec((1, D_MODEL, EXPERT_FF), lambda c, be, nu: (be[c], 0, 0)),
                      pl.BlockSpec((1, EXPERT_FF, D_MODEL), lambda c, be, nu: (be[c], 0, 0))],
            out_specs=rows,
        ),
        out_shape=jax.ShapeDtypeStruct(xs.shape, F32),
        compiler_params=_cparams(1),
        name="expert_ffn",
    )(blk_exp, n_used, xs, wg, wu, wd)


def _combine_kernel(dest_hbm, gate_ref, h_ref, ys_hbm, sg_ref, su_ref, sd_ref, l2g_ref, l2b_ref, o_ref,
                    dest_smem, buf, sem_idx, sem_rows, *, tc, alpha):
    i = pl.program_id(0)
    idx_copy = pltpu.make_async_copy(dest_hbm.at[i], dest_smem, sem_idx)
    idx_copy.start()
    idx_copy.wait()

    def start_rows(r, c):
        for j in range(TOP_K):
            _row_copy(ys_hbm, _dest_of(dest_smem, r, j), buf.at[j], r, sem_rows).start(priority=j % 2)
        return c

    def wait_rows(r, c):
        for j in range(TOP_K):
            _row_copy(ys_hbm, _dest_of(dest_smem, r, j), buf.at[j], r, sem_rows).wait()
        return c

    lax.fori_loop(0, tc, start_rows, 0)
    h = h_ref[...]
    hb = h.astype(BF16)
    sgate = jnp.dot(hb, sg_ref[...], preferred_element_type=F32)
    sup = jnp.dot(hb, su_ref[...], preferred_element_type=F32)
    shared = jnp.dot(((sgate * _sigmoid(sgate)) * sup).astype(BF16), sd_ref[...], preferred_element_type=F32)
    lax.fori_loop(0, tc, wait_rows, 0)
    gate = gate_ref[...]
    routed = _load_row_tiles(buf.at[0]) * gate[:, 0:1]
    for j in range(1, TOP_K):
        routed = routed + _load_row_tiles(buf.at[j]) * gate[:, j:j + 1]
    o_ref[...] = _layer_norm(alpha * h + (routed + shared), l2g_ref[...], l2b_ref[...])


def _combine(dest_tiles, gate, h, ys, sg, su, sd, l2g, l2b, tc, alpha):
    n = h.shape[0]
    assert n % tc == 0
    tok = lambda w: pl.BlockSpec((tc, w), lambda i: (i, 0))
    full = lambda a: pl.BlockSpec(a.shape, lambda i: (0, 0))
    anyspec = pl.BlockSpec(memory_space=pl.ANY)
    return pl.pallas_call(
        functools.partial(_combine_kernel, tc=tc, alpha=alpha),
        grid=(n // tc,),
        in_specs=[anyspec, tok(TOP_K), tok(D_MODEL), anyspec, full(sg), full(su), full(sd), full(l2g), full(l2b)],
        out_specs=tok(D_MODEL),
        out_shape=jax.ShapeDtypeStruct((n, D_MODEL), F32),
        scratch_shapes=[pltpu.SMEM((tc // TOKENS_PER_IDX_ROW, LANES), I32), pltpu.VMEM((TOP_K, tc * ROW_SUB, LANES), F32),
                        pltpu.SemaphoreType.DMA, pltpu.SemaphoreType.DMA],
        compiler_params=_cparams(1),
        name="combine",
    )(dest_tiles, gate, h, ys, sg, su, sd, l2g, l2b)


def _pad_cols(a, width):
    return jnp.pad(a, ((0, 0), (0, width - a.shape[1])))


def _pack_shift_cols(a):
    c = 3 * RWKV_W
    return jnp.concatenate([a[:, :c], _pad_cols(a[:, c:c + DECAY_LORA], LORA_PAD),
                            _pad_cols(a[:, c + DECAY_LORA:c + DECAY_LORA + AAA_LORA], LORA_PAD),
                            _pad_cols(a[:, c + DECAY_LORA + AAA_LORA:], GATE_PAD)], axis=1)


def _keys_to_lanes(a, b, t, halves):
    a = a.reshape(b, t, N_HEADS, halves, HEAD // halves).transpose(1, 4, 3, 0, 2)
    return a.reshape(t, HEAD // halves, halves * b * N_HEADS)


def _values_to_lanes(a, b, t):
    return a.reshape(b, t, N_HEADS, HEAD).transpose(1, 3, 0, 2).reshape(t, HEAD, b * N_HEADS)


def _values_from_lanes(y, b, t):
    return y.reshape(t, HEAD, b, N_HEADS).transpose(2, 0, 3, 1).reshape(b * t, RWKV_W)


def _state_to_lanes(s, b, halves):
    s = s.reshape(b, N_HEADS, HEAD, halves, HEAD // halves).transpose(4, 2, 3, 0, 1)
    return s.reshape(HEAD // halves, HEAD, halves * b * N_HEADS)


def _state_from_lanes(s, b, halves):
    s = s.reshape(HEAD // halves, HEAD, halves, b, N_HEADS).transpose(3, 4, 1, 2, 0)
    return s.reshape(b, N_HEADS, HEAD, HEAD)


def _mixer(z, zprev, x2, b, t, wkv0, pool_prev, mk, mv, pos0, wts, tiles):
    n = b * t
    tt, tb, pool_bb, att_tt = tiles
    z3 = z[:n].reshape(b, t, Z_W)
    r, w, kp, v, kk, nb, g, bonus = _rwkv_prep(z, zprev, b, t, tt, wts["prep"])

    halves = 1 if (b * N_HEADS) % LANES == 0 else LANES // (b * N_HEADS)
    lay = lambda a: _keys_to_lanes(a, b, t, halves)
    y_l, s_l = _wkv(lay(r), lay(w), lay(kp), lay(kk), lay(nb), _values_to_lanes(v, b, t),
                    _state_to_lanes(wkv0, b, halves), tb)
    y = _values_from_lanes(y_l, b, t)
    wkv_new = _state_from_lanes(s_l, b, halves)

    zp = z3[:, :, ZS_W:ZS_W + POOL_W]
    full = jnp.concatenate([jnp.zeros((b, POOL_PAD - POOL_BUF, POOL_W), F32), pool_prev, zp], axis=1)
    o_pool = _pool(full, wts["pool_w"], wts["pool_scale"], pool_bb, pos0).reshape(n, POOL_W)
    pool_new = full[:, -POOL_BUF:]

    o_mem = _memattn(z3, mk, mv, att_tt).reshape(n, MEM_W)
    h = _post(y, bonus, g, o_pool, o_mem, x2, *wts["post"], tt, wts["alpha"])
    return h, wkv_new, pool_new


def kernel(x_prompt, x_sample, mem_prompt, state_wkv, state_shift, state_pool, cache_mem_k, cache_mem_v, w_in, mu_shift, w0, w_up_decay, a0, w_up_aaa, w_up_gate, k_k, k_a, r_k, ln_x_w, ln_x_b, pool_w, pool_scale, mem_wk, mem_wv, w_out, ln1_g, ln1_b, router_w, router_b, exp_gate, exp_up, exp_down, sh_gate, sh_up, sh_down, ln2_g, ln2_b):
    depth = w_in.shape[0]
    assert depth == 1
    l = 0
    alpha = (2.0 * depth) ** 0.25
    bp, tp, d = x_prompt.shape
    bs, ts, _ = x_sample.shape
    n_p, n_s = bp * tp, bs * ts
    n = n_p + n_s

    w_in_p = jnp.concatenate([_pack_shift_cols(w_in[l][:, :1824]), w_in[l][:, 1824:]], axis=1).astype(BF16)
    row = lambda a: a.reshape(1, -1)
    pad_rows = lambda a, rows: jnp.pad(a, ((0, rows - a.shape[0]), (0, 0)))
    bd = jnp.kron(jnp.eye(N_HEADS, dtype=F32), jnp.ones((HEAD, HEAD), F32)).astype(BF16)
    pw = pool_w[l]
    pw_bd = jnp.zeros((POOL_W, POOL_W), F32)
    for gi in range(4):
        pw_bd = pw_bd.at[gi * 64:(gi + 1) * 64, gi * 64:(gi + 1) * 64].set(pw[gi])
    wts = {
        "prep": (_pack_shift_cols(row(mu_shift[l])), row(w0[l]), row(a0[l]), pad_rows(w_up_decay[l], LORA_PAD),
                 pad_rows(w_up_aaa[l], LORA_PAD), pad_rows(w_up_gate[l], GATE_PAD), row(k_k[l]), row(k_a[l]),
                 row(r_k[l]), bd),
        "pool_w": pw_bd.astype(BF16), "pool_scale": row(pool_scale[l]),
        "post": (w_out[l].astype(BF16), row(ln_x_w[l]), row(ln_x_b[l]), row(ln1_g[l]), row(ln1_b[l]), bd),
        "alpha": alpha,
    }

    xp2 = x_prompt.reshape(n_p, d)
    xs2 = x_sample.reshape(n_s, d)
    z_p = _matmul(xp2, w_in_p, _tile(n_p, 512))
    z_s = _matmul(jnp.concatenate([xs2, state_shift[l]], axis=0), w_in_p, _tile(n_s + bs, 512))
    mkv = _matmul(mem_prompt.reshape(bp * MEM_TOKENS, d),
                  jnp.concatenate([mem_wk[l], mem_wv[l]], axis=1).astype(BF16), _tile(bp * MEM_TOKENS, 512))
    mk_p = mkv[:, :MEM_W].reshape(bp, MEM_TOKENS, MEM_W)
    mv_p = mkv[:, MEM_W:].reshape(bp, MEM_TOKENS, MEM_W)

    h_p, wkv_p, pool_p = _mixer(z_p, jnp.zeros((bp, ZS_W), F32), xp2, bp, tp,
                                jnp.zeros((bp, N_HEADS, HEAD, HEAD), F32), jnp.zeros((bp, POOL_BUF, POOL_W), F32),
                                mk_p, mv_p, 0, wts,
                                (_tile(n_p, 256), _tile(tp, 32, 1), 1, _tile(tp, 256)))
    h_s, wkv_s, pool_s = _mixer(z_s, z_s[n_s:, :ZS_W], xs2, bs, ts, state_wkv[l], state_pool[l],
                                cache_mem_k[l].reshape(bs, MEM_TOKENS, MEM_W),
                                cache_mem_v[l].reshape(bs, MEM_TOKENS, MEM_W), PAST_LEN, wts,
                                (_tile(n_s, 256), _tile(ts, 32, 1), _tile(bs, 16, 1), _tile(ts, 256)))

    h = jnp.concatenate([h_p, h_s], axis=0)
    idx, gate, rank, cnt = _router(h, router_w[l].T, router_b[l].reshape(N_EXPERTS, 1), _tile(n, 256, LANES))
    counts = cnt[:, 0].astype(I32)
    padded = (counts + CHUNK - 1) // CHUNK * CHUNK
    pad_end = jnp.cumsum(padded)
    pad_start = pad_end - padded
    n_chunks = (n * TOP_K + N_EXPERTS * (CHUNK - 1) + CHUNK - 1) // CHUNK
    experts = jnp.arange(N_EXPERTS, dtype=I32)
    dest = rank + jnp.sum(jnp.where(idx[..., None] == experts, pad_start, 0), axis=-1)
    td = _tile(n, 256, LANES)
    dest_tiles = dest.T.reshape(n // td, td // TOKENS_PER_IDX_ROW, LANES)
    zpos = jnp.where(padded > 0, pad_end - CHUNK, -1).astype(I32)
    chunk_start = jnp.arange(n_chunks, dtype=I32) * CHUNK
    blk_exp = jnp.minimum(jnp.sum((pad_end[None, :] <= chunk_start[:, None]).astype(I32), axis=1), N_EXPERTS - 1)
    n_used = (pad_end[-1:] // CHUNK).astype(I32)
    xs_sorted = _dispatch(zpos, dest_tiles, h, n_chunks * CHUNK, td)
    ys_sorted = _ffn(blk_exp, n_used, xs_sorted, exp_gate[l], exp_up[l], exp_down[l])
    y = _combine(dest_tiles, gate.T, h, ys_sorted, sh_gate[l].astype(BF16), sh_up[l].astype(BF16),
                 sh_down[l].astype(BF16), row(ln2_g[l]), row(ln2_b[l]), td, alpha)

    return (y[:n_p].reshape(bp, tp, d), y[n_p:].reshape(bs, ts, d),
            wkv_p[None], x_prompt[:, -1][None], pool_p[None],
            mk_p.reshape(bp, MEM_TOKENS, MEM_HEADS, HEAD)[None], mv_p.reshape(bp, MEM_TOKENS, MEM_HEADS, HEAD)[None],
            wkv_s[None], x_sample[:, -1][None], pool_s[None])
```

```python
import functools

import jax
import jax.numpy as jnp
from jax import lax
from jax.experimental import pallas as pl
from jax.experimental.pallas import tpu as pltpu

F32 = jnp.float32
BF16 = jnp.bfloat16
I32 = jnp.int32
HIGHEST = lax.Precision.HIGHEST

D_MODEL = 1024
HEAD = 64
N_HEADS = 8
RWKV_W = 512
POOL_W = 256
MEM_W = 256
MEM_HEADS = 4
MEM_TOKENS = 256
POOL_BUF = 15
POOL_PAD = 16
DECAY_LORA = 64
AAA_LORA = 64
GATE_LORA = 160
LORA_PAD = 128
GATE_PAD = 256
ZS_W = 3 * RWKV_W + 2 * LORA_PAD + GATE_PAD
Z_W = ZS_W + POOL_W + MEM_W
N_EXPERTS = 256
TOP_K = 8
N_GROUPS = 8
GROUP_SIZE = N_EXPERTS // N_GROUPS
TOPK_GROUPS = 4
EXPERT_FF = 256
ROUTED_SCALE = 2.5
CHUNK = 256
PAST_LEN = 16384
LN_EPS = 1e-5
GN_EPS = 64e-5
LANES = 128
SUBLANES = 8
VMEM_LIMIT = 48 * 1024 * 1024


def _cparams(n_axes):
    return pltpu.CompilerParams(dimension_semantics=("arbitrary",) * n_axes, vmem_limit_bytes=VMEM_LIMIT)


def _tile(n, preferred, multiple=8):
    best = None
    for c in range(multiple, min(n, preferred) + 1, multiple):
        if n % c == 0:
            best = c
    assert best is not None, (n, preferred, multiple)
    return best


def _sigmoid(x):
    return 1.0 / (1.0 + jnp.exp(-x))


def _matmul_kernel(x_ref, w_ref, o_ref):
    o_ref[...] = jnp.dot(x_ref[...].astype(BF16), w_ref[...], preferred_element_type=F32)


def _matmul(x, w, tm):
    m, k = x.shape
    n = w.shape[1]
    assert m % tm == 0
    return pl.pallas_call(
        _matmul_kernel,
        grid=(m // tm,),
        in_specs=[pl.BlockSpec((tm, k), lambda i: (i, 0)), pl.BlockSpec((k, n), lambda i: (0, 0))],
        out_specs=pl.BlockSpec((tm, n), lambda i: (i, 0)),
        out_shape=jax.ShapeDtypeStruct((m, n), F32),
        compiler_params=_cparams(1),
        name="matmul",
    )(x, w)


def _segsum(x, bd):
    hi = x.astype(BF16)
    lo = (x - hi.astype(F32)).astype(BF16)
    return jnp.dot(hi, bd, preferred_element_type=F32) + jnp.dot(lo, bd, preferred_element_type=F32)


def _prep_kernel(zs_ref, zp_ref, mu_ref, w0_ref, a0_ref, wd_ref, wa_ref, wg_ref, kk_ref, ka_ref, rk_ref, bd_ref,
                 r_o, w_o, k_o, v_o, kk_o, nb_o, g_o, bonus_o, carry_ref, *, tiles_per_seq, seq_len):
    zs = zs_ref[...]
    rolled = pltpu.roll(zs, 1, 0)
    row = lax.broadcasted_iota(I32, zs.shape, 0)
    if tiles_per_seq:
        i = pl.program_id(0)

        @pl.when(i == 0)
        def _():
            carry_ref[...] = jnp.zeros_like(carry_ref)

        prev = jnp.where(i % tiles_per_seq == 0, zp_ref[0], carry_ref[...])
        shifted = jnp.where(row == 0, prev, rolled)
        carry_ref[...] = zs[zs.shape[0] - 1:, :]
    else:
        shifted = jnp.where((row & (seq_len - 1)) == 0, zp_ref[...], rolled)
    zm = zs + mu_ref[...] * (shifted - zs)
    r = zm[:, 0:RWKV_W]
    k = zm[:, RWKV_W:2 * RWKV_W]
    v = zm[:, 2 * RWKV_W:3 * RWKV_W]
    c0 = 3 * RWKV_W
    xw = zm[:, c0:c0 + LORA_PAD]
    xa = zm[:, c0 + LORA_PAD:c0 + 2 * LORA_PAD]
    xg = zm[:, c0 + 2 * LORA_PAD:c0 + 2 * LORA_PAD + GATE_PAD]
    bd = bd_ref[...]
    u = -(w0_ref[...] + jnp.dot(jnp.tanh(xw), wd_ref[...], precision=HIGHEST, preferred_element_type=F32))
    softplus = jnp.maximum(u, 0.0) + jnp.log(1.0 + jnp.exp(-jnp.abs(u)))
    decay = jnp.exp(-jnp.exp(-softplus - 0.5))
    a = _sigmoid(a0_ref[...] + jnp.dot(xa, wa_ref[...], precision=HIGHEST, preferred_element_type=F32))
    g = jnp.dot(_sigmoid(xg), wg_ref[...], precision=HIGHEST, preferred_element_type=F32)
    kk = k * kk_ref[...]
    kk = kk / jnp.maximum(jnp.sqrt(_segsum(kk * kk, bd)), 1e-12)
    kp = k * (1.0 + (a - 1.0) * ka_ref[...])
    r_o[...] = r
    w_o[...] = decay
    k_o[...] = kp
    v_o[...] = v
    kk_o[...] = kk
    nb_o[...] = -(kk * a)
    g_o[...] = g
    bonus_o[...] = _segsum(r * kp * rk_ref[...], bd) * v


def _rwkv_prep(z, zprev, b, t, tt, params):
    n_tok = b * t
    assert n_tok % tt == 0
    row = lambda w: pl.BlockSpec((1, w), lambda i: (0, 0))
    full = lambda a: pl.BlockSpec(a.shape, lambda i: (0, 0))
    tok = pl.BlockSpec((tt, RWKV_W), lambda i: (i, 0))
    mu, w0, a0, wd, wa, wg, k_k, k_a, r_k, bd = params
    if t % tt == 0:
        tiles_per_seq = t // tt
        zp = zprev.reshape(b, 1, ZS_W)
        zp_spec = pl.BlockSpec((1, 1, ZS_W), lambda i: (i // tiles_per_seq, 0, 0))
    else:
        assert tt % t == 0 and t & (t - 1) == 0
        tiles_per_seq = 0
        zp = jnp.repeat(zprev, t, axis=0)
        zp_spec = pl.BlockSpec((tt, ZS_W), lambda i: (i, 0))
    return pl.pallas_call(
        functools.partial(_prep_kernel, tiles_per_seq=tiles_per_seq, seq_len=t),
        grid=(n_tok // tt,),
        in_specs=[pl.BlockSpec((tt, ZS_W), lambda i: (i, 0)), zp_spec,
                  row(ZS_W), row(RWKV_W), row(RWKV_W), full(wd), full(wa), full(wg),
                  row(RWKV_W), row(RWKV_W), row(RWKV_W), full(bd)],
        out_specs=[tok] * 8,
        out_shape=[jax.ShapeDtypeStruct((n_tok, RWKV_W), F32)] * 8,
        scratch_shapes=[pltpu.VMEM((1, ZS_W), F32)],
        compiler_params=_cparams(1),
        name="rwkv_prep",
    )(z, zp, mu, w0, a0, wd, wa, wg, k_k, k_a, r_k, bd)


def _wkv_kernel(r_ref, w_ref, k_ref, kk_ref, nb_ref, v_ref, s0_ref, y_ref, s_ref, *scratch, tb, dup):
    slabs = s_ref.shape[1]
    key_unroll = 8

    @pl.when(pl.program_id(1) == 0)
    def _():
        s_ref[...] = s0_ref[...]

    if dup:
        for src, dst in zip((r_ref, w_ref, k_ref, kk_ref, nb_ref), scratch):
            x = src[...]
            dst[...] = jnp.concatenate([x, x], axis=-1)
        r_ref, w_ref, k_ref, kk_ref, nb_ref = scratch

    def rows(ref, t, k):
        return jnp.broadcast_to(ref[t, pl.ds(k, 1), :], (SUBLANES, LANES))

    def step(t, sa, with_next):
        vt = [v_ref[t, i * SUBLANES:(i + 1) * SUBLANES, :] for i in range(slabs)]
        zero = jnp.zeros((SUBLANES, LANES), F32)

        def key(k, acc):
            y_acc, next_acc = list(acc[0]), list(acc[1])
            wb, nbb, kb, rb = rows(w_ref, t, k), rows(nb_ref, t, k), rows(k_ref, t, k), rows(r_ref, t, k)
            kkb = rows(kk_ref, t + 1, k) if with_next else None
            for i in range(slabs):
                sn = s_ref[k, i] * wb + sa[i] * nbb + vt[i] * kb
                s_ref[k, i] = sn
                y_acc[i] = y_acc[i] + sn * rb
                if with_next:
                    next_acc[i] = next_acc[i] + sn * kkb
            return tuple(y_acc), tuple(next_acc)

        y_acc, next_acc = lax.fori_loop(0, HEAD, key, ((zero,) * slabs, (zero,) * slabs), unroll=key_unroll)
        for i, y in enumerate(y_acc):
            y_ref[t, i * SUBLANES:(i + 1) * SUBLANES, :] = y
        return next_acc

    def first(k, acc):
        kkb = rows(kk_ref, 0, k)
        return tuple(a + s_ref[k, i] * kkb for i, a in enumerate(acc))

    sa0 = lax.fori_loop(0, HEAD, first, (jnp.zeros((SUBLANES, LANES), F32),) * slabs, unroll=key_unroll)
    sa_last = lax.fori_loop(0, tb - 1, lambda t, sa: step(t, sa, True), sa0)
    step(tb - 1, sa_last, False)


def _wkv(r, w, k, kk, nb, v, s0, tb):
    t, _, lr = r.shape
    vr, l = v.shape[1], v.shape[2]
    dup = lr != l
    assert t % tb == 0 and l % LANES == 0 and vr % SUBLANES == 0 and (not dup or (l == LANES and 2 * lr == l))
    vec = pl.BlockSpec((tb, HEAD, lr if dup else LANES), lambda g, i: (i, 0, g))
    val = pl.BlockSpec((tb, vr, LANES), lambda g, i: (i, 0, g))
    slabs = vr // SUBLANES
    st = pl.BlockSpec((HEAD, slabs, SUBLANES, LANES), lambda g, i: (0, 0, 0, g))
    s0 = s0.reshape(HEAD, slabs, SUBLANES, l)
    y, s_new = pl.pallas_call(
        functools.partial(_wkv_kernel, tb=tb, dup=dup),
        grid=(l // LANES, t // tb),
        in_specs=[vec, vec, vec, vec, vec, val, st],
        out_specs=[val, st],
        out_shape=[jax.ShapeDtypeStruct(v.shape, F32), jax.ShapeDtypeStruct(s0.shape, F32)],
        scratch_shapes=[pltpu.VMEM((tb, HEAD, LANES), F32)] * 5 if dup else [],
        compiler_params=_cparams(2),
        name="wkv",
    )(r, w, k, kk, nb, v, s0)
    return y, s_new.reshape(HEAD, vr, l)


def _pool_kernel(full_ref, pw_ref, ps_ref, o_ref, *, t, pos0):
    bb = full_ref.shape[0]
    lane = lax.broadcasted_iota(I32, (t, POOL_W), 1)
    window = jnp.where(lane < 64, 2, jnp.where(lane < 128, 4, jnp.where(lane < 192, 8, 16)))
    pos = lax.broadcasted_iota(I32, (t, POOL_W), 0) + (pos0 + 1)
    cnt = jnp.minimum(pos, window).astype(F32)
    diffs = []
    for i in range(bb):
        f = full_ref[i]
        s2 = f + pltpu.roll(f, 1, 0)
        s4 = s2 + pltpu.roll(s2, 2, 0)
        s8 = s4 + pltpu.roll(s4, 4, 0)
        s16 = s8 + pltpu.roll(s8, 8, 0)
        wsum = jnp.where(lane < 64, s2[POOL_PAD:], jnp.where(lane < 128, s4[POOL_PAD:],
                                                            jnp.where(lane < 192, s8[POOL_PAD:], s16[POOL_PAD:])))
        diffs.append(wsum / cnt - f[POOL_PAD:])
    diff = diffs[0] if bb == 1 else jnp.concatenate(diffs, axis=0)
    out = jnp.dot(diff.astype(BF16), pw_ref[...], preferred_element_type=F32) * ps_ref[...]
    for i in range(bb):
        o_ref[i] = out[i * t:(i + 1) * t]


def _pool(full, pw_bd, pscale, bb, pos0):
    b, rows, _ = full.shape
    t = rows - POOL_PAD
    assert b % bb == 0
    return pl.pallas_call(
        functools.partial(_pool_kernel, t=t, pos0=pos0),
        grid=(b // bb,),
        in_specs=[pl.BlockSpec((bb, rows, POOL_W), lambda i: (i, 0, 0)),
                  pl.BlockSpec((POOL_W, POOL_W), lambda i: (0, 0)), pl.BlockSpec((1, POOL_W), lambda i: (0, 0))],
        out_specs=pl.BlockSpec((bb, t, POOL_W), lambda i: (i, 0, 0)),
        out_shape=jax.ShapeDtypeStruct((b, t, POOL_W), F32),
        compiler_params=_cparams(1),
        name="pool",
    )(full, pw_bd, pscale)


def _memattn_kernel(q_ref, k_ref, v_ref, o_ref):
    q = q_ref[0].astype(BF16)
    kf = k_ref[0]
    vf = v_ref[0]
    head_of_lane = lax.broadcasted_iota(I32, kf.shape, 1) // (MEM_W // MEM_HEADS)
    out = None
    for h in range(MEM_HEADS):
        kh = jnp.where(head_of_lane == h, kf, 0.0).astype(BF16)
        vh = jnp.where(head_of_lane == h, vf, 0.0).astype(BF16)
        s = lax.dot_general(q, kh, (((1,), (1,)), ((), ())), preferred_element_type=F32) * (64 ** -0.5)
        e = jnp.exp(s - jnp.max(s, axis=-1, keepdims=True))
        p = e / jnp.sum(e, axis=-1, keepdims=True)
        o = jnp.dot(p.astype(BF16), vh, preferred_element_type=F32)
        out = o if out is None else out + o
    o_ref[0] = out


def _memattn(z3, mk, mv, tt):
    b, t, _ = z3.shape
    assert t % tt == 0
    qcol = (Z_W - MEM_W) // MEM_W
    kv = pl.BlockSpec((1, MEM_TOKENS, MEM_W), lambda i, j: (i, 0, 0))
    return pl.pallas_call(
        _memattn_kernel,
        grid=(b, t // tt),
        in_specs=[pl.BlockSpec((1, tt, MEM_W), lambda i, j: (i, j, qcol)), kv, kv],
        out_specs=pl.BlockSpec((1, tt, MEM_W), lambda i, j: (i, j, 0)),
        out_shape=jax.ShapeDtypeStruct((b, t, MEM_W), F32),
        compiler_params=_cparams(2),
        name="mem_attn",
    )(z3, mk, mv)


def _layer_norm(x, g, b):
    mu = jnp.mean(x, axis=-1, keepdims=True)
    xc = x - mu
    var = jnp.mean(xc * xc, axis=-1, keepdims=True)
    return xc * lax.rsqrt(var + LN_EPS) * g + b


def _post_kernel(y_ref, bonus_ref, g_ref, op_ref, om_ref, x_ref, wo_ref, lxw_ref, lxb_ref, l1g_ref, l1b_ref, bd_ref,
                 h_ref, *, alpha):
    y = y_ref[...]
    bd = bd_ref[...]
    mu = _segsum(y, bd) * (1.0 / HEAD)
    yc = y - mu
    var = _segsum(yc * yc, bd) * (1.0 / HEAD)
    yn = yc * lax.rsqrt(var + GN_EPS) * lxw_ref[...] + lxb_ref[...]
    o_rwkv = (yn + bonus_ref[...]) * g_ref[...]
    mixed = (jnp.dot(o_rwkv.astype(BF16), wo_ref[0:RWKV_W, :], preferred_element_type=F32)
             + jnp.dot(op_ref[...].astype(BF16), wo_ref[RWKV_W:RWKV_W + POOL_W, :], preferred_element_type=F32)
             + jnp.dot(om_ref[...].astype(BF16), wo_ref[RWKV_W + POOL_W:, :], preferred_element_type=F32))
    h_ref[...] = _layer_norm(alpha * x_ref[...] + mixed, l1g_ref[...], l1b_ref[...])


def _post(y, bonus, g, o_pool, o_mem, x, wo, lxw, lxb, l1g, l1b, bd, tt, alpha):
    n = y.shape[0]
    assert n % tt == 0
    tok = lambda w: pl.BlockSpec((tt, w), lambda i: (i, 0))
    row = lambda w: pl.BlockSpec((1, w), lambda i: (0, 0))
    full = lambda a: pl.BlockSpec(a.shape, lambda i: (0, 0))
    return pl.pallas_call(
        functools.partial(_post_kernel, alpha=alpha),
        grid=(n // tt,),
        in_specs=[tok(RWKV_W), tok(RWKV_W), tok(RWKV_W), tok(POOL_W), tok(MEM_W), tok(D_MODEL), full(wo),
                  row(RWKV_W), row(RWKV_W), row(D_MODEL), row(D_MODEL), full(bd)],
        out_specs=tok(D_MODEL),
        out_shape=jax.ShapeDtypeStruct((n, D_MODEL), F32),
        compiler_params=_cparams(1),
        name="post",
    )(y, bonus, g, o_pool, o_mem, x, wo, lxw, lxb, l1g, l1b, bd)


def _router_kernel(h_ref, rwt_ref, bias_ref, tri_ref, idx_o, gate_o, rank_o, cnt_o, carry_ref):
    @pl.when(pl.program_id(0) == 0)
    def _():
        carry_ref[...] = jnp.zeros_like(carry_ref)

    neg = -jnp.inf
    logits = lax.dot_general(rwt_ref[...], h_ref[...], (((1,), (1,)), ((), ())),
                             precision=HIGHEST, preferred_element_type=F32)
    scores = _sigmoid(logits)
    sel = scores + bias_ref[...]
    tt = sel.shape[1]
    gio = lax.broadcasted_iota(I32, (GROUP_SIZE, tt), 0).astype(F32)
    blocks, gscore = [], []
    for g in range(N_GROUPS):
        blk = sel[g * GROUP_SIZE:(g + 1) * GROUP_SIZE, :]
        m1 = jnp.max(blk, axis=0, keepdims=True)
        first = jnp.min(jnp.where(blk == m1, gio, float(GROUP_SIZE)), axis=0, keepdims=True)
        m2 = jnp.max(jnp.where(gio == first, neg, blk), axis=0, keepdims=True)
        blocks.append(blk)
        gscore.append(m1 + m2)
    masked = []
    for g in range(N_GROUPS):
        beaten_by = jnp.zeros((1, tt), F32)
        for g2 in range(N_GROUPS):
            if g2 != g:
                wins = (gscore[g2] >= gscore[g]) if g2 < g else (gscore[g2] > gscore[g])
                beaten_by = beaten_by + jnp.where(wins, 1.0, 0.0)
        masked.append(jnp.where(beaten_by < TOPK_GROUPS, blocks[g], neg))
    msel = jnp.concatenate(masked, axis=0)
    eio = lax.broadcasted_iota(I32, msel.shape, 0).astype(F32)
    chosen = jnp.zeros(msel.shape, F32)
    idxs, scs = [], []
    for _ in range(TOP_K):
        m = jnp.max(msel, axis=0, keepdims=True)
        first = jnp.min(jnp.where(msel == m, eio, float(N_EXPERTS)), axis=0, keepdims=True)
        hit = eio == first
        scs.append(jnp.sum(jnp.where(hit, scores, 0.0), axis=0, keepdims=True))
        msel = jnp.where(hit, neg, msel)
        chosen = jnp.where(hit, 1.0, chosen)
        idxs.append(first)
    total = scs[0]
    for s in scs[1:]:
        total = total + s
    before = jnp.dot(chosen.astype(BF16), tri_ref[...], preferred_element_type=F32) + carry_ref[...]
    ranks = [jnp.sum(jnp.where(eio == i, before, 0.0), axis=0, keepdims=True) for i in idxs]
    carry_ref[...] = carry_ref[...] + jnp.sum(chosen, axis=1, keepdims=True)
    idx_o[...] = jnp.concatenate(idxs, axis=0).astype(I32)
    gate_o[...] = jnp.concatenate([s / total * ROUTED_SCALE for s in scs], axis=0)
    rank_o[...] = jnp.concatenate(ranks, axis=0).astype(I32)
    cnt_o[...] = jnp.broadcast_to(carry_ref[...], cnt_o.shape)


def _router(h, rwt, bias, tt):
    n = h.shape[0]
    assert n % tt == 0
    tri = (lax.broadcasted_iota(I32, (tt, tt), 0) < lax.broadcasted_iota(I32, (tt, tt), 1)).astype(BF16)
    tokT = pl.BlockSpec((TOP_K, tt), lambda i: (0, i))
    return pl.pallas_call(
        _router_kernel,
        grid=(n // tt,),
        in_specs=[pl.BlockSpec((tt, D_MODEL), lambda i: (i, 0)), pl.BlockSpec((N_EXPERTS, D_MODEL), lambda i: (0, 0)),
                  pl.BlockSpec((N_EXPERTS, 1), lambda i: (0, 0)), pl.BlockSpec((tt, tt), lambda i: (0, 0))],
        out_specs=[tokT, tokT, tokT, pl.BlockSpec((N_EXPERTS, LANES), lambda i: (0, 0))],
        out_shape=[jax.ShapeDtypeStruct((TOP_K, n), I32), jax.ShapeDtypeStruct((TOP_K, n), F32),
                   jax.ShapeDtypeStruct((TOP_K, n), I32), jax.ShapeDtypeStruct((N_EXPERTS, LANES), F32)],
        scratch_shapes=[pltpu.VMEM((N_EXPERTS, 1), F32)],
        compiler_params=_cparams(1),
        name="router",
    )(h, rwt, bias, tri)


ROW_SUB = D_MODEL // LANES


def _row_copy(src_ref, src_row, dst_ref, dst_row, sem):
    src = src_ref.at[pl.ds(pl.multiple_of(src_row * ROW_SUB, ROW_SUB), ROW_SUB), :]
    dst = dst_ref.at[pl.ds(pl.multiple_of(dst_row * ROW_SUB, ROW_SUB), ROW_SUB), :]
    return pltpu.make_async_copy(src, dst, sem)


def _store_row_tiles(ref, x):
    rows = x.shape[0]
    for c in range(ROW_SUB):
        ref[pl.ds(c, rows, stride=ROW_SUB), :] = x[:, c * LANES:(c + 1) * LANES]


def _load_row_tiles(ref):
    rows = ref.shape[0] // ROW_SUB
    return jnp.concatenate([ref[pl.ds(c, rows, stride=ROW_SUB), :] for c in range(ROW_SUB)], axis=-1)


TOKENS_PER_IDX_ROW = LANES // TOP_K


def _dest_of(dest_smem, r, j):
    return dest_smem[r // TOKENS_PER_IDX_ROW, (r % TOKENS_PER_IDX_ROW) * TOP_K + j]


def _dispatch_kernel(zpos_ref, dest_hbm, h_ref, xs_hbm, dest_smem, zero_buf, rows_buf, sem_idx, sem_zero, sem_rows,
                     *, td):
    i = pl.program_id(0)

    def zero_copy(e):
        start = pl.multiple_of(zpos_ref[e] * ROW_SUB, CHUNK * ROW_SUB)
        return pltpu.make_async_copy(zero_buf, xs_hbm.at[pl.ds(start, CHUNK * ROW_SUB), :], sem_zero)

    @pl.when(i == 0)
    def _():
        zero_buf[...] = jnp.zeros_like(zero_buf)

        def start(e, c):
            @pl.when(zpos_ref[e] >= 0)
            def _():
                zero_copy(e).start()
            return c

        def wait(e, c):
            @pl.when(zpos_ref[e] >= 0)
            def _():
                zero_copy(e).wait()
            return c

        lax.fori_loop(0, N_EXPERTS, start, 0)
        lax.fori_loop(0, N_EXPERTS, wait, 0)

    idx_copy = pltpu.make_async_copy(dest_hbm.at[i], dest_smem, sem_idx)
    idx_copy.start()
    _store_row_tiles(rows_buf, h_ref[...])
    idx_copy.wait()

    def start_rows(r, c):
        for j in range(TOP_K):
            _row_copy(rows_buf, r, xs_hbm, _dest_of(dest_smem, r, j), sem_rows).start(priority=j % 2)
        return c

    def wait_rows(r, c):
        for j in range(TOP_K):
            _row_copy(rows_buf, r, xs_hbm, _dest_of(dest_smem, r, j), sem_rows).wait()
        return c

    lax.fori_loop(0, td, start_rows, 0)
    lax.fori_loop(0, td, wait_rows, 0)


def _dispatch(zpos, dest_tiles, h, n_rows, td):
    n = h.shape[0]
    assert n % td == 0
    return pl.pallas_call(
        functools.partial(_dispatch_kernel, td=td),
        grid_spec=pltpu.PrefetchScalarGridSpec(
            num_scalar_prefetch=1,
            grid=(n // td,),
            in_specs=[pl.BlockSpec(memory_space=pl.ANY), pl.BlockSpec((td, D_MODEL), lambda i, z: (i, 0))],
            out_specs=pl.BlockSpec(memory_space=pl.ANY),
            scratch_shapes=[pltpu.SMEM((td // TOKENS_PER_IDX_ROW, LANES), I32), pltpu.VMEM((CHUNK * ROW_SUB, LANES), F32),
                            pltpu.VMEM((td * ROW_SUB, LANES), F32),
                            pltpu.SemaphoreType.DMA, pltpu.SemaphoreType.DMA, pltpu.SemaphoreType.DMA],
        ),
        out_shape=jax.ShapeDtypeStruct((n_rows * ROW_SUB, LANES), F32),
        compiler_params=_cparams(1),
        name="dispatch",
    )(zpos, dest_tiles, h)


def _ffn_kernel(be_ref, nu_ref, x_ref, wg_ref, wu_ref, wd_ref, o_ref):
    @pl.when(pl.program_id(0) < nu_ref[0])
    def _():
        x = _load_row_tiles(x_ref).astype(BF16)
        gate = jnp.dot(x, wg_ref[0].astype(BF16), preferred_element_type=F32)
        up = jnp.dot(x, wu_ref[0].astype(BF16), preferred_element_type=F32)
        act = (gate * _sigmoid(gate)) * up
        _store_row_tiles(o_ref, jnp.dot(act.astype(BF16), wd_ref[0].astype(BF16), preferred_element_type=F32))


def _ffn(blk_exp, n_used, xs, wg, wu, wd):
    n_chunks = xs.shape[0] // (CHUNK * ROW_SUB)
    rows = pl.BlockSpec((CHUNK * ROW_SUB, LANES), lambda c, be, nu: (jnp.minimum(c, nu[0] - 1), 0))
    return pl.pallas_call(
        _ffn_kernel,
        grid_spec=pltpu.PrefetchScalarGridSpec(
            num_scalar_prefetch=2,
            grid=(n_chunks,),
            in_specs=[rows,
                      pl.BlockSpec((1, D_MODEL, EXPERT_FF), lambda c, be, nu: (be[c], 0, 0)),
                      pl.BlockSpec((1, D_MODEL, EXPERT_FF), lambda c, be, nu: (be[c], 0, 0)),
                      pl.BlockSpec((1, EXPERT_FF, D_MODEL), lambda c, be, nu: (be[c], 0, 0))],
            out_specs=rows,
        ),
        out_shape=jax.ShapeDtypeStruct(xs.shape, F32),
        compiler_params=_cparams(1),
        name="expert_ffn",
    )(blk_exp, n_used, xs, wg, wu, wd)


def _combine_kernel(dest_hbm, gate_ref, h_ref, ys_hbm, sg_ref, su_ref, sd_ref, l2g_ref, l2b_ref, o_ref,
                    dest_smem, buf, sem_idx, sem_rows, *, tc, alpha):
    i = pl.program_id(0)
    idx_copy = pltpu.make_async_copy(dest_hbm.at[i], dest_smem, sem_idx)
    idx_copy.start()
    idx_copy.wait()

    def start_rows(r, c):
        for j in range(TOP_K):
            _row_copy(ys_hbm, _dest_of(dest_smem, r, j), buf.at[j], r, sem_rows).start(priority=j % 2)
        return c

    def wait_rows(r, c):
        for j in range(TOP_K):
            _row_copy(ys_hbm, _dest_of(dest_smem, r, j), buf.at[j], r, sem_rows).wait()
        return c

    lax.fori_loop(0, tc, start_rows, 0)
    h = h_ref[...]
    hb = h.astype(BF16)
    sgate = jnp.dot(hb, sg_ref[...], preferred_element_type=F32)
    sup = jnp.dot(hb, su_ref[...], preferred_element_type=F32)
    shared = jnp.dot(((sgate * _sigmoid(sgate)) * sup).astype(BF16), sd_ref[...], preferred_element_type=F32)
    lax.fori_loop(0, tc, wait_rows, 0)
    gate = gate_ref[...]
    routed = _load_row_tiles(buf.at[0]) * gate[:, 0:1]
    for j in range(1, TOP_K):
        routed = routed + _load_row_tiles(buf.at[j]) * gate[:, j:j + 1]
    o_ref[...] = _layer_norm(alpha * h + (routed + shared), l2g_ref[...], l2b_ref[...])


def _combine(dest_tiles, gate, h, ys, sg, su, sd, l2g, l2b, tc, alpha):
    n = h.shape[0]
    assert n % tc == 0
    tok = lambda w: pl.BlockSpec((tc, w), lambda i: (i, 0))
    full = lambda a: pl.BlockSpec(a.shape, lambda i: (0, 0))
    anyspec = pl.BlockSpec(memory_space=pl.ANY)
    return pl.pallas_call(
        functools.partial(_combine_kernel, tc=tc, alpha=alpha),
        grid=(n // tc,),
        in_specs=[anyspec, tok(TOP_K), tok(D_MODEL), anyspec, full(sg), full(su), full(sd), full(l2g), full(l2b)],
        out_specs=tok(D_MODEL),
        out_shape=jax.ShapeDtypeStruct((n, D_MODEL), F32),
        scratch_shapes=[pltpu.SMEM((tc // TOKENS_PER_IDX_ROW, LANES), I32), pltpu.VMEM((TOP_K, tc * ROW_SUB, LANES), F32),
                        pltpu.SemaphoreType.DMA, pltpu.SemaphoreType.DMA],
        compiler_params=_cparams(1),
        name="combine",
    )(dest_tiles, gate, h, ys, sg, su, sd, l2g, l2b)


def _pad_cols(a, width):
    return jnp.pad(a, ((0, 0), (0, width - a.shape[1])))


def _pack_shift_cols(a):
    c = 3 * RWKV_W
    return jnp.concatenate([a[:, :c], _pad_cols(a[:, c:c + DECAY_LORA], LORA_PAD),
                            _pad_cols(a[:, c + DECAY_LORA:c + DECAY_LORA + AAA_LORA], LORA_PAD),
                            _pad_cols(a[:, c + DECAY_LORA + AAA_LORA:], GATE_PAD)], axis=1)


def _keys_to_lanes(a, b, t):
    return a.reshape(b, t, N_HEADS, HEAD).transpose(1, 3, 0, 2).reshape(t, HEAD, b * N_HEADS)


def _values_to_lanes(a, b, t, split):
    a = a.reshape(b, t, N_HEADS, split, HEAD // split).transpose(1, 4, 3, 0, 2)
    return a.reshape(t, HEAD // split, split * b * N_HEADS)


def _values_from_lanes(y, b, t, split):
    y = y.reshape(t, HEAD // split, split, b, N_HEADS).transpose(3, 0, 4, 2, 1)
    return y.reshape(b * t, RWKV_W)


def _state_to_lanes(s, b, split):
    s = s.reshape(b, N_HEADS, split, HEAD // split, HEAD).transpose(4, 3, 2, 0, 1)
    return s.reshape(HEAD, HEAD // split, split * b * N_HEADS)


def _state_from_lanes(s, b, split):
    s = s.reshape(HEAD, HEAD // split, split, b, N_HEADS).transpose(3, 4, 2, 1, 0)
    return s.reshape(b, N_HEADS, HEAD, HEAD)


def _mixer(z, zprev, x2, b, t, wkv0, pool_prev, mk, mv, pos0, wts, tiles):
    n = b * t
    tt, tb, pool_bb, att_tt = tiles
    z3 = z[:n].reshape(b, t, Z_W)
    r, w, kp, v, kk, nb, g, bonus = _rwkv_prep(z, zprev, b, t, tt, wts["prep"])

    split = 1 if (b * N_HEADS) % LANES == 0 else LANES // (b * N_HEADS)
    lay = lambda a: _keys_to_lanes(a, b, t)
    y_l, s_l = _wkv(lay(r), lay(w), lay(kp), lay(kk), lay(nb), _values_to_lanes(v, b, t, split),
                    _state_to_lanes(wkv0, b, split), tb)
    y = _values_from_lanes(y_l, b, t, split)
    wkv_new = _state_from_lanes(s_l, b, split)

    zp = z3[:, :, ZS_W:ZS_W + POOL_W]
    full = jnp.concatenate([jnp.zeros((b, POOL_PAD - POOL_BUF, POOL_W), F32), pool_prev, zp], axis=1)
    o_pool = _pool(full, wts["pool_w"], wts["pool_scale"], pool_bb, pos0).reshape(n, POOL_W)
    pool_new = full[:, -POOL_BUF:]

    o_mem = _memattn(z3, mk, mv, att_tt).reshape(n, MEM_W)
    h = _post(y, bonus, g, o_pool, o_mem, x2, *wts["post"], tt, wts["alpha"])
    return h, wkv_new, pool_new


def kernel(x_prompt, x_sample, mem_prompt, state_wkv, state_shift, state_pool, cache_mem_k, cache_mem_v, w_in, mu_shift, w0, w_up_decay, a0, w_up_aaa, w_up_gate, k_k, k_a, r_k, ln_x_w, ln_x_b, pool_w, pool_scale, mem_wk, mem_wv, w_out, ln1_g, ln1_b, router_w, router_b, exp_gate, exp_up, exp_down, sh_gate, sh_up, sh_down, ln2_g, ln2_b):
    depth = w_in.shape[0]
    assert depth == 1
    l = 0
    alpha = (2.0 * depth) ** 0.25
    bp, tp, d = x_prompt.shape
    bs, ts, _ = x_sample.shape
    n_p, n_s = bp * tp, bs * ts
    n = n_p + n_s

    w_in_p = jnp.concatenate([_pack_shift_cols(w_in[l][:, :1824]), w_in[l][:, 1824:]], axis=1).astype(BF16)
    row = lambda a: a.reshape(1, -1)
    pad_rows = lambda a, rows: jnp.pad(a, ((0, rows - a.shape[0]), (0, 0)))
    bd = jnp.kron(jnp.eye(N_HEADS, dtype=F32), jnp.ones((HEAD, HEAD), F32)).astype(BF16)
    pw = pool_w[l]
    pw_bd = jnp.zeros((POOL_W, POOL_W), F32)
    for gi in range(4):
        pw_bd = pw_bd.at[gi * 64:(gi + 1) * 64, gi * 64:(gi + 1) * 64].set(pw[gi])
    wts = {
        "prep": (_pack_shift_cols(row(mu_shift[l])), row(w0[l]), row(a0[l]), pad_rows(w_up_decay[l], LORA_PAD),
                 pad_rows(w_up_aaa[l], LORA_PAD), pad_rows(w_up_gate[l], GATE_PAD), row(k_k[l]), row(k_a[l]),
                 row(r_k[l]), bd),
        "pool_w": pw_bd.astype(BF16), "pool_scale": row(pool_scale[l]),
        "post": (w_out[l].astype(BF16), row(ln_x_w[l]), row(ln_x_b[l]), row(ln1_g[l]), row(ln1_b[l]), bd),
        "alpha": alpha,
    }

    xp2 = x_prompt.reshape(n_p, d)
    xs2 = x_sample.reshape(n_s, d)
    z_p = _matmul(xp2, w_in_p, _tile(n_p, 512))
    z_s = _matmul(jnp.concatenate([xs2, state_shift[l]], axis=0), w_in_p, _tile(n_s + bs, 512))
    mkv = _matmul(mem_prompt.reshape(bp * MEM_TOKENS, d),
                  jnp.concatenate([mem_wk[l], mem_wv[l]], axis=1).astype(BF16), _tile(bp * MEM_TOKENS, 512))
    mk_p = mkv[:, :MEM_W].reshape(bp, MEM_TOKENS, MEM_W)
    mv_p = mkv[:, MEM_W:].reshape(bp, MEM_TOKENS, MEM_W)

    h_p, wkv_p, pool_p = _mixer(z_p, jnp.zeros((bp, ZS_W), F32), xp2, bp, tp,
                                jnp.zeros((bp, N_HEADS, HEAD, HEAD), F32), jnp.zeros((bp, POOL_BUF, POOL_W), F32),
                                mk_p, mv_p, 0, wts,
                                (_tile(n_p, 256), _tile(tp, 32, 1), 1, _tile(tp, 256)))
    h_s, wkv_s, pool_s = _mixer(z_s, z_s[n_s:, :ZS_W], xs2, bs, ts, state_wkv[l], state_pool[l],
                                cache_mem_k[l].reshape(bs, MEM_TOKENS, MEM_W),
                                cache_mem_v[l].reshape(bs, MEM_TOKENS, MEM_W), PAST_LEN, wts,
                                (_tile(n_s, 256), _tile(ts, 32, 1), _tile(bs, 16, 1), _tile(ts, 256)))

    h = jnp.concatenate([h_p, h_s], axis=0)
    idx, gate, rank, cnt = _router(h, router_w[l].T, router_b[l].reshape(N_EXPERTS, 1), _tile(n, 256, LANES))
    counts = cnt[:, 0].astype(I32)
    padded = (counts + CHUNK - 1) // CHUNK * CHUNK
    pad_end = jnp.cumsum(padded)
    pad_start = pad_end - padded
    n_chunks = (n * TOP_K + N_EXPERTS * (CHUNK - 1) + CHUNK - 1) // CHUNK
    experts = jnp.arange(N_EXPERTS, dtype=I32)
    dest = rank + jnp.sum(jnp.where(idx[..., None] == experts, pad_start, 0), axis=-1)
    td = _tile(n, 256, LANES)
    dest_tiles = dest.T.reshape(n // td, td // TOKENS_PER_IDX_ROW, LANES)
    zpos = jnp.where(padded > 0, pad_end - CHUNK, -1).astype(I32)
    chunk_start = jnp.arange(n_chunks, dtype=I32) * CHUNK
    blk_exp = jnp.minimum(jnp.sum((pad_end[None, :] <= chunk_start[:, None]).astype(I32), axis=1), N_EXPERTS - 1)
    n_used = (pad_end[-1:] // CHUNK).astype(I32)
    xs_sorted = _dispatch(zpos, dest_tiles, h, n_chunks * CHUNK, td)
    ys_sorted = _ffn(blk_exp, n_used, xs_sorted, exp_gate[l], exp_up[l], exp_down[l])
    y = _combine(dest_tiles, gate.T, h, ys_sorted, sh_gate[l].astype(BF16), sh_up[l].astype(BF16),
                 sh_down[l].astype(BF16), row(ln2_g[l]), row(ln2_b[l]), td, alpha)

    return (y[:n_p].reshape(bp, tp, d), y[n_p:].reshape(bs, ts, d),
            wkv_p[None], x_prompt[:, -1][None], pool_p[None],
            mk_p.reshape(bp, MEM_TOKENS, MEM_HEADS, HEAD)[None], mv_p.reshape(bp, MEM_TOKENS, MEM_HEADS, HEAD)[None],
            wkv_s[None], x_sample[:, -1][None], pool_s[None])
```

```python
import functools

import jax
import jax.numpy as jnp
from jax import lax
from jax.experimental import pallas as pl
from jax.experimental.pallas import tpu as pltpu

F32 = jnp.float32
BF16 = jnp.bfloat16
I32 = jnp.int32
HIGHEST = lax.Precision.HIGHEST

D_MODEL = 1024
HEAD = 64
N_HEADS = 8
RWKV_W = 512
POOL_W = 256
MEM_W = 256
MEM_HEADS = 4
MEM_TOKENS = 256
POOL_BUF = 15
POOL_PAD = 16
DECAY_LORA = 64
AAA_LORA = 64
GATE_LORA = 160
LORA_PAD = 128
GATE_PAD = 256
ZS_W = 3 * RWKV_W + 2 * LORA_PAD + GATE_PAD
Z_W = ZS_W + POOL_W + MEM_W
N_EXPERTS = 256
TOP_K = 8
N_GROUPS = 8
GROUP_SIZE = N_EXPERTS // N_GROUPS
TOPK_GROUPS = 4
EXPERT_FF = 256
ROUTED_SCALE = 2.5
CHUNK = 256
PAST_LEN = 16384
LN_EPS = 1e-5
GN_EPS = 64e-5
LANES = 128
SUBLANES = 8
VMEM_LIMIT = 48 * 1024 * 1024


def _cparams(n_axes):
    return pltpu.CompilerParams(dimension_semantics=("arbitrary",) * n_axes, vmem_limit_bytes=VMEM_LIMIT)


def _tile(n, preferred, multiple=8):
    best = None
    for c in range(multiple, min(n, preferred) + 1, multiple):
        if n % c == 0:
            best = c
    assert best is not None, (n, preferred, multiple)
    return best


def _sigmoid(x):
    return 1.0 / (1.0 + jnp.exp(-x))


def _matmul_kernel(x_ref, w_ref, o_ref):
    o_ref[...] = jnp.dot(x_ref[...].astype(BF16), w_ref[...], preferred_element_type=F32)


def _matmul(x, w, tm):
    m, k = x.shape
    n = w.shape[1]
    assert m % tm == 0
    return pl.pallas_call(
        _matmul_kernel,
        grid=(m // tm,),
        in_specs=[pl.BlockSpec((tm, k), lambda i: (i, 0)), pl.BlockSpec((k, n), lambda i: (0, 0))],
        out_specs=pl.BlockSpec((tm, n), lambda i: (i, 0)),
        out_shape=jax.ShapeDtypeStruct((m, n), F32),
        compiler_params=_cparams(1),
        name="matmul",
    )(x, w)


def _segsum(x, bd):
    hi = x.astype(BF16)
    lo = (x - hi.astype(F32)).astype(BF16)
    return jnp.dot(hi, bd, preferred_element_type=F32) + jnp.dot(lo, bd, preferred_element_type=F32)


def _prep_kernel(zs_ref, zp_ref, mu_ref, w0_ref, a0_ref, wd_ref, wa_ref, wg_ref, kk_ref, ka_ref, rk_ref, bd_ref,
                 r_o, w_o, k_o, v_o, kk_o, nb_o, g_o, bonus_o, carry_ref, *, tiles_per_seq, seq_len):
    zs = zs_ref[...]
    rolled = pltpu.roll(zs, 1, 0)
    row = lax.broadcasted_iota(I32, zs.shape, 0)
    if tiles_per_seq:
        i = pl.program_id(0)

        @pl.when(i == 0)
        def _():
            carry_ref[...] = jnp.zeros_like(carry_ref)

        prev = jnp.where(i % tiles_per_seq == 0, zp_ref[0], carry_ref[...])
        shifted = jnp.where(row == 0, prev, rolled)
        carry_ref[...] = zs[zs.shape[0] - 1:, :]
    else:
        shifted = jnp.where((row & (seq_len - 1)) == 0, zp_ref[...], rolled)
    zm = zs + mu_ref[...] * (shifted - zs)
    r = zm[:, 0:RWKV_W]
    k = zm[:, RWKV_W:2 * RWKV_W]
    v = zm[:, 2 * RWKV_W:3 * RWKV_W]
    c0 = 3 * RWKV_W
    xw = zm[:, c0:c0 + LORA_PAD]
    xa = zm[:, c0 + LORA_PAD:c0 + 2 * LORA_PAD]
    xg = zm[:, c0 + 2 * LORA_PAD:c0 + 2 * LORA_PAD + GATE_PAD]
    bd = bd_ref[...]
    u = -(w0_ref[...] + jnp.dot(jnp.tanh(xw), wd_ref[...], precision=HIGHEST, preferred_element_type=F32))
    softplus = jnp.maximum(u, 0.0) + jnp.log(1.0 + jnp.exp(-jnp.abs(u)))
    decay = jnp.exp(-jnp.exp(-softplus - 0.5))
    a = _sigmoid(a0_ref[...] + jnp.dot(xa, wa_ref[...], precision=HIGHEST, preferred_element_type=F32))
    g = jnp.dot(_sigmoid(xg), wg_ref[...], precision=HIGHEST, preferred_element_type=F32)
    kk = k * kk_ref[...]
    kk = kk / jnp.maximum(jnp.sqrt(_segsum(kk * kk, bd)), 1e-12)
    kp = k * (1.0 + (a - 1.0) * ka_ref[...])
    r_o[...] = r
    w_o[...] = decay
    k_o[...] = kp
    v_o[...] = v
    kk_o[...] = kk
    nb_o[...] = -(kk * a)
    g_o[...] = g
    bonus_o[...] = _segsum(r * kp * rk_ref[...], bd) * v


def _rwkv_prep(z, zprev, b, t, tt, params):
    n_tok = b * t
    assert n_tok % tt == 0
    row = lambda w: pl.BlockSpec((1, w), lambda i: (0, 0))
    full = lambda a: pl.BlockSpec(a.shape, lambda i: (0, 0))
    tok = pl.BlockSpec((tt, RWKV_W), lambda i: (i, 0))
    mu, w0, a0, wd, wa, wg, k_k, k_a, r_k, bd = params
    if t % tt == 0:
        tiles_per_seq = t // tt
        zp = zprev.reshape(b, 1, ZS_W)
        zp_spec = pl.BlockSpec((1, 1, ZS_W), lambda i: (i // tiles_per_seq, 0, 0))
    else:
        assert tt % t == 0 and t & (t - 1) == 0
        tiles_per_seq = 0
        zp = jnp.repeat(zprev, t, axis=0)
        zp_spec = pl.BlockSpec((tt, ZS_W), lambda i: (i, 0))
    return pl.pallas_call(
        functools.partial(_prep_kernel, tiles_per_seq=tiles_per_seq, seq_len=t),
        grid=(n_tok // tt,),
        in_specs=[pl.BlockSpec((tt, ZS_W), lambda i: (i, 0)), zp_spec,
                  row(ZS_W), row(RWKV_W), row(RWKV_W), full(wd), full(wa), full(wg),
                  row(RWKV_W), row(RWKV_W), row(RWKV_W), full(bd)],
        out_specs=[tok] * 8,
        out_shape=[jax.ShapeDtypeStruct((n_tok, RWKV_W), F32)] * 8,
        scratch_shapes=[pltpu.VMEM((1, ZS_W), F32)],
        compiler_params=_cparams(1),
        name="rwkv_prep",
    )(z, zp, mu, w0, a0, wd, wa, wg, k_k, k_a, r_k, bd)


def _wkv_kernel(r_ref, w_ref, k_ref, kk_ref, nb_ref, v_ref, s0_ref, y_ref, s_ref, *scratch, tb, dup):
    slabs = s_ref.shape[1]
    key_unroll = 8

    @pl.when(pl.program_id(1) == 0)
    def _():
        s_ref[...] = s0_ref[...]

    if dup:
        for src, dst in zip((r_ref, w_ref, k_ref, kk_ref, nb_ref), scratch):
            x = src[...]
            dst[...] = jnp.concatenate([x, x], axis=-1)
        r_ref, w_ref, k_ref, kk_ref, nb_ref = scratch

    def rows(ref, t, k):
        return jnp.broadcast_to(ref[t, pl.ds(k, 1), :], (SUBLANES, LANES))

    def step(t, sa, with_next):
        vt = [v_ref[t, i * SUBLANES:(i + 1) * SUBLANES, :] for i in range(slabs)]
        zero = jnp.zeros((SUBLANES, LANES), F32)

        def key(k, acc):
            y_acc, next_acc = list(acc[0]), list(acc[1])
            wb, nbb, kb, rb = rows(w_ref, t, k), rows(nb_ref, t, k), rows(k_ref, t, k), rows(r_ref, t, k)
            kkb = rows(kk_ref, t + 1, k) if with_next else None
            for i in range(slabs):
                sn = s_ref[k, i] * wb + sa[i] * nbb + vt[i] * kb
                s_ref[k, i] = sn
                y_acc[i] = y_acc[i] + sn * rb
                if with_next:
                    next_acc[i] = next_acc[i] + sn * kkb
            return tuple(y_acc), tuple(next_acc)

        y_acc, next_acc = lax.fori_loop(0, HEAD, key, ((zero,) * slabs, (zero,) * slabs), unroll=key_unroll)
        for i, y in enumerate(y_acc):
            y_ref[t, i * SUBLANES:(i + 1) * SUBLANES, :] = y
        return next_acc

    def first(k, acc):
        kkb = rows(kk_ref, 0, k)
        return tuple(a + s_ref[k, i] * kkb for i, a in enumerate(acc))

    sa0 = lax.fori_loop(0, HEAD, first, (jnp.zeros((SUBLANES, LANES), F32),) * slabs, unroll=key_unroll)
    sa_last = lax.fori_loop(0, tb - 1, lambda t, sa: step(t, sa, True), sa0)
    step(tb - 1, sa_last, False)


def _wkv(r, w, k, kk, nb, v, s0, tb):
    t, _, lr = r.shape
    vr, l = v.shape[1], v.shape[2]
    dup = lr != l
    assert t % tb == 0 and l % LANES == 0 and vr % SUBLANES == 0 and (not dup or (l == LANES and 2 * lr == l))
    vec = pl.BlockSpec((tb, HEAD, lr if dup else LANES), lambda g, i: (i, 0, g))
    val = pl.BlockSpec((tb, vr, LANES), lambda g, i: (i, 0, g))
    slabs = vr // SUBLANES
    st = pl.BlockSpec((HEAD, slabs, SUBLANES, LANES), lambda g, i: (0, 0, 0, g))
    s0 = s0.reshape(HEAD, slabs, SUBLANES, l)
    y, s_new = pl.pallas_call(
        functools.partial(_wkv_kernel, tb=tb, dup=dup),
        grid=(l // LANES, t // tb),
        in_specs=[vec, vec, vec, vec, vec, val, st],
        out_specs=[val, st],
        out_shape=[jax.ShapeDtypeStruct(v.shape, F32), jax.ShapeDtypeStruct(s0.shape, F32)],
        scratch_shapes=[pltpu.VMEM((tb, HEAD, LANES), F32)] * 5 if dup else [],
        compiler_params=_cparams(2),
        name="wkv",
    )(r, w, k, kk, nb, v, s0)
    return y, s_new.reshape(HEAD, vr, l)


def _pool_kernel(full_ref, pw_ref, ps_ref, o_ref, *, t, pos0):
    bb = full_ref.shape[0]
    lane = lax.broadcasted_iota(I32, (t, POOL_W), 1)
    window = jnp.where(lane < 64, 2, jnp.where(lane < 128, 4, jnp.where(lane < 192, 8, 16)))
    pos = lax.broadcasted_iota(I32, (t, POOL_W), 0) + (pos0 + 1)
    cnt = jnp.minimum(pos, window).astype(F32)
    diffs = []
    for i in range(bb):
        f = full_ref[i]
        s2 = f + pltpu.roll(f, 1, 0)
        s4 = s2 + pltpu.roll(s2, 2, 0)
        s8 = s4 + pltpu.roll(s4, 4, 0)
        s16 = s8 + pltpu.roll(s8, 8, 0)
        wsum = jnp.where(lane < 64, s2[POOL_PAD:], jnp.where(lane < 128, s4[POOL_PAD:],
                                                            jnp.where(lane < 192, s8[POOL_PAD:], s16[POOL_PAD:])))
        diffs.append(wsum / cnt - f[POOL_PAD:])
    diff = diffs[0] if bb == 1 else jnp.concatenate(diffs, axis=0)
    out = jnp.dot(diff.astype(BF16), pw_ref[...], preferred_element_type=F32) * ps_ref[...]
    for i in range(bb):
        o_ref[i] = out[i * t:(i + 1) * t]


def _pool(full, pw_bd, pscale, bb, pos0):
    b, rows, _ = full.shape
    t = rows - POOL_PAD
    assert b % bb == 0
    return pl.pallas_call(
        functools.partial(_pool_kernel, t=t, pos0=pos0),
        grid=(b // bb,),
        in_specs=[pl.BlockSpec((bb, rows, POOL_W), lambda i: (i, 0, 0)),
                  pl.BlockSpec((POOL_W, POOL_W), lambda i: (0, 0)), pl.BlockSpec((1, POOL_W), lambda i: (0, 0))],
        out_specs=pl.BlockSpec((bb, t, POOL_W), lambda i: (i, 0, 0)),
        out_shape=jax.ShapeDtypeStruct((b, t, POOL_W), F32),
        compiler_params=_cparams(1),
        name="pool",
    )(full, pw_bd, pscale)


def _memattn_kernel(q_ref, k_ref, v_ref, o_ref):
    q = q_ref[0].astype(BF16)
    kf = k_ref[0]
    vf = v_ref[0]
    head_of_lane = lax.broadcasted_iota(I32, kf.shape, 1) // (MEM_W // MEM_HEADS)
    out = None
    for h in range(MEM_HEADS):
        kh = jnp.where(head_of_lane == h, kf, 0.0).astype(BF16)
        vh = jnp.where(head_of_lane == h, vf, 0.0).astype(BF16)
        s = lax.dot_general(q, kh, (((1,), (1,)), ((), ())), preferred_element_type=F32) * (64 ** -0.5)
        e = jnp.exp(s - jnp.max(s, axis=-1, keepdims=True))
        p = e / jnp.sum(e, axis=-1, keepdims=True)
        o = jnp.dot(p.astype(BF16), vh, preferred_element_type=F32)
        out = o if out is None else out + o
    o_ref[0] = out


def _memattn(z3, mk, mv, tt):
    b, t, _ = z3.shape
    assert t % tt == 0
    qcol = (Z_W - MEM_W) // MEM_W
    kv = pl.BlockSpec((1, MEM_TOKENS, MEM_W), lambda i, j: (i, 0, 0))
    return pl.pallas_call(
        _memattn_kernel,
        grid=(b, t // tt),
        in_specs=[pl.BlockSpec((1, tt, MEM_W), lambda i, j: (i, j, qcol)), kv, kv],
        out_specs=pl.BlockSpec((1, tt, MEM_W), lambda i, j: (i, j, 0)),
        out_shape=jax.ShapeDtypeStruct((b, t, MEM_W), F32),
        compiler_params=_cparams(2),
        name="mem_attn",
    )(z3, mk, mv)


def _layer_norm(x, g, b):
    mu = jnp.mean(x, axis=-1, keepdims=True)
    xc = x - mu
    var = jnp.mean(xc * xc, axis=-1, keepdims=True)
    return xc * lax.rsqrt(var + LN_EPS) * g + b


def _post_kernel(y_ref, bonus_ref, g_ref, op_ref, om_ref, x_ref, wo_ref, lxw_ref, lxb_ref, l1g_ref, l1b_ref, bd_ref,
                 h_ref, *, alpha):
    y = y_ref[...]
    bd = bd_ref[...]
    mu = _segsum(y, bd) * (1.0 / HEAD)
    yc = y - mu
    var = _segsum(yc * yc, bd) * (1.0 / HEAD)
    yn = yc * lax.rsqrt(var + GN_EPS) * lxw_ref[...] + lxb_ref[...]
    o_rwkv = (yn + bonus_ref[...]) * g_ref[...]
    mixed = (jnp.dot(o_rwkv.astype(BF16), wo_ref[0:RWKV_W, :], preferred_element_type=F32)
             + jnp.dot(op_ref[...].astype(BF16), wo_ref[RWKV_W:RWKV_W + POOL_W, :], preferred_element_type=F32)
             + jnp.dot(om_ref[...].astype(BF16), wo_ref[RWKV_W + POOL_W:, :], preferred_element_type=F32))
    h_ref[...] = _layer_norm(alpha * x_ref[...] + mixed, l1g_ref[...], l1b_ref[...])


def _post(y, bonus, g, o_pool, o_mem, x, wo, lxw, lxb, l1g, l1b, bd, tt, alpha):
    n = y.shape[0]
    assert n % tt == 0
    tok = lambda w: pl.BlockSpec((tt, w), lambda i: (i, 0))
    row = lambda w: pl.BlockSpec((1, w), lambda i: (0, 0))
    full = lambda a: pl.BlockSpec(a.shape, lambda i: (0, 0))
    return pl.pallas_call(
        functools.partial(_post_kernel, alpha=alpha),
        grid=(n // tt,),
        in_specs=[tok(RWKV_W), tok(RWKV_W), tok(RWKV_W), tok(POOL_W), tok(MEM_W), tok(D_MODEL), full(wo),
                  row(RWKV_W), row(RWKV_W), row(D_MODEL), row(D_MODEL), full(bd)],
        out_specs=tok(D_MODEL),
        out_shape=jax.ShapeDtypeStruct((n, D_MODEL), F32),
        compiler_params=_cparams(1),
        name="post",
    )(y, bonus, g, o_pool, o_mem, x, wo, lxw, lxb, l1g, l1b, bd)


def _router_kernel(h_ref, rwt_ref, bias_ref, tri_ref, below_ref, slot_o, gate_o, run_o, cnt_o, carry_ref):
    @pl.when(pl.program_id(0) == 0)
    def _():
        carry_ref[...] = jnp.zeros_like(carry_ref)

    neg = -jnp.inf
    logits = lax.dot_general(rwt_ref[...], h_ref[...], (((1,), (1,)), ((), ())),
                             precision=HIGHEST, preferred_element_type=F32)
    scores = _sigmoid(logits)
    sel = scores + bias_ref[...]
    tt = sel.shape[1]
    gio = lax.broadcasted_iota(I32, (GROUP_SIZE, tt), 0).astype(F32)
    blocks, gscore = [], []
    for g in range(N_GROUPS):
        blk = sel[g * GROUP_SIZE:(g + 1) * GROUP_SIZE, :]
        m1 = jnp.max(blk, axis=0, keepdims=True)
        first = jnp.min(jnp.where(blk == m1, gio, float(GROUP_SIZE)), axis=0, keepdims=True)
        m2 = jnp.max(jnp.where(gio == first, neg, blk), axis=0, keepdims=True)
        blocks.append(blk)
        gscore.append(m1 + m2)
    masked = []
    for g in range(N_GROUPS):
        beaten_by = jnp.zeros((1, tt), F32)
        for g2 in range(N_GROUPS):
            if g2 != g:
                wins = (gscore[g2] >= gscore[g]) if g2 < g else (gscore[g2] > gscore[g])
                beaten_by = beaten_by + jnp.where(wins, 1.0, 0.0)
        masked.append(jnp.where(beaten_by < TOPK_GROUPS, blocks[g], neg))
    msel = jnp.concatenate(masked, axis=0)
    eio = lax.broadcasted_iota(I32, msel.shape, 0).astype(F32)
    chosen = jnp.zeros(msel.shape, F32)
    idxs, scs = [], []
    for _ in range(TOP_K):
        m = jnp.max(msel, axis=0, keepdims=True)
        first = jnp.min(jnp.where(msel == m, eio, float(N_EXPERTS)), axis=0, keepdims=True)
        hit = eio == first
        scs.append(jnp.sum(jnp.where(hit, scores, 0.0), axis=0, keepdims=True))
        msel = jnp.where(hit, neg, msel)
        chosen = jnp.where(hit, 1.0, chosen)
        idxs.append(first)
    total = scs[0]
    for s in scs[1:]:
        total = total + s
    chosen_b = chosen.astype(BF16)
    earlier = jnp.dot(chosen_b, tri_ref[...], preferred_element_type=F32)
    smaller = jnp.dot(below_ref[...], chosen_b, preferred_element_type=F32)
    run_len = jnp.sum(chosen, axis=1, keepdims=True)
    run_off = jnp.sum(smaller, axis=1, keepdims=True)
    slot_of = run_off + earlier
    slots = [jnp.sum(jnp.where(eio == i, slot_of, 0.0), axis=0, keepdims=True) for i in idxs]
    slot_o[...] = jnp.concatenate(slots, axis=0).astype(I32)
    gate_o[...] = jnp.concatenate([s / total * ROUTED_SCALE for s in scs], axis=0)
    lane = lax.broadcasted_iota(I32, run_o.shape, 1)
    run_o[...] = jnp.where(lane == 0, run_len, jnp.where(lane == 1, run_off, jnp.where(lane == 2, carry_ref[...], 0.0)))
    carry_ref[...] = carry_ref[...] + run_len
    cnt_o[...] = jnp.broadcast_to(carry_ref[...], cnt_o.shape)


def _router(h, rwt, bias, tt):
    n = h.shape[0]
    assert n % tt == 0
    tri = (lax.broadcasted_iota(I32, (tt, tt), 0) < lax.broadcasted_iota(I32, (tt, tt), 1)).astype(BF16)
    below = (lax.broadcasted_iota(I32, (N_EXPERTS, N_EXPERTS), 1)
             < lax.broadcasted_iota(I32, (N_EXPERTS, N_EXPERTS), 0)).astype(BF16)
    tokT = pl.BlockSpec((TOP_K, tt), lambda i: (0, i))
    return pl.pallas_call(
        _router_kernel,
        grid=(n // tt,),
        in_specs=[pl.BlockSpec((tt, D_MODEL), lambda i: (i, 0)), pl.BlockSpec((N_EXPERTS, D_MODEL), lambda i: (0, 0)),
                  pl.BlockSpec((N_EXPERTS, 1), lambda i: (0, 0)), pl.BlockSpec((tt, tt), lambda i: (0, 0)),
                  pl.BlockSpec((N_EXPERTS, N_EXPERTS), lambda i: (0, 0))],
        out_specs=[tokT, tokT, pl.BlockSpec((N_EXPERTS, LANES), lambda i: (i, 0)),
                   pl.BlockSpec((N_EXPERTS, LANES), lambda i: (0, 0))],
        out_shape=[jax.ShapeDtypeStruct((TOP_K, n), I32), jax.ShapeDtypeStruct((TOP_K, n), F32),
                   jax.ShapeDtypeStruct((n // tt * N_EXPERTS, LANES), F32),
                   jax.ShapeDtypeStruct((N_EXPERTS, LANES), F32)],
        scratch_shapes=[pltpu.VMEM((N_EXPERTS, 1), F32)],
        compiler_params=_cparams(1),
        name="router",
    )(h, rwt, bias, tri, below)


ROW_SUB = D_MODEL // LANES


def _store_row_tiles(ref, x):
    rows = x.shape[0]
    for c in range(ROW_SUB):
        ref[pl.ds(c, rows, stride=ROW_SUB), :] = x[:, c * LANES:(c + 1) * LANES]


def _load_row_tiles(ref):
    rows = ref.shape[0] // ROW_SUB
    return jnp.concatenate([ref[pl.ds(c, rows, stride=ROW_SUB), :] for c in range(ROW_SUB)], axis=-1)


RUN_FIELDS = 3


def _run_copy(runs_ref, tile, e, local_ref, sorted_ref, sem, to_sorted):
    size = pl.multiple_of(runs_ref[tile * RUN_FIELDS + 2, e] * ROW_SUB, ROW_SUB)
    local = local_ref.at[pl.ds(pl.multiple_of(runs_ref[tile * RUN_FIELDS, e] * ROW_SUB, ROW_SUB), size), :]
    remote = sorted_ref.at[pl.ds(pl.multiple_of(runs_ref[tile * RUN_FIELDS + 1, e] * ROW_SUB, ROW_SUB), size), :]
    return pltpu.make_async_copy(local, remote, sem) if to_sorted else pltpu.make_async_copy(remote, local, sem)


def _for_each_run(runs_ref, tile, fn):
    def body(e, c):
        @pl.when(runs_ref[tile * RUN_FIELDS + 2, e] > 0)
        def _():
            fn(e)
        return c

    lax.fori_loop(0, N_EXPERTS, body, 0)


def _dispatch_kernel(zpos_ref, runs_ref, slot_ref, h_ref, xs_hbm, zero_buf, rows_buf, sem_zero, sem_rows, *, td):
    i = pl.program_id(0)

    def zero_copy(e):
        start = pl.multiple_of(zpos_ref[e] * ROW_SUB, CHUNK * ROW_SUB)
        return pltpu.make_async_copy(zero_buf, xs_hbm.at[pl.ds(start, CHUNK * ROW_SUB), :], sem_zero)

    @pl.when(i == 0)
    def _():
        zero_buf[...] = jnp.zeros_like(zero_buf)

        def start(e, c):
            @pl.when(zpos_ref[e] >= 0)
            def _():
                zero_copy(e).start()
            return c

        def wait(e, c):
            @pl.when(zpos_ref[e] >= 0)
            def _():
                zero_copy(e).wait()
            return c

        lax.fori_loop(0, N_EXPERTS, start, 0)
        lax.fori_loop(0, N_EXPERTS, wait, 0)

    slots = slot_ref[...]
    slot_iota = lax.broadcasted_iota(I32, (TOP_K * td, td), 0)
    select = jnp.zeros((TOP_K * td, td), F32)
    for j in range(TOP_K):
        select = jnp.where(slot_iota == slots[j:j + 1, :], 1.0, select)
    local = jnp.dot(select.astype(BF16), h_ref[...].astype(BF16), preferred_element_type=F32)
    _store_row_tiles(rows_buf, local)
    _for_each_run(runs_ref, i, lambda e: _run_copy(runs_ref, i, e, rows_buf, xs_hbm, sem_rows, True).start())
    _for_each_run(runs_ref, i, lambda e: _run_copy(runs_ref, i, e, rows_buf, xs_hbm, sem_rows, True).wait())


def _dispatch(zpos, runs, slots, h, n_rows, td):
    n = h.shape[0]
    assert n % td == 0
    return pl.pallas_call(
        functools.partial(_dispatch_kernel, td=td),
        grid_spec=pltpu.PrefetchScalarGridSpec(
            num_scalar_prefetch=2,
            grid=(n // td,),
            in_specs=[pl.BlockSpec((TOP_K, td), lambda i, z, r: (0, i)),
                      pl.BlockSpec((td, D_MODEL), lambda i, z, r: (i, 0))],
            out_specs=pl.BlockSpec(memory_space=pl.ANY),
            scratch_shapes=[pltpu.VMEM((CHUNK * ROW_SUB, LANES), F32), pltpu.VMEM((TOP_K * td * ROW_SUB, LANES), F32),
                            pltpu.SemaphoreType.DMA, pltpu.SemaphoreType.DMA],
        ),
        out_shape=jax.ShapeDtypeStruct((n_rows * ROW_SUB, LANES), F32),
        compiler_params=_cparams(1),
        name="dispatch",
    )(zpos, runs, slots, h)


def _ffn_kernel(be_ref, nu_ref, x_ref, wg_ref, wu_ref, wd_ref, o_ref):
    @pl.when(pl.program_id(0) < nu_ref[0])
    def _():
        x = _load_row_tiles(x_ref).astype(BF16)
        gate = jnp.dot(x, wg_ref[0].astype(BF16), preferred_element_type=F32)
        up = jnp.dot(x, wu_ref[0].astype(BF16), preferred_element_type=F32)
        act = (gate * _sigmoid(gate)) * up
        _store_row_tiles(o_ref, jnp.dot(act.astype(BF16), wd_ref[0].astype(BF16), preferred_element_type=F32))


def _ffn(blk_exp, n_used, xs, wg, wu, wd):
    n_chunks = xs.shape[0] // (CHUNK * ROW_SUB)
    rows = pl.BlockSpec((CHUNK * ROW_SUB, LANES), lambda c, be, nu: (jnp.minimum(c, nu[0] - 1), 0))
    return pl.pallas_call(
        _ffn_kernel,
        grid_spec=pltpu.PrefetchScalarGridSpec(
            num_scalar_prefetch=2,
            grid=(n_chunks,),
            in_specs=[rows,
                      pl.BlockSpec((1, D_MODEL, EXPERT_FF), lambda c, be, nu: (be[c], 0, 0)),
                      pl.BlockSpec((1, D_MODEL, EXPERT_FF), lambda c, be, nu: (be[c], 0, 0)),
                      pl.BlockSpec((1, EXPERT_FF, D_MODEL), lambda c, be, nu: (be[c], 0, 0))],
            out_specs=rows,
        ),
        out_shape=jax.ShapeDtypeStruct(xs.shape, F32),
        compiler_params=_cparams(1),
        name="expert_ffn",
    )(blk_exp, n_used, xs, wg, wu, wd)


def _combine_kernel(runs_ref, slot_ref, gate_ref, h_ref, ys_hbm, sg_ref, su_ref, sd_ref, l2g_ref, l2b_ref, o_ref,
                    rows_buf, sem_rows, *, tc, alpha):
    i = pl.program_id(0)
    _for_each_run(runs_ref, i, lambda e: _run_copy(runs_ref, i, e, rows_buf, ys_hbm, sem_rows, False).start())
    h = h_ref[...]
    hb = h.astype(BF16)
    sgate = jnp.dot(hb, sg_ref[...], preferred_element_type=F32)
    sup = jnp.dot(hb, su_ref[...], preferred_element_type=F32)
    shared = jnp.dot(((sgate * _sigmoid(sgate)) * sup).astype(BF16), sd_ref[...], preferred_element_type=F32)
    slots = slot_ref[...]
    gate = gate_ref[...]
    slot_iota = lax.broadcasted_iota(I32, (tc, TOP_K * tc), 1)
    weights = jnp.zeros((tc, TOP_K * tc), F32)
    for j in range(TOP_K):
        weights = jnp.where(slot_iota == slots[:, j:j + 1], gate[:, j:j + 1], weights)
    w_hi = weights.astype(BF16)
    w_lo = (weights - w_hi.astype(F32)).astype(BF16)
    _for_each_run(runs_ref, i, lambda e: _run_copy(runs_ref, i, e, rows_buf, ys_hbm, sem_rows, False).wait())
    local = _load_row_tiles(rows_buf).astype(BF16)
    routed = (jnp.dot(w_hi, local, preferred_element_type=F32) + jnp.dot(w_lo, local, preferred_element_type=F32))
    o_ref[...] = _layer_norm(alpha * h + (routed + shared), l2g_ref[...], l2b_ref[...])


def _combine(runs, slots, gate, h, ys, sg, su, sd, l2g, l2b, tc, alpha):
    n = h.shape[0]
    assert n % tc == 0
    tok = lambda w: pl.BlockSpec((tc, w), lambda i, r: (i, 0))
    full = lambda a: pl.BlockSpec(a.shape, lambda i, r: (0, 0))
    return pl.pallas_call(
        functools.partial(_combine_kernel, tc=tc, alpha=alpha),
        grid_spec=pltpu.PrefetchScalarGridSpec(
            num_scalar_prefetch=1,
            grid=(n // tc,),
            in_specs=[tok(TOP_K), tok(TOP_K), tok(D_MODEL), pl.BlockSpec(memory_space=pl.ANY),
                      full(sg), full(su), full(sd), full(l2g), full(l2b)],
            out_specs=tok(D_MODEL),
            scratch_shapes=[pltpu.VMEM((TOP_K * tc * ROW_SUB, LANES), F32), pltpu.SemaphoreType.DMA],
        ),
        out_shape=jax.ShapeDtypeStruct((n, D_MODEL), F32),
        compiler_params=_cparams(1),
        name="combine",
    )(runs, slots, gate, h, ys, sg, su, sd, l2g, l2b)


def _pad_cols(a, width):
    return jnp.pad(a, ((0, 0), (0, width - a.shape[1])))


def _pack_shift_cols(a):
    c = 3 * RWKV_W
    return jnp.concatenate([a[:, :c], _pad_cols(a[:, c:c + DECAY_LORA], LORA_PAD),
                            _pad_cols(a[:, c + DECAY_LORA:c + DECAY_LORA + AAA_LORA], LORA_PAD),
                            _pad_cols(a[:, c + DECAY_LORA + AAA_LORA:], GATE_PAD)], axis=1)


def _keys_to_lanes(a, b, t):
    return a.reshape(b, t, N_HEADS, HEAD).transpose(1, 3, 0, 2).reshape(t, HEAD, b * N_HEADS)


def _values_to_lanes(a, b, t, split):
    a = a.reshape(b, t, N_HEADS, split, HEAD // split).transpose(1, 4, 3, 0, 2)
    return a.reshape(t, HEAD // split, split * b * N_HEADS)


def _values_from_lanes(y, b, t, split):
    y = y.reshape(t, HEAD // split, split, b, N_HEADS).transpose(3, 0, 4, 2, 1)
    return y.reshape(b * t, RWKV_W)


def _state_to_lanes(s, b, split):
    s = s.reshape(b, N_HEADS, split, HEAD // split, HEAD).transpose(4, 3, 2, 0, 1)
    return s.reshape(HEAD, HEAD // split, split * b * N_HEADS)


def _state_from_lanes(s, b, split):
    s = s.reshape(HEAD, HEAD // split, split, b, N_HEADS).transpose(3, 4, 2, 1, 0)
    return s.reshape(b, N_HEADS, HEAD, HEAD)


def _mixer(z, zprev, x2, b, t, wkv0, pool_prev, mk, mv, pos0, wts, tiles):
    n = b * t
    tt, tb, pool_bb, att_tt = tiles
    z3 = z[:n].reshape(b, t, Z_W)
    r, w, kp, v, kk, nb, g, bonus = _rwkv_prep(z, zprev, b, t, tt, wts["prep"])

    split = 1 if (b * N_HEADS) % LANES == 0 else LANES // (b * N_HEADS)
    lay = lambda a: _keys_to_lanes(a, b, t)
    y_l, s_l = _wkv(lay(r), lay(w), lay(kp), lay(kk), lay(nb), _values_to_lanes(v, b, t, split),
                    _state_to_lanes(wkv0, b, split), tb)
    y = _values_from_lanes(y_l, b, t, split)
    wkv_new = _state_from_lanes(s_l, b, split)

    zp = z3[:, :, ZS_W:ZS_W + POOL_W]
    full = jnp.concatenate([jnp.zeros((b, POOL_PAD - POOL_BUF, POOL_W), F32), pool_prev, zp], axis=1)
    o_pool = _pool(full, wts["pool_w"], wts["pool_scale"], pool_bb, pos0).reshape(n, POOL_W)
    pool_new = full[:, -POOL_BUF:]

    o_mem = _memattn(z3, mk, mv, att_tt).reshape(n, MEM_W)
    h = _post(y, bonus, g, o_pool, o_mem, x2, *wts["post"], tt, wts["alpha"])
    return h, wkv_new, pool_new


def kernel(x_prompt, x_sample, mem_prompt, state_wkv, state_shift, state_pool, cache_mem_k, cache_mem_v, w_in, mu_shift, w0, w_up_decay, a0, w_up_aaa, w_up_gate, k_k, k_a, r_k, ln_x_w, ln_x_b, pool_w, pool_scale, mem_wk, mem_wv, w_out, ln1_g, ln1_b, router_w, router_b, exp_gate, exp_up, exp_down, sh_gate, sh_up, sh_down, ln2_g, ln2_b):
    depth = w_in.shape[0]
    assert depth == 1
    l = 0
    alpha = (2.0 * depth) ** 0.25
    bp, tp, d = x_prompt.shape
    bs, ts, _ = x_sample.shape
    n_p, n_s = bp * tp, bs * ts
    n = n_p + n_s

    w_in_p = jnp.concatenate([_pack_shift_cols(w_in[l][:, :1824]), w_in[l][:, 1824:]], axis=1).astype(BF16)
    row = lambda a: a.reshape(1, -1)
    pad_rows = lambda a, rows: jnp.pad(a, ((0, rows - a.shape[0]), (0, 0)))
    bd = jnp.kron(jnp.eye(N_HEADS, dtype=F32), jnp.ones((HEAD, HEAD), F32)).astype(BF16)
    pw = pool_w[l]
    pw_bd = jnp.zeros((POOL_W, POOL_W), F32)
    for gi in range(4):
        pw_bd = pw_bd.at[gi * 64:(gi + 1) * 64, gi * 64:(gi + 1) * 64].set(pw[gi])
    wts = {
        "prep": (_pack_shift_cols(row(mu_shift[l])), row(w0[l]), row(a0[l]), pad_rows(w_up_decay[l], LORA_PAD),
                 pad_rows(w_up_aaa[l], LORA_PAD), pad_rows(w_up_gate[l], GATE_PAD), row(k_k[l]), row(k_a[l]),
                 row(r_k[l]), bd),
        "pool_w": pw_bd.astype(BF16), "pool_scale": row(pool_scale[l]),
        "post": (w_out[l].astype(BF16), row(ln_x_w[l]), row(ln_x_b[l]), row(ln1_g[l]), row(ln1_b[l]), bd),
        "alpha": alpha,
    }

    xp2 = x_prompt.reshape(n_p, d)
    xs2 = x_sample.reshape(n_s, d)
    z_p = _matmul(xp2, w_in_p, _tile(n_p, 512))
    z_s = _matmul(jnp.concatenate([xs2, state_shift[l]], axis=0), w_in_p, _tile(n_s + bs, 512))
    mkv = _matmul(mem_prompt.reshape(bp * MEM_TOKENS, d),
                  jnp.concatenate([mem_wk[l], mem_wv[l]], axis=1).astype(BF16), _tile(bp * MEM_TOKENS, 512))
    mk_p = mkv[:, :MEM_W].reshape(bp, MEM_TOKENS, MEM_W)
    mv_p = mkv[:, MEM_W:].reshape(bp, MEM_TOKENS, MEM_W)

    h_p, wkv_p, pool_p = _mixer(z_p, jnp.zeros((bp, ZS_W), F32), xp2, bp, tp,
                                jnp.zeros((bp, N_HEADS, HEAD, HEAD), F32), jnp.zeros((bp, POOL_BUF, POOL_W), F32),
                                mk_p, mv_p, 0, wts,
                                (_tile(n_p, 256), _tile(tp, 32, 1), 1, _tile(tp, 256)))
    h_s, wkv_s, pool_s = _mixer(z_s, z_s[n_s:, :ZS_W], xs2, bs, ts, state_wkv[l], state_pool[l],
                                cache_mem_k[l].reshape(bs, MEM_TOKENS, MEM_W),
                                cache_mem_v[l].reshape(bs, MEM_TOKENS, MEM_W), PAST_LEN, wts,
                                (_tile(n_s, 256), _tile(ts, 32, 1), _tile(bs, 16, 1), _tile(ts, 256)))

    h = jnp.concatenate([h_p, h_s], axis=0)
    td = _tile(n, 256, LANES)
    slots, gate, run_tab, cnt = _router(h, router_w[l].T, router_b[l].reshape(N_EXPERTS, 1), td)
    counts = cnt[:, 0].astype(I32)
    padded = (counts + CHUNK - 1) // CHUNK * CHUNK
    pad_end = jnp.cumsum(padded)
    pad_start = pad_end - padded
    n_chunks = (n * TOP_K + N_EXPERTS * (CHUNK - 1) + CHUNK - 1) // CHUNK
    run_tab = run_tab.reshape(n // td, N_EXPERTS, LANES)[:, :, :RUN_FIELDS].astype(I32)
    runs = jnp.stack([run_tab[:, :, 1], pad_start[None, :] + run_tab[:, :, 2], run_tab[:, :, 0]], axis=1)
    runs = runs.reshape(n // td * RUN_FIELDS, N_EXPERTS)
    zpos = jnp.where(padded > 0, pad_end - CHUNK, -1).astype(I32)
    chunk_start = jnp.arange(n_chunks, dtype=I32) * CHUNK
    blk_exp = jnp.minimum(jnp.sum((pad_end[None, :] <= chunk_start[:, None]).astype(I32), axis=1), N_EXPERTS - 1)
    n_used = (pad_end[-1:] // CHUNK).astype(I32)
    xs_sorted = _dispatch(zpos, runs, slots, h, n_chunks * CHUNK, td)
    ys_sorted = _ffn(blk_exp, n_used, xs_sorted, exp_gate[l], exp_up[l], exp_down[l])
    y = _combine(runs, slots.T, gate.T, h, ys_sorted, sh_gate[l].astype(BF16), sh_up[l].astype(BF16),
                 sh_down[l].astype(BF16), row(ln2_g[l]), row(ln2_b[l]), td, alpha)

    return (y[:n_p].reshape(bp, tp, d), y[n_p:].reshape(bs, ts, d),
            wkv_p[None], x_prompt[:, -1][None], pool_p[None],
            mk_p.reshape(bp, MEM_TOKENS, MEM_HEADS, HEAD)[None], mv_p.reshape(bp, MEM_TOKENS, MEM_HEADS, HEAD)[None],
            wkv_s[None], x_sample[:, -1][None], pool_s[None])
```

```python
import functools

import jax
import jax.numpy as jnp
from jax import lax
from jax.experimental import pallas as pl
from jax.experimental.pallas import tpu as pltpu

F32 = jnp.float32
BF16 = jnp.bfloat16
I32 = jnp.int32
HIGHEST = lax.Precision.HIGHEST

D_MODEL = 1024
HEAD = 64
N_HEADS = 8
RWKV_W = 512
POOL_W = 256
MEM_W = 256
MEM_HEADS = 4
MEM_TOKENS = 256
POOL_BUF = 15
POOL_PAD = 16
DECAY_LORA = 64
AAA_LORA = 64
GATE_LORA = 160
LORA_PAD = 128
GATE_PAD = 256
ZS_W = 3 * RWKV_W + 2 * LORA_PAD + GATE_PAD
Z_W = ZS_W + POOL_W + MEM_W
N_EXPERTS = 256
TOP_K = 8
N_GROUPS = 8
GROUP_SIZE = N_EXPERTS // N_GROUPS
TOPK_GROUPS = 4
EXPERT_FF = 256
ROUTED_SCALE = 2.5
CHUNK = 256
PAST_LEN = 16384
LN_EPS = 1e-5
GN_EPS = 64e-5
LANES = 128
SUBLANES = 8
VMEM_LIMIT = 48 * 1024 * 1024


def _cparams(n_axes):
    return pltpu.CompilerParams(dimension_semantics=("arbitrary",) * n_axes, vmem_limit_bytes=VMEM_LIMIT)


def _tile(n, preferred, multiple=8):
    best = None
    for c in range(multiple, min(n, preferred) + 1, multiple):
        if n % c == 0:
            best = c
    assert best is not None, (n, preferred, multiple)
    return best


def _sigmoid(x):
    return 1.0 / (1.0 + jnp.exp(-x))


def _matmul_kernel(x_ref, w_ref, o_ref):
    o_ref[...] = jnp.dot(x_ref[...].astype(BF16), w_ref[...], preferred_element_type=F32)


def _matmul(x, w, tm):
    m, k = x.shape
    n = w.shape[1]
    assert m % tm == 0
    return pl.pallas_call(
        _matmul_kernel,
        grid=(m // tm,),
        in_specs=[pl.BlockSpec((tm, k), lambda i: (i, 0)), pl.BlockSpec((k, n), lambda i: (0, 0))],
        out_specs=pl.BlockSpec((tm, n), lambda i: (i, 0)),
        out_shape=jax.ShapeDtypeStruct((m, n), F32),
        compiler_params=_cparams(1),
        name="matmul",
    )(x, w)


def _segsum(x, bd):
    hi = x.astype(BF16)
    lo = (x - hi.astype(F32)).astype(BF16)
    return jnp.dot(hi, bd, preferred_element_type=F32) + jnp.dot(lo, bd, preferred_element_type=F32)


def _prep_kernel(zs_ref, zp_ref, mu_ref, w0_ref, a0_ref, wd_ref, wa_ref, wg_ref, kk_ref, ka_ref, rk_ref, bd_ref,
                 r_o, w_o, k_o, v_o, kk_o, nb_o, g_o, bonus_o, carry_ref, *, tiles_per_seq, seq_len):
    zs = zs_ref[...]
    rolled = pltpu.roll(zs, 1, 0)
    row = lax.broadcasted_iota(I32, zs.shape, 0)
    if tiles_per_seq:
        i = pl.program_id(0)

        @pl.when(i == 0)
        def _():
            carry_ref[...] = jnp.zeros_like(carry_ref)

        prev = jnp.where(i % tiles_per_seq == 0, zp_ref[0], carry_ref[...])
        shifted = jnp.where(row == 0, prev, rolled)
        carry_ref[...] = zs[zs.shape[0] - 1:, :]
    else:
        shifted = jnp.where((row & (seq_len - 1)) == 0, zp_ref[...], rolled)
    zm = zs + mu_ref[...] * (shifted - zs)
    r = zm[:, 0:RWKV_W]
    k = zm[:, RWKV_W:2 * RWKV_W]
    v = zm[:, 2 * RWKV_W:3 * RWKV_W]
    c0 = 3 * RWKV_W
    xw = zm[:, c0:c0 + LORA_PAD]
    xa = zm[:, c0 + LORA_PAD:c0 + 2 * LORA_PAD]
    xg = zm[:, c0 + 2 * LORA_PAD:c0 + 2 * LORA_PAD + GATE_PAD]
    bd = bd_ref[...]
    u = -(w0_ref[...] + jnp.dot(jnp.tanh(xw), wd_ref[...], precision=HIGHEST, preferred_element_type=F32))
    softplus = jnp.maximum(u, 0.0) + jnp.log(1.0 + jnp.exp(-jnp.abs(u)))
    decay = jnp.exp(-jnp.exp(-softplus - 0.5))
    a = _sigmoid(a0_ref[...] + jnp.dot(xa, wa_ref[...], precision=HIGHEST, preferred_element_type=F32))
    g = jnp.dot(_sigmoid(xg), wg_ref[...], precision=HIGHEST, preferred_element_type=F32)
    kk = k * kk_ref[...]
    kk = kk / jnp.maximum(jnp.sqrt(_segsum(kk * kk, bd)), 1e-12)
    kp = k * (1.0 + (a - 1.0) * ka_ref[...])
    r_o[...] = r
    w_o[...] = decay
    k_o[...] = kp
    v_o[...] = v
    kk_o[...] = kk
    nb_o[...] = -(kk * a)
    g_o[...] = g
    bonus_o[...] = _segsum(r * kp * rk_ref[...], bd) * v


def _rwkv_prep(z, zprev, b, t, tt, params):
    n_tok = b * t
    assert n_tok % tt == 0
    row = lambda w: pl.BlockSpec((1, w), lambda i: (0, 0))
    full = lambda a: pl.BlockSpec(a.shape, lambda i: (0, 0))
    tok = pl.BlockSpec((tt, RWKV_W), lambda i: (i, 0))
    mu, w0, a0, wd, wa, wg, k_k, k_a, r_k, bd = params
    if t % tt == 0:
        tiles_per_seq = t // tt
        zp = zprev.reshape(b, 1, ZS_W)
        zp_spec = pl.BlockSpec((1, 1, ZS_W), lambda i: (i // tiles_per_seq, 0, 0))
    else:
        assert tt % t == 0 and t & (t - 1) == 0
        tiles_per_seq = 0
        zp = jnp.repeat(zprev, t, axis=0)
        zp_spec = pl.BlockSpec((tt, ZS_W), lambda i: (i, 0))
    return pl.pallas_call(
        functools.partial(_prep_kernel, tiles_per_seq=tiles_per_seq, seq_len=t),
        grid=(n_tok // tt,),
        in_specs=[pl.BlockSpec((tt, ZS_W), lambda i: (i, 0)), zp_spec,
                  row(ZS_W), row(RWKV_W), row(RWKV_W), full(wd), full(wa), full(wg),
                  row(RWKV_W), row(RWKV_W), row(RWKV_W), full(bd)],
        out_specs=[tok] * 8,
        out_shape=[jax.ShapeDtypeStruct((n_tok, RWKV_W), F32)] * 8,
        scratch_shapes=[pltpu.VMEM((1, ZS_W), F32)],
        compiler_params=_cparams(1),
        name="rwkv_prep",
    )(z, zp, mu, w0, a0, wd, wa, wg, k_k, k_a, r_k, bd)


def _wkv_kernel(r_ref, w_ref, k_ref, kk_ref, nb_ref, v_ref, s0_ref, y_ref, s_ref, *scratch, tb, dup):
    slabs = s_ref.shape[1]
    key_unroll = 8

    @pl.when(pl.program_id(1) == 0)
    def _():
        s_ref[...] = s0_ref[...]

    if dup:
        for src, dst in zip((r_ref, w_ref, k_ref, kk_ref, nb_ref), scratch):
            x = src[...]
            dst[...] = jnp.concatenate([x, x], axis=-1)
        r_ref, w_ref, k_ref, kk_ref, nb_ref = scratch

    def rows(ref, t, k):
        return jnp.broadcast_to(ref[t, pl.ds(k, 1), :], (SUBLANES, LANES))

    def step(t, sa, with_next):
        vt = [v_ref[t, i * SUBLANES:(i + 1) * SUBLANES, :] for i in range(slabs)]
        zero = jnp.zeros((SUBLANES, LANES), F32)

        def key(k, acc):
            y_acc, next_acc = list(acc[0]), list(acc[1])
            wb, nbb, kb, rb = rows(w_ref, t, k), rows(nb_ref, t, k), rows(k_ref, t, k), rows(r_ref, t, k)
            kkb = rows(kk_ref, t + 1, k) if with_next else None
            for i in range(slabs):
                sn = s_ref[k, i] * wb + sa[i] * nbb + vt[i] * kb
                s_ref[k, i] = sn
                y_acc[i] = y_acc[i] + sn * rb
                if with_next:
                    next_acc[i] = next_acc[i] + sn * kkb
            return tuple(y_acc), tuple(next_acc)

        y_acc, next_acc = lax.fori_loop(0, HEAD, key, ((zero,) * slabs, (zero,) * slabs), unroll=key_unroll)
        for i, y in enumerate(y_acc):
            y_ref[t, i * SUBLANES:(i + 1) * SUBLANES, :] = y
        return next_acc

    def first(k, acc):
        kkb = rows(kk_ref, 0, k)
        return tuple(a + s_ref[k, i] * kkb for i, a in enumerate(acc))

    sa0 = lax.fori_loop(0, HEAD, first, (jnp.zeros((SUBLANES, LANES), F32),) * slabs, unroll=key_unroll)
    sa_last = lax.fori_loop(0, tb - 1, lambda t, sa: step(t, sa, True), sa0)
    step(tb - 1, sa_last, False)


def _wkv(r, w, k, kk, nb, v, s0, tb):
    t, _, lr = r.shape
    vr, l = v.shape[1], v.shape[2]
    dup = lr != l
    assert t % tb == 0 and l % LANES == 0 and vr % SUBLANES == 0 and (not dup or (l == LANES and 2 * lr == l))
    vec = pl.BlockSpec((tb, HEAD, lr if dup else LANES), lambda g, i: (i, 0, g))
    val = pl.BlockSpec((tb, vr, LANES), lambda g, i: (i, 0, g))
    slabs = vr // SUBLANES
    st = pl.BlockSpec((HEAD, slabs, SUBLANES, LANES), lambda g, i: (0, 0, 0, g))
    s0 = s0.reshape(HEAD, slabs, SUBLANES, l)
    y, s_new = pl.pallas_call(
        functools.partial(_wkv_kernel, tb=tb, dup=dup),
        grid=(l // LANES, t // tb),
        in_specs=[vec, vec, vec, vec, vec, val, st],
        out_specs=[val, st],
        out_shape=[jax.ShapeDtypeStruct(v.shape, F32), jax.ShapeDtypeStruct(s0.shape, F32)],
        scratch_shapes=[pltpu.VMEM((tb, HEAD, LANES), F32)] * 5 if dup else [],
        compiler_params=_cparams(2),
        name="wkv",
    )(r, w, k, kk, nb, v, s0)
    return y, s_new.reshape(HEAD, vr, l)


def _pool_kernel(full_ref, pw_ref, ps_ref, o_ref, *, t, pos0):
    bb = full_ref.shape[0]
    lane = lax.broadcasted_iota(I32, (t, POOL_W), 1)
    window = jnp.where(lane < 64, 2, jnp.where(lane < 128, 4, jnp.where(lane < 192, 8, 16)))
    pos = lax.broadcasted_iota(I32, (t, POOL_W), 0) + (pos0 + 1)
    cnt = jnp.minimum(pos, window).astype(F32)
    diffs = []
    for i in range(bb):
        f = full_ref[i]
        s2 = f + pltpu.roll(f, 1, 0)
        s4 = s2 + pltpu.roll(s2, 2, 0)
        s8 = s4 + pltpu.roll(s4, 4, 0)
        s16 = s8 + pltpu.roll(s8, 8, 0)
        wsum = jnp.where(lane < 64, s2[POOL_PAD:], jnp.where(lane < 128, s4[POOL_PAD:],
                                                            jnp.where(lane < 192, s8[POOL_PAD:], s16[POOL_PAD:])))
        diffs.append(wsum / cnt - f[POOL_PAD:])
    diff = diffs[0] if bb == 1 else jnp.concatenate(diffs, axis=0)
    out = jnp.dot(diff.astype(BF16), pw_ref[...], preferred_element_type=F32) * ps_ref[...]
    for i in range(bb):
        o_ref[i] = out[i * t:(i + 1) * t]


def _pool(full, pw_bd, pscale, bb, pos0):
    b, rows, _ = full.shape
    t = rows - POOL_PAD
    assert b % bb == 0
    return pl.pallas_call(
        functools.partial(_pool_kernel, t=t, pos0=pos0),
        grid=(b // bb,),
        in_specs=[pl.BlockSpec((bb, rows, POOL_W), lambda i: (i, 0, 0)),
                  pl.BlockSpec((POOL_W, POOL_W), lambda i: (0, 0)), pl.BlockSpec((1, POOL_W), lambda i: (0, 0))],
        out_specs=pl.BlockSpec((bb, t, POOL_W), lambda i: (i, 0, 0)),
        out_shape=jax.ShapeDtypeStruct((b, t, POOL_W), F32),
        compiler_params=_cparams(1),
        name="pool",
    )(full, pw_bd, pscale)


def _memattn_kernel(q_ref, k_ref, v_ref, o_ref):
    q = q_ref[0].astype(BF16)
    kf = k_ref[0]
    vf = v_ref[0]
    head_of_lane = lax.broadcasted_iota(I32, kf.shape, 1) // (MEM_W // MEM_HEADS)
    out = None
    for h in range(MEM_HEADS):
        kh = jnp.where(head_of_lane == h, kf, 0.0).astype(BF16)
        vh = jnp.where(head_of_lane == h, vf, 0.0).astype(BF16)
        s = lax.dot_general(q, kh, (((1,), (1,)), ((), ())), preferred_element_type=F32) * (64 ** -0.5)
        e = jnp.exp(s - jnp.max(s, axis=-1, keepdims=True))
        p = e / jnp.sum(e, axis=-1, keepdims=True)
        o = jnp.dot(p.astype(BF16), vh, preferred_element_type=F32)
        out = o if out is None else out + o
    o_ref[0] = out


def _memattn(z3, mk, mv, tt):
    b, t, _ = z3.shape
    assert t % tt == 0
    qcol = (Z_W - MEM_W) // MEM_W
    kv = pl.BlockSpec((1, MEM_TOKENS, MEM_W), lambda i, j: (i, 0, 0))
    return pl.pallas_call(
        _memattn_kernel,
        grid=(b, t // tt),
        in_specs=[pl.BlockSpec((1, tt, MEM_W), lambda i, j: (i, j, qcol)), kv, kv],
        out_specs=pl.BlockSpec((1, tt, MEM_W), lambda i, j: (i, j, 0)),
        out_shape=jax.ShapeDtypeStruct((b, t, MEM_W), F32),
        compiler_params=_cparams(2),
        name="mem_attn",
    )(z3, mk, mv)


def _layer_norm(x, g, b):
    mu = jnp.mean(x, axis=-1, keepdims=True)
    xc = x - mu
    var = jnp.mean(xc * xc, axis=-1, keepdims=True)
    return xc * lax.rsqrt(var + LN_EPS) * g + b


def _post_kernel(y_ref, bonus_ref, g_ref, op_ref, om_ref, x_ref, wo_ref, lxw_ref, lxb_ref, l1g_ref, l1b_ref, bd_ref,
                 h_ref, *, alpha):
    y = y_ref[...]
    bd = bd_ref[...]
    mu = _segsum(y, bd) * (1.0 / HEAD)
    yc = y - mu
    var = _segsum(yc * yc, bd) * (1.0 / HEAD)
    yn = yc * lax.rsqrt(var + GN_EPS) * lxw_ref[...] + lxb_ref[...]
    o_rwkv = (yn + bonus_ref[...]) * g_ref[...]
    mixed = (jnp.dot(o_rwkv.astype(BF16), wo_ref[0:RWKV_W, :], preferred_element_type=F32)
             + jnp.dot(op_ref[...].astype(BF16), wo_ref[RWKV_W:RWKV_W + POOL_W, :], preferred_element_type=F32)
             + jnp.dot(om_ref[...].astype(BF16), wo_ref[RWKV_W + POOL_W:, :], preferred_element_type=F32))
    h_ref[...] = _layer_norm(alpha * x_ref[...] + mixed, l1g_ref[...], l1b_ref[...])


def _post(y, bonus, g, o_pool, o_mem, x, wo, lxw, lxb, l1g, l1b, bd, tt, alpha):
    n = y.shape[0]
    assert n % tt == 0
    tok = lambda w: pl.BlockSpec((tt, w), lambda i: (i, 0))
    row = lambda w: pl.BlockSpec((1, w), lambda i: (0, 0))
    full = lambda a: pl.BlockSpec(a.shape, lambda i: (0, 0))
    return pl.pallas_call(
        functools.partial(_post_kernel, alpha=alpha),
        grid=(n // tt,),
        in_specs=[tok(RWKV_W), tok(RWKV_W), tok(RWKV_W), tok(POOL_W), tok(MEM_W), tok(D_MODEL), full(wo),
                  row(RWKV_W), row(RWKV_W), row(D_MODEL), row(D_MODEL), full(bd)],
        out_specs=tok(D_MODEL),
        out_shape=jax.ShapeDtypeStruct((n, D_MODEL), F32),
        compiler_params=_cparams(1),
        name="post",
    )(y, bonus, g, o_pool, o_mem, x, wo, lxw, lxb, l1g, l1b, bd)


def _router_kernel(h_ref, rwt_ref, bias_ref, tri_ref, below_ref, slot_o, gate_o, run_o, cnt_o, carry_ref):
    @pl.when(pl.program_id(0) == 0)
    def _():
        carry_ref[...] = jnp.zeros_like(carry_ref)

    neg = -jnp.inf
    logits = lax.dot_general(rwt_ref[...], h_ref[...], (((1,), (1,)), ((), ())),
                             precision=HIGHEST, preferred_element_type=F32)
    scores = _sigmoid(logits)
    sel = scores + bias_ref[...]
    tt = sel.shape[1]
    gio = lax.broadcasted_iota(I32, (GROUP_SIZE, tt), 0).astype(F32)
    blocks, gscore = [], []
    for g in range(N_GROUPS):
        blk = sel[g * GROUP_SIZE:(g + 1) * GROUP_SIZE, :]
        m1 = jnp.max(blk, axis=0, keepdims=True)
        first = jnp.min(jnp.where(blk == m1, gio, float(GROUP_SIZE)), axis=0, keepdims=True)
        m2 = jnp.max(jnp.where(gio == first, neg, blk), axis=0, keepdims=True)
        blocks.append(blk)
        gscore.append(m1 + m2)
    masked = []
    for g in range(N_GROUPS):
        beaten_by = jnp.zeros((1, tt), F32)
        for g2 in range(N_GROUPS):
            if g2 != g:
                wins = (gscore[g2] >= gscore[g]) if g2 < g else (gscore[g2] > gscore[g])
                beaten_by = beaten_by + jnp.where(wins, 1.0, 0.0)
        masked.append(jnp.where(beaten_by < TOPK_GROUPS, blocks[g], neg))
    msel = jnp.concatenate(masked, axis=0)
    eio = lax.broadcasted_iota(I32, msel.shape, 0).astype(F32)
    chosen = jnp.zeros(msel.shape, F32)
    idxs, scs = [], []
    for _ in range(TOP_K):
        m = jnp.max(msel, axis=0, keepdims=True)
        first = jnp.min(jnp.where(msel == m, eio, float(N_EXPERTS)), axis=0, keepdims=True)
        hit = eio == first
        scs.append(jnp.sum(jnp.where(hit, scores, 0.0), axis=0, keepdims=True))
        msel = jnp.where(hit, neg, msel)
        chosen = jnp.where(hit, 1.0, chosen)
        idxs.append(first)
    total = scs[0]
    for s in scs[1:]:
        total = total + s
    chosen_b = chosen.astype(BF16)
    earlier = jnp.dot(chosen_b, tri_ref[...], preferred_element_type=F32)
    smaller = jnp.dot(below_ref[...], chosen_b, preferred_element_type=F32)
    run_len = jnp.sum(chosen, axis=1, keepdims=True)
    run_off = jnp.sum(smaller, axis=1, keepdims=True)
    slot_of = run_off + earlier
    slots = [jnp.sum(jnp.where(eio == i, slot_of, 0.0), axis=0, keepdims=True) for i in idxs]
    slot_o[...] = jnp.concatenate(slots, axis=0).astype(I32)
    gate_o[...] = jnp.concatenate([s / total * ROUTED_SCALE for s in scs], axis=0)
    lane = lax.broadcasted_iota(I32, run_o.shape, 1)
    run_o[...] = jnp.where(lane == 0, run_len, jnp.where(lane == 1, run_off, jnp.where(lane == 2, carry_ref[...], 0.0)))
    carry_ref[...] = carry_ref[...] + run_len
    cnt_o[...] = jnp.broadcast_to(carry_ref[...], cnt_o.shape)


def _router(h, rwt, bias, tt):
    n = h.shape[0]
    assert n % tt == 0
    tri = (lax.broadcasted_iota(I32, (tt, tt), 0) < lax.broadcasted_iota(I32, (tt, tt), 1)).astype(BF16)
    below = (lax.broadcasted_iota(I32, (N_EXPERTS, N_EXPERTS), 1)
             < lax.broadcasted_iota(I32, (N_EXPERTS, N_EXPERTS), 0)).astype(BF16)
    tokT = pl.BlockSpec((TOP_K, tt), lambda i: (0, i))
    return pl.pallas_call(
        _router_kernel,
        grid=(n // tt,),
        in_specs=[pl.BlockSpec((tt, D_MODEL), lambda i: (i, 0)), pl.BlockSpec((N_EXPERTS, D_MODEL), lambda i: (0, 0)),
                  pl.BlockSpec((N_EXPERTS, 1), lambda i: (0, 0)), pl.BlockSpec((tt, tt), lambda i: (0, 0)),
                  pl.BlockSpec((N_EXPERTS, N_EXPERTS), lambda i: (0, 0))],
        out_specs=[tokT, tokT, pl.BlockSpec((N_EXPERTS, LANES), lambda i: (i, 0)),
                   pl.BlockSpec((N_EXPERTS, LANES), lambda i: (0, 0))],
        out_shape=[jax.ShapeDtypeStruct((TOP_K, n), I32), jax.ShapeDtypeStruct((TOP_K, n), F32),
                   jax.ShapeDtypeStruct((n // tt * N_EXPERTS, LANES), F32),
                   jax.ShapeDtypeStruct((N_EXPERTS, LANES), F32)],
        scratch_shapes=[pltpu.VMEM((N_EXPERTS, 1), F32)],
        compiler_params=_cparams(1),
        name="router",
    )(h, rwt, bias, tri, below)


ROW_SUB = D_MODEL // LANES


def _store_row_tiles(ref, x):
    rows = x.shape[0]
    for c in range(ROW_SUB):
        ref[pl.ds(c, rows, stride=ROW_SUB), :] = x[:, c * LANES:(c + 1) * LANES]


def _load_row_tiles(ref):
    rows = ref.shape[0] // ROW_SUB
    return jnp.concatenate([ref[pl.ds(c, rows, stride=ROW_SUB), :] for c in range(ROW_SUB)], axis=-1)


RUN_FIELDS = 3
RUN_UNROLL = 8


def _start_run_copies(runs_ref, tile, local_ref, sorted_ref, sem, to_sorted):
    def body(e, c):
        size = pl.multiple_of(runs_ref[tile * RUN_FIELDS + 2, e], ROW_SUB)
        local = local_ref.at[pl.ds(pl.multiple_of(runs_ref[tile * RUN_FIELDS, e], ROW_SUB), size), :]
        remote = sorted_ref.at[pl.ds(pl.multiple_of(runs_ref[tile * RUN_FIELDS + 1, e], ROW_SUB), size), :]
        (pltpu.make_async_copy(local, remote, sem) if to_sorted else pltpu.make_async_copy(remote, local, sem)).start()
        return c

    lax.fori_loop(0, N_EXPERTS, body, 0, unroll=RUN_UNROLL)


def _wait_run_copies(local_ref, sorted_ref, sem):
    pltpu.make_async_copy(sorted_ref.at[pl.ds(0, local_ref.shape[0]), :], local_ref, sem).wait()


def _dispatch_kernel(zpos_ref, runs_ref, slot_ref, h_ref, xs_hbm, zero_buf, rows_buf, sem_zero, sem_rows, *, td):
    i = pl.program_id(0)

    def zero_copy(e):
        start = pl.multiple_of(zpos_ref[e] * ROW_SUB, CHUNK * ROW_SUB)
        return pltpu.make_async_copy(zero_buf, xs_hbm.at[pl.ds(start, CHUNK * ROW_SUB), :], sem_zero)

    @pl.when(i == 0)
    def _():
        zero_buf[...] = jnp.zeros_like(zero_buf)

        def start(e, c):
            @pl.when(zpos_ref[e] >= 0)
            def _():
                zero_copy(e).start()
            return c

        def wait(e, c):
            @pl.when(zpos_ref[e] >= 0)
            def _():
                zero_copy(e).wait()
            return c

        lax.fori_loop(0, N_EXPERTS, start, 0)
        lax.fori_loop(0, N_EXPERTS, wait, 0)

    slots = slot_ref[...]
    slot_iota = lax.broadcasted_iota(I32, (TOP_K * td, td), 0)
    select = jnp.zeros((TOP_K * td, td), F32)
    for j in range(TOP_K):
        select = jnp.where(slot_iota == slots[j:j + 1, :], 1.0, select)
    local = jnp.dot(select.astype(BF16), h_ref[...].astype(BF16), preferred_element_type=F32)
    _store_row_tiles(rows_buf, local)
    _start_run_copies(runs_ref, i, rows_buf, xs_hbm, sem_rows, True)
    _wait_run_copies(rows_buf, xs_hbm, sem_rows)


def _dispatch(zpos, runs, slots, h, n_rows, td):
    n = h.shape[0]
    assert n % td == 0
    return pl.pallas_call(
        functools.partial(_dispatch_kernel, td=td),
        grid_spec=pltpu.PrefetchScalarGridSpec(
            num_scalar_prefetch=2,
            grid=(n // td,),
            in_specs=[pl.BlockSpec((TOP_K, td), lambda i, z, r: (0, i)),
                      pl.BlockSpec((td, D_MODEL), lambda i, z, r: (i, 0))],
            out_specs=pl.BlockSpec(memory_space=pl.ANY),
            scratch_shapes=[pltpu.VMEM((CHUNK * ROW_SUB, LANES), F32), pltpu.VMEM((TOP_K * td * ROW_SUB, LANES), F32),
                            pltpu.SemaphoreType.DMA, pltpu.SemaphoreType.DMA],
        ),
        out_shape=jax.ShapeDtypeStruct((n_rows * ROW_SUB, LANES), F32),
        compiler_params=_cparams(1),
        name="dispatch",
    )(zpos, runs, slots, h)


def _ffn_kernel(be_ref, nu_ref, x_ref, wg_ref, wu_ref, wd_ref, o_ref):
    @pl.when(pl.program_id(0) < nu_ref[0])
    def _():
        x = _load_row_tiles(x_ref).astype(BF16)
        gate = jnp.dot(x, wg_ref[0].astype(BF16), preferred_element_type=F32)
        up = jnp.dot(x, wu_ref[0].astype(BF16), preferred_element_type=F32)
        act = (gate * _sigmoid(gate)) * up
        _store_row_tiles(o_ref, jnp.dot(act.astype(BF16), wd_ref[0].astype(BF16), preferred_element_type=F32))


def _ffn(blk_exp, n_used, xs, wg, wu, wd):
    n_chunks = xs.shape[0] // (CHUNK * ROW_SUB)
    rows = pl.BlockSpec((CHUNK * ROW_SUB, LANES), lambda c, be, nu: (jnp.minimum(c, nu[0] - 1), 0))
    return pl.pallas_call(
        _ffn_kernel,
        grid_spec=pltpu.PrefetchScalarGridSpec(
            num_scalar_prefetch=2,
            grid=(n_chunks,),
            in_specs=[rows,
                      pl.BlockSpec((1, D_MODEL, EXPERT_FF), lambda c, be, nu: (be[c], 0, 0)),
                      pl.BlockSpec((1, D_MODEL, EXPERT_FF), lambda c, be, nu: (be[c], 0, 0)),
                      pl.BlockSpec((1, EXPERT_FF, D_MODEL), lambda c, be, nu: (be[c], 0, 0))],
            out_specs=rows,
        ),
        out_shape=jax.ShapeDtypeStruct(xs.shape, F32),
        compiler_params=_cparams(1),
        name="expert_ffn",
    )(blk_exp, n_used, xs, wg, wu, wd)


def _combine_kernel(runs_ref, slot_ref, gate_ref, h_ref, ys_hbm, sg_ref, su_ref, sd_ref, l2g_ref, l2b_ref, o_ref,
                    rows_buf, sem_rows, *, tc, alpha):
    i = pl.program_id(0)
    _start_run_copies(runs_ref, i, rows_buf, ys_hbm, sem_rows, False)
    h = h_ref[...]
    hb = h.astype(BF16)
    sgate = jnp.dot(hb, sg_ref[...], preferred_element_type=F32)
    sup = jnp.dot(hb, su_ref[...], preferred_element_type=F32)
    shared = jnp.dot(((sgate * _sigmoid(sgate)) * sup).astype(BF16), sd_ref[...], preferred_element_type=F32)
    slots = slot_ref[...]
    gate = gate_ref[...]
    slot_iota = lax.broadcasted_iota(I32, (tc, TOP_K * tc), 1)
    weights = jnp.zeros((tc, TOP_K * tc), F32)
    for j in range(TOP_K):
        weights = jnp.where(slot_iota == slots[:, j:j + 1], gate[:, j:j + 1], weights)
    w_hi = weights.astype(BF16)
    w_lo = (weights - w_hi.astype(F32)).astype(BF16)
    _wait_run_copies(rows_buf, ys_hbm, sem_rows)
    local = _load_row_tiles(rows_buf).astype(BF16)
    routed = (jnp.dot(w_hi, local, preferred_element_type=F32) + jnp.dot(w_lo, local, preferred_element_type=F32))
    o_ref[...] = _layer_norm(alpha * h + (routed + shared), l2g_ref[...], l2b_ref[...])


def _combine(runs, slots, gate, h, ys, sg, su, sd, l2g, l2b, tc, alpha):
    n = h.shape[0]
    assert n % tc == 0
    tok = lambda w: pl.BlockSpec((tc, w), lambda i, r: (i, 0))
    full = lambda a: pl.BlockSpec(a.shape, lambda i, r: (0, 0))
    return pl.pallas_call(
        functools.partial(_combine_kernel, tc=tc, alpha=alpha),
        grid_spec=pltpu.PrefetchScalarGridSpec(
            num_scalar_prefetch=1,
            grid=(n // tc,),
            in_specs=[tok(TOP_K), tok(TOP_K), tok(D_MODEL), pl.BlockSpec(memory_space=pl.ANY),
                      full(sg), full(su), full(sd), full(l2g), full(l2b)],
            out_specs=tok(D_MODEL),
            scratch_shapes=[pltpu.VMEM((TOP_K * tc * ROW_SUB, LANES), F32), pltpu.SemaphoreType.DMA],
        ),
        out_shape=jax.ShapeDtypeStruct((n, D_MODEL), F32),
        compiler_params=_cparams(1),
        name="combine",
    )(runs, slots, gate, h, ys, sg, su, sd, l2g, l2b)


def _pad_cols(a, width):
    return jnp.pad(a, ((0, 0), (0, width - a.shape[1])))


def _pack_shift_cols(a):
    c = 3 * RWKV_W
    return jnp.concatenate([a[:, :c], _pad_cols(a[:, c:c + DECAY_LORA], LORA_PAD),
                            _pad_cols(a[:, c + DECAY_LORA:c + DECAY_LORA + AAA_LORA], LORA_PAD),
                            _pad_cols(a[:, c + DECAY_LORA + AAA_LORA:], GATE_PAD)], axis=1)


def _keys_to_lanes(a, b, t):
    return a.reshape(b, t, N_HEADS, HEAD).transpose(1, 3, 0, 2).reshape(t, HEAD, b * N_HEADS)


def _values_to_lanes(a, b, t, split):
    a = a.reshape(b, t, N_HEADS, split, HEAD // split).transpose(1, 4, 3, 0, 2)
    return a.reshape(t, HEAD // split, split * b * N_HEADS)


def _values_from_lanes(y, b, t, split):
    y = y.reshape(t, HEAD // split, split, b, N_HEADS).transpose(3, 0, 4, 2, 1)
    return y.reshape(b * t, RWKV_W)


def _state_to_lanes(s, b, split):
    s = s.reshape(b, N_HEADS, split, HEAD // split, HEAD).transpose(4, 3, 2, 0, 1)
    return s.reshape(HEAD, HEAD // split, split * b * N_HEADS)


def _state_from_lanes(s, b, split):
    s = s.reshape(HEAD, HEAD // split, split, b, N_HEADS).transpose(3, 4, 2, 1, 0)
    return s.reshape(b, N_HEADS, HEAD, HEAD)


def _mixer(z, zprev, x2, b, t, wkv0, pool_prev, mk, mv, pos0, wts, tiles):
    n = b * t
    tt, tb, pool_bb, att_tt = tiles
    z3 = z[:n].reshape(b, t, Z_W)
    r, w, kp, v, kk, nb, g, bonus = _rwkv_prep(z, zprev, b, t, tt, wts["prep"])

    split = 1 if (b * N_HEADS) % LANES == 0 else LANES // (b * N_HEADS)
    lay = lambda a: _keys_to_lanes(a, b, t)
    y_l, s_l = _wkv(lay(r), lay(w), lay(kp), lay(kk), lay(nb), _values_to_lanes(v, b, t, split),
                    _state_to_lanes(wkv0, b, split), tb)
    y = _values_from_lanes(y_l, b, t, split)
    wkv_new = _state_from_lanes(s_l, b, split)

    zp = z3[:, :, ZS_W:ZS_W + POOL_W]
    full = jnp.concatenate([jnp.zeros((b, POOL_PAD - POOL_BUF, POOL_W), F32), pool_prev, zp], axis=1)
    o_pool = _pool(full, wts["pool_w"], wts["pool_scale"], pool_bb, pos0).reshape(n, POOL_W)
    pool_new = full[:, -POOL_BUF:]

    o_mem = _memattn(z3, mk, mv, att_tt).reshape(n, MEM_W)
    h = _post(y, bonus, g, o_pool, o_mem, x2, *wts["post"], tt, wts["alpha"])
    return h, wkv_new, pool_new


def kernel(x_prompt, x_sample, mem_prompt, state_wkv, state_shift, state_pool, cache_mem_k, cache_mem_v, w_in, mu_shift, w0, w_up_decay, a0, w_up_aaa, w_up_gate, k_k, k_a, r_k, ln_x_w, ln_x_b, pool_w, pool_scale, mem_wk, mem_wv, w_out, ln1_g, ln1_b, router_w, router_b, exp_gate, exp_up, exp_down, sh_gate, sh_up, sh_down, ln2_g, ln2_b):
    depth = w_in.shape[0]
    assert depth == 1
    l = 0
    alpha = (2.0 * depth) ** 0.25
    bp, tp, d = x_prompt.shape
    bs, ts, _ = x_sample.shape
    n_p, n_s = bp * tp, bs * ts
    n = n_p + n_s

    w_in_p = jnp.concatenate([_pack_shift_cols(w_in[l][:, :1824]), w_in[l][:, 1824:]], axis=1).astype(BF16)
    row = lambda a: a.reshape(1, -1)
    pad_rows = lambda a, rows: jnp.pad(a, ((0, rows - a.shape[0]), (0, 0)))
    bd = jnp.kron(jnp.eye(N_HEADS, dtype=F32), jnp.ones((HEAD, HEAD), F32)).astype(BF16)
    pw = pool_w[l]
    pw_bd = jnp.zeros((POOL_W, POOL_W), F32)
    for gi in range(4):
        pw_bd = pw_bd.at[gi * 64:(gi + 1) * 64, gi * 64:(gi + 1) * 64].set(pw[gi])
    wts = {
        "prep": (_pack_shift_cols(row(mu_shift[l])), row(w0[l]), row(a0[l]), pad_rows(w_up_decay[l], LORA_PAD),
                 pad_rows(w_up_aaa[l], LORA_PAD), pad_rows(w_up_gate[l], GATE_PAD), row(k_k[l]), row(k_a[l]),
                 row(r_k[l]), bd),
        "pool_w": pw_bd.astype(BF16), "pool_scale": row(pool_scale[l]),
        "post": (w_out[l].astype(BF16), row(ln_x_w[l]), row(ln_x_b[l]), row(ln1_g[l]), row(ln1_b[l]), bd),
        "alpha": alpha,
    }

    xp2 = x_prompt.reshape(n_p, d)
    xs2 = x_sample.reshape(n_s, d)
    z_p = _matmul(xp2, w_in_p, _tile(n_p, 512))
    z_s = _matmul(jnp.concatenate([xs2, state_shift[l]], axis=0), w_in_p, _tile(n_s + bs, 512))
    mkv = _matmul(mem_prompt.reshape(bp * MEM_TOKENS, d),
                  jnp.concatenate([mem_wk[l], mem_wv[l]], axis=1).astype(BF16), _tile(bp * MEM_TOKENS, 512))
    mk_p = mkv[:, :MEM_W].reshape(bp, MEM_TOKENS, MEM_W)
    mv_p = mkv[:, MEM_W:].reshape(bp, MEM_TOKENS, MEM_W)

    h_p, wkv_p, pool_p = _mixer(z_p, jnp.zeros((bp, ZS_W), F32), xp2, bp, tp,
                                jnp.zeros((bp, N_HEADS, HEAD, HEAD), F32), jnp.zeros((bp, POOL_BUF, POOL_W), F32),
                                mk_p, mv_p, 0, wts,
                                (_tile(n_p, 256), _tile(tp, 32, 1), 1, _tile(tp, 256)))
    h_s, wkv_s, pool_s = _mixer(z_s, z_s[n_s:, :ZS_W], xs2, bs, ts, state_wkv[l], state_pool[l],
                                cache_mem_k[l].reshape(bs, MEM_TOKENS, MEM_W),
                                cache_mem_v[l].reshape(bs, MEM_TOKENS, MEM_W), PAST_LEN, wts,
                                (_tile(n_s, 256), _tile(ts, 32, 1), _tile(bs, 16, 1), _tile(ts, 256)))

    h = jnp.concatenate([h_p, h_s], axis=0)
    td = _tile(n, 256, LANES)
    slots, gate, run_tab, cnt = _router(h, router_w[l].T, router_b[l].reshape(N_EXPERTS, 1), td)
    counts = cnt[:, 0].astype(I32)
    padded = (counts + CHUNK - 1) // CHUNK * CHUNK
    pad_end = jnp.cumsum(padded)
    pad_start = pad_end - padded
    n_chunks = (n * TOP_K + N_EXPERTS * (CHUNK - 1) + CHUNK - 1) // CHUNK
    run_tab = run_tab.reshape(n // td, N_EXPERTS, LANES)[:, :, :RUN_FIELDS].astype(I32)
    runs = jnp.stack([run_tab[:, :, 1], pad_start[None, :] + run_tab[:, :, 2], run_tab[:, :, 0]], axis=1)
    runs = runs.reshape(n // td * RUN_FIELDS, N_EXPERTS) * ROW_SUB
    zpos = jnp.where(padded > 0, pad_end - CHUNK, -1).astype(I32)
    chunk_start = jnp.arange(n_chunks, dtype=I32) * CHUNK
    blk_exp = jnp.minimum(jnp.sum((pad_end[None, :] <= chunk_start[:, None]).astype(I32), axis=1), N_EXPERTS - 1)
    n_used = (pad_end[-1:] // CHUNK).astype(I32)
    xs_sorted = _dispatch(zpos, runs, slots, h, n_chunks * CHUNK, td)
    ys_sorted = _ffn(blk_exp, n_used, xs_sorted, exp_gate[l], exp_up[l], exp_down[l])
    y = _combine(runs, slots.T, gate.T, h, ys_sorted, sh_gate[l].astype(BF16), sh_up[l].astype(BF16),
                 sh_down[l].astype(BF16), row(ln2_g[l]), row(ln2_b[l]), td, alpha)

    return (y[:n_p].reshape(bp, tp, d), y[n_p:].reshape(bs, ts, d),
            wkv_p[None], x_prompt[:, -1][None], pool_p[None],
            mk_p.reshape(bp, MEM_TOKENS, MEM_HEADS, HEAD)[None], mv_p.reshape(bp, MEM_TOKENS, MEM_HEADS, HEAD)[None],
            wkv_s[None], x_sample[:, -1][None], pool_s[None])
```

```python
import functools

import jax
import jax.numpy as jnp
from jax import lax
from jax.experimental import pallas as pl
from jax.experimental.pallas import tpu as pltpu

F32 = jnp.float32
BF16 = jnp.bfloat16
I32 = jnp.int32
HIGHEST = lax.Precision.HIGHEST

D_MODEL = 1024
HEAD = 64
N_HEADS = 8
RWKV_W = 512
POOL_W = 256
MEM_W = 256
MEM_HEADS = 4
MEM_TOKENS = 256
POOL_BUF = 15
POOL_PAD = 16
DECAY_LORA = 64
AAA_LORA = 64
GATE_LORA = 160
LORA_PAD = 128
GATE_PAD = 256
ZS_W = 3 * RWKV_W + 2 * LORA_PAD + GATE_PAD
Z_W = ZS_W + POOL_W + MEM_W
N_EXPERTS = 256
TOP_K = 8
N_GROUPS = 8
GROUP_SIZE = N_EXPERTS // N_GROUPS
TOPK_GROUPS = 4
EXPERT_FF = 256
ROUTED_SCALE = 2.5
CHUNK = 256
PAST_LEN = 16384
LN_EPS = 1e-5
GN_EPS = 64e-5
LANES = 128
SUBLANES = 8
VMEM_LIMIT = 48 * 1024 * 1024


def _cparams(n_axes):
    return pltpu.CompilerParams(dimension_semantics=("arbitrary",) * n_axes, vmem_limit_bytes=VMEM_LIMIT)


def _tile(n, preferred, multiple=8):
    best = None
    for c in range(multiple, min(n, preferred) + 1, multiple):
        if n % c == 0:
            best = c
    assert best is not None, (n, preferred, multiple)
    return best


def _sigmoid(x):
    return 1.0 / (1.0 + jnp.exp(-x))


def _matmul_kernel(x_ref, w_ref, o_ref):
    o_ref[...] = jnp.dot(x_ref[...].astype(BF16), w_ref[...], preferred_element_type=F32)


def _matmul(x, w, tm):
    m, k = x.shape
    n = w.shape[1]
    assert m % tm == 0
    return pl.pallas_call(
        _matmul_kernel,
        grid=(m // tm,),
        in_specs=[pl.BlockSpec((tm, k), lambda i: (i, 0)), pl.BlockSpec((k, n), lambda i: (0, 0))],
        out_specs=pl.BlockSpec((tm, n), lambda i: (i, 0)),
        out_shape=jax.ShapeDtypeStruct((m, n), F32),
        compiler_params=_cparams(1),
        name="matmul",
    )(x, w)


def _segsum(x, bd):
    hi = x.astype(BF16)
    lo = (x - hi.astype(F32)).astype(BF16)
    return jnp.dot(hi, bd, preferred_element_type=F32) + jnp.dot(lo, bd, preferred_element_type=F32)


def _prep_kernel(zs_ref, zp_ref, mu_ref, w0_ref, a0_ref, wd_ref, wa_ref, wg_ref, kk_ref, ka_ref, rk_ref, bd_ref,
                 r_o, w_o, k_o, v_o, kk_o, nb_o, g_o, bonus_o, carry_ref, *, tiles_per_seq, seq_len):
    zs = zs_ref[...]
    rolled = pltpu.roll(zs, 1, 0)
    row = lax.broadcasted_iota(I32, zs.shape, 0)
    if tiles_per_seq:
        i = pl.program_id(0)

        @pl.when(i == 0)
        def _():
            carry_ref[...] = jnp.zeros_like(carry_ref)

        prev = jnp.where(i % tiles_per_seq == 0, zp_ref[0], carry_ref[...])
        shifted = jnp.where(row == 0, prev, rolled)
        carry_ref[...] = zs[zs.shape[0] - 1:, :]
    else:
        shifted = jnp.where((row & (seq_len - 1)) == 0, zp_ref[...], rolled)
    zm = zs + mu_ref[...] * (shifted - zs)
    r = zm[:, 0:RWKV_W]
    k = zm[:, RWKV_W:2 * RWKV_W]
    v = zm[:, 2 * RWKV_W:3 * RWKV_W]
    c0 = 3 * RWKV_W
    xw = zm[:, c0:c0 + LORA_PAD]
    xa = zm[:, c0 + LORA_PAD:c0 + 2 * LORA_PAD]
    xg = zm[:, c0 + 2 * LORA_PAD:c0 + 2 * LORA_PAD + GATE_PAD]
    bd = bd_ref[...]
    u = -(w0_ref[...] + jnp.dot(jnp.tanh(xw), wd_ref[...], precision=HIGHEST, preferred_element_type=F32))
    softplus = jnp.maximum(u, 0.0) + jnp.log(1.0 + jnp.exp(-jnp.abs(u)))
    decay = jnp.exp(-jnp.exp(-softplus - 0.5))
    a = _sigmoid(a0_ref[...] + jnp.dot(xa, wa_ref[...], precision=HIGHEST, preferred_element_type=F32))
    g = jnp.dot(_sigmoid(xg), wg_ref[...], precision=HIGHEST, preferred_element_type=F32)
    kk = k * kk_ref[...]
    kk = kk / jnp.maximum(jnp.sqrt(_segsum(kk * kk, bd)), 1e-12)
    kp = k * (1.0 + (a - 1.0) * ka_ref[...])
    r_o[...] = r
    w_o[...] = decay
    k_o[...] = kp
    v_o[...] = v
    kk_o[...] = kk
    nb_o[...] = -(kk * a)
    g_o[...] = g
    bonus_o[...] = _segsum(r * kp * rk_ref[...], bd) * v


def _rwkv_prep(z, zprev, b, t, tt, params):
    n_tok = b * t
    assert n_tok % tt == 0
    row = lambda w: pl.BlockSpec((1, w), lambda i: (0, 0))
    full = lambda a: pl.BlockSpec(a.shape, lambda i: (0, 0))
    tok = pl.BlockSpec((tt, RWKV_W), lambda i: (i, 0))
    mu, w0, a0, wd, wa, wg, k_k, k_a, r_k, bd = params
    if t % tt == 0:
        tiles_per_seq = t // tt
        zp = zprev.reshape(b, 1, ZS_W)
        zp_spec = pl.BlockSpec((1, 1, ZS_W), lambda i: (i // tiles_per_seq, 0, 0))
    else:
        assert tt % t == 0 and t & (t - 1) == 0
        tiles_per_seq = 0
        zp = jnp.repeat(zprev, t, axis=0)
        zp_spec = pl.BlockSpec((tt, ZS_W), lambda i: (i, 0))
    return pl.pallas_call(
        functools.partial(_prep_kernel, tiles_per_seq=tiles_per_seq, seq_len=t),
        grid=(n_tok // tt,),
        in_specs=[pl.BlockSpec((tt, ZS_W), lambda i: (i, 0)), zp_spec,
                  row(ZS_W), row(RWKV_W), row(RWKV_W), full(wd), full(wa), full(wg),
                  row(RWKV_W), row(RWKV_W), row(RWKV_W), full(bd)],
        out_specs=[tok] * 8,
        out_shape=[jax.ShapeDtypeStruct((n_tok, RWKV_W), F32)] * 8,
        scratch_shapes=[pltpu.VMEM((1, ZS_W), F32)],
        compiler_params=_cparams(1),
        name="rwkv_prep",
    )(z, zp, mu, w0, a0, wd, wa, wg, k_k, k_a, r_k, bd)


def _wkv_kernel(r_ref, w_ref, k_ref, kk_ref, nb_ref, v_ref, s0_ref, y_ref, s_ref, *scratch, tb, dup):
    slabs = s_ref.shape[1]
    key_unroll = 8

    @pl.when(pl.program_id(1) == 0)
    def _():
        s_ref[...] = s0_ref[...]

    if dup:
        for src, dst in zip((r_ref, w_ref, k_ref, kk_ref, nb_ref), scratch):
            x = src[...].reshape(tb * (HEAD // 2), LANES)
            swapped = pltpu.roll(x, LANES // 2, 1)
            low = lax.broadcasted_iota(I32, x.shape, 1) < LANES // 2
            dst[:, 0:HEAD // 2, :] = jnp.where(low, x, swapped).reshape(tb, HEAD // 2, LANES)
            dst[:, HEAD // 2:, :] = jnp.where(low, swapped, x).reshape(tb, HEAD // 2, LANES)
        r_ref, w_ref, k_ref, kk_ref, nb_ref = scratch

    def rows(ref, t, k):
        return jnp.broadcast_to(ref[t, pl.ds(k, 1), :], (SUBLANES, LANES))

    def step(t, sa, with_next):
        vt = [v_ref[t, i * SUBLANES:(i + 1) * SUBLANES, :] for i in range(slabs)]
        zero = jnp.zeros((SUBLANES, LANES), F32)

        def key(k, acc):
            y_acc, next_acc = list(acc[0]), list(acc[1])
            wb, nbb, kb, rb = rows(w_ref, t, k), rows(nb_ref, t, k), rows(k_ref, t, k), rows(r_ref, t, k)
            kkb = rows(kk_ref, t + 1, k) if with_next else None
            for i in range(slabs):
                sn = s_ref[k, i] * wb + sa[i] * nbb + vt[i] * kb
                s_ref[k, i] = sn
                y_acc[i] = y_acc[i] + sn * rb
                if with_next:
                    next_acc[i] = next_acc[i] + sn * kkb
            return tuple(y_acc), tuple(next_acc)

        y_acc, next_acc = lax.fori_loop(0, HEAD, key, ((zero,) * slabs, (zero,) * slabs), unroll=key_unroll)
        for i, y in enumerate(y_acc):
            y_ref[t, i * SUBLANES:(i + 1) * SUBLANES, :] = y
        return next_acc

    def first(k, acc):
        kkb = rows(kk_ref, 0, k)
        return tuple(a + s_ref[k, i] * kkb for i, a in enumerate(acc))

    sa0 = lax.fori_loop(0, HEAD, first, (jnp.zeros((SUBLANES, LANES), F32),) * slabs, unroll=key_unroll)
    sa_last = lax.fori_loop(0, tb - 1, lambda t, sa: step(t, sa, True), sa0)
    step(tb - 1, sa_last, False)


def _wkv(r, w, k, kk, nb, v, s0, tb):
    t, nk, _ = r.shape
    vr, l = v.shape[1], v.shape[2]
    dup = nk != HEAD
    assert t % tb == 0 and l % LANES == 0 and vr % SUBLANES == 0 and (not dup or (l == LANES and 2 * nk == HEAD))
    vec = pl.BlockSpec((tb, nk, LANES), lambda g, i: (i, 0, g))
    val = pl.BlockSpec((tb, vr, LANES), lambda g, i: (i, 0, g))
    slabs = vr // SUBLANES
    st = pl.BlockSpec((HEAD, slabs, SUBLANES, LANES), lambda g, i: (0, 0, 0, g))
    s0 = s0.reshape(HEAD, slabs, SUBLANES, l)
    y, s_new = pl.pallas_call(
        functools.partial(_wkv_kernel, tb=tb, dup=dup),
        grid=(l // LANES, t // tb),
        in_specs=[vec, vec, vec, vec, vec, val, st],
        out_specs=[val, st],
        out_shape=[jax.ShapeDtypeStruct(v.shape, F32), jax.ShapeDtypeStruct(s0.shape, F32)],
        scratch_shapes=[pltpu.VMEM((tb, HEAD, LANES), F32)] * 5 if dup else [],
        compiler_params=_cparams(2),
        name="wkv",
    )(r, w, k, kk, nb, v, s0)
    return y, s_new.reshape(HEAD, vr, l)


def _pool_kernel(full_ref, pw_ref, ps_ref, o_ref, *, t, pos0):
    bb = full_ref.shape[0]
    lane = lax.broadcasted_iota(I32, (t, POOL_W), 1)
    window = jnp.where(lane < 64, 2, jnp.where(lane < 128, 4, jnp.where(lane < 192, 8, 16)))
    pos = lax.broadcasted_iota(I32, (t, POOL_W), 0) + (pos0 + 1)
    cnt = jnp.minimum(pos, window).astype(F32)
    diffs = []
    for i in range(bb):
        f = full_ref[i]
        s2 = f + pltpu.roll(f, 1, 0)
        s4 = s2 + pltpu.roll(s2, 2, 0)
        s8 = s4 + pltpu.roll(s4, 4, 0)
        s16 = s8 + pltpu.roll(s8, 8, 0)
        wsum = jnp.where(lane < 64, s2[POOL_PAD:], jnp.where(lane < 128, s4[POOL_PAD:],
                                                            jnp.where(lane < 192, s8[POOL_PAD:], s16[POOL_PAD:])))
        diffs.append(wsum / cnt - f[POOL_PAD:])
    diff = diffs[0] if bb == 1 else jnp.concatenate(diffs, axis=0)
    out = jnp.dot(diff.astype(BF16), pw_ref[...], preferred_element_type=F32) * ps_ref[...]
    for i in range(bb):
        o_ref[i] = out[i * t:(i + 1) * t]


def _pool(full, pw_bd, pscale, bb, pos0):
    b, rows, _ = full.shape
    t = rows - POOL_PAD
    assert b % bb == 0
    return pl.pallas_call(
        functools.partial(_pool_kernel, t=t, pos0=pos0),
        grid=(b // bb,),
        in_specs=[pl.BlockSpec((bb, rows, POOL_W), lambda i: (i, 0, 0)),
                  pl.BlockSpec((POOL_W, POOL_W), lambda i: (0, 0)), pl.BlockSpec((1, POOL_W), lambda i: (0, 0))],
        out_specs=pl.BlockSpec((bb, t, POOL_W), lambda i: (i, 0, 0)),
        out_shape=jax.ShapeDtypeStruct((b, t, POOL_W), F32),
        compiler_params=_cparams(1),
        name="pool",
    )(full, pw_bd, pscale)


def _memattn_kernel(q_ref, k_ref, v_ref, o_ref):
    q = q_ref[0].astype(BF16)
    kf = k_ref[0]
    vf = v_ref[0]
    head_of_lane = lax.broadcasted_iota(I32, kf.shape, 1) // (MEM_W // MEM_HEADS)
    out = None
    for h in range(MEM_HEADS):
        kh = jnp.where(head_of_lane == h, kf, 0.0).astype(BF16)
        vh = jnp.where(head_of_lane == h, vf, 0.0).astype(BF16)
        s = lax.dot_general(q, kh, (((1,), (1,)), ((), ())), preferred_element_type=F32) * (64 ** -0.5)
        e = jnp.exp(s - jnp.max(s, axis=-1, keepdims=True))
        p = e / jnp.sum(e, axis=-1, keepdims=True)
        o = jnp.dot(p.astype(BF16), vh, preferred_element_type=F32)
        out = o if out is None else out + o
    o_ref[0] = out


def _memattn(z3, mk, mv, tt):
    b, t, _ = z3.shape
    assert t % tt == 0
    qcol = (Z_W - MEM_W) // MEM_W
    kv = pl.BlockSpec((1, MEM_TOKENS, MEM_W), lambda i, j: (i, 0, 0))
    return pl.pallas_call(
        _memattn_kernel,
        grid=(b, t // tt),
        in_specs=[pl.BlockSpec((1, tt, MEM_W), lambda i, j: (i, j, qcol)), kv, kv],
        out_specs=pl.BlockSpec((1, tt, MEM_W), lambda i, j: (i, j, 0)),
        out_shape=jax.ShapeDtypeStruct((b, t, MEM_W), F32),
        compiler_params=_cparams(2),
        name="mem_attn",
    )(z3, mk, mv)


def _layer_norm(x, g, b):
    mu = jnp.mean(x, axis=-1, keepdims=True)
    xc = x - mu
    var = jnp.mean(xc * xc, axis=-1, keepdims=True)
    return xc * lax.rsqrt(var + LN_EPS) * g + b


def _post_kernel(y_ref, bonus_ref, g_ref, op_ref, om_ref, x_ref, wo_ref, lxw_ref, lxb_ref, l1g_ref, l1b_ref, bd_ref,
                 h_ref, *, alpha):
    y = y_ref[...]
    bd = bd_ref[...]
    mu = _segsum(y, bd) * (1.0 / HEAD)
    yc = y - mu
    var = _segsum(yc * yc, bd) * (1.0 / HEAD)
    yn = yc * lax.rsqrt(var + GN_EPS) * lxw_ref[...] + lxb_ref[...]
    o_rwkv = (yn + bonus_ref[...]) * g_ref[...]
    mixed = (jnp.dot(o_rwkv.astype(BF16), wo_ref[0:RWKV_W, :], preferred_element_type=F32)
             + jnp.dot(op_ref[...].astype(BF16), wo_ref[RWKV_W:RWKV_W + POOL_W, :], preferred_element_type=F32)
             + jnp.dot(om_ref[...].astype(BF16), wo_ref[RWKV_W + POOL_W:, :], preferred_element_type=F32))
    h_ref[...] = _layer_norm(alpha * x_ref[...] + mixed, l1g_ref[...], l1b_ref[...])


def _post(y, bonus, g, o_pool, o_mem, x, wo, lxw, lxb, l1g, l1b, bd, tt, alpha):
    n = y.shape[0]
    assert n % tt == 0
    tok = lambda w: pl.BlockSpec((tt, w), lambda i: (i, 0))
    row = lambda w: pl.BlockSpec((1, w), lambda i: (0, 0))
    full = lambda a: pl.BlockSpec(a.shape, lambda i: (0, 0))
    return pl.pallas_call(
        functools.partial(_post_kernel, alpha=alpha),
        grid=(n // tt,),
        in_specs=[tok(RWKV_W), tok(RWKV_W), tok(RWKV_W), tok(POOL_W), tok(MEM_W), tok(D_MODEL), full(wo),
                  row(RWKV_W), row(RWKV_W), row(D_MODEL), row(D_MODEL), full(bd)],
        out_specs=tok(D_MODEL),
        out_shape=jax.ShapeDtypeStruct((n, D_MODEL), F32),
        compiler_params=_cparams(1),
        name="post",
    )(y, bonus, g, o_pool, o_mem, x, wo, lxw, lxb, l1g, l1b, bd)


def _router_kernel(h_ref, rwt_ref, bias_ref, tri_ref, below_ref, slot_o, gate_o, run_o, cnt_o, carry_ref):
    @pl.when(pl.program_id(0) == 0)
    def _():
        carry_ref[...] = jnp.zeros_like(carry_ref)

    neg = -jnp.inf
    logits = lax.dot_general(rwt_ref[...], h_ref[...], (((1,), (1,)), ((), ())),
                             precision=HIGHEST, preferred_element_type=F32)
    scores = _sigmoid(logits)
    sel = scores + bias_ref[...]
    tt = sel.shape[1]
    gio = lax.broadcasted_iota(I32, (GROUP_SIZE, tt), 0).astype(F32)
    blocks, gscore = [], []
    for g in range(N_GROUPS):
        blk = sel[g * GROUP_SIZE:(g + 1) * GROUP_SIZE, :]
        m1 = jnp.max(blk, axis=0, keepdims=True)
        first = jnp.min(jnp.where(blk == m1, gio, float(GROUP_SIZE)), axis=0, keepdims=True)
        m2 = jnp.max(jnp.where(gio == first, neg, blk), axis=0, keepdims=True)
        blocks.append(blk)
        gscore.append(m1 + m2)
    masked = []
    for g in range(N_GROUPS):
        beaten_by = jnp.zeros((1, tt), F32)
        for g2 in range(N_GROUPS):
            if g2 != g:
                wins = (gscore[g2] >= gscore[g]) if g2 < g else (gscore[g2] > gscore[g])
                beaten_by = beaten_by + jnp.where(wins, 1.0, 0.0)
        masked.append(jnp.where(beaten_by < TOPK_GROUPS, blocks[g], neg))
    msel = jnp.concatenate(masked, axis=0)
    eio = lax.broadcasted_iota(I32, msel.shape, 0).astype(F32)
    chosen = jnp.zeros(msel.shape, F32)
    idxs, scs = [], []
    for _ in range(TOP_K):
        m = jnp.max(msel, axis=0, keepdims=True)
        first = jnp.min(jnp.where(msel == m, eio, float(N_EXPERTS)), axis=0, keepdims=True)
        hit = eio == first
        scs.append(jnp.sum(jnp.where(hit, scores, 0.0), axis=0, keepdims=True))
        msel = jnp.where(hit, neg, msel)
        chosen = jnp.where(hit, 1.0, chosen)
        idxs.append(first)
    total = scs[0]
    for s in scs[1:]:
        total = total + s
    chosen_b = chosen.astype(BF16)
    earlier = jnp.dot(chosen_b, tri_ref[...], preferred_element_type=F32)
    smaller = jnp.dot(below_ref[...], chosen_b, preferred_element_type=F32)
    run_len = jnp.sum(chosen, axis=1, keepdims=True)
    run_off = jnp.sum(smaller, axis=1, keepdims=True)
    slot_of = run_off + earlier
    slots = [jnp.sum(jnp.where(eio == i, slot_of, 0.0), axis=0, keepdims=True) for i in idxs]
    slot_o[...] = jnp.concatenate(slots, axis=0).astype(I32)
    gate_o[...] = jnp.concatenate([s / total * ROUTED_SCALE for s in scs], axis=0)
    lane = lax.broadcasted_iota(I32, run_o.shape, 1)
    run_o[...] = jnp.where(lane == 0, run_len, jnp.where(lane == 1, run_off, jnp.where(lane == 2, carry_ref[...], 0.0)))
    carry_ref[...] = carry_ref[...] + run_len
    cnt_o[...] = jnp.broadcast_to(carry_ref[...], cnt_o.shape)


def _router(h, rwt, bias, tt):
    n = h.shape[0]
    assert n % tt == 0
    tri = (lax.broadcasted_iota(I32, (tt, tt), 0) < lax.broadcasted_iota(I32, (tt, tt), 1)).astype(BF16)
    below = (lax.broadcasted_iota(I32, (N_EXPERTS, N_EXPERTS), 1)
             < lax.broadcasted_iota(I32, (N_EXPERTS, N_EXPERTS), 0)).astype(BF16)
    tokT = pl.BlockSpec((TOP_K, tt), lambda i: (0, i))
    return pl.pallas_call(
        _router_kernel,
        grid=(n // tt,),
        in_specs=[pl.BlockSpec((tt, D_MODEL), lambda i: (i, 0)), pl.BlockSpec((N_EXPERTS, D_MODEL), lambda i: (0, 0)),
                  pl.BlockSpec((N_EXPERTS, 1), lambda i: (0, 0)), pl.BlockSpec((tt, tt), lambda i: (0, 0)),
                  pl.BlockSpec((N_EXPERTS, N_EXPERTS), lambda i: (0, 0))],
        out_specs=[tokT, tokT, pl.BlockSpec((N_EXPERTS, LANES), lambda i: (i, 0)),
                   pl.BlockSpec((N_EXPERTS, LANES), lambda i: (0, 0))],
        out_shape=[jax.ShapeDtypeStruct((TOP_K, n), I32), jax.ShapeDtypeStruct((TOP_K, n), F32),
                   jax.ShapeDtypeStruct((n // tt * N_EXPERTS, LANES), F32),
                   jax.ShapeDtypeStruct((N_EXPERTS, LANES), F32)],
        scratch_shapes=[pltpu.VMEM((N_EXPERTS, 1), F32)],
        compiler_params=_cparams(1),
        name="router",
    )(h, rwt, bias, tri, below)


ROW_SUB = D_MODEL // LANES


def _store_row_tiles(ref, x):
    rows = x.shape[0]
    for c in range(ROW_SUB):
        ref[pl.ds(c, rows, stride=ROW_SUB), :] = x[:, c * LANES:(c + 1) * LANES]


def _load_row_tiles(ref):
    rows = ref.shape[0] // ROW_SUB
    return jnp.concatenate([ref[pl.ds(c, rows, stride=ROW_SUB), :] for c in range(ROW_SUB)], axis=-1)


RUN_FIELDS = 3
RUN_UNROLL = 8


def _start_run_copies(runs_ref, tile, local_ref, sorted_ref, sem, to_sorted):
    def body(e, c):
        size = pl.multiple_of(runs_ref[tile * RUN_FIELDS + 2, e], ROW_SUB)
        local = local_ref.at[pl.ds(pl.multiple_of(runs_ref[tile * RUN_FIELDS, e], ROW_SUB), size), :]
        remote = sorted_ref.at[pl.ds(pl.multiple_of(runs_ref[tile * RUN_FIELDS + 1, e], ROW_SUB), size), :]
        (pltpu.make_async_copy(local, remote, sem) if to_sorted else pltpu.make_async_copy(remote, local, sem)).start()
        return c

    lax.fori_loop(0, N_EXPERTS, body, 0, unroll=RUN_UNROLL)


def _wait_run_copies(local_ref, sorted_ref, sem):
    pltpu.make_async_copy(sorted_ref.at[pl.ds(0, local_ref.shape[0]), :], local_ref, sem).wait()


def _dispatch_kernel(zpos_ref, runs_ref, slot_ref, h_ref, xs_hbm, zero_buf, rows_buf, sem_zero, sem_rows, *, td):
    i = pl.program_id(0)

    def zero_copy(e):
        start = pl.multiple_of(zpos_ref[e] * ROW_SUB, CHUNK * ROW_SUB)
        return pltpu.make_async_copy(zero_buf, xs_hbm.at[pl.ds(start, CHUNK * ROW_SUB), :], sem_zero)

    @pl.when(i == 0)
    def _():
        zero_buf[...] = jnp.zeros_like(zero_buf)

        def start(e, c):
            @pl.when(zpos_ref[e] >= 0)
            def _():
                zero_copy(e).start()
            return c

        def wait(e, c):
            @pl.when(zpos_ref[e] >= 0)
            def _():
                zero_copy(e).wait()
            return c

        lax.fori_loop(0, N_EXPERTS, start, 0)
        lax.fori_loop(0, N_EXPERTS, wait, 0)

    slots = slot_ref[...]
    slot_iota = lax.broadcasted_iota(I32, (TOP_K * td, td), 0)
    select = jnp.zeros((TOP_K * td, td), F32)
    for j in range(TOP_K):
        select = jnp.where(slot_iota == slots[j:j + 1, :], 1.0, select)
    local = jnp.dot(select.astype(BF16), h_ref[...].astype(BF16), preferred_element_type=F32)
    _store_row_tiles(rows_buf, local)
    _start_run_copies(runs_ref, i, rows_buf, xs_hbm, sem_rows, True)
    _wait_run_copies(rows_buf, xs_hbm, sem_rows)


def _dispatch(zpos, runs, slots, h, n_rows, td):
    n = h.shape[0]
    assert n % td == 0
    return pl.pallas_call(
        functools.partial(_dispatch_kernel, td=td),
        grid_spec=pltpu.PrefetchScalarGridSpec(
            num_scalar_prefetch=2,
            grid=(n // td,),
            in_specs=[pl.BlockSpec((TOP_K, td), lambda i, z, r: (0, i)),
                      pl.BlockSpec((td, D_MODEL), lambda i, z, r: (i, 0))],
            out_specs=pl.BlockSpec(memory_space=pl.ANY),
            scratch_shapes=[pltpu.VMEM((CHUNK * ROW_SUB, LANES), F32), pltpu.VMEM((TOP_K * td * ROW_SUB, LANES), F32),
                            pltpu.SemaphoreType.DMA, pltpu.SemaphoreType.DMA],
        ),
        out_shape=jax.ShapeDtypeStruct((n_rows * ROW_SUB, LANES), F32),
        compiler_params=_cparams(1),
        name="dispatch",
    )(zpos, runs, slots, h)


CHUNK_ROWS = CHUNK * ROW_SUB


def _ffn_kernel(first_ref, count_ref, nu_ref, xs_hbm, wg_ref, wu_ref, wd_ref, ys_hbm, xbuf, obuf, xsem, osem):
    e = pl.program_id(0)
    n_used = nu_ref[0]

    def x_copy(g, slot):
        src = xs_hbm.at[pl.ds(pl.multiple_of(g * CHUNK_ROWS, CHUNK_ROWS), CHUNK_ROWS), :]
        return pltpu.make_async_copy(src, xbuf.at[slot], xsem.at[slot])

    def o_copy(g, slot):
        dst = ys_hbm.at[pl.ds(pl.multiple_of(g * CHUNK_ROWS, CHUNK_ROWS), CHUNK_ROWS), :]
        return pltpu.make_async_copy(obuf.at[slot], dst, osem.at[slot])

    @pl.when((e == 0) & (n_used > 0))
    def _():
        x_copy(0, 0).start()

    wg = wg_ref[0].astype(BF16)
    wu = wu_ref[0].astype(BF16)
    wd = wd_ref[0].astype(BF16)

    def chunk(c, carry):
        g = first_ref[e] + c
        slot = g & 1
        x_copy(g, slot).wait()

        @pl.when(g + 1 < n_used)
        def _():
            x_copy(g + 1, 1 - slot).start()

        @pl.when(g >= 2)
        def _():
            o_copy(g - 2, slot).wait()

        x = _load_row_tiles(xbuf.at[slot]).astype(BF16)
        gate = jnp.dot(x, wg, preferred_element_type=F32)
        up = jnp.dot(x, wu, preferred_element_type=F32)
        act = (gate * _sigmoid(gate)) * up
        _store_row_tiles(obuf.at[slot], jnp.dot(act.astype(BF16), wd, preferred_element_type=F32))
        o_copy(g, slot).start()
        return carry

    lax.fori_loop(0, count_ref[e], chunk, 0)

    @pl.when(e == N_EXPERTS - 1)
    def _():
        for back in (2, 1):
            @pl.when(n_used >= back)
            def _():
                o_copy(n_used - back, (n_used - back) & 1).wait()


def _ffn(first_chunk, chunk_count, n_used, xs, wg, wu, wd):
    weights = lambda a, b: pl.BlockSpec((1, a, b), lambda e, f, c, nu: (e, 0, 0))
    return pl.pallas_call(
        _ffn_kernel,
        grid_spec=pltpu.PrefetchScalarGridSpec(
            num_scalar_prefetch=3,
            grid=(N_EXPERTS,),
            in_specs=[pl.BlockSpec(memory_space=pl.ANY), weights(D_MODEL, EXPERT_FF), weights(D_MODEL, EXPERT_FF),
                      weights(EXPERT_FF, D_MODEL)],
            out_specs=pl.BlockSpec(memory_space=pl.ANY),
            scratch_shapes=[pltpu.VMEM((2, CHUNK_ROWS, LANES), F32), pltpu.VMEM((2, CHUNK_ROWS, LANES), F32),
                            pltpu.SemaphoreType.DMA((2,)), pltpu.SemaphoreType.DMA((2,))],
        ),
        out_shape=jax.ShapeDtypeStruct(xs.shape, F32),
        compiler_params=_cparams(1),
        name="expert_ffn",
    )(first_chunk, chunk_count, n_used, xs, wg, wu, wd)


def _combine_kernel(runs_ref, slot_ref, gate_ref, h_ref, ys_hbm, sg_ref, su_ref, sd_ref, l2g_ref, l2b_ref, o_ref,
                    rows_buf, sem_rows, *, tc, alpha):
    i = pl.program_id(0)
    _start_run_copies(runs_ref, i, rows_buf, ys_hbm, sem_rows, False)
    h = h_ref[...]
    hb = h.astype(BF16)
    sgate = jnp.dot(hb, sg_ref[...], preferred_element_type=F32)
    sup = jnp.dot(hb, su_ref[...], preferred_element_type=F32)
    shared = jnp.dot(((sgate * _sigmoid(sgate)) * sup).astype(BF16), sd_ref[...], preferred_element_type=F32)
    slots = slot_ref[...]
    gate = gate_ref[...]
    slot_iota = lax.broadcasted_iota(I32, (tc, TOP_K * tc), 1)
    weights = jnp.zeros((tc, TOP_K * tc), F32)
    for j in range(TOP_K):
        weights = jnp.where(slot_iota == slots[:, j:j + 1], gate[:, j:j + 1], weights)
    w_hi = weights.astype(BF16)
    w_lo = (weights - w_hi.astype(F32)).astype(BF16)
    _wait_run_copies(rows_buf, ys_hbm, sem_rows)
    local = _load_row_tiles(rows_buf).astype(BF16)
    routed = (jnp.dot(w_hi, local, preferred_element_type=F32) + jnp.dot(w_lo, local, preferred_element_type=F32))
    o_ref[...] = _layer_norm(alpha * h + (routed + shared), l2g_ref[...], l2b_ref[...])


def _combine(runs, slots, gate, h, ys, sg, su, sd, l2g, l2b, tc, alpha):
    n = h.shape[0]
    assert n % tc == 0
    tok = lambda w: pl.BlockSpec((tc, w), lambda i, r: (i, 0))
    full = lambda a: pl.BlockSpec(a.shape, lambda i, r: (0, 0))
    return pl.pallas_call(
        functools.partial(_combine_kernel, tc=tc, alpha=alpha),
        grid_spec=pltpu.PrefetchScalarGridSpec(
            num_scalar_prefetch=1,
            grid=(n // tc,),
            in_specs=[tok(TOP_K), tok(TOP_K), tok(D_MODEL), pl.BlockSpec(memory_space=pl.ANY),
                      full(sg), full(su), full(sd), full(l2g), full(l2b)],
            out_specs=tok(D_MODEL),
            scratch_shapes=[pltpu.VMEM((TOP_K * tc * ROW_SUB, LANES), F32), pltpu.SemaphoreType.DMA],
        ),
        out_shape=jax.ShapeDtypeStruct((n, D_MODEL), F32),
        compiler_params=_cparams(1),
        name="combine",
    )(runs, slots, gate, h, ys, sg, su, sd, l2g, l2b)


def _pad_cols(a, width):
    return jnp.pad(a, ((0, 0), (0, width - a.shape[1])))


def _pack_shift_cols(a):
    c = 3 * RWKV_W
    return jnp.concatenate([a[:, :c], _pad_cols(a[:, c:c + DECAY_LORA], LORA_PAD),
                            _pad_cols(a[:, c + DECAY_LORA:c + DECAY_LORA + AAA_LORA], LORA_PAD),
                            _pad_cols(a[:, c + DECAY_LORA + AAA_LORA:], GATE_PAD)], axis=1)


def _keys_to_lanes(a, b, t):
    return a.reshape(b, t, N_HEADS, HEAD).transpose(1, 3, 0, 2).reshape(t, HEAD, b * N_HEADS)


def _values_to_lanes(a, b, t, split):
    a = a.reshape(b, t, N_HEADS, split, HEAD // split).transpose(1, 4, 3, 0, 2)
    return a.reshape(t, HEAD // split, split * b * N_HEADS)


def _values_from_lanes(y, b, t, split):
    y = y.reshape(t, HEAD // split, split, b, N_HEADS).transpose(3, 0, 4, 2, 1)
    return y.reshape(b * t, RWKV_W)


def _state_to_lanes(s, b, split):
    s = s.reshape(b, N_HEADS, split, HEAD // split, HEAD).transpose(4, 3, 2, 0, 1)
    return s.reshape(HEAD, HEAD // split, split * b * N_HEADS)


def _state_from_lanes(s, b, split):
    s = s.reshape(HEAD, HEAD // split, split, b, N_HEADS).transpose(3, 4, 2, 1, 0)
    return s.reshape(b, N_HEADS, HEAD, HEAD)


def _mixer(z, zprev, x2, b, t, wkv0, pool_prev, mk, mv, pos0, wts, tiles):
    n = b * t
    tt, tb, pool_bb, att_tt = tiles
    z3 = z[:n].reshape(b, t, Z_W)
    r, w, kp, v, kk, nb, g, bonus = _rwkv_prep(z, zprev, b, t, tt, wts["prep"])

    split = 1 if (b * N_HEADS) % LANES == 0 else LANES // (b * N_HEADS)
    lay = lambda a: _keys_to_lanes(a, b, t).reshape(t, HEAD // split, split * b * N_HEADS)
    key_order = jnp.concatenate([jnp.arange(i, HEAD, split) for i in range(split)])
    s_in = wkv0[..., key_order] if split > 1 else wkv0
    y_l, s_l = _wkv(lay(r), lay(w), lay(kp), lay(kk), lay(nb), _values_to_lanes(v, b, t, split),
                    _state_to_lanes(s_in, b, split), tb)
    y = _values_from_lanes(y_l, b, t, split)
    wkv_new = _state_from_lanes(s_l, b, split)
    if split > 1:
        wkv_new = wkv_new[..., jnp.argsort(key_order)]

    zp = z3[:, :, ZS_W:ZS_W + POOL_W]
    full = jnp.concatenate([jnp.zeros((b, POOL_PAD - POOL_BUF, POOL_W), F32), pool_prev, zp], axis=1)
    o_pool = _pool(full, wts["pool_w"], wts["pool_scale"], pool_bb, pos0).reshape(n, POOL_W)
    pool_new = full[:, -POOL_BUF:]

    o_mem = _memattn(z3, mk, mv, att_tt).reshape(n, MEM_W)
    h = _post(y, bonus, g, o_pool, o_mem, x2, *wts["post"], tt, wts["alpha"])
    return h, wkv_new, pool_new


def kernel(x_prompt, x_sample, mem_prompt, state_wkv, state_shift, state_pool, cache_mem_k, cache_mem_v, w_in, mu_shift, w0, w_up_decay, a0, w_up_aaa, w_up_gate, k_k, k_a, r_k, ln_x_w, ln_x_b, pool_w, pool_scale, mem_wk, mem_wv, w_out, ln1_g, ln1_b, router_w, router_b, exp_gate, exp_up, exp_down, sh_gate, sh_up, sh_down, ln2_g, ln2_b):
    depth = w_in.shape[0]
    assert depth == 1
    l = 0
    alpha = (2.0 * depth) ** 0.25
    bp, tp, d = x_prompt.shape
    bs, ts, _ = x_sample.shape
    n_p, n_s = bp * tp, bs * ts
    n = n_p + n_s

    w_in_p = jnp.concatenate([_pack_shift_cols(w_in[l][:, :1824]), w_in[l][:, 1824:]], axis=1).astype(BF16)
    row = lambda a: a.reshape(1, -1)
    pad_rows = lambda a, rows: jnp.pad(a, ((0, rows - a.shape[0]), (0, 0)))
    bd = jnp.kron(jnp.eye(N_HEADS, dtype=F32), jnp.ones((HEAD, HEAD), F32)).astype(BF16)
    pw = pool_w[l]
    pw_bd = jnp.zeros((POOL_W, POOL_W), F32)
    for gi in range(4):
        pw_bd = pw_bd.at[gi * 64:(gi + 1) * 64, gi * 64:(gi + 1) * 64].set(pw[gi])
    wts = {
        "prep": (_pack_shift_cols(row(mu_shift[l])), row(w0[l]), row(a0[l]), pad_rows(w_up_decay[l], LORA_PAD),
                 pad_rows(w_up_aaa[l], LORA_PAD), pad_rows(w_up_gate[l], GATE_PAD), row(k_k[l]), row(k_a[l]),
                 row(r_k[l]), bd),
        "pool_w": pw_bd.astype(BF16), "pool_scale": row(pool_scale[l]),
        "post": (w_out[l].astype(BF16), row(ln_x_w[l]), row(ln_x_b[l]), row(ln1_g[l]), row(ln1_b[l]), bd),
        "alpha": alpha,
    }

    xp2 = x_prompt.reshape(n_p, d)
    xs2 = x_sample.reshape(n_s, d)
    z_p = _matmul(xp2, w_in_p, _tile(n_p, 512))
    z_s = _matmul(jnp.concatenate([xs2, state_shift[l]], axis=0), w_in_p, _tile(n_s + bs, 512))
    mkv = _matmul(mem_prompt.reshape(bp * MEM_TOKENS, d),
                  jnp.concatenate([mem_wk[l], mem_wv[l]], axis=1).astype(BF16), _tile(bp * MEM_TOKENS, 512))
    mk_p = mkv[:, :MEM_W].reshape(bp, MEM_TOKENS, MEM_W)
    mv_p = mkv[:, MEM_W:].reshape(bp, MEM_TOKENS, MEM_W)

    h_p, wkv_p, pool_p = _mixer(z_p, jnp.zeros((bp, ZS_W), F32), xp2, bp, tp,
                                jnp.zeros((bp, N_HEADS, HEAD, HEAD), F32), jnp.zeros((bp, POOL_BUF, POOL_W), F32),
                                mk_p, mv_p, 0, wts,
                                (_tile(n_p, 256), _tile(tp, 32, 1), 1, _tile(tp, 256)))
    h_s, wkv_s, pool_s = _mixer(z_s, z_s[n_s:, :ZS_W], xs2, bs, ts, state_wkv[l], state_pool[l],
                                cache_mem_k[l].reshape(bs, MEM_TOKENS, MEM_W),
                                cache_mem_v[l].reshape(bs, MEM_TOKENS, MEM_W), PAST_LEN, wts,
                                (_tile(n_s, 256), _tile(ts, 32, 1), _tile(bs, 16, 1), _tile(ts, 256)))

    h = jnp.concatenate([h_p, h_s], axis=0)
    td = _tile(n, 256, LANES)
    slots, gate, run_tab, cnt = _router(h, router_w[l].T, router_b[l].reshape(N_EXPERTS, 1), td)
    counts = cnt[:, 0].astype(I32)
    padded = (counts + CHUNK - 1) // CHUNK * CHUNK
    pad_end = jnp.cumsum(padded)
    pad_start = pad_end - padded
    n_chunks = (n * TOP_K + N_EXPERTS * (CHUNK - 1) + CHUNK - 1) // CHUNK
    run_tab = run_tab.reshape(n // td, N_EXPERTS, LANES)[:, :, :RUN_FIELDS].astype(I32)
    runs = jnp.stack([run_tab[:, :, 1], pad_start[None, :] + run_tab[:, :, 2], run_tab[:, :, 0]], axis=1)
    runs = runs.reshape(n // td * RUN_FIELDS, N_EXPERTS) * ROW_SUB
    zpos = jnp.where(padded > 0, pad_end - CHUNK, -1).astype(I32)
    n_used = (pad_end[-1:] // CHUNK).astype(I32)
    xs_sorted = _dispatch(zpos, runs, slots, h, n_chunks * CHUNK, td)
    ys_sorted = _ffn((pad_start // CHUNK).astype(I32), (padded // CHUNK).astype(I32), n_used, xs_sorted,
                     exp_gate[l], exp_up[l], exp_down[l])
    y = _combine(runs, slots.T, gate.T, h, ys_sorted, sh_gate[l].astype(BF16), sh_up[l].astype(BF16),
                 sh_down[l].astype(BF16), row(ln2_g[l]), row(ln2_b[l]), td, alpha)

    return (y[:n_p].reshape(bp, tp, d), y[n_p:].reshape(bs, ts, d),
            wkv_p[None], x_prompt[:, -1][None], pool_p[None],
            mk_p.reshape(bp, MEM_TOKENS, MEM_HEADS, HEAD)[None], mv_p.reshape(bp, MEM_TOKENS, MEM_HEADS, HEAD)[None],
            wkv_s[None], x_sample[:, -1][None], pool_s[None])
```

```python
import functools

import jax
import jax.numpy as jnp
from jax import lax
from jax.experimental import pallas as pl
from jax.experimental.pallas import tpu as pltpu

F32 = jnp.float32
BF16 = jnp.bfloat16
I32 = jnp.int32
HIGHEST = lax.Precision.HIGHEST

D_MODEL = 1024
HEAD = 64
N_HEADS = 8
RWKV_W = 512
POOL_W = 256
MEM_W = 256
MEM_HEADS = 4
MEM_TOKENS = 256
POOL_BUF = 15
POOL_PAD = 16
DECAY_LORA = 64
AAA_LORA = 64
GATE_LORA = 160
LORA_PAD = 128
GATE_PAD = 256
ZS_W = 3 * RWKV_W + 2 * LORA_PAD + GATE_PAD
Z_W = ZS_W + POOL_W + MEM_W
N_EXPERTS = 256
TOP_K = 8
N_GROUPS = 8
GROUP_SIZE = N_EXPERTS // N_GROUPS
TOPK_GROUPS = 4
EXPERT_FF = 256
ROUTED_SCALE = 2.5
CHUNK = 256
PAST_LEN = 16384
LN_EPS = 1e-5
GN_EPS = 64e-5
LANES = 128
SUBLANES = 8
VMEM_LIMIT = 48 * 1024 * 1024


def _cparams(n_axes):
    return pltpu.CompilerParams(dimension_semantics=("arbitrary",) * n_axes, vmem_limit_bytes=VMEM_LIMIT)


def _tile(n, preferred, multiple=8):
    best = None
    for c in range(multiple, min(n, preferred) + 1, multiple):
        if n % c == 0:
            best = c
    assert best is not None, (n, preferred, multiple)
    return best


def _sigmoid(x):
    return 1.0 / (1.0 + jnp.exp(-x))


def _matmul_kernel(x_ref, w_ref, o_ref):
    o_ref[...] = jnp.dot(x_ref[...].astype(BF16), w_ref[...], preferred_element_type=F32)


def _matmul(x, w, tm):
    m, k = x.shape
    n = w.shape[1]
    assert m % tm == 0
    return pl.pallas_call(
        _matmul_kernel,
        grid=(m // tm,),
        in_specs=[pl.BlockSpec((tm, k), lambda i: (i, 0)), pl.BlockSpec((k, n), lambda i: (0, 0))],
        out_specs=pl.BlockSpec((tm, n), lambda i: (i, 0)),
        out_shape=jax.ShapeDtypeStruct((m, n), F32),
        compiler_params=_cparams(1),
        name="matmul",
    )(x, w)


def _segsum(x, bd):
    hi = x.astype(BF16)
    lo = (x - hi.astype(F32)).astype(BF16)
    return jnp.dot(hi, bd, preferred_element_type=F32) + jnp.dot(lo, bd, preferred_element_type=F32)


def _prep_kernel(zs_ref, zp_ref, mu_ref, w0_ref, a0_ref, wd_ref, wa_ref, wg_ref, kk_ref, ka_ref, rk_ref, bd_ref,
                 r_o, w_o, k_o, v_o, kk_o, nb_o, g_o, bonus_o, carry_ref, *, tiles_per_seq, seq_len):
    zs = zs_ref[...]
    rolled = pltpu.roll(zs, 1, 0)
    row = lax.broadcasted_iota(I32, zs.shape, 0)
    if tiles_per_seq:
        i = pl.program_id(0)

        @pl.when(i == 0)
        def _():
            carry_ref[...] = jnp.zeros_like(carry_ref)

        prev = jnp.where(i % tiles_per_seq == 0, zp_ref[0], carry_ref[...])
        shifted = jnp.where(row == 0, prev, rolled)
        carry_ref[...] = zs[zs.shape[0] - 1:, :]
    else:
        shifted = jnp.where((row & (seq_len - 1)) == 0, zp_ref[...], rolled)
    zm = zs + mu_ref[...] * (shifted - zs)
    r = zm[:, 0:RWKV_W]
    k = zm[:, RWKV_W:2 * RWKV_W]
    v = zm[:, 2 * RWKV_W:3 * RWKV_W]
    c0 = 3 * RWKV_W
    xw = zm[:, c0:c0 + LORA_PAD]
    xa = zm[:, c0 + LORA_PAD:c0 + 2 * LORA_PAD]
    xg = zm[:, c0 + 2 * LORA_PAD:c0 + 2 * LORA_PAD + GATE_PAD]
    bd = bd_ref[...]
    u = -(w0_ref[...] + jnp.dot(jnp.tanh(xw), wd_ref[...], precision=HIGHEST, preferred_element_type=F32))
    softplus = jnp.maximum(u, 0.0) + jnp.log(1.0 + jnp.exp(-jnp.abs(u)))
    decay = jnp.exp(-jnp.exp(-softplus - 0.5))
    a = _sigmoid(a0_ref[...] + jnp.dot(xa, wa_ref[...], precision=HIGHEST, preferred_element_type=F32))
    g = jnp.dot(_sigmoid(xg), wg_ref[...], precision=HIGHEST, preferred_element_type=F32)
    kk = k * kk_ref[...]
    kk = kk / jnp.maximum(jnp.sqrt(_segsum(kk * kk, bd)), 1e-12)
    kp = k * (1.0 + (a - 1.0) * ka_ref[...])
    r_o[...] = r
    w_o[...] = decay
    k_o[...] = kp
    v_o[...] = v
    kk_o[...] = kk
    nb_o[...] = -(kk * a)
    g_o[...] = g
    bonus_o[...] = _segsum(r * kp * rk_ref[...], bd) * v


def _rwkv_prep(z, zprev, b, t, tt, params):
    n_tok = b * t
    assert n_tok % tt == 0
    row = lambda w: pl.BlockSpec((1, w), lambda i: (0, 0))
    full = lambda a: pl.BlockSpec(a.shape, lambda i: (0, 0))
    tok = pl.BlockSpec((tt, RWKV_W), lambda i: (i, 0))
    mu, w0, a0, wd, wa, wg, k_k, k_a, r_k, bd = params
    if t % tt == 0:
        tiles_per_seq = t // tt
        zp = zprev.reshape(b, 1, ZS_W)
        zp_spec = pl.BlockSpec((1, 1, ZS_W), lambda i: (i // tiles_per_seq, 0, 0))
    else:
        assert tt % t == 0 and t & (t - 1) == 0
        tiles_per_seq = 0
        zp = jnp.repeat(zprev, t, axis=0)
        zp_spec = pl.BlockSpec((tt, ZS_W), lambda i: (i, 0))
    return pl.pallas_call(
        functools.partial(_prep_kernel, tiles_per_seq=tiles_per_seq, seq_len=t),
        grid=(n_tok // tt,),
        in_specs=[pl.BlockSpec((tt, ZS_W), lambda i: (i, 0)), zp_spec,
                  row(ZS_W), row(RWKV_W), row(RWKV_W), full(wd), full(wa), full(wg),
                  row(RWKV_W), row(RWKV_W), row(RWKV_W), full(bd)],
        out_specs=[tok] * 8,
        out_shape=[jax.ShapeDtypeStruct((n_tok, RWKV_W), F32)] * 8,
        scratch_shapes=[pltpu.VMEM((1, ZS_W), F32)],
        compiler_params=_cparams(1),
        name="rwkv_prep",
    )(z, zp, mu, w0, a0, wd, wa, wg, k_k, k_a, r_k, bd)


def _wkv_kernel(r_ref, w_ref, k_ref, kk_ref, nb_ref, v_ref, s0_ref, y_ref, s_ref, *scratch, tb, dup):
    slabs = s_ref.shape[1]
    key_unroll = 8

    @pl.when(pl.program_id(1) == 0)
    def _():
        s_ref[...] = s0_ref[...]

    if dup:
        for src, dst in zip((r_ref, w_ref, k_ref, kk_ref, nb_ref), scratch):
            x = src[...].reshape(tb * (HEAD // 2), LANES)
            swapped = pltpu.roll(x, LANES // 2, 1)
            low = lax.broadcasted_iota(I32, x.shape, 1) < LANES // 2
            dst[:, 0:HEAD // 2, :] = jnp.where(low, x, swapped).reshape(tb, HEAD // 2, LANES)
            dst[:, HEAD // 2:, :] = jnp.where(low, swapped, x).reshape(tb, HEAD // 2, LANES)
        r_ref, w_ref, k_ref, kk_ref, nb_ref = scratch

    def rows(ref, t, k):
        return jnp.broadcast_to(ref[t, pl.ds(k, 1), :], (SUBLANES, LANES))

    def step(t, sa, with_next):
        vt = [v_ref[t, i * SUBLANES:(i + 1) * SUBLANES, :] for i in range(slabs)]
        zero = jnp.zeros((SUBLANES, LANES), F32)

        def key(k, acc):
            y_acc, next_acc = list(acc[0]), list(acc[1])
            wb, nbb, kb, rb = rows(w_ref, t, k), rows(nb_ref, t, k), rows(k_ref, t, k), rows(r_ref, t, k)
            kkb = rows(kk_ref, t + 1, k) if with_next else None
            for i in range(slabs):
                sn = s_ref[k, i] * wb + sa[i] * nbb + vt[i] * kb
                s_ref[k, i] = sn
                y_acc[i] = y_acc[i] + sn * rb
                if with_next:
                    next_acc[i] = next_acc[i] + sn * kkb
            return tuple(y_acc), tuple(next_acc)

        y_acc, next_acc = lax.fori_loop(0, HEAD, key, ((zero,) * slabs, (zero,) * slabs), unroll=key_unroll)
        for i, y in enumerate(y_acc):
            y_ref[t, i * SUBLANES:(i + 1) * SUBLANES, :] = y
        return next_acc

    def first(k, acc):
        kkb = rows(kk_ref, 0, k)
        return tuple(a + s_ref[k, i] * kkb for i, a in enumerate(acc))

    sa0 = lax.fori_loop(0, HEAD, first, (jnp.zeros((SUBLANES, LANES), F32),) * slabs, unroll=key_unroll)
    sa_last = lax.fori_loop(0, tb - 1, lambda t, sa: step(t, sa, True), sa0)
    step(tb - 1, sa_last, False)


def _wkv(r, w, k, kk, nb, v, s0, tb):
    t, nk, _ = r.shape
    vr, l = v.shape[1], v.shape[2]
    dup = nk != HEAD
    assert t % tb == 0 and l % LANES == 0 and vr % SUBLANES == 0 and (not dup or (l == LANES and 2 * nk == HEAD))
    vec = pl.BlockSpec((tb, nk, LANES), lambda g, i: (i, 0, g))
    val = pl.BlockSpec((tb, vr, LANES), lambda g, i: (i, 0, g))
    slabs = vr // SUBLANES
    st = pl.BlockSpec((HEAD, slabs, SUBLANES, LANES), lambda g, i: (0, 0, 0, g))
    s0 = s0.reshape(HEAD, slabs, SUBLANES, l)
    y, s_new = pl.pallas_call(
        functools.partial(_wkv_kernel, tb=tb, dup=dup),
        grid=(l // LANES, t // tb),
        in_specs=[vec, vec, vec, vec, vec, val, st],
        out_specs=[val, st],
        out_shape=[jax.ShapeDtypeStruct(v.shape, F32), jax.ShapeDtypeStruct(s0.shape, F32)],
        scratch_shapes=[pltpu.VMEM((tb, HEAD, LANES), F32)] * 5 if dup else [],
        compiler_params=_cparams(2),
        name="wkv",
    )(r, w, k, kk, nb, v, s0)
    return y, s_new.reshape(HEAD, vr, l)


def _pool_kernel(full_ref, pw_ref, ps_ref, o_ref, *, t, pos0):
    bb = full_ref.shape[0]
    lane = lax.broadcasted_iota(I32, (t, POOL_W), 1)
    window = jnp.where(lane < 64, 2, jnp.where(lane < 128, 4, jnp.where(lane < 192, 8, 16)))
    pos = lax.broadcasted_iota(I32, (t, POOL_W), 0) + (pos0 + 1)
    cnt = jnp.minimum(pos, window).astype(F32)
    diffs = []
    for i in range(bb):
        f = full_ref[i]
        s2 = f + pltpu.roll(f, 1, 0)
        s4 = s2 + pltpu.roll(s2, 2, 0)
        s8 = s4 + pltpu.roll(s4, 4, 0)
        s16 = s8 + pltpu.roll(s8, 8, 0)
        wsum = jnp.where(lane < 64, s2[POOL_PAD:], jnp.where(lane < 128, s4[POOL_PAD:],
                                                            jnp.where(lane < 192, s8[POOL_PAD:], s16[POOL_PAD:])))
        diffs.append(wsum / cnt - f[POOL_PAD:])
    diff = diffs[0] if bb == 1 else jnp.concatenate(diffs, axis=0)
    out = jnp.dot(diff.astype(BF16), pw_ref[...], preferred_element_type=F32) * ps_ref[...]
    for i in range(bb):
        o_ref[i] = out[i * t:(i + 1) * t]


def _pool(full, pw_bd, pscale, bb, pos0):
    b, rows, _ = full.shape
    t = rows - POOL_PAD
    assert b % bb == 0
    return pl.pallas_call(
        functools.partial(_pool_kernel, t=t, pos0=pos0),
        grid=(b // bb,),
        in_specs=[pl.BlockSpec((bb, rows, POOL_W), lambda i: (i, 0, 0)),
                  pl.BlockSpec((POOL_W, POOL_W), lambda i: (0, 0)), pl.BlockSpec((1, POOL_W), lambda i: (0, 0))],
        out_specs=pl.BlockSpec((bb, t, POOL_W), lambda i: (i, 0, 0)),
        out_shape=jax.ShapeDtypeStruct((b, t, POOL_W), F32),
        compiler_params=_cparams(1),
        name="pool",
    )(full, pw_bd, pscale)


def _memattn_kernel(q_ref, k_ref, v_ref, o_ref):
    q = q_ref[0].astype(BF16)
    kf = k_ref[0]
    vf = v_ref[0]
    head_of_lane = lax.broadcasted_iota(I32, kf.shape, 1) // (MEM_W // MEM_HEADS)
    out = None
    for h in range(MEM_HEADS):
        kh = jnp.where(head_of_lane == h, kf, 0.0).astype(BF16)
        vh = jnp.where(head_of_lane == h, vf, 0.0).astype(BF16)
        s = lax.dot_general(q, kh, (((1,), (1,)), ((), ())), preferred_element_type=F32) * (64 ** -0.5)
        e = jnp.exp(s - jnp.max(s, axis=-1, keepdims=True))
        p = e / jnp.sum(e, axis=-1, keepdims=True)
        o = jnp.dot(p.astype(BF16), vh, preferred_element_type=F32)
        out = o if out is None else out + o
    o_ref[0] = out


def _memattn(z3, mk, mv, tt):
    b, t, _ = z3.shape
    assert t % tt == 0
    qcol = (Z_W - MEM_W) // MEM_W
    kv = pl.BlockSpec((1, MEM_TOKENS, MEM_W), lambda i, j: (i, 0, 0))
    return pl.pallas_call(
        _memattn_kernel,
        grid=(b, t // tt),
        in_specs=[pl.BlockSpec((1, tt, MEM_W), lambda i, j: (i, j, qcol)), kv, kv],
        out_specs=pl.BlockSpec((1, tt, MEM_W), lambda i, j: (i, j, 0)),
        out_shape=jax.ShapeDtypeStruct((b, t, MEM_W), F32),
        compiler_params=_cparams(2),
        name="mem_attn",
    )(z3, mk, mv)


def _layer_norm(x, g, b):
    mu = jnp.mean(x, axis=-1, keepdims=True)
    xc = x - mu
    var = jnp.mean(xc * xc, axis=-1, keepdims=True)
    return xc * lax.rsqrt(var + LN_EPS) * g + b


def _post_kernel(y_ref, bonus_ref, g_ref, op_ref, om_ref, x_ref, wo_ref, lxw_ref, lxb_ref, l1g_ref, l1b_ref, bd_ref,
                 h_ref, *, alpha):
    y = y_ref[...]
    bd = bd_ref[...]
    mu = _segsum(y, bd) * (1.0 / HEAD)
    yc = y - mu
    var = _segsum(yc * yc, bd) * (1.0 / HEAD)
    yn = yc * lax.rsqrt(var + GN_EPS) * lxw_ref[...] + lxb_ref[...]
    o_rwkv = (yn + bonus_ref[...]) * g_ref[...]
    mixed = (jnp.dot(o_rwkv.astype(BF16), wo_ref[0:RWKV_W, :], preferred_element_type=F32)
             + jnp.dot(op_ref[...].astype(BF16), wo_ref[RWKV_W:RWKV_W + POOL_W, :], preferred_element_type=F32)
             + jnp.dot(om_ref[...].astype(BF16), wo_ref[RWKV_W + POOL_W:, :], preferred_element_type=F32))
    h_ref[...] = _layer_norm(alpha * x_ref[...] + mixed, l1g_ref[...], l1b_ref[...])


def _post(y, bonus, g, o_pool, o_mem, x, wo, lxw, lxb, l1g, l1b, bd, tt, alpha):
    n = y.shape[0]
    assert n % tt == 0
    tok = lambda w: pl.BlockSpec((tt, w), lambda i: (i, 0))
    row = lambda w: pl.BlockSpec((1, w), lambda i: (0, 0))
    full = lambda a: pl.BlockSpec(a.shape, lambda i: (0, 0))
    return pl.pallas_call(
        functools.partial(_post_kernel, alpha=alpha),
        grid=(n // tt,),
        in_specs=[tok(RWKV_W), tok(RWKV_W), tok(RWKV_W), tok(POOL_W), tok(MEM_W), tok(D_MODEL), full(wo),
                  row(RWKV_W), row(RWKV_W), row(D_MODEL), row(D_MODEL), full(bd)],
        out_specs=tok(D_MODEL),
        out_shape=jax.ShapeDtypeStruct((n, D_MODEL), F32),
        compiler_params=_cparams(1),
        name="post",
    )(y, bonus, g, o_pool, o_mem, x, wo, lxw, lxb, l1g, l1b, bd)


def _router_kernel(h_ref, rwt_ref, bias_ref, tri_ref, below_ref, slot_o, gate_o, run_o, cnt_o, carry_ref):
    @pl.when(pl.program_id(0) == 0)
    def _():
        carry_ref[...] = jnp.zeros_like(carry_ref)

    neg = -jnp.inf
    logits = lax.dot_general(rwt_ref[...], h_ref[...], (((1,), (1,)), ((), ())),
                             precision=HIGHEST, preferred_element_type=F32)
    scores = _sigmoid(logits)
    sel = scores + bias_ref[...]
    tt = sel.shape[1]
    gio = lax.broadcasted_iota(I32, (GROUP_SIZE, tt), 0).astype(F32)
    blocks, gscore = [], []
    for g in range(N_GROUPS):
        blk = sel[g * GROUP_SIZE:(g + 1) * GROUP_SIZE, :]
        m1 = jnp.max(blk, axis=0, keepdims=True)
        first = jnp.min(jnp.where(blk == m1, gio, float(GROUP_SIZE)), axis=0, keepdims=True)
        m2 = jnp.max(jnp.where(gio == first, neg, blk), axis=0, keepdims=True)
        blocks.append(blk)
        gscore.append(m1 + m2)
    masked = []
    for g in range(N_GROUPS):
        beaten_by = jnp.zeros((1, tt), F32)
        for g2 in range(N_GROUPS):
            if g2 != g:
                wins = (gscore[g2] >= gscore[g]) if g2 < g else (gscore[g2] > gscore[g])
                beaten_by = beaten_by + jnp.where(wins, 1.0, 0.0)
        masked.append(jnp.where(beaten_by < TOPK_GROUPS, blocks[g], neg))
    msel = jnp.concatenate(masked, axis=0)
    eio = lax.broadcasted_iota(I32, msel.shape, 0).astype(F32)
    chosen = jnp.zeros(msel.shape, F32)
    idxs, scs = [], []
    for _ in range(TOP_K):
        m = jnp.max(msel, axis=0, keepdims=True)
        first = jnp.min(jnp.where(msel == m, eio, float(N_EXPERTS)), axis=0, keepdims=True)
        hit = eio == first
        scs.append(jnp.sum(jnp.where(hit, scores, 0.0), axis=0, keepdims=True))
        msel = jnp.where(hit, neg, msel)
        chosen = jnp.where(hit, 1.0, chosen)
        idxs.append(first)
    total = scs[0]
    for s in scs[1:]:
        total = total + s
    chosen_b = chosen.astype(BF16)
    earlier = jnp.dot(chosen_b, tri_ref[...], preferred_element_type=F32)
    smaller = jnp.dot(below_ref[...], chosen_b, preferred_element_type=F32)
    run_len = jnp.sum(chosen, axis=1, keepdims=True)
    run_off = jnp.sum(smaller, axis=1, keepdims=True)
    slot_of = run_off + earlier
    slots = [jnp.sum(jnp.where(eio == i, slot_of, 0.0), axis=0, keepdims=True) for i in idxs]
    slot_o[...] = jnp.concatenate(slots, axis=0).astype(I32)
    gate_o[...] = jnp.concatenate([s / total * ROUTED_SCALE for s in scs], axis=0)
    lane = lax.broadcasted_iota(I32, run_o.shape, 1)
    run_o[...] = jnp.where(lane == 0, run_len, jnp.where(lane == 1, run_off, jnp.where(lane == 2, carry_ref[...], 0.0)))
    carry_ref[...] = carry_ref[...] + run_len
    cnt_o[...] = jnp.broadcast_to(carry_ref[...], cnt_o.shape)


def _router(h, rwt, bias, tt):
    n = h.shape[0]
    assert n % tt == 0
    tri = (lax.broadcasted_iota(I32, (tt, tt), 0) < lax.broadcasted_iota(I32, (tt, tt), 1)).astype(BF16)
    below = (lax.broadcasted_iota(I32, (N_EXPERTS, N_EXPERTS), 1)
             < lax.broadcasted_iota(I32, (N_EXPERTS, N_EXPERTS), 0)).astype(BF16)
    tokT = pl.BlockSpec((TOP_K, tt), lambda i: (0, i))
    return pl.pallas_call(
        _router_kernel,
        grid=(n // tt,),
        in_specs=[pl.BlockSpec((tt, D_MODEL), lambda i: (i, 0)), pl.BlockSpec((N_EXPERTS, D_MODEL), lambda i: (0, 0)),
                  pl.BlockSpec((N_EXPERTS, 1), lambda i: (0, 0)), pl.BlockSpec((tt, tt), lambda i: (0, 0)),
                  pl.BlockSpec((N_EXPERTS, N_EXPERTS), lambda i: (0, 0))],
        out_specs=[tokT, tokT, pl.BlockSpec((N_EXPERTS, LANES), lambda i: (i, 0)),
                   pl.BlockSpec((N_EXPERTS, LANES), lambda i: (0, 0))],
        out_shape=[jax.ShapeDtypeStruct((TOP_K, n), I32), jax.ShapeDtypeStruct((TOP_K, n), F32),
                   jax.ShapeDtypeStruct((n // tt * N_EXPERTS, LANES), F32),
                   jax.ShapeDtypeStruct((N_EXPERTS, LANES), F32)],
        scratch_shapes=[pltpu.VMEM((N_EXPERTS, 1), F32)],
        compiler_params=_cparams(1),
        name="router",
    )(h, rwt, bias, tri, below)


ROW_SUB = D_MODEL // LANES


def _store_row_tiles(ref, x):
    rows = x.shape[0]
    for c in range(ROW_SUB):
        ref[pl.ds(c, rows, stride=ROW_SUB), :] = x[:, c * LANES:(c + 1) * LANES]


def _load_row_tiles(ref):
    rows = ref.shape[0] // ROW_SUB
    return jnp.concatenate([ref[pl.ds(c, rows, stride=ROW_SUB), :] for c in range(ROW_SUB)], axis=-1)


RUN_FIELDS = 3
RUN_UNROLL = 8


def _start_run_copies(runs_ref, tile, local_ref, sorted_ref, sem, to_sorted):
    def body(e, c):
        size = pl.multiple_of(runs_ref[tile * RUN_FIELDS + 2, e], ROW_SUB)
        local = local_ref.at[pl.ds(pl.multiple_of(runs_ref[tile * RUN_FIELDS, e], ROW_SUB), size), :]
        remote = sorted_ref.at[pl.ds(pl.multiple_of(runs_ref[tile * RUN_FIELDS + 1, e], ROW_SUB), size), :]
        (pltpu.make_async_copy(local, remote, sem) if to_sorted else pltpu.make_async_copy(remote, local, sem)).start()
        return c

    lax.fori_loop(0, N_EXPERTS, body, 0, unroll=RUN_UNROLL)


def _wait_run_copies(local_ref, sorted_ref, sem):
    pltpu.make_async_copy(sorted_ref.at[pl.ds(0, local_ref.shape[0]), :], local_ref, sem).wait()


def _dispatch_kernel(zpos_ref, runs_ref, slot_ref, h_ref, xs_hbm, zero_buf, rows_buf, sem_zero, sem_rows, *, td):
    i = pl.program_id(0)

    def zero_copy(e):
        start = pl.multiple_of(zpos_ref[e] * ROW_SUB, CHUNK * ROW_SUB)
        return pltpu.make_async_copy(zero_buf, xs_hbm.at[pl.ds(start, CHUNK * ROW_SUB), :], sem_zero)

    @pl.when(i == 0)
    def _():
        zero_buf[...] = jnp.zeros_like(zero_buf)

        def start(e, c):
            @pl.when(zpos_ref[e] >= 0)
            def _():
                zero_copy(e).start()
            return c

        def wait(e, c):
            @pl.when(zpos_ref[e] >= 0)
            def _():
                zero_copy(e).wait()
            return c

        lax.fori_loop(0, N_EXPERTS, start, 0)
        lax.fori_loop(0, N_EXPERTS, wait, 0)

    slots = slot_ref[...]
    slot_iota = lax.broadcasted_iota(I32, (TOP_K * td, td), 0)
    select = jnp.zeros((TOP_K * td, td), F32)
    for j in range(TOP_K):
        select = jnp.where(slot_iota == slots[j:j + 1, :], 1.0, select)
    local = jnp.dot(select.astype(BF16), h_ref[...].astype(BF16), preferred_element_type=F32)
    _store_row_tiles(rows_buf, local)
    _start_run_copies(runs_ref, i, rows_buf, xs_hbm, sem_rows, True)
    _wait_run_copies(rows_buf, xs_hbm, sem_rows)


def _dispatch(zpos, runs, slots, h, n_rows, td):
    n = h.shape[0]
    assert n % td == 0
    return pl.pallas_call(
        functools.partial(_dispatch_kernel, td=td),
        grid_spec=pltpu.PrefetchScalarGridSpec(
            num_scalar_prefetch=2,
            grid=(n // td,),
            in_specs=[pl.BlockSpec((TOP_K, td), lambda i, z, r: (0, i)),
                      pl.BlockSpec((td, D_MODEL), lambda i, z, r: (i, 0))],
            out_specs=pl.BlockSpec(memory_space=pl.ANY),
            scratch_shapes=[pltpu.VMEM((CHUNK * ROW_SUB, LANES), F32), pltpu.VMEM((TOP_K * td * ROW_SUB, LANES), F32),
                            pltpu.SemaphoreType.DMA, pltpu.SemaphoreType.DMA],
        ),
        out_shape=jax.ShapeDtypeStruct((n_rows * ROW_SUB, LANES), F32),
        compiler_params=_cparams(1),
        name="dispatch",
    )(zpos, runs, slots, h)


CHUNK_ROWS = CHUNK * ROW_SUB
RING = 4


def _ffn_kernel(first_ref, count_ref, nu_ref, xs_hbm, wg_ref, wu_ref, wd_ref, ys_hbm, xbuf, obuf, xsem, osem):
    e = pl.program_id(0)
    n_used = nu_ref[0]

    def x_copy(g, slot):
        src = xs_hbm.at[pl.ds(pl.multiple_of(g * CHUNK_ROWS, CHUNK_ROWS), CHUNK_ROWS), :]
        return pltpu.make_async_copy(src, xbuf.at[slot], xsem.at[slot])

    def o_copy(g, slot):
        dst = ys_hbm.at[pl.ds(pl.multiple_of(g * CHUNK_ROWS, CHUNK_ROWS), CHUNK_ROWS), :]
        return pltpu.make_async_copy(obuf.at[slot], dst, osem.at[slot])

    @pl.when(e == 0)
    def _():
        for g in range(RING - 1):
            @pl.when(g < n_used)
            def _():
                x_copy(g, g).start()

    wg = wg_ref[0].astype(BF16)
    wu = wu_ref[0].astype(BF16)
    wd = wd_ref[0].astype(BF16)

    def chunk(c, carry):
        g = first_ref[e] + c
        slot = g & (RING - 1)
        x_copy(g, slot).wait()

        @pl.when(g + RING - 1 < n_used)
        def _():
            x_copy(g + RING - 1, (g + RING - 1) & (RING - 1)).start()

        @pl.when(g >= RING)
        def _():
            o_copy(g - RING, slot).wait()

        x = _load_row_tiles(xbuf.at[slot]).astype(BF16)
        gate = jnp.dot(x, wg, preferred_element_type=F32)
        up = jnp.dot(x, wu, preferred_element_type=F32)
        act = (gate * _sigmoid(gate)) * up
        _store_row_tiles(obuf.at[slot], jnp.dot(act.astype(BF16), wd, preferred_element_type=F32))
        o_copy(g, slot).start()
        return carry

    lax.fori_loop(0, count_ref[e], chunk, 0)

    @pl.when(e == N_EXPERTS - 1)
    def _():
        for back in range(RING, 0, -1):
            @pl.when(n_used >= back)
            def _():
                o_copy(n_used - back, (n_used - back) & (RING - 1)).wait()


def _ffn(first_chunk, chunk_count, n_used, xs, wg, wu, wd):
    weights = lambda a, b: pl.BlockSpec((1, a, b), lambda e, f, c, nu: (e, 0, 0))
    return pl.pallas_call(
        _ffn_kernel,
        grid_spec=pltpu.PrefetchScalarGridSpec(
            num_scalar_prefetch=3,
            grid=(N_EXPERTS,),
            in_specs=[pl.BlockSpec(memory_space=pl.ANY), weights(D_MODEL, EXPERT_FF), weights(D_MODEL, EXPERT_FF),
                      weights(EXPERT_FF, D_MODEL)],
            out_specs=pl.BlockSpec(memory_space=pl.ANY),
            scratch_shapes=[pltpu.VMEM((RING, CHUNK_ROWS, LANES), F32), pltpu.VMEM((RING, CHUNK_ROWS, LANES), F32),
                            pltpu.SemaphoreType.DMA((RING,)), pltpu.SemaphoreType.DMA((RING,))],
        ),
        out_shape=jax.ShapeDtypeStruct(xs.shape, F32),
        compiler_params=_cparams(1),
        name="expert_ffn",
    )(first_chunk, chunk_count, n_used, xs, wg, wu, wd)


def _combine_kernel(runs_ref, slot_ref, gate_ref, h_ref, ys_hbm, sg_ref, su_ref, sd_ref, l2g_ref, l2b_ref, o_ref,
                    rows_buf, sem_rows, *, tc, alpha):
    i = pl.program_id(0)
    _start_run_copies(runs_ref, i, rows_buf, ys_hbm, sem_rows, False)
    h = h_ref[...]
    hb = h.astype(BF16)
    sgate = jnp.dot(hb, sg_ref[...], preferred_element_type=F32)
    sup = jnp.dot(hb, su_ref[...], preferred_element_type=F32)
    shared = jnp.dot(((sgate * _sigmoid(sgate)) * sup).astype(BF16), sd_ref[...], preferred_element_type=F32)
    slots = slot_ref[...]
    gate = gate_ref[...]
    slot_iota = lax.broadcasted_iota(I32, (tc, TOP_K * tc), 1)
    weights = jnp.zeros((tc, TOP_K * tc), F32)
    for j in range(TOP_K):
        weights = jnp.where(slot_iota == slots[:, j:j + 1], gate[:, j:j + 1], weights)
    w_hi = weights.astype(BF16)
    w_lo = (weights - w_hi.astype(F32)).astype(BF16)
    _wait_run_copies(rows_buf, ys_hbm, sem_rows)
    local = _load_row_tiles(rows_buf).astype(BF16)
    routed = (jnp.dot(w_hi, local, preferred_element_type=F32) + jnp.dot(w_lo, local, preferred_element_type=F32))
    o_ref[...] = _layer_norm(alpha * h + (routed + shared), l2g_ref[...], l2b_ref[...])


def _combine(runs, slots, gate, h, ys, sg, su, sd, l2g, l2b, tc, alpha):
    n = h.shape[0]
    assert n % tc == 0
    tok = lambda w: pl.BlockSpec((tc, w), lambda i, r: (i, 0))
    full = lambda a: pl.BlockSpec(a.shape, lambda i, r: (0, 0))
    return pl.pallas_call(
        functools.partial(_combine_kernel, tc=tc, alpha=alpha),
        grid_spec=pltpu.PrefetchScalarGridSpec(
            num_scalar_prefetch=1,
            grid=(n // tc,),
            in_specs=[tok(TOP_K), tok(TOP_K), tok(D_MODEL), pl.BlockSpec(memory_space=pl.ANY),
                      full(sg), full(su), full(sd), full(l2g), full(l2b)],
            out_specs=tok(D_MODEL),
            scratch_shapes=[pltpu.VMEM((TOP_K * tc * ROW_SUB, LANES), F32), pltpu.SemaphoreType.DMA],
        ),
        out_shape=jax.ShapeDtypeStruct((n, D_MODEL), F32),
        compiler_params=_cparams(1),
        name="combine",
    )(runs, slots, gate, h, ys, sg, su, sd, l2g, l2b)


def _pad_cols(a, width):
    return jnp.pad(a, ((0, 0), (0, width - a.shape[1])))


def _pack_shift_cols(a):
    c = 3 * RWKV_W
    return jnp.concatenate([a[:, :c], _pad_cols(a[:, c:c + DECAY_LORA], LORA_PAD),
                            _pad_cols(a[:, c + DECAY_LORA:c + DECAY_LORA + AAA_LORA], LORA_PAD),
                            _pad_cols(a[:, c + DECAY_LORA + AAA_LORA:], GATE_PAD)], axis=1)


def _keys_to_lanes(a, b, t):
    return a.reshape(b, t, N_HEADS, HEAD).transpose(1, 3, 0, 2).reshape(t, HEAD, b * N_HEADS)


def _values_to_lanes(a, b, t, split):
    a = a.reshape(b, t, N_HEADS, split, HEAD // split).transpose(1, 4, 3, 0, 2)
    return a.reshape(t, HEAD // split, split * b * N_HEADS)


def _values_from_lanes(y, b, t, split):
    y = y.reshape(t, HEAD // split, split, b, N_HEADS).transpose(3, 0, 4, 2, 1)
    return y.reshape(b * t, RWKV_W)


def _state_to_lanes(s, b, split):
    s = s.reshape(b, N_HEADS, split, HEAD // split, HEAD).transpose(4, 3, 2, 0, 1)
    return s.reshape(HEAD, HEAD // split, split * b * N_HEADS)


def _state_from_lanes(s, b, split):
    s = s.reshape(HEAD, HEAD // split, split, b, N_HEADS).transpose(3, 4, 2, 1, 0)
    return s.reshape(b, N_HEADS, HEAD, HEAD)


def _mixer(z, zprev, x2, b, t, wkv0, pool_prev, mk, mv, pos0, wts, tiles):
    n = b * t
    tt, tb, pool_bb, att_tt = tiles
    z3 = z[:n].reshape(b, t, Z_W)
    r, w, kp, v, kk, nb, g, bonus = _rwkv_prep(z, zprev, b, t, tt, wts["prep"])

    split = 1 if (b * N_HEADS) % LANES == 0 else LANES // (b * N_HEADS)
    lay = lambda a: _keys_to_lanes(a, b, t).reshape(t, HEAD // split, split * b * N_HEADS)
    key_order = jnp.concatenate([jnp.arange(i, HEAD, split) for i in range(split)])
    s_in = wkv0[..., key_order] if split > 1 else wkv0
    y_l, s_l = _wkv(lay(r), lay(w), lay(kp), lay(kk), lay(nb), _values_to_lanes(v, b, t, split),
                    _state_to_lanes(s_in, b, split), tb)
    y = _values_from_lanes(y_l, b, t, split)
    wkv_new = _state_from_lanes(s_l, b, split)
    if split > 1:
        wkv_new = wkv_new[..., jnp.argsort(key_order)]

    zp = z3[:, :, ZS_W:ZS_W + POOL_W]
    full = jnp.concatenate([jnp.zeros((b, POOL_PAD - POOL_BUF, POOL_W), F32), pool_prev, zp], axis=1)
    o_pool = _pool(full, wts["pool_w"], wts["pool_scale"], pool_bb, pos0).reshape(n, POOL_W)
    pool_new = full[:, -POOL_BUF:]

    o_mem = _memattn(z3, mk, mv, att_tt).reshape(n, MEM_W)
    h = _post(y, bonus, g, o_pool, o_mem, x2, *wts["post"], tt, wts["alpha"])
    return h, wkv_new, pool_new


def kernel(x_prompt, x_sample, mem_prompt, state_wkv, state_shift, state_pool, cache_mem_k, cache_mem_v, w_in, mu_shift, w0, w_up_decay, a0, w_up_aaa, w_up_gate, k_k, k_a, r_k, ln_x_w, ln_x_b, pool_w, pool_scale, mem_wk, mem_wv, w_out, ln1_g, ln1_b, router_w, router_b, exp_gate, exp_up, exp_down, sh_gate, sh_up, sh_down, ln2_g, ln2_b):
    depth = w_in.shape[0]
    assert depth == 1
    l = 0
    alpha = (2.0 * depth) ** 0.25
    bp, tp, d = x_prompt.shape
    bs, ts, _ = x_sample.shape
    n_p, n_s = bp * tp, bs * ts
    n = n_p + n_s

    w_in_p = jnp.concatenate([_pack_shift_cols(w_in[l][:, :1824]), w_in[l][:, 1824:]], axis=1).astype(BF16)
    row = lambda a: a.reshape(1, -1)
    pad_rows = lambda a, rows: jnp.pad(a, ((0, rows - a.shape[0]), (0, 0)))
    bd = jnp.kron(jnp.eye(N_HEADS, dtype=F32), jnp.ones((HEAD, HEAD), F32)).astype(BF16)
    pw = pool_w[l]
    pw_bd = jnp.zeros((POOL_W, POOL_W), F32)
    for gi in range(4):
        pw_bd = pw_bd.at[gi * 64:(gi + 1) * 64, gi * 64:(gi + 1) * 64].set(pw[gi])
    wts = {
        "prep": (_pack_shift_cols(row(mu_shift[l])), row(w0[l]), row(a0[l]), pad_rows(w_up_decay[l], LORA_PAD),
                 pad_rows(w_up_aaa[l], LORA_PAD), pad_rows(w_up_gate[l], GATE_PAD), row(k_k[l]), row(k_a[l]),
                 row(r_k[l]), bd),
        "pool_w": pw_bd.astype(BF16), "pool_scale": row(pool_scale[l]),
        "post": (w_out[l].astype(BF16), row(ln_x_w[l]), row(ln_x_b[l]), row(ln1_g[l]), row(ln1_b[l]), bd),
        "alpha": alpha,
    }

    xp2 = x_prompt.reshape(n_p, d)
    xs2 = x_sample.reshape(n_s, d)
    z_p = _matmul(xp2, w_in_p, _tile(n_p, 512))
    z_s = _matmul(jnp.concatenate([xs2, state_shift[l]], axis=0), w_in_p, _tile(n_s + bs, 512))
    mkv = _matmul(mem_prompt.reshape(bp * MEM_TOKENS, d),
                  jnp.concatenate([mem_wk[l], mem_wv[l]], axis=1).astype(BF16), _tile(bp * MEM_TOKENS, 512))
    mk_p = mkv[:, :MEM_W].reshape(bp, MEM_TOKENS, MEM_W)
    mv_p = mkv[:, MEM_W:].reshape(bp, MEM_TOKENS, MEM_W)

    h_p, wkv_p, pool_p = _mixer(z_p, jnp.zeros((bp, ZS_W), F32), xp2, bp, tp,
                                jnp.zeros((bp, N_HEADS, HEAD, HEAD), F32), jnp.zeros((bp, POOL_BUF, POOL_W), F32),
                                mk_p, mv_p, 0, wts,
                                (_tile(n_p, 256), _tile(tp, 32, 1), 1, _tile(tp, 256)))
    h_s, wkv_s, pool_s = _mixer(z_s, z_s[n_s:, :ZS_W], xs2, bs, ts, state_wkv[l], state_pool[l],
                                cache_mem_k[l].reshape(bs, MEM_TOKENS, MEM_W),
                                cache_mem_v[l].reshape(bs, MEM_TOKENS, MEM_W), PAST_LEN, wts,
                                (_tile(n_s, 256), _tile(ts, 32, 1), _tile(bs, 16, 1), _tile(ts, 256)))

    h = jnp.concatenate([h_p, h_s], axis=0)
    td = _tile(n, 256, LANES)
    slots, gate, run_tab, cnt = _router(h, router_w[l].T, router_b[l].reshape(N_EXPERTS, 1), td)
    counts = cnt[:, 0].astype(I32)
    padded = (counts + CHUNK - 1) // CHUNK * CHUNK
    pad_end = jnp.cumsum(padded)
    pad_start = pad_end - padded
    n_chunks = (n * TOP_K + N_EXPERTS * (CHUNK - 1) + CHUNK - 1) // CHUNK
    run_tab = run_tab.reshape(n // td, N_EXPERTS, LANES)[:, :, :RUN_FIELDS].astype(I32)
    runs = jnp.stack([run_tab[:, :, 1], pad_start[None, :] + run_tab[:, :, 2], run_tab[:, :, 0]], axis=1)
    runs = runs.reshape(n // td * RUN_FIELDS, N_EXPERTS) * ROW_SUB
    zpos = jnp.where(padded > 0, pad_end - CHUNK, -1).astype(I32)
    n_used = (pad_end[-1:] // CHUNK).astype(I32)
    xs_sorted = _dispatch(zpos, runs, slots, h, n_chunks * CHUNK, td)
    ys_sorted = _ffn((pad_start // CHUNK).astype(I32), (padded // CHUNK).astype(I32), n_used, xs_sorted,
                     exp_gate[l], exp_up[l], exp_down[l])
    y = _combine(runs, slots.T, gate.T, h, ys_sorted, sh_gate[l].astype(BF16), sh_up[l].astype(BF16),
                 sh_down[l].astype(BF16), row(ln2_g[l]), row(ln2_b[l]), td, alpha)

    return (y[:n_p].reshape(bp, tp, d), y[n_p:].reshape(bs, ts, d),
            wkv_p[None], x_prompt[:, -1][None], pool_p[None],
            mk_p.reshape(bp, MEM_TOKENS, MEM_HEADS, HEAD)[None], mv_p.reshape(bp, MEM_TOKENS, MEM_HEADS, HEAD)[None],
            wkv_s[None], x_sample[:, -1][None], pool_s[None])
```

```python
import functools

import jax
import jax.numpy as jnp
from jax import lax
from jax.experimental import pallas as pl
from jax.experimental.pallas import tpu as pltpu

F32 = jnp.float32
BF16 = jnp.bfloat16
I32 = jnp.int32
HIGHEST = lax.Precision.HIGHEST

D_MODEL = 1024
HEAD = 64
N_HEADS = 8
RWKV_W = 512
POOL_W = 256
MEM_W = 256
MEM_HEADS = 4
MEM_TOKENS = 256
POOL_BUF = 15
POOL_PAD = 16
DECAY_LORA = 64
AAA_LORA = 64
GATE_LORA = 160
LORA_PAD = 128
GATE_PAD = 256
ZS_W = 3 * RWKV_W + 2 * LORA_PAD + GATE_PAD
Z_W = ZS_W + POOL_W + MEM_W
N_EXPERTS = 256
TOP_K = 8
N_GROUPS = 8
GROUP_SIZE = N_EXPERTS // N_GROUPS
TOPK_GROUPS = 4
EXPERT_FF = 256
ROUTED_SCALE = 2.5
CHUNK = 256
PAST_LEN = 16384
LN_EPS = 1e-5
GN_EPS = 64e-5
LANES = 128
SUBLANES = 8
VMEM_LIMIT = 56 * 1024 * 1024


def _cparams(n_axes):
    return pltpu.CompilerParams(dimension_semantics=("arbitrary",) * n_axes, vmem_limit_bytes=VMEM_LIMIT)


def _tile(n, preferred, multiple=8):
    best = None
    for c in range(multiple, min(n, preferred) + 1, multiple):
        if n % c == 0:
            best = c
    assert best is not None, (n, preferred, multiple)
    return best


def _sigmoid(x):
    return 1.0 / (1.0 + jnp.exp(-x))


def _matmul_kernel(x_ref, w_ref, o_ref):
    o_ref[...] = jnp.dot(x_ref[...].astype(BF16), w_ref[...], preferred_element_type=F32)


def _matmul(x, w, tm):
    m, k = x.shape
    n = w.shape[1]
    assert m % tm == 0
    return pl.pallas_call(
        _matmul_kernel,
        grid=(m // tm,),
        in_specs=[pl.BlockSpec((tm, k), lambda i: (i, 0)), pl.BlockSpec((k, n), lambda i: (0, 0))],
        out_specs=pl.BlockSpec((tm, n), lambda i: (i, 0)),
        out_shape=jax.ShapeDtypeStruct((m, n), F32),
        compiler_params=_cparams(1),
        name="matmul",
    )(x, w)


def _segsum(x, bd):
    hi = x.astype(BF16)
    lo = (x - hi.astype(F32)).astype(BF16)
    return jnp.dot(hi, bd, preferred_element_type=F32) + jnp.dot(lo, bd, preferred_element_type=F32)


def _prep_kernel(zs_ref, zp_ref, mu_ref, w0_ref, a0_ref, wd_ref, wa_ref, wg_ref, kk_ref, ka_ref, rk_ref, bd_ref,
                 r_o, w_o, k_o, v_o, kk_o, nb_o, g_o, bonus_o, carry_ref, *, tiles_per_seq, seq_len):
    zs = zs_ref[...]
    rolled = pltpu.roll(zs, 1, 0)
    row = lax.broadcasted_iota(I32, zs.shape, 0)
    if tiles_per_seq:
        i = pl.program_id(0)

        @pl.when(i == 0)
        def _():
            carry_ref[...] = jnp.zeros_like(carry_ref)

        prev = jnp.where(i % tiles_per_seq == 0, zp_ref[0], carry_ref[...])
        shifted = jnp.where(row == 0, prev, rolled)
        carry_ref[...] = zs[zs.shape[0] - 1:, :]
    else:
        shifted = jnp.where((row & (seq_len - 1)) == 0, zp_ref[...], rolled)
    zm = zs + mu_ref[...] * (shifted - zs)
    r = zm[:, 0:RWKV_W]
    k = zm[:, RWKV_W:2 * RWKV_W]
    v = zm[:, 2 * RWKV_W:3 * RWKV_W]
    c0 = 3 * RWKV_W
    xw = zm[:, c0:c0 + LORA_PAD]
    xa = zm[:, c0 + LORA_PAD:c0 + 2 * LORA_PAD]
    xg = zm[:, c0 + 2 * LORA_PAD:c0 + 2 * LORA_PAD + GATE_PAD]
    bd = bd_ref[...]
    u = -(w0_ref[...] + jnp.dot(jnp.tanh(xw), wd_ref[...], precision=HIGHEST, preferred_element_type=F32))
    softplus = jnp.maximum(u, 0.0) + jnp.log(1.0 + jnp.exp(-jnp.abs(u)))
    decay = jnp.exp(-jnp.exp(-softplus - 0.5))
    a = _sigmoid(a0_ref[...] + jnp.dot(xa, wa_ref[...], precision=HIGHEST, preferred_element_type=F32))
    g = jnp.dot(_sigmoid(xg), wg_ref[...], precision=HIGHEST, preferred_element_type=F32)
    kk = k * kk_ref[...]
    kk = kk / jnp.maximum(jnp.sqrt(_segsum(kk * kk, bd)), 1e-12)
    kp = k * (1.0 + (a - 1.0) * ka_ref[...])
    r_o[...] = r
    w_o[...] = decay
    k_o[...] = kp
    v_o[...] = v
    kk_o[...] = kk
    nb_o[...] = -(kk * a)
    g_o[...] = g
    bonus_o[...] = _segsum(r * kp * rk_ref[...], bd) * v


def _rwkv_prep(z, zprev, b, t, tt, params):
    n_tok = b * t
    assert n_tok % tt == 0
    row = lambda w: pl.BlockSpec((1, w), lambda i: (0, 0))
    full = lambda a: pl.BlockSpec(a.shape, lambda i: (0, 0))
    tok = pl.BlockSpec((tt, RWKV_W), lambda i: (i, 0))
    mu, w0, a0, wd, wa, wg, k_k, k_a, r_k, bd = params
    if t % tt == 0:
        tiles_per_seq = t // tt
        zp = zprev.reshape(b, 1, ZS_W)
        zp_spec = pl.BlockSpec((1, 1, ZS_W), lambda i: (i // tiles_per_seq, 0, 0))
    else:
        assert tt % t == 0 and t & (t - 1) == 0
        tiles_per_seq = 0
        zp = jnp.repeat(zprev, t, axis=0)
        zp_spec = pl.BlockSpec((tt, ZS_W), lambda i: (i, 0))
    return pl.pallas_call(
        functools.partial(_prep_kernel, tiles_per_seq=tiles_per_seq, seq_len=t),
        grid=(n_tok // tt,),
        in_specs=[pl.BlockSpec((tt, ZS_W), lambda i: (i, 0)), zp_spec,
                  row(ZS_W), row(RWKV_W), row(RWKV_W), full(wd), full(wa), full(wg),
                  row(RWKV_W), row(RWKV_W), row(RWKV_W), full(bd)],
        out_specs=[tok] * 8,
        out_shape=[jax.ShapeDtypeStruct((n_tok, RWKV_W), F32)] * 8,
        scratch_shapes=[pltpu.VMEM((1, ZS_W), F32)],
        compiler_params=_cparams(1),
        name="rwkv_prep",
    )(z, zp, mu, w0, a0, wd, wa, wg, k_k, k_a, r_k, bd)


def _wkv_kernel(r_ref, w_ref, k_ref, kk_ref, nb_ref, v_ref, s0_ref, y_ref, s_ref, *scratch, tb, dup):
    slabs = s_ref.shape[1]
    key_unroll = 8

    @pl.when(pl.program_id(1) == 0)
    def _():
        s_ref[...] = s0_ref[...]

    if dup:
        for src, dst in zip((r_ref, w_ref, k_ref, kk_ref, nb_ref), scratch):
            x = src[...].reshape(tb * (HEAD // 2), LANES)
            swapped = pltpu.roll(x, LANES // 2, 1)
            low = lax.broadcasted_iota(I32, x.shape, 1) < LANES // 2
            dst[:, 0:HEAD // 2, :] = jnp.where(low, x, swapped).reshape(tb, HEAD // 2, LANES)
            dst[:, HEAD // 2:, :] = jnp.where(low, swapped, x).reshape(tb, HEAD // 2, LANES)
        r_ref, w_ref, k_ref, kk_ref, nb_ref = scratch

    def rows(ref, t, k):
        return jnp.broadcast_to(ref[t, pl.ds(k, 1), :], (SUBLANES, LANES))

    def step(t, sa, with_next):
        vt = [v_ref[t, i * SUBLANES:(i + 1) * SUBLANES, :] for i in range(slabs)]
        zero = jnp.zeros((SUBLANES, LANES), F32)

        def key(k, acc):
            y_acc, next_acc = list(acc[0]), list(acc[1])
            wb, nbb, kb, rb = rows(w_ref, t, k), rows(nb_ref, t, k), rows(k_ref, t, k), rows(r_ref, t, k)
            kkb = rows(kk_ref, t + 1, k) if with_next else None
            for i in range(slabs):
                sn = s_ref[k, i] * wb + sa[i] * nbb + vt[i] * kb
                s_ref[k, i] = sn
                y_acc[i] = y_acc[i] + sn * rb
                if with_next:
                    next_acc[i] = next_acc[i] + sn * kkb
            return tuple(y_acc), tuple(next_acc)

        y_acc, next_acc = lax.fori_loop(0, HEAD, key, ((zero,) * slabs, (zero,) * slabs), unroll=key_unroll)
        for i, y in enumerate(y_acc):
            y_ref[t, i * SUBLANES:(i + 1) * SUBLANES, :] = y
        return next_acc

    def first(k, acc):
        kkb = rows(kk_ref, 0, k)
        return tuple(a + s_ref[k, i] * kkb for i, a in enumerate(acc))

    sa0 = lax.fori_loop(0, HEAD, first, (jnp.zeros((SUBLANES, LANES), F32),) * slabs, unroll=key_unroll)
    sa_last = lax.fori_loop(0, tb - 1, lambda t, sa: step(t, sa, True), sa0)
    step(tb - 1, sa_last, False)


def _wkv(r, w, k, kk, nb, v, s0, tb):
    t, nk, _ = r.shape
    vr, l = v.shape[1], v.shape[2]
    dup = nk != HEAD
    assert t % tb == 0 and l % LANES == 0 and vr % SUBLANES == 0 and (not dup or (l == LANES and 2 * nk == HEAD))
    vec = pl.BlockSpec((tb, nk, LANES), lambda g, i: (i, 0, g))
    val = pl.BlockSpec((tb, vr, LANES), lambda g, i: (i, 0, g))
    slabs = vr // SUBLANES
    st = pl.BlockSpec((HEAD, slabs, SUBLANES, LANES), lambda g, i: (0, 0, 0, g))
    s0 = s0.reshape(HEAD, slabs, SUBLANES, l)
    y, s_new = pl.pallas_call(
        functools.partial(_wkv_kernel, tb=tb, dup=dup),
        grid=(l // LANES, t // tb),
        in_specs=[vec, vec, vec, vec, vec, val, st],
        out_specs=[val, st],
        out_shape=[jax.ShapeDtypeStruct(v.shape, F32), jax.ShapeDtypeStruct(s0.shape, F32)],
        scratch_shapes=[pltpu.VMEM((tb, HEAD, LANES), F32)] * 5 if dup else [],
        compiler_params=_cparams(2),
        name="wkv",
    )(r, w, k, kk, nb, v, s0)
    return y, s_new.reshape(HEAD, vr, l)


def _pool_kernel(full_ref, pw_ref, ps_ref, o_ref, *, t, pos0):
    bb = full_ref.shape[0]
    lane = lax.broadcasted_iota(I32, (t, POOL_W), 1)
    window = jnp.where(lane < 64, 2, jnp.where(lane < 128, 4, jnp.where(lane < 192, 8, 16)))
    pos = lax.broadcasted_iota(I32, (t, POOL_W), 0) + (pos0 + 1)
    cnt = jnp.minimum(pos, window).astype(F32)
    diffs = []
    for i in range(bb):
        f = full_ref[i]
        s2 = f + pltpu.roll(f, 1, 0)
        s4 = s2 + pltpu.roll(s2, 2, 0)
        s8 = s4 + pltpu.roll(s4, 4, 0)
        s16 = s8 + pltpu.roll(s8, 8, 0)
        wsum = jnp.where(lane < 64, s2[POOL_PAD:], jnp.where(lane < 128, s4[POOL_PAD:],
                                                            jnp.where(lane < 192, s8[POOL_PAD:], s16[POOL_PAD:])))
        diffs.append(wsum / cnt - f[POOL_PAD:])
    diff = diffs[0] if bb == 1 else jnp.concatenate(diffs, axis=0)
    out = jnp.dot(diff.astype(BF16), pw_ref[...], preferred_element_type=F32) * ps_ref[...]
    for i in range(bb):
        o_ref[i] = out[i * t:(i + 1) * t]


def _pool(full, pw_bd, pscale, bb, pos0):
    b, rows, _ = full.shape
    t = rows - POOL_PAD
    assert b % bb == 0
    return pl.pallas_call(
        functools.partial(_pool_kernel, t=t, pos0=pos0),
        grid=(b // bb,),
        in_specs=[pl.BlockSpec((bb, rows, POOL_W), lambda i: (i, 0, 0)),
                  pl.BlockSpec((POOL_W, POOL_W), lambda i: (0, 0)), pl.BlockSpec((1, POOL_W), lambda i: (0, 0))],
        out_specs=pl.BlockSpec((bb, t, POOL_W), lambda i: (i, 0, 0)),
        out_shape=jax.ShapeDtypeStruct((b, t, POOL_W), F32),
        compiler_params=_cparams(1),
        name="pool",
    )(full, pw_bd, pscale)


def _memattn_kernel(q_ref, k_ref, v_ref, o_ref):
    q = q_ref[0].astype(BF16)
    kf = k_ref[0]
    vf = v_ref[0]
    head_of_lane = lax.broadcasted_iota(I32, kf.shape, 1) // (MEM_W // MEM_HEADS)
    out = None
    for h in range(MEM_HEADS):
        kh = jnp.where(head_of_lane == h, kf, 0.0).astype(BF16)
        vh = jnp.where(head_of_lane == h, vf, 0.0).astype(BF16)
        s = lax.dot_general(q, kh, (((1,), (1,)), ((), ())), preferred_element_type=F32) * (64 ** -0.5)
        e = jnp.exp(s - jnp.max(s, axis=-1, keepdims=True))
        p = e / jnp.sum(e, axis=-1, keepdims=True)
        o = jnp.dot(p.astype(BF16), vh, preferred_element_type=F32)
        out = o if out is None else out + o
    o_ref[0] = out


def _memattn(z3, mk, mv, tt):
    b, t, _ = z3.shape
    assert t % tt == 0
    qcol = (Z_W - MEM_W) // MEM_W
    kv = pl.BlockSpec((1, MEM_TOKENS, MEM_W), lambda i, j: (i, 0, 0))
    return pl.pallas_call(
        _memattn_kernel,
        grid=(b, t // tt),
        in_specs=[pl.BlockSpec((1, tt, MEM_W), lambda i, j: (i, j, qcol)), kv, kv],
        out_specs=pl.BlockSpec((1, tt, MEM_W), lambda i, j: (i, j, 0)),
        out_shape=jax.ShapeDtypeStruct((b, t, MEM_W), F32),
        compiler_params=_cparams(2),
        name="mem_attn",
    )(z3, mk, mv)


def _layer_norm(x, g, b):
    mu = jnp.mean(x, axis=-1, keepdims=True)
    xc = x - mu
    var = jnp.mean(xc * xc, axis=-1, keepdims=True)
    return xc * lax.rsqrt(var + LN_EPS) * g + b


def _post_kernel(y_ref, bonus_ref, g_ref, op_ref, om_ref, x_ref, wo_ref, lxw_ref, lxb_ref, l1g_ref, l1b_ref, bd_ref,
                 h_ref, *, alpha):
    y = y_ref[...]
    bd = bd_ref[...]
    mu = _segsum(y, bd) * (1.0 / HEAD)
    yc = y - mu
    var = _segsum(yc * yc, bd) * (1.0 / HEAD)
    yn = yc * lax.rsqrt(var + GN_EPS) * lxw_ref[...] + lxb_ref[...]
    o_rwkv = (yn + bonus_ref[...]) * g_ref[...]
    mixed = (jnp.dot(o_rwkv.astype(BF16), wo_ref[0:RWKV_W, :], preferred_element_type=F32)
             + jnp.dot(op_ref[...].astype(BF16), wo_ref[RWKV_W:RWKV_W + POOL_W, :], preferred_element_type=F32)
             + jnp.dot(om_ref[...].astype(BF16), wo_ref[RWKV_W + POOL_W:, :], preferred_element_type=F32))
    h_ref[...] = _layer_norm(alpha * x_ref[...] + mixed, l1g_ref[...], l1b_ref[...])


def _post(y, bonus, g, o_pool, o_mem, x, wo, lxw, lxb, l1g, l1b, bd, tt, alpha):
    n = y.shape[0]
    assert n % tt == 0
    tok = lambda w: pl.BlockSpec((tt, w), lambda i: (i, 0))
    row = lambda w: pl.BlockSpec((1, w), lambda i: (0, 0))
    full = lambda a: pl.BlockSpec(a.shape, lambda i: (0, 0))
    return pl.pallas_call(
        functools.partial(_post_kernel, alpha=alpha),
        grid=(n // tt,),
        in_specs=[tok(RWKV_W), tok(RWKV_W), tok(RWKV_W), tok(POOL_W), tok(MEM_W), tok(D_MODEL), full(wo),
                  row(RWKV_W), row(RWKV_W), row(D_MODEL), row(D_MODEL), full(bd)],
        out_specs=tok(D_MODEL),
        out_shape=jax.ShapeDtypeStruct((n, D_MODEL), F32),
        compiler_params=_cparams(1),
        name="post",
    )(y, bonus, g, o_pool, o_mem, x, wo, lxw, lxb, l1g, l1b, bd)


def _router_kernel(h_ref, rwt_ref, bias_ref, tri_ref, below_ref, slot_o, gate_o, run_o, cnt_o, carry_ref):
    @pl.when(pl.program_id(0) == 0)
    def _():
        carry_ref[...] = jnp.zeros_like(carry_ref)

    neg = -jnp.inf
    logits = lax.dot_general(rwt_ref[...], h_ref[...], (((1,), (1,)), ((), ())),
                             precision=HIGHEST, preferred_element_type=F32)
    scores = _sigmoid(logits)
    sel = scores + bias_ref[...]
    tt = sel.shape[1]
    gio = lax.broadcasted_iota(I32, (GROUP_SIZE, tt), 0).astype(F32)
    blocks, gscore = [], []
    for g in range(N_GROUPS):
        blk = sel[g * GROUP_SIZE:(g + 1) * GROUP_SIZE, :]
        m1 = jnp.max(blk, axis=0, keepdims=True)
        first = jnp.min(jnp.where(blk == m1, gio, float(GROUP_SIZE)), axis=0, keepdims=True)
        m2 = jnp.max(jnp.where(gio == first, neg, blk), axis=0, keepdims=True)
        blocks.append(blk)
        gscore.append(m1 + m2)
    masked = []
    for g in range(N_GROUPS):
        beaten_by = jnp.zeros((1, tt), F32)
        for g2 in range(N_GROUPS):
            if g2 != g:
                wins = (gscore[g2] >= gscore[g]) if g2 < g else (gscore[g2] > gscore[g])
                beaten_by = beaten_by + jnp.where(wins, 1.0, 0.0)
        masked.append(jnp.where(beaten_by < TOPK_GROUPS, blocks[g], neg))
    msel = jnp.concatenate(masked, axis=0)
    eio = lax.broadcasted_iota(I32, msel.shape, 0).astype(F32)
    chosen = jnp.zeros(msel.shape, F32)
    idxs, scs = [], []
    for _ in range(TOP_K):
        m = jnp.max(msel, axis=0, keepdims=True)
        first = jnp.min(jnp.where(msel == m, eio, float(N_EXPERTS)), axis=0, keepdims=True)
        hit = eio == first
        scs.append(jnp.sum(jnp.where(hit, scores, 0.0), axis=0, keepdims=True))
        msel = jnp.where(hit, neg, msel)
        chosen = jnp.where(hit, 1.0, chosen)
        idxs.append(first)
    total = scs[0]
    for s in scs[1:]:
        total = total + s
    chosen_b = chosen.astype(BF16)
    earlier = jnp.dot(chosen_b, tri_ref[...], preferred_element_type=F32)
    smaller = jnp.dot(below_ref[...], chosen_b, preferred_element_type=F32)
    run_len = jnp.sum(chosen, axis=1, keepdims=True)
    run_off = jnp.sum(smaller, axis=1, keepdims=True)
    slot_of = run_off + earlier
    slots = [jnp.sum(jnp.where(eio == i, slot_of, 0.0), axis=0, keepdims=True) for i in idxs]
    slot_o[...] = jnp.concatenate(slots, axis=0).astype(I32)
    gate_o[...] = jnp.concatenate([s / total * ROUTED_SCALE for s in scs], axis=0)
    lane = lax.broadcasted_iota(I32, run_o.shape, 1)
    run_o[...] = jnp.where(lane == 0, run_len, jnp.where(lane == 1, run_off, jnp.where(lane == 2, carry_ref[...], 0.0)))
    carry_ref[...] = carry_ref[...] + run_len
    cnt_o[...] = jnp.broadcast_to(carry_ref[...], cnt_o.shape)


def _router(h, rwt, bias, tt):
    n = h.shape[0]
    assert n % tt == 0
    tri = (lax.broadcasted_iota(I32, (tt, tt), 0) < lax.broadcasted_iota(I32, (tt, tt), 1)).astype(BF16)
    below = (lax.broadcasted_iota(I32, (N_EXPERTS, N_EXPERTS), 1)
             < lax.broadcasted_iota(I32, (N_EXPERTS, N_EXPERTS), 0)).astype(BF16)
    tokT = pl.BlockSpec((TOP_K, tt), lambda i: (0, i))
    return pl.pallas_call(
        _router_kernel,
        grid=(n // tt,),
        in_specs=[pl.BlockSpec((tt, D_MODEL), lambda i: (i, 0)), pl.BlockSpec((N_EXPERTS, D_MODEL), lambda i: (0, 0)),
                  pl.BlockSpec((N_EXPERTS, 1), lambda i: (0, 0)), pl.BlockSpec((tt, tt), lambda i: (0, 0)),
                  pl.BlockSpec((N_EXPERTS, N_EXPERTS), lambda i: (0, 0))],
        out_specs=[tokT, tokT, pl.BlockSpec((N_EXPERTS, LANES), lambda i: (i, 0)),
                   pl.BlockSpec((N_EXPERTS, LANES), lambda i: (0, 0))],
        out_shape=[jax.ShapeDtypeStruct((TOP_K, n), I32), jax.ShapeDtypeStruct((TOP_K, n), F32),
                   jax.ShapeDtypeStruct((n // tt * N_EXPERTS, LANES), F32),
                   jax.ShapeDtypeStruct((N_EXPERTS, LANES), F32)],
        scratch_shapes=[pltpu.VMEM((N_EXPERTS, 1), F32)],
        compiler_params=_cparams(1),
        name="router",
    )(h, rwt, bias, tri, below)


ROW_SUB = D_MODEL // LANES


def _store_row_tiles(ref, x):
    rows = x.shape[0]
    for c in range(ROW_SUB):
        ref[pl.ds(c, rows, stride=ROW_SUB), :] = x[:, c * LANES:(c + 1) * LANES]


def _load_row_tiles(ref):
    rows = ref.shape[0] // ROW_SUB
    return jnp.concatenate([ref[pl.ds(c, rows, stride=ROW_SUB), :] for c in range(ROW_SUB)], axis=-1)


RUN_FIELDS = 3
RUN_UNROLL = 8


def _start_run_copies(runs_ref, tile, local_ref, sorted_ref, sem, to_sorted):
    def body(e, c):
        size = pl.multiple_of(runs_ref[tile * RUN_FIELDS + 2, e], ROW_SUB)
        local = local_ref.at[pl.ds(pl.multiple_of(runs_ref[tile * RUN_FIELDS, e], ROW_SUB), size), :]
        remote = sorted_ref.at[pl.ds(pl.multiple_of(runs_ref[tile * RUN_FIELDS + 1, e], ROW_SUB), size), :]
        (pltpu.make_async_copy(local, remote, sem) if to_sorted else pltpu.make_async_copy(remote, local, sem)).start()
        return c

    lax.fori_loop(0, N_EXPERTS, body, 0, unroll=RUN_UNROLL)


def _wait_run_copies(local_ref, sorted_ref, sem):
    pltpu.make_async_copy(sorted_ref.at[pl.ds(0, local_ref.shape[0]), :], local_ref, sem).wait()


def _dispatch_kernel(zpos_ref, runs_ref, slot_ref, h_ref, xs_hbm, zero_buf, rows_buf, sem_zero, sem_rows, *, td):
    i = pl.program_id(0)

    def zero_copy(e):
        start = pl.multiple_of(zpos_ref[e] * ROW_SUB, CHUNK * ROW_SUB)
        return pltpu.make_async_copy(zero_buf, xs_hbm.at[pl.ds(start, CHUNK * ROW_SUB), :], sem_zero)

    @pl.when(i == 0)
    def _():
        zero_buf[...] = jnp.zeros_like(zero_buf)

        def start(e, c):
            @pl.when(zpos_ref[e] >= 0)
            def _():
                zero_copy(e).start()
            return c

        def wait(e, c):
            @pl.when(zpos_ref[e] >= 0)
            def _():
                zero_copy(e).wait()
            return c

        lax.fori_loop(0, N_EXPERTS, start, 0)
        lax.fori_loop(0, N_EXPERTS, wait, 0)

    slots = slot_ref[...]
    slot_iota = lax.broadcasted_iota(I32, (TOP_K * td, td), 0)
    select = jnp.zeros((TOP_K * td, td), F32)
    for j in range(TOP_K):
        select = jnp.where(slot_iota == slots[j:j + 1, :], 1.0, select)
    local = jnp.dot(select.astype(BF16), h_ref[...].astype(BF16), preferred_element_type=F32)
    slot = i & 1
    buf = rows_buf.at[slot]

    @pl.when(i >= 2)
    def _():
        _wait_run_copies(buf, xs_hbm, sem_rows.at[slot])

    _store_row_tiles(buf, local)
    _start_run_copies(runs_ref, i, buf, xs_hbm, sem_rows.at[slot], True)

    @pl.when(i == pl.num_programs(0) - 1)
    def _():
        _wait_run_copies(buf, xs_hbm, sem_rows.at[slot])

        @pl.when(i >= 1)
        def _():
            _wait_run_copies(rows_buf.at[1 - slot], xs_hbm, sem_rows.at[1 - slot])


def _dispatch(zpos, runs, slots, h, n_rows, td):
    n = h.shape[0]
    assert n % td == 0
    return pl.pallas_call(
        functools.partial(_dispatch_kernel, td=td),
        grid_spec=pltpu.PrefetchScalarGridSpec(
            num_scalar_prefetch=2,
            grid=(n // td,),
            in_specs=[pl.BlockSpec((TOP_K, td), lambda i, z, r: (0, i)),
                      pl.BlockSpec((td, D_MODEL), lambda i, z, r: (i, 0))],
            out_specs=pl.BlockSpec(memory_space=pl.ANY),
            scratch_shapes=[pltpu.VMEM((CHUNK * ROW_SUB, LANES), F32),
                            pltpu.VMEM((2, TOP_K * td * ROW_SUB, LANES), F32),
                            pltpu.SemaphoreType.DMA, pltpu.SemaphoreType.DMA((2,))],
        ),
        out_shape=jax.ShapeDtypeStruct((n_rows * ROW_SUB, LANES), F32),
        compiler_params=_cparams(1),
        name="dispatch",
    )(zpos, runs, slots, h)


CHUNK_ROWS = CHUNK * ROW_SUB
RING = 4


def _ffn_kernel(first_ref, count_ref, nu_ref, xs_hbm, wg_ref, wu_ref, wd_ref, ys_hbm, xbuf, obuf, xsem, osem):
    e = pl.program_id(0)
    n_used = nu_ref[0]

    def x_copy(g, slot):
        src = xs_hbm.at[pl.ds(pl.multiple_of(g * CHUNK_ROWS, CHUNK_ROWS), CHUNK_ROWS), :]
        return pltpu.make_async_copy(src, xbuf.at[slot], xsem.at[slot])

    def o_copy(g, slot):
        dst = ys_hbm.at[pl.ds(pl.multiple_of(g * CHUNK_ROWS, CHUNK_ROWS), CHUNK_ROWS), :]
        return pltpu.make_async_copy(obuf.at[slot], dst, osem.at[slot])

    @pl.when(e == 0)
    def _():
        for g in range(RING - 1):
            @pl.when(g < n_used)
            def _():
                x_copy(g, g).start()

    wg = wg_ref[0].astype(BF16)
    wu = wu_ref[0].astype(BF16)
    wd = wd_ref[0].astype(BF16)

    def chunk(c, carry):
        g = first_ref[e] + c
        slot = g & (RING - 1)
        x_copy(g, slot).wait()

        @pl.when(g + RING - 1 < n_used)
        def _():
            x_copy(g + RING - 1, (g + RING - 1) & (RING - 1)).start()

        @pl.when(g >= RING)
        def _():
            o_copy(g - RING, slot).wait()

        x = _load_row_tiles(xbuf.at[slot]).astype(BF16)
        gate = jnp.dot(x, wg, preferred_element_type=F32)
        up = jnp.dot(x, wu, preferred_element_type=F32)
        act = (gate * _sigmoid(gate)) * up
        _store_row_tiles(obuf.at[slot], jnp.dot(act.astype(BF16), wd, preferred_element_type=F32))
        o_copy(g, slot).start()
        return carry

    lax.fori_loop(0, count_ref[e], chunk, 0)

    @pl.when(e == N_EXPERTS - 1)
    def _():
        for back in range(RING, 0, -1):
            @pl.when(n_used >= back)
            def _():
                o_copy(n_used - back, (n_used - back) & (RING - 1)).wait()


def _ffn(first_chunk, chunk_count, n_used, xs, wg, wu, wd):
    weights = lambda a, b: pl.BlockSpec((1, a, b), lambda e, f, c, nu: (e, 0, 0))
    return pl.pallas_call(
        _ffn_kernel,
        grid_spec=pltpu.PrefetchScalarGridSpec(
            num_scalar_prefetch=3,
            grid=(N_EXPERTS,),
            in_specs=[pl.BlockSpec(memory_space=pl.ANY), weights(D_MODEL, EXPERT_FF), weights(D_MODEL, EXPERT_FF),
                      weights(EXPERT_FF, D_MODEL)],
            out_specs=pl.BlockSpec(memory_space=pl.ANY),
            scratch_shapes=[pltpu.VMEM((RING, CHUNK_ROWS, LANES), F32), pltpu.VMEM((RING, CHUNK_ROWS, LANES), F32),
                            pltpu.SemaphoreType.DMA((RING,)), pltpu.SemaphoreType.DMA((RING,))],
        ),
        out_shape=jax.ShapeDtypeStruct(xs.shape, F32),
        compiler_params=_cparams(1),
        name="expert_ffn",
    )(first_chunk, chunk_count, n_used, xs, wg, wu, wd)


def _combine_kernel(runs_ref, slot_ref, gate_ref, h_ref, ys_hbm, sg_ref, su_ref, sd_ref, l2g_ref, l2b_ref, o_ref,
                    rows_buf, sem_rows, *, tc, alpha):
    i = pl.program_id(0)
    slot = i & 1
    buf = rows_buf.at[slot]

    @pl.when(i == 0)
    def _():
        _start_run_copies(runs_ref, i, buf, ys_hbm, sem_rows.at[slot], False)

    @pl.when(i + 1 < pl.num_programs(0))
    def _():
        _start_run_copies(runs_ref, i + 1, rows_buf.at[1 - slot], ys_hbm, sem_rows.at[1 - slot], False)

    h = h_ref[...]
    hb = h.astype(BF16)
    sgate = jnp.dot(hb, sg_ref[...], preferred_element_type=F32)
    sup = jnp.dot(hb, su_ref[...], preferred_element_type=F32)
    shared = jnp.dot(((sgate * _sigmoid(sgate)) * sup).astype(BF16), sd_ref[...], preferred_element_type=F32)
    slots = slot_ref[...]
    gate = gate_ref[...]
    slot_iota = lax.broadcasted_iota(I32, (tc, TOP_K * tc), 1)
    weights = jnp.zeros((tc, TOP_K * tc), F32)
    for j in range(TOP_K):
        weights = jnp.where(slot_iota == slots[:, j:j + 1], gate[:, j:j + 1], weights)
    w_hi = weights.astype(BF16)
    w_lo = (weights - w_hi.astype(F32)).astype(BF16)
    _wait_run_copies(buf, ys_hbm, sem_rows.at[slot])
    local = _load_row_tiles(buf).astype(BF16)
    routed = (jnp.dot(w_hi, local, preferred_element_type=F32) + jnp.dot(w_lo, local, preferred_element_type=F32))
    o_ref[...] = _layer_norm(alpha * h + (routed + shared), l2g_ref[...], l2b_ref[...])


def _combine(runs, slots, gate, h, ys, sg, su, sd, l2g, l2b, tc, alpha):
    n = h.shape[0]
    assert n % tc == 0
    tok = lambda w: pl.BlockSpec((tc, w), lambda i, r: (i, 0))
    full = lambda a: pl.BlockSpec(a.shape, lambda i, r: (0, 0))
    return pl.pallas_call(
        functools.partial(_combine_kernel, tc=tc, alpha=alpha),
        grid_spec=pltpu.PrefetchScalarGridSpec(
            num_scalar_prefetch=1,
            grid=(n // tc,),
            in_specs=[tok(TOP_K), tok(TOP_K), tok(D_MODEL), pl.BlockSpec(memory_space=pl.ANY),
                      full(sg), full(su), full(sd), full(l2g), full(l2b)],
            out_specs=tok(D_MODEL),
            scratch_shapes=[pltpu.VMEM((2, TOP_K * tc * ROW_SUB, LANES), F32), pltpu.SemaphoreType.DMA((2,))],
        ),
        out_shape=jax.ShapeDtypeStruct((n, D_MODEL), F32),
        compiler_params=_cparams(1),
        name="combine",
    )(runs, slots, gate, h, ys, sg, su, sd, l2g, l2b)


def _pad_cols(a, width):
    return jnp.pad(a, ((0, 0), (0, width - a.shape[1])))


def _pack_shift_cols(a):
    c = 3 * RWKV_W
    return jnp.concatenate([a[:, :c], _pad_cols(a[:, c:c + DECAY_LORA], LORA_PAD),
                            _pad_cols(a[:, c + DECAY_LORA:c + DECAY_LORA + AAA_LORA], LORA_PAD),
                            _pad_cols(a[:, c + DECAY_LORA + AAA_LORA:], GATE_PAD)], axis=1)


def _keys_to_lanes(a, b, t):
    return a.reshape(b, t, N_HEADS, HEAD).transpose(1, 3, 0, 2).reshape(t, HEAD, b * N_HEADS)


def _values_to_lanes(a, b, t, split):
    a = a.reshape(b, t, N_HEADS, split, HEAD // split).transpose(1, 4, 3, 0, 2)
    return a.reshape(t, HEAD // split, split * b * N_HEADS)


def _values_from_lanes(y, b, t, split):
    y = y.reshape(t, HEAD // split, split, b, N_HEADS).transpose(3, 0, 4, 2, 1)
    return y.reshape(b * t, RWKV_W)


def _state_to_lanes(s, b, split):
    s = s.reshape(b, N_HEADS, split, HEAD // split, HEAD).transpose(4, 3, 2, 0, 1)
    return s.reshape(HEAD, HEAD // split, split * b * N_HEADS)


def _state_from_lanes(s, b, split):
    s = s.reshape(HEAD, HEAD // split, split, b, N_HEADS).transpose(3, 4, 2, 1, 0)
    return s.reshape(b, N_HEADS, HEAD, HEAD)


def _mixer(z, zprev, x2, b, t, wkv0, pool_prev, mk, mv, pos0, wts, tiles):
    n = b * t
    tt, tb, pool_bb, att_tt = tiles
    z3 = z[:n].reshape(b, t, Z_W)
    r, w, kp, v, kk, nb, g, bonus = _rwkv_prep(z, zprev, b, t, tt, wts["prep"])

    split = 1 if (b * N_HEADS) % LANES == 0 else LANES // (b * N_HEADS)
    lay = lambda a: _keys_to_lanes(a, b, t).reshape(t, HEAD // split, split * b * N_HEADS)
    key_order = jnp.concatenate([jnp.arange(i, HEAD, split) for i in range(split)])
    s_in = wkv0[..., key_order] if split > 1 else wkv0
    y_l, s_l = _wkv(lay(r), lay(w), lay(kp), lay(kk), lay(nb), _values_to_lanes(v, b, t, split),
                    _state_to_lanes(s_in, b, split), tb)
    y = _values_from_lanes(y_l, b, t, split)
    wkv_new = _state_from_lanes(s_l, b, split)
    if split > 1:
        wkv_new = wkv_new[..., jnp.argsort(key_order)]

    zp = z3[:, :, ZS_W:ZS_W + POOL_W]
    full = jnp.concatenate([jnp.zeros((b, POOL_PAD - POOL_BUF, POOL_W), F32), pool_prev, zp], axis=1)
    o_pool = _pool(full, wts["pool_w"], wts["pool_scale"], pool_bb, pos0).reshape(n, POOL_W)
    pool_new = full[:, -POOL_BUF:]

    o_mem = _memattn(z3, mk, mv, att_tt).reshape(n, MEM_W)
    h = _post(y, bonus, g, o_pool, o_mem, x2, *wts["post"], tt, wts["alpha"])
    return h, wkv_new, pool_new


def kernel(x_prompt, x_sample, mem_prompt, state_wkv, state_shift, state_pool, cache_mem_k, cache_mem_v, w_in, mu_shift, w0, w_up_decay, a0, w_up_aaa, w_up_gate, k_k, k_a, r_k, ln_x_w, ln_x_b, pool_w, pool_scale, mem_wk, mem_wv, w_out, ln1_g, ln1_b, router_w, router_b, exp_gate, exp_up, exp_down, sh_gate, sh_up, sh_down, ln2_g, ln2_b):
    depth = w_in.shape[0]
    assert depth == 1
    l = 0
    alpha = (2.0 * depth) ** 0.25
    bp, tp, d = x_prompt.shape
    bs, ts, _ = x_sample.shape
    n_p, n_s = bp * tp, bs * ts
    n = n_p + n_s

    w_in_p = jnp.concatenate([_pack_shift_cols(w_in[l][:, :1824]), w_in[l][:, 1824:]], axis=1).astype(BF16)
    row = lambda a: a.reshape(1, -1)
    pad_rows = lambda a, rows: jnp.pad(a, ((0, rows - a.shape[0]), (0, 0)))
    bd = jnp.kron(jnp.eye(N_HEADS, dtype=F32), jnp.ones((HEAD, HEAD), F32)).astype(BF16)
    pw = pool_w[l]
    pw_bd = jnp.zeros((POOL_W, POOL_W), F32)
    for gi in range(4):
        pw_bd = pw_bd.at[gi * 64:(gi + 1) * 64, gi * 64:(gi + 1) * 64].set(pw[gi])
    wts = {
        "prep": (_pack_shift_cols(row(mu_shift[l])), row(w0[l]), row(a0[l]), pad_rows(w_up_decay[l], LORA_PAD),
                 pad_rows(w_up_aaa[l], LORA_PAD), pad_rows(w_up_gate[l], GATE_PAD), row(k_k[l]), row(k_a[l]),
                 row(r_k[l]), bd),
        "pool_w": pw_bd.astype(BF16), "pool_scale": row(pool_scale[l]),
        "post": (w_out[l].astype(BF16), row(ln_x_w[l]), row(ln_x_b[l]), row(ln1_g[l]), row(ln1_b[l]), bd),
        "alpha": alpha,
    }

    xp2 = x_prompt.reshape(n_p, d)
    xs2 = x_sample.reshape(n_s, d)
    z_p = _matmul(xp2, w_in_p, _tile(n_p, 512))
    z_s = _matmul(jnp.concatenate([xs2, state_shift[l]], axis=0), w_in_p, _tile(n_s + bs, 512))
    mkv = _matmul(mem_prompt.reshape(bp * MEM_TOKENS, d),
                  jnp.concatenate([mem_wk[l], mem_wv[l]], axis=1).astype(BF16), _tile(bp * MEM_TOKENS, 512))
    mk_p = mkv[:, :MEM_W].reshape(bp, MEM_TOKENS, MEM_W)
    mv_p = mkv[:, MEM_W:].reshape(bp, MEM_TOKENS, MEM_W)

    h_p, wkv_p, pool_p = _mixer(z_p, jnp.zeros((bp, ZS_W), F32), xp2, bp, tp,
                                jnp.zeros((bp, N_HEADS, HEAD, HEAD), F32), jnp.zeros((bp, POOL_BUF, POOL_W), F32),
                                mk_p, mv_p, 0, wts,
                                (_tile(n_p, 256), _tile(tp, 32, 1), 1, _tile(tp, 256)))
    h_s, wkv_s, pool_s = _mixer(z_s, z_s[n_s:, :ZS_W], xs2, bs, ts, state_wkv[l], state_pool[l],
                                cache_mem_k[l].reshape(bs, MEM_TOKENS, MEM_W),
                                cache_mem_v[l].reshape(bs, MEM_TOKENS, MEM_W), PAST_LEN, wts,
                                (_tile(n_s, 256), _tile(ts, 32, 1), _tile(bs, 16, 1), _tile(ts, 256)))

    h = jnp.concatenate([h_p, h_s], axis=0)
    td = _tile(n, 256, LANES)
    slots, gate, run_tab, cnt = _router(h, router_w[l].T, router_b[l].reshape(N_EXPERTS, 1), td)
    counts = cnt[:, 0].astype(I32)
    padded = (counts + CHUNK - 1) // CHUNK * CHUNK
    pad_end = jnp.cumsum(padded)
    pad_start = pad_end - padded
    n_chunks = (n * TOP_K + N_EXPERTS * (CHUNK - 1) + CHUNK - 1) // CHUNK
    run_tab = run_tab.reshape(n // td, N_EXPERTS, LANES)[:, :, :RUN_FIELDS].astype(I32)
    runs = jnp.stack([run_tab[:, :, 1], pad_start[None, :] + run_tab[:, :, 2], run_tab[:, :, 0]], axis=1)
    runs = runs.reshape(n // td * RUN_FIELDS, N_EXPERTS) * ROW_SUB
    zpos = jnp.where(padded > 0, pad_end - CHUNK, -1).astype(I32)
    n_used = (pad_end[-1:] // CHUNK).astype(I32)
    xs_sorted = _dispatch(zpos, runs, slots, h, n_chunks * CHUNK, td)
    ys_sorted = _ffn((pad_start // CHUNK).astype(I32), (padded // CHUNK).astype(I32), n_used, xs_sorted,
                     exp_gate[l], exp_up[l], exp_down[l])
    y = _combine(runs, slots.T, gate.T, h, ys_sorted, sh_gate[l].astype(BF16), sh_up[l].astype(BF16),
                 sh_down[l].astype(BF16), row(ln2_g[l]), row(ln2_b[l]), td, alpha)

    return (y[:n_p].reshape(bp, tp, d), y[n_p:].reshape(bs, ts, d),
            wkv_p[None], x_prompt[:, -1][None], pool_p[None],
            mk_p.reshape(bp, MEM_TOKENS, MEM_HEADS, HEAD)[None], mv_p.reshape(bp, MEM_TOKENS, MEM_HEADS, HEAD)[None],
            wkv_s[None], x_sample[:, -1][None], pool_s[None])
```

```python
import functools

import jax
import jax.numpy as jnp
from jax import lax
from jax.experimental import pallas as pl
from jax.experimental.pallas import tpu as pltpu

F32 = jnp.float32
BF16 = jnp.bfloat16
I32 = jnp.int32
HIGHEST = lax.Precision.HIGHEST

D_MODEL = 1024
HEAD = 64
N_HEADS = 8
RWKV_W = 512
POOL_W = 256
MEM_W = 256
MEM_HEADS = 4
MEM_TOKENS = 256
POOL_BUF = 15
POOL_PAD = 16
DECAY_LORA = 64
AAA_LORA = 64
GATE_LORA = 160
LORA_PAD = 128
GATE_PAD = 256
ZS_W = 3 * RWKV_W + 2 * LORA_PAD + GATE_PAD
Z_W = ZS_W + POOL_W + MEM_W
N_EXPERTS = 256
TOP_K = 8
N_GROUPS = 8
GROUP_SIZE = N_EXPERTS // N_GROUPS
TOPK_GROUPS = 4
EXPERT_FF = 256
ROUTED_SCALE = 2.5
CHUNK = 256
PAST_LEN = 16384
LN_EPS = 1e-5
GN_EPS = 64e-5
LANES = 128
SUBLANES = 8
SCAN_INPUTS = 6
VMEM_LIMIT = 56 * 1024 * 1024


def _cparams(n_axes):
    return pltpu.CompilerParams(dimension_semantics=("arbitrary",) * n_axes, vmem_limit_bytes=VMEM_LIMIT)


def _tile(n, preferred, multiple=8):
    best = None
    for c in range(multiple, min(n, preferred) + 1, multiple):
        if n % c == 0:
            best = c
    assert best is not None, (n, preferred, multiple)
    return best


def _sigmoid(x):
    return 1.0 / (1.0 + jnp.exp(-x))


def _matmul_kernel(x_ref, w_ref, o_ref):
    o_ref[...] = jnp.dot(x_ref[...].astype(BF16), w_ref[...], preferred_element_type=F32)


def _matmul(x, w, tm):
    m, k = x.shape
    n = w.shape[1]
    assert m % tm == 0
    return pl.pallas_call(
        _matmul_kernel,
        grid=(m // tm,),
        in_specs=[pl.BlockSpec((tm, k), lambda i: (i, 0)), pl.BlockSpec((k, n), lambda i: (0, 0))],
        out_specs=pl.BlockSpec((tm, n), lambda i: (i, 0)),
        out_shape=jax.ShapeDtypeStruct((m, n), F32),
        compiler_params=_cparams(1),
        name="matmul",
    )(x, w)


def _segsum(x, bd):
    hi = x.astype(BF16)
    lo = (x - hi.astype(F32)).astype(BF16)
    return jnp.dot(hi, bd, preferred_element_type=F32) + jnp.dot(lo, bd, preferred_element_type=F32)


def _prep_kernel(zs_ref, zp_ref, mu_ref, w0_ref, a0_ref, wd_ref, wa_ref, wg_ref, kk_ref, ka_ref, rk_ref, bd_ref,
                 scan_o, g_o, bonus_o, carry_ref, *, tiles_per_seq, seq_len):
    zs = zs_ref[...]
    rolled = pltpu.roll(zs, 1, 0)
    row = lax.broadcasted_iota(I32, zs.shape, 0)
    if tiles_per_seq:
        i = pl.program_id(0)

        @pl.when(i == 0)
        def _():
            carry_ref[...] = jnp.zeros_like(carry_ref)

        prev = jnp.where(i % tiles_per_seq == 0, zp_ref[0], carry_ref[...])
        shifted = jnp.where(row == 0, prev, rolled)
        carry_ref[...] = zs[zs.shape[0] - 1:, :]
    else:
        shifted = jnp.where((row & (seq_len - 1)) == 0, zp_ref[...], rolled)
    zm = zs + mu_ref[...] * (shifted - zs)
    r = zm[:, 0:RWKV_W]
    k = zm[:, RWKV_W:2 * RWKV_W]
    v = zm[:, 2 * RWKV_W:3 * RWKV_W]
    c0 = 3 * RWKV_W
    xw = zm[:, c0:c0 + LORA_PAD]
    xa = zm[:, c0 + LORA_PAD:c0 + 2 * LORA_PAD]
    xg = zm[:, c0 + 2 * LORA_PAD:c0 + 2 * LORA_PAD + GATE_PAD]
    bd = bd_ref[...]
    u = -(w0_ref[...] + jnp.dot(jnp.tanh(xw), wd_ref[...], precision=HIGHEST, preferred_element_type=F32))
    softplus = jnp.maximum(u, 0.0) + jnp.log(1.0 + jnp.exp(-jnp.abs(u)))
    decay = jnp.exp(-jnp.exp(-softplus - 0.5))
    a = _sigmoid(a0_ref[...] + jnp.dot(xa, wa_ref[...], precision=HIGHEST, preferred_element_type=F32))
    g = jnp.dot(_sigmoid(xg), wg_ref[...], precision=HIGHEST, preferred_element_type=F32)
    kk = k * kk_ref[...]
    kk = kk / jnp.maximum(jnp.sqrt(_segsum(kk * kk, bd)), 1e-12)
    kp = k * (1.0 + (a - 1.0) * ka_ref[...])
    for slot, value in enumerate((r, decay, kp, kk, -(kk * a), v)):
        scan_o[:, slot * RWKV_W:(slot + 1) * RWKV_W] = value
    g_o[...] = g
    bonus_o[...] = _segsum(r * kp * rk_ref[...], bd) * v


def _rwkv_prep(z, zprev, b, t, tt, params):
    n_tok = b * t
    assert n_tok % tt == 0
    row = lambda w: pl.BlockSpec((1, w), lambda i: (0, 0))
    full = lambda a: pl.BlockSpec(a.shape, lambda i: (0, 0))
    tok = pl.BlockSpec((tt, RWKV_W), lambda i: (i, 0))
    mu, w0, a0, wd, wa, wg, k_k, k_a, r_k, bd = params
    if t % tt == 0:
        tiles_per_seq = t // tt
        zp = zprev.reshape(b, 1, ZS_W)
        zp_spec = pl.BlockSpec((1, 1, ZS_W), lambda i: (i // tiles_per_seq, 0, 0))
    else:
        assert tt % t == 0 and t & (t - 1) == 0
        tiles_per_seq = 0
        zp = jnp.repeat(zprev, t, axis=0)
        zp_spec = pl.BlockSpec((tt, ZS_W), lambda i: (i, 0))
    return pl.pallas_call(
        functools.partial(_prep_kernel, tiles_per_seq=tiles_per_seq, seq_len=t),
        grid=(n_tok // tt,),
        in_specs=[pl.BlockSpec((tt, ZS_W), lambda i: (i, 0)), zp_spec,
                  row(ZS_W), row(RWKV_W), row(RWKV_W), full(wd), full(wa), full(wg),
                  row(RWKV_W), row(RWKV_W), row(RWKV_W), full(bd)],
        out_specs=[pl.BlockSpec((tt, SCAN_INPUTS * RWKV_W), lambda i: (i, 0)), tok, tok],
        out_shape=[jax.ShapeDtypeStruct((n_tok, SCAN_INPUTS * RWKV_W), F32),
                   jax.ShapeDtypeStruct((n_tok, RWKV_W), F32), jax.ShapeDtypeStruct((n_tok, RWKV_W), F32)],
        scratch_shapes=[pltpu.VMEM((1, ZS_W), F32)],
        compiler_params=_cparams(1),
        name="rwkv_prep",
    )(z, zp, mu, w0, a0, wd, wa, wg, k_k, k_a, r_k, bd)


def _wkv_kernel(r_ref, w_ref, k_ref, kk_ref, nb_ref, v_ref, s0_ref, y_ref, s_ref, *scratch, tb, dup):
    slabs = s_ref.shape[1]
    key_unroll = 8

    @pl.when(pl.program_id(1) == 0)
    def _():
        s_ref[...] = s0_ref[...]

    if dup:
        for src, dst in zip((r_ref, w_ref, k_ref, kk_ref, nb_ref), scratch):
            x = src[...].reshape(tb * (HEAD // 2), LANES)
            swapped = pltpu.roll(x, LANES // 2, 1)
            low = lax.broadcasted_iota(I32, x.shape, 1) < LANES // 2
            dst[:, 0:HEAD // 2, :] = jnp.where(low, x, swapped).reshape(tb, HEAD // 2, LANES)
            dst[:, HEAD // 2:, :] = jnp.where(low, swapped, x).reshape(tb, HEAD // 2, LANES)
        r_ref, w_ref, k_ref, kk_ref, nb_ref = scratch

    def rows(ref, t, k):
        return jnp.broadcast_to(ref[t, pl.ds(k, 1), :], (SUBLANES, LANES))

    def step(t, sa, with_next):
        vt = [v_ref[t, i * SUBLANES:(i + 1) * SUBLANES, :] for i in range(slabs)]
        zero = jnp.zeros((SUBLANES, LANES), F32)

        def key(k, acc):
            y_acc, next_acc = list(acc[0]), list(acc[1])
            wb, nbb, kb, rb = rows(w_ref, t, k), rows(nb_ref, t, k), rows(k_ref, t, k), rows(r_ref, t, k)
            kkb = rows(kk_ref, t + 1, k) if with_next else None
            for i in range(slabs):
                sn = s_ref[k, i] * wb + sa[i] * nbb + vt[i] * kb
                s_ref[k, i] = sn
                y_acc[i] = y_acc[i] + sn * rb
                if with_next:
                    next_acc[i] = next_acc[i] + sn * kkb
            return tuple(y_acc), tuple(next_acc)

        y_acc, next_acc = lax.fori_loop(0, HEAD, key, ((zero,) * slabs, (zero,) * slabs), unroll=key_unroll)
        for i, y in enumerate(y_acc):
            y_ref[t, i * SUBLANES:(i + 1) * SUBLANES, :] = y
        return next_acc

    def first(k, acc):
        kkb = rows(kk_ref, 0, k)
        return tuple(a + s_ref[k, i] * kkb for i, a in enumerate(acc))

    sa0 = lax.fori_loop(0, HEAD, first, (jnp.zeros((SUBLANES, LANES), F32),) * slabs, unroll=key_unroll)
    sa_last = lax.fori_loop(0, tb - 1, lambda t, sa: step(t, sa, True), sa0)
    step(tb - 1, sa_last, False)


def _wkv(scan, s0, tb):
    t, _, rows, l = scan.shape
    dup = rows != HEAD
    assert t % tb == 0 and l % LANES == 0 and rows % SUBLANES == 0 and (not dup or (l == LANES and 2 * rows == HEAD))
    vec = lambda a: pl.BlockSpec((tb, None, rows, LANES), lambda g, i: (i, a, 0, g))
    slabs = rows // SUBLANES
    st = pl.BlockSpec((HEAD, slabs, SUBLANES, LANES), lambda g, i: (0, 0, 0, g))
    s0 = s0.reshape(HEAD, slabs, SUBLANES, l)
    y, s_new = pl.pallas_call(
        functools.partial(_wkv_kernel, tb=tb, dup=dup),
        grid=(l // LANES, t // tb),
        in_specs=[vec(a) for a in range(SCAN_INPUTS)] + [st],
        out_specs=[pl.BlockSpec((tb, rows, LANES), lambda g, i: (i, 0, g)), st],
        out_shape=[jax.ShapeDtypeStruct((t, rows, l), F32), jax.ShapeDtypeStruct(s0.shape, F32)],
        scratch_shapes=[pltpu.VMEM((tb, HEAD, LANES), F32)] * 5 if dup else [],
        compiler_params=_cparams(2),
        name="wkv",
    )(*([scan] * SCAN_INPUTS), s0)
    return y, s_new.reshape(HEAD, rows, l)


def _pool_kernel(full_ref, pw_ref, ps_ref, o_ref, *, t, pos0):
    bb = full_ref.shape[0]
    lane = lax.broadcasted_iota(I32, (t, POOL_W), 1)
    window = jnp.where(lane < 64, 2, jnp.where(lane < 128, 4, jnp.where(lane < 192, 8, 16)))
    pos = lax.broadcasted_iota(I32, (t, POOL_W), 0) + (pos0 + 1)
    cnt = jnp.minimum(pos, window).astype(F32)
    diffs = []
    for i in range(bb):
        f = full_ref[i]
        s2 = f + pltpu.roll(f, 1, 0)
        s4 = s2 + pltpu.roll(s2, 2, 0)
        s8 = s4 + pltpu.roll(s4, 4, 0)
        s16 = s8 + pltpu.roll(s8, 8, 0)
        wsum = jnp.where(lane < 64, s2[POOL_PAD:], jnp.where(lane < 128, s4[POOL_PAD:],
                                                            jnp.where(lane < 192, s8[POOL_PAD:], s16[POOL_PAD:])))
        diffs.append(wsum / cnt - f[POOL_PAD:])
    diff = diffs[0] if bb == 1 else jnp.concatenate(diffs, axis=0)
    out = jnp.dot(diff.astype(BF16), pw_ref[...], preferred_element_type=F32) * ps_ref[...]
    for i in range(bb):
        o_ref[i] = out[i * t:(i + 1) * t]


def _pool(full, pw_bd, pscale, bb, pos0):
    b, rows, _ = full.shape
    t = rows - POOL_PAD
    assert b % bb == 0
    return pl.pallas_call(
        functools.partial(_pool_kernel, t=t, pos0=pos0),
        grid=(b // bb,),
        in_specs=[pl.BlockSpec((bb, rows, POOL_W), lambda i: (i, 0, 0)),
                  pl.BlockSpec((POOL_W, POOL_W), lambda i: (0, 0)), pl.BlockSpec((1, POOL_W), lambda i: (0, 0))],
        out_specs=pl.BlockSpec((bb, t, POOL_W), lambda i: (i, 0, 0)),
        out_shape=jax.ShapeDtypeStruct((b, t, POOL_W), F32),
        compiler_params=_cparams(1),
        name="pool",
    )(full, pw_bd, pscale)


def _memattn_kernel(q_ref, k_ref, v_ref, o_ref):
    for i in range(q_ref.shape[0]):
        q = q_ref[i].astype(BF16)
        kf = k_ref[i]
        vf = v_ref[i]
        head_of_lane = lax.broadcasted_iota(I32, kf.shape, 1) // (MEM_W // MEM_HEADS)
        out = None
        for h in range(MEM_HEADS):
            kh = jnp.where(head_of_lane == h, kf, 0.0).astype(BF16)
            vh = jnp.where(head_of_lane == h, vf, 0.0).astype(BF16)
            s = lax.dot_general(q, kh, (((1,), (1,)), ((), ())), preferred_element_type=F32) * (64 ** -0.5)
            e = jnp.exp(s - jnp.max(s, axis=-1, keepdims=True))
            p = e / jnp.sum(e, axis=-1, keepdims=True)
            o = jnp.dot(p.astype(BF16), vh, preferred_element_type=F32)
            out = o if out is None else out + o
        o_ref[i] = out


def _memattn(z3, mk, mv, tt, bb):
    b, t, _ = z3.shape
    assert t % tt == 0 and b % bb == 0
    qcol = (Z_W - MEM_W) // MEM_W
    kv = pl.BlockSpec((bb, MEM_TOKENS, MEM_W), lambda i, j: (i, 0, 0))
    return pl.pallas_call(
        _memattn_kernel,
        grid=(b // bb, t // tt),
        in_specs=[pl.BlockSpec((bb, tt, MEM_W), lambda i, j: (i, j, qcol)), kv, kv],
        out_specs=pl.BlockSpec((bb, tt, MEM_W), lambda i, j: (i, j, 0)),
        out_shape=jax.ShapeDtypeStruct((b, t, MEM_W), F32),
        compiler_params=_cparams(2),
        name="mem_attn",
    )(z3, mk, mv)


def _layer_norm(x, g, b):
    mu = jnp.mean(x, axis=-1, keepdims=True)
    xc = x - mu
    var = jnp.mean(xc * xc, axis=-1, keepdims=True)
    return xc * lax.rsqrt(var + LN_EPS) * g + b


def _post_kernel(y_ref, bonus_ref, g_ref, op_ref, om_ref, x_ref, wo_ref, lxw_ref, lxb_ref, l1g_ref, l1b_ref, bd_ref,
                 *rest, alpha):
    h_ref = rest[-1]
    y = y_ref[...]
    bd = bd_ref[...]
    mu = _segsum(y, bd) * (1.0 / HEAD)
    yc = y - mu
    var = _segsum(yc * yc, bd) * (1.0 / HEAD)
    yn = yc * lax.rsqrt(var + GN_EPS) * lxw_ref[...] + lxb_ref[...]
    o_rwkv = (yn + bonus_ref[...]) * g_ref[...]
    mixed = (jnp.dot(o_rwkv.astype(BF16), wo_ref[0:RWKV_W, :], preferred_element_type=F32)
             + jnp.dot(op_ref[...].astype(BF16), wo_ref[RWKV_W:RWKV_W + POOL_W, :], preferred_element_type=F32)
             + jnp.dot(om_ref[...].astype(BF16), wo_ref[RWKV_W + POOL_W:, :], preferred_element_type=F32))
    h_ref[...] = _layer_norm(alpha * x_ref[...] + mixed, l1g_ref[...], l1b_ref[...])


def _post(y, bonus, g, o_pool, o_mem, x, wo, lxw, lxb, l1g, l1b, bd, tt, alpha, h_all):
    n = y.shape[0]
    assert n % tt == 0
    tok = lambda w: pl.BlockSpec((tt, w), lambda i: (i, 0))
    row = lambda w: pl.BlockSpec((1, w), lambda i: (0, 0))
    full = lambda a: pl.BlockSpec(a.shape, lambda i: (0, 0))
    in_specs = [tok(RWKV_W), tok(RWKV_W), tok(RWKV_W), tok(POOL_W), tok(MEM_W), tok(D_MODEL), full(wo),
                row(RWKV_W), row(RWKV_W), row(D_MODEL), row(D_MODEL), full(bd)]
    args = [y, bonus, g, o_pool, o_mem, x, wo, lxw, lxb, l1g, l1b, bd]
    if isinstance(h_all, tuple):
        n_total, first = h_all
        aliases = {}
    else:
        n_total, first = h_all.shape[0], h_all.shape[0] - n
        in_specs.append(pl.BlockSpec(memory_space=pl.ANY))
        args.append(h_all)
        aliases = {len(args) - 1: 0}
    assert first % tt == 0
    return pl.pallas_call(
        functools.partial(_post_kernel, alpha=alpha),
        grid=(n // tt,),
        in_specs=in_specs,
        out_specs=pl.BlockSpec((tt, D_MODEL), lambda i: (i + first // tt, 0)),
        out_shape=jax.ShapeDtypeStruct((n_total, D_MODEL), F32),
        input_output_aliases=aliases,
        compiler_params=_cparams(1),
        name="post",
    )(*args)


def _router_kernel(h_ref, rwt_ref, bias_ref, tri_ref, below_ref, slot_o, gate_o, run_o, cnt_o, carry_ref):
    @pl.when(pl.program_id(0) == 0)
    def _():
        carry_ref[...] = jnp.zeros_like(carry_ref)

    neg = -jnp.inf
    logits = lax.dot_general(rwt_ref[...], h_ref[...], (((1,), (1,)), ((), ())),
                             precision=HIGHEST, preferred_element_type=F32)
    scores = _sigmoid(logits)
    sel = scores + bias_ref[...]
    tt = sel.shape[1]
    gio = lax.broadcasted_iota(I32, (GROUP_SIZE, tt), 0).astype(F32)
    blocks, gscore = [], []
    for g in range(N_GROUPS):
        blk = sel[g * GROUP_SIZE:(g + 1) * GROUP_SIZE, :]
        m1 = jnp.max(blk, axis=0, keepdims=True)
        first = jnp.min(jnp.where(blk == m1, gio, float(GROUP_SIZE)), axis=0, keepdims=True)
        m2 = jnp.max(jnp.where(gio == first, neg, blk), axis=0, keepdims=True)
        blocks.append(blk)
        gscore.append(m1 + m2)
    masked = []
    for g in range(N_GROUPS):
        beaten_by = jnp.zeros((1, tt), F32)
        for g2 in range(N_GROUPS):
            if g2 != g:
                wins = (gscore[g2] >= gscore[g]) if g2 < g else (gscore[g2] > gscore[g])
                beaten_by = beaten_by + jnp.where(wins, 1.0, 0.0)
        masked.append(jnp.where(beaten_by < TOPK_GROUPS, blocks[g], neg))
    msel = jnp.concatenate(masked, axis=0)
    eio = lax.broadcasted_iota(I32, msel.shape, 0).astype(F32)
    chosen = jnp.zeros(msel.shape, F32)
    idxs, scs = [], []
    for _ in range(TOP_K):
        m = jnp.max(msel, axis=0, keepdims=True)
        first = jnp.min(jnp.where(msel == m, eio, float(N_EXPERTS)), axis=0, keepdims=True)
        hit = eio == first
        scs.append(jnp.sum(jnp.where(hit, scores, 0.0), axis=0, keepdims=True))
        msel = jnp.where(hit, neg, msel)
        chosen = jnp.where(hit, 1.0, chosen)
        idxs.append(first)
    total = scs[0]
    for s in scs[1:]:
        total = total + s
    chosen_b = chosen.astype(BF16)
    earlier = jnp.dot(chosen_b, tri_ref[...], preferred_element_type=F32)
    smaller = jnp.dot(below_ref[...], chosen_b, preferred_element_type=F32)
    run_len = jnp.sum(chosen, axis=1, keepdims=True)
    run_off = jnp.sum(smaller, axis=1, keepdims=True)
    slot_of = run_off + earlier
    slots = [jnp.sum(jnp.where(eio == i, slot_of, 0.0), axis=0, keepdims=True) for i in idxs]
    slot_o[...] = jnp.concatenate(slots, axis=0).astype(I32)
    gate_o[...] = jnp.concatenate([s / total * ROUTED_SCALE for s in scs], axis=0)
    lane = lax.broadcasted_iota(I32, run_o.shape, 1)
    run_o[...] = jnp.where(lane == 0, run_len, jnp.where(lane == 1, run_off, jnp.where(lane == 2, carry_ref[...], 0.0)))
    carry_ref[...] = carry_ref[...] + run_len
    cnt_o[...] = jnp.broadcast_to(carry_ref[...], cnt_o.shape)


def _router(h, rwt, bias, tt):
    n = h.shape[0]
    assert n % tt == 0
    tri = (lax.broadcasted_iota(I32, (tt, tt), 0) < lax.broadcasted_iota(I32, (tt, tt), 1)).astype(BF16)
    below = (lax.broadcasted_iota(I32, (N_EXPERTS, N_EXPERTS), 1)
             < lax.broadcasted_iota(I32, (N_EXPERTS, N_EXPERTS), 0)).astype(BF16)
    tokT = pl.BlockSpec((TOP_K, tt), lambda i: (0, i))
    return pl.pallas_call(
        _router_kernel,
        grid=(n // tt,),
        in_specs=[pl.BlockSpec((tt, D_MODEL), lambda i: (i, 0)), pl.BlockSpec((N_EXPERTS, D_MODEL), lambda i: (0, 0)),
                  pl.BlockSpec((N_EXPERTS, 1), lambda i: (0, 0)), pl.BlockSpec((tt, tt), lambda i: (0, 0)),
                  pl.BlockSpec((N_EXPERTS, N_EXPERTS), lambda i: (0, 0))],
        out_specs=[tokT, tokT, pl.BlockSpec((N_EXPERTS, LANES), lambda i: (i, 0)),
                   pl.BlockSpec((N_EXPERTS, LANES), lambda i: (0, 0))],
        out_shape=[jax.ShapeDtypeStruct((TOP_K, n), I32), jax.ShapeDtypeStruct((TOP_K, n), F32),
                   jax.ShapeDtypeStruct((n // tt * N_EXPERTS, LANES), F32),
                   jax.ShapeDtypeStruct((N_EXPERTS, LANES), F32)],
        scratch_shapes=[pltpu.VMEM((N_EXPERTS, 1), F32)],
        compiler_params=_cparams(1),
        name="router",
    )(h, rwt, bias, tri, below)


ROW_SUB = D_MODEL // LANES


def _store_row_tiles(ref, x):
    rows = x.shape[0]
    for c in range(ROW_SUB):
        ref[pl.ds(c, rows, stride=ROW_SUB), :] = x[:, c * LANES:(c + 1) * LANES]


def _load_row_tiles(ref):
    rows = ref.shape[0] // ROW_SUB
    return jnp.concatenate([ref[pl.ds(c, rows, stride=ROW_SUB), :] for c in range(ROW_SUB)], axis=-1)


RUN_FIELDS = 3
RUN_UNROLL = 8


def _start_run_copies(runs_ref, tile, local_ref, sorted_ref, sem, to_sorted):
    def body(e, c):
        size = pl.multiple_of(runs_ref[tile * RUN_FIELDS + 2, e], ROW_SUB)
        local = local_ref.at[pl.ds(pl.multiple_of(runs_ref[tile * RUN_FIELDS, e], ROW_SUB), size), :]
        remote = sorted_ref.at[pl.ds(pl.multiple_of(runs_ref[tile * RUN_FIELDS + 1, e], ROW_SUB), size), :]
        (pltpu.make_async_copy(local, remote, sem) if to_sorted else pltpu.make_async_copy(remote, local, sem)).start()
        return c

    lax.fori_loop(0, N_EXPERTS, body, 0, unroll=RUN_UNROLL)


def _wait_run_copies(local_ref, sorted_ref, sem):
    pltpu.make_async_copy(sorted_ref.at[pl.ds(0, local_ref.shape[0]), :], local_ref, sem).wait()


def _dispatch_kernel(zpos_ref, runs_ref, slot_ref, h_ref, xs_hbm, zero_buf, rows_buf, sem_zero, sem_rows, *, td):
    i = pl.program_id(0)

    def zero_copy(e):
        start = pl.multiple_of(zpos_ref[e] * ROW_SUB, CHUNK * ROW_SUB)
        return pltpu.make_async_copy(zero_buf, xs_hbm.at[pl.ds(start, CHUNK * ROW_SUB), :], sem_zero)

    @pl.when(i == 0)
    def _():
        zero_buf[...] = jnp.zeros_like(zero_buf)

        def start(e, c):
            @pl.when(zpos_ref[e] >= 0)
            def _():
                zero_copy(e).start()
            return c

        def wait(e, c):
            @pl.when(zpos_ref[e] >= 0)
            def _():
                zero_copy(e).wait()
            return c

        lax.fori_loop(0, N_EXPERTS, start, 0)
        lax.fori_loop(0, N_EXPERTS, wait, 0)

    slots = slot_ref[...]
    slot_iota = lax.broadcasted_iota(I32, (TOP_K * td, td), 0)
    select = jnp.zeros((TOP_K * td, td), F32)
    for j in range(TOP_K):
        select = jnp.where(slot_iota == slots[j:j + 1, :], 1.0, select)
    local = jnp.dot(select.astype(BF16), h_ref[...].astype(BF16), preferred_element_type=F32)
    _store_row_tiles(rows_buf, local)
    _start_run_copies(runs_ref, i, rows_buf, xs_hbm, sem_rows, True)
    _wait_run_copies(rows_buf, xs_hbm, sem_rows)


def _dispatch(zpos, runs, slots, h, n_rows, td):
    n = h.shape[0]
    assert n % td == 0
    return pl.pallas_call(
        functools.partial(_dispatch_kernel, td=td),
        grid_spec=pltpu.PrefetchScalarGridSpec(
            num_scalar_prefetch=2,
            grid=(n // td,),
            in_specs=[pl.BlockSpec((TOP_K, td), lambda i, z, r: (0, i)),
                      pl.BlockSpec((td, D_MODEL), lambda i, z, r: (i, 0))],
            out_specs=pl.BlockSpec(memory_space=pl.ANY),
            scratch_shapes=[pltpu.VMEM((CHUNK * ROW_SUB, LANES), F32), pltpu.VMEM((TOP_K * td * ROW_SUB, LANES), F32),
                            pltpu.SemaphoreType.DMA, pltpu.SemaphoreType.DMA],
        ),
        out_shape=jax.ShapeDtypeStruct((n_rows * ROW_SUB, LANES), F32),
        compiler_params=_cparams(1),
        name="dispatch",
    )(zpos, runs, slots, h)


CHUNK_ROWS = CHUNK * ROW_SUB
RING = 4


def _ffn_kernel(first_ref, count_ref, nu_ref, xs_hbm, wg_ref, wu_ref, wd_ref, ys_hbm, xbuf, obuf, xsem, osem):
    e = pl.program_id(0)
    n_used = nu_ref[0]

    def x_copy(g, slot):
        src = xs_hbm.at[pl.ds(pl.multiple_of(g * CHUNK_ROWS, CHUNK_ROWS), CHUNK_ROWS), :]
        return pltpu.make_async_copy(src, xbuf.at[slot], xsem.at[slot])

    def o_copy(g, slot):
        dst = ys_hbm.at[pl.ds(pl.multiple_of(g * CHUNK_ROWS, CHUNK_ROWS), CHUNK_ROWS), :]
        return pltpu.make_async_copy(obuf.at[slot], dst, osem.at[slot])

    @pl.when(e == 0)
    def _():
        for g in range(RING - 1):
            @pl.when(g < n_used)
            def _():
                x_copy(g, g).start()

    wg = wg_ref[0].astype(BF16)
    wu = wu_ref[0].astype(BF16)
    wd = wd_ref[0].astype(BF16)

    def chunk(c, carry):
        g = first_ref[e] + c
        slot = g & (RING - 1)
        x_copy(g, slot).wait()

        @pl.when(g + RING - 1 < n_used)
        def _():
            x_copy(g + RING - 1, (g + RING - 1) & (RING - 1)).start()

        @pl.when(g >= RING)
        def _():
            o_copy(g - RING, slot).wait()

        x = _load_row_tiles(xbuf.at[slot]).astype(BF16)
        gate = jnp.dot(x, wg, preferred_element_type=F32)
        up = jnp.dot(x, wu, preferred_element_type=F32)
        act = (gate * _sigmoid(gate)) * up
        _store_row_tiles(obuf.at[slot], jnp.dot(act.astype(BF16), wd, preferred_element_type=F32))
        o_copy(g, slot).start()
        return carry

    lax.fori_loop(0, count_ref[e], chunk, 0)

    @pl.when(e == N_EXPERTS - 1)
    def _():
        for back in range(RING, 0, -1):
            @pl.when(n_used >= back)
            def _():
                o_copy(n_used - back, (n_used - back) & (RING - 1)).wait()


def _ffn(first_chunk, chunk_count, n_used, xs, wg, wu, wd):
    weights = lambda a, b: pl.BlockSpec((1, a, b), lambda e, f, c, nu: (e, 0, 0))
    return pl.pallas_call(
        _ffn_kernel,
        grid_spec=pltpu.PrefetchScalarGridSpec(
            num_scalar_prefetch=3,
            grid=(N_EXPERTS,),
            in_specs=[pl.BlockSpec(memory_space=pl.ANY), weights(D_MODEL, EXPERT_FF), weights(D_MODEL, EXPERT_FF),
                      weights(EXPERT_FF, D_MODEL)],
            out_specs=pl.BlockSpec(memory_space=pl.ANY),
            scratch_shapes=[pltpu.VMEM((RING, CHUNK_ROWS, LANES), F32), pltpu.VMEM((RING, CHUNK_ROWS, LANES), F32),
                            pltpu.SemaphoreType.DMA((RING,)), pltpu.SemaphoreType.DMA((RING,))],
        ),
        out_shape=jax.ShapeDtypeStruct(xs.shape, F32),
        compiler_params=_cparams(1),
        name="expert_ffn",
    )(first_chunk, chunk_count, n_used, xs, wg, wu, wd)


def _combine_kernel(runs_ref, slot_ref, gate_ref, h_ref, ys_hbm, sg_ref, su_ref, sd_ref, l2g_ref, l2b_ref, o_ref,
                    rows_buf, sem_rows, *, tc, alpha):
    i = pl.program_id(0)
    slot = i & 1
    buf = rows_buf.at[slot]

    @pl.when(i == 0)
    def _():
        _start_run_copies(runs_ref, i, buf, ys_hbm, sem_rows.at[slot], False)

    @pl.when(i + 1 < pl.num_programs(0))
    def _():
        _start_run_copies(runs_ref, i + 1, rows_buf.at[1 - slot], ys_hbm, sem_rows.at[1 - slot], False)

    h = h_ref[...]
    hb = h.astype(BF16)
    sgate = jnp.dot(hb, sg_ref[...], preferred_element_type=F32)
    sup = jnp.dot(hb, su_ref[...], preferred_element_type=F32)
    shared = jnp.dot(((sgate * _sigmoid(sgate)) * sup).astype(BF16), sd_ref[...], preferred_element_type=F32)
    slots = slot_ref[...]
    gate = gate_ref[...]
    slot_iota = lax.broadcasted_iota(I32, (tc, TOP_K * tc), 1)
    weights = jnp.zeros((tc, TOP_K * tc), F32)
    for j in range(TOP_K):
        weights = jnp.where(slot_iota == slots[:, j:j + 1], gate[:, j:j + 1], weights)
    w_hi = weights.astype(BF16)
    w_lo = (weights - w_hi.astype(F32)).astype(BF16)
    _wait_run_copies(buf, ys_hbm, sem_rows.at[slot])
    local = _load_row_tiles(buf).astype(BF16)
    routed = (jnp.dot(w_hi, local, preferred_element_type=F32) + jnp.dot(w_lo, local, preferred_element_type=F32))
    o_ref[...] = _layer_norm(alpha * h + (routed + shared), l2g_ref[...], l2b_ref[...])


def _combine(runs, slots, gate, h, ys, sg, su, sd, l2g, l2b, tc, alpha):
    n = h.shape[0]
    assert n % tc == 0
    tok = lambda w: pl.BlockSpec((tc, w), lambda i, r: (i, 0))
    full = lambda a: pl.BlockSpec(a.shape, lambda i, r: (0, 0))
    return pl.pallas_call(
        functools.partial(_combine_kernel, tc=tc, alpha=alpha),
        grid_spec=pltpu.PrefetchScalarGridSpec(
            num_scalar_prefetch=1,
            grid=(n // tc,),
            in_specs=[tok(TOP_K), tok(TOP_K), tok(D_MODEL), pl.BlockSpec(memory_space=pl.ANY),
                      full(sg), full(su), full(sd), full(l2g), full(l2b)],
            out_specs=tok(D_MODEL),
            scratch_shapes=[pltpu.VMEM((2, TOP_K * tc * ROW_SUB, LANES), F32), pltpu.SemaphoreType.DMA((2,))],
        ),
        out_shape=jax.ShapeDtypeStruct((n, D_MODEL), F32),
        compiler_params=_cparams(1),
        name="combine",
    )(runs, slots, gate, h, ys, sg, su, sd, l2g, l2b)


def _pad_cols(a, width):
    return jnp.pad(a, ((0, 0), (0, width - a.shape[1])))


def _pack_shift_cols(a):
    c = 3 * RWKV_W
    return jnp.concatenate([a[:, :c], _pad_cols(a[:, c:c + DECAY_LORA], LORA_PAD),
                            _pad_cols(a[:, c + DECAY_LORA:c + DECAY_LORA + AAA_LORA], LORA_PAD),
                            _pad_cols(a[:, c + DECAY_LORA + AAA_LORA:], GATE_PAD)], axis=1)


def _scan_to_lanes(a, b, t, split):
    a = a.reshape(b, t, SCAN_INPUTS, N_HEADS, HEAD).transpose(1, 2, 4, 0, 3)
    return a.reshape(t, SCAN_INPUTS, HEAD // split, split * b * N_HEADS)


def _values_from_lanes(y, b, t):
    return y.reshape(t, HEAD, b, N_HEADS).transpose(2, 0, 3, 1).reshape(b * t, RWKV_W)


def _state_to_lanes(s, b, split):
    s = s.reshape(b, N_HEADS, HEAD // split, split, HEAD).transpose(4, 2, 3, 0, 1)
    return s.reshape(HEAD, HEAD // split, split * b * N_HEADS)


def _state_from_lanes(s, b, split):
    s = s.reshape(HEAD, HEAD // split, split, b, N_HEADS).transpose(3, 4, 1, 2, 0)
    return s.reshape(b, N_HEADS, HEAD, HEAD)


def _mixer(z, zprev, x2, b, t, wkv0, pool_prev, mk, mv, pos0, wts, tiles, h_all):
    n = b * t
    tt, tb, seqs, att_tt = tiles
    z3 = z[:n].reshape(b, t, Z_W)
    scan, g, bonus = _rwkv_prep(z, zprev, b, t, tt, wts["prep"])

    split = 1 if (b * N_HEADS) % LANES == 0 else LANES // (b * N_HEADS)
    key_order = jnp.concatenate([jnp.arange(i, HEAD, split) for i in range(split)])
    s_in = wkv0[..., key_order] if split > 1 else wkv0
    y_l, s_l = _wkv(_scan_to_lanes(scan, b, t, split), _state_to_lanes(s_in, b, split), tb)
    y = _values_from_lanes(y_l, b, t)
    wkv_new = _state_from_lanes(s_l, b, split)
    if split > 1:
        wkv_new = wkv_new[..., jnp.argsort(key_order)]

    zp = z3[:, :, ZS_W:ZS_W + POOL_W]
    full = jnp.concatenate([jnp.zeros((b, POOL_PAD - POOL_BUF, POOL_W), F32), pool_prev, zp], axis=1)
    o_pool = _pool(full, wts["pool_w"], wts["pool_scale"], seqs, pos0).reshape(n, POOL_W)
    pool_new = full[:, -POOL_BUF:]

    o_mem = _memattn(z3, mk, mv, att_tt, seqs).reshape(n, MEM_W)
    h = _post(y, bonus, g, o_pool, o_mem, x2, *wts["post"], tt, wts["alpha"], h_all)
    return h, wkv_new, pool_new


def kernel(x_prompt, x_sample, mem_prompt, state_wkv, state_shift, state_pool, cache_mem_k, cache_mem_v, w_in, mu_shift, w0, w_up_decay, a0, w_up_aaa, w_up_gate, k_k, k_a, r_k, ln_x_w, ln_x_b, pool_w, pool_scale, mem_wk, mem_wv, w_out, ln1_g, ln1_b, router_w, router_b, exp_gate, exp_up, exp_down, sh_gate, sh_up, sh_down, ln2_g, ln2_b):
    depth = w_in.shape[0]
    assert depth == 1
    l = 0
    alpha = (2.0 * depth) ** 0.25
    bp, tp, d = x_prompt.shape
    bs, ts, _ = x_sample.shape
    n_p, n_s = bp * tp, bs * ts
    n = n_p + n_s

    w_in_p = jnp.concatenate([_pack_shift_cols(w_in[l][:, :1824]), w_in[l][:, 1824:]], axis=1).astype(BF16)
    row = lambda a: a.reshape(1, -1)
    pad_rows = lambda a, rows: jnp.pad(a, ((0, rows - a.shape[0]), (0, 0)))
    bd = jnp.kron(jnp.eye(N_HEADS, dtype=F32), jnp.ones((HEAD, HEAD), F32)).astype(BF16)
    pw = pool_w[l]
    pw_bd = jnp.zeros((POOL_W, POOL_W), F32)
    for gi in range(4):
        pw_bd = pw_bd.at[gi * 64:(gi + 1) * 64, gi * 64:(gi + 1) * 64].set(pw[gi])
    wts = {
        "prep": (_pack_shift_cols(row(mu_shift[l])), row(w0[l]), row(a0[l]), pad_rows(w_up_decay[l], LORA_PAD),
                 pad_rows(w_up_aaa[l], LORA_PAD), pad_rows(w_up_gate[l], GATE_PAD), row(k_k[l]), row(k_a[l]),
                 row(r_k[l]), bd),
        "pool_w": pw_bd.astype(BF16), "pool_scale": row(pool_scale[l]),
        "post": (w_out[l].astype(BF16), row(ln_x_w[l]), row(ln_x_b[l]), row(ln1_g[l]), row(ln1_b[l]), bd),
        "alpha": alpha,
    }

    xp2 = x_prompt.reshape(n_p, d)
    xs2 = x_sample.reshape(n_s, d)
    z_p = _matmul(xp2, w_in_p, _tile(n_p, 512))
    z_s = _matmul(jnp.concatenate([xs2, state_shift[l]], axis=0), w_in_p, _tile(n_s + bs, 512))
    mkv = _matmul(mem_prompt.reshape(bp * MEM_TOKENS, d),
                  jnp.concatenate([mem_wk[l], mem_wv[l]], axis=1).astype(BF16), _tile(bp * MEM_TOKENS, 512))
    mk_p = mkv[:, :MEM_W].reshape(bp, MEM_TOKENS, MEM_W)
    mv_p = mkv[:, MEM_W:].reshape(bp, MEM_TOKENS, MEM_W)

    h_p, wkv_p, pool_p = _mixer(z_p, jnp.zeros((bp, ZS_W), F32), xp2, bp, tp,
                                jnp.zeros((bp, N_HEADS, HEAD, HEAD), F32), jnp.zeros((bp, POOL_BUF, POOL_W), F32),
                                mk_p, mv_p, 0, wts,
                                (_tile(n_p, 256), _tile(tp, 32, 1), 1, _tile(tp, 256)), (n, 0))
    h, wkv_s, pool_s = _mixer(z_s, z_s[n_s:, :ZS_W], xs2, bs, ts, state_wkv[l], state_pool[l],
                              cache_mem_k[l].reshape(bs, MEM_TOKENS, MEM_W),
                              cache_mem_v[l].reshape(bs, MEM_TOKENS, MEM_W), PAST_LEN, wts,
                              (_tile(n_s, 256), _tile(ts, 32, 1), _tile(bs, 16, 1), _tile(ts, 256)), h_p)

    td = _tile(n, 256, LANES)
    slots, gate, run_tab, cnt = _router(h, router_w[l].T, router_b[l].reshape(N_EXPERTS, 1), td)
    counts = cnt[:, 0].astype(I32)
    padded = (counts + CHUNK - 1) // CHUNK * CHUNK
    pad_end = jnp.cumsum(padded)
    pad_start = pad_end - padded
    n_chunks = (n * TOP_K + N_EXPERTS * (CHUNK - 1) + CHUNK - 1) // CHUNK
    run_tab = run_tab.reshape(n // td, N_EXPERTS, LANES)[:, :, :RUN_FIELDS].astype(I32)
    runs = jnp.stack([run_tab[:, :, 1], pad_start[None, :] + run_tab[:, :, 2], run_tab[:, :, 0]], axis=1)
    runs = runs.reshape(n // td * RUN_FIELDS, N_EXPERTS) * ROW_SUB
    zpos = jnp.where(padded > 0, pad_end - CHUNK, -1).astype(I32)
    n_used = (pad_end[-1:] // CHUNK).astype(I32)
    xs_sorted = _dispatch(zpos, runs, slots, h, n_chunks * CHUNK, td)
    ys_sorted = _ffn((pad_start // CHUNK).astype(I32), (padded // CHUNK).astype(I32), n_used, xs_sorted,
                     exp_gate[l], exp_up[l], exp_down[l])
    y = _combine(runs, slots.T, gate.T, h, ys_sorted, sh_gate[l].astype(BF16), sh_up[l].astype(BF16),
                 sh_down[l].astype(BF16), row(ln2_g[l]), row(ln2_b[l]), td, alpha)

    return (y[:n_p].reshape(bp, tp, d), y[n_p:].reshape(bs, ts, d),
            wkv_p[None], x_prompt[:, -1][None], pool_p[None],
            mk_p.reshape(bp, MEM_TOKENS, MEM_HEADS, HEAD)[None], mv_p.reshape(bp, MEM_TOKENS, MEM_HEADS, HEAD)[None],
            wkv_s[None], x_sample[:, -1][None], pool_s[None])
```

```python
import functools

import jax
import jax.numpy as jnp
from jax import lax
from jax.experimental import pallas as pl
from jax.experimental.pallas import tpu as pltpu

F32 = jnp.float32
BF16 = jnp.bfloat16
I32 = jnp.int32
HIGHEST = lax.Precision.HIGHEST

D_MODEL = 1024
HEAD = 64
N_HEADS = 8
RWKV_W = 512
POOL_W = 256
MEM_W = 256
MEM_HEADS = 4
MEM_TOKENS = 256
POOL_BUF = 15
POOL_PAD = 16
DECAY_LORA = 64
AAA_LORA = 64
GATE_LORA = 160
LORA_PAD = 128
GATE_PAD = 256
ZS_W = 3 * RWKV_W + 2 * LORA_PAD + GATE_PAD
Z_W = ZS_W + POOL_W + MEM_W
N_EXPERTS = 256
TOP_K = 8
N_GROUPS = 8
GROUP_SIZE = N_EXPERTS // N_GROUPS
TOPK_GROUPS = 4
EXPERT_FF = 256
ROUTED_SCALE = 2.5
CHUNK = 256
PAST_LEN = 16384
LN_EPS = 1e-5
GN_EPS = 64e-5
LANES = 128
SUBLANES = 8
SCAN_INPUTS = 6
VMEM_LIMIT = 56 * 1024 * 1024


def _cparams(n_axes):
    return pltpu.CompilerParams(dimension_semantics=("arbitrary",) * n_axes, vmem_limit_bytes=VMEM_LIMIT)


def _tile(n, preferred, multiple=8):
    best = None
    for c in range(multiple, min(n, preferred) + 1, multiple):
        if n % c == 0:
            best = c
    assert best is not None, (n, preferred, multiple)
    return best


def _sigmoid(x):
    return 1.0 / (1.0 + jnp.exp(-x))


def _matmul_kernel(x_ref, w_ref, o_ref):
    o_ref[...] = jnp.dot(x_ref[...].astype(BF16), w_ref[...], preferred_element_type=F32)


def _matmul(x, w, tm):
    m, k = x.shape
    n = w.shape[1]
    assert m % tm == 0
    return pl.pallas_call(
        _matmul_kernel,
        grid=(m // tm,),
        in_specs=[pl.BlockSpec((tm, k), lambda i: (i, 0)), pl.BlockSpec((k, n), lambda i: (0, 0))],
        out_specs=pl.BlockSpec((tm, n), lambda i: (i, 0)),
        out_shape=jax.ShapeDtypeStruct((m, n), F32),
        compiler_params=_cparams(1),
        name="matmul",
    )(x, w)


def _segsum(x, bd):
    hi = x.astype(BF16)
    lo = (x - hi.astype(F32)).astype(BF16)
    return jnp.dot(hi, bd, preferred_element_type=F32) + jnp.dot(lo, bd, preferred_element_type=F32)


def _prep_kernel(zs_ref, zp_ref, mu_ref, w0_ref, a0_ref, wd_ref, wa_ref, wg_ref, kk_ref, ka_ref, rk_ref, bd_ref,
                 scan_o, g_o, bonus_o, carry_ref, *, tiles_per_seq, seq_len):
    zs = zs_ref[...]
    rolled = pltpu.roll(zs, 1, 0)
    row = lax.broadcasted_iota(I32, zs.shape, 0)
    if tiles_per_seq:
        i = pl.program_id(0)

        @pl.when(i == 0)
        def _():
            carry_ref[...] = jnp.zeros_like(carry_ref)

        prev = jnp.where(i % tiles_per_seq == 0, zp_ref[0], carry_ref[...])
        shifted = jnp.where(row == 0, prev, rolled)
        carry_ref[...] = zs[zs.shape[0] - 1:, :]
    else:
        shifted = jnp.where((row & (seq_len - 1)) == 0, zp_ref[...], rolled)
    zm = zs + mu_ref[...] * (shifted - zs)
    r = zm[:, 0:RWKV_W]
    k = zm[:, RWKV_W:2 * RWKV_W]
    v = zm[:, 2 * RWKV_W:3 * RWKV_W]
    c0 = 3 * RWKV_W
    xw = zm[:, c0:c0 + LORA_PAD]
    xa = zm[:, c0 + LORA_PAD:c0 + 2 * LORA_PAD]
    xg = zm[:, c0 + 2 * LORA_PAD:c0 + 2 * LORA_PAD + GATE_PAD]
    bd = bd_ref[...]
    u = -(w0_ref[...] + jnp.dot(jnp.tanh(xw), wd_ref[...], precision=HIGHEST, preferred_element_type=F32))
    softplus = jnp.maximum(u, 0.0) + jnp.log(1.0 + jnp.exp(-jnp.abs(u)))
    decay = jnp.exp(-jnp.exp(-softplus - 0.5))
    a = _sigmoid(a0_ref[...] + jnp.dot(xa, wa_ref[...], precision=HIGHEST, preferred_element_type=F32))
    g = jnp.dot(_sigmoid(xg), wg_ref[...], precision=HIGHEST, preferred_element_type=F32)
    kk = k * kk_ref[...]
    kk = kk / jnp.maximum(jnp.sqrt(_segsum(kk * kk, bd)), 1e-12)
    kp = k * (1.0 + (a - 1.0) * ka_ref[...])
    for slot, value in enumerate((r, decay, kp, kk, -(kk * a), v)):
        scan_o[:, slot * RWKV_W:(slot + 1) * RWKV_W] = value
    g_o[...] = g
    bonus_o[...] = _segsum(r * kp * rk_ref[...], bd) * v


def _rwkv_prep(z, zprev, b, t, tt, params):
    n_tok = b * t
    assert n_tok % tt == 0
    row = lambda w: pl.BlockSpec((1, w), lambda i: (0, 0))
    full = lambda a: pl.BlockSpec(a.shape, lambda i: (0, 0))
    tok = pl.BlockSpec((tt, RWKV_W), lambda i: (i, 0))
    mu, w0, a0, wd, wa, wg, k_k, k_a, r_k, bd = params
    if t % tt == 0:
        tiles_per_seq = t // tt
        zp = zprev.reshape(b, 1, ZS_W)
        zp_spec = pl.BlockSpec((1, 1, ZS_W), lambda i: (i // tiles_per_seq, 0, 0))
    else:
        assert tt % t == 0 and t & (t - 1) == 0
        tiles_per_seq = 0
        zp = jnp.repeat(zprev, t, axis=0)
        zp_spec = pl.BlockSpec((tt, ZS_W), lambda i: (i, 0))
    return pl.pallas_call(
        functools.partial(_prep_kernel, tiles_per_seq=tiles_per_seq, seq_len=t),
        grid=(n_tok // tt,),
        in_specs=[pl.BlockSpec((tt, ZS_W), lambda i: (i, 0)), zp_spec,
                  row(ZS_W), row(RWKV_W), row(RWKV_W), full(wd), full(wa), full(wg),
                  row(RWKV_W), row(RWKV_W), row(RWKV_W), full(bd)],
        out_specs=[pl.BlockSpec((tt, SCAN_INPUTS * RWKV_W), lambda i: (i, 0)), tok, tok],
        out_shape=[jax.ShapeDtypeStruct((n_tok, SCAN_INPUTS * RWKV_W), F32),
                   jax.ShapeDtypeStruct((n_tok, RWKV_W), F32), jax.ShapeDtypeStruct((n_tok, RWKV_W), F32)],
        scratch_shapes=[pltpu.VMEM((1, ZS_W), F32)],
        compiler_params=_cparams(1),
        name="rwkv_prep",
    )(z, zp, mu, w0, a0, wd, wa, wg, k_k, k_a, r_k, bd)


def _wkv_kernel(r_ref, w_ref, k_ref, kk_ref, nb_ref, v_ref, s0_ref, y_ref, s_ref, *scratch, tb, dup):
    slabs = s_ref.shape[1]
    key_unroll = 8

    @pl.when(pl.program_id(1) == 0)
    def _():
        s_ref[...] = s0_ref[...]

    if dup:
        for src, dst in zip((r_ref, w_ref, k_ref, kk_ref, nb_ref), scratch):
            x = src[...].reshape(tb * (HEAD // 2), LANES)
            swapped = pltpu.roll(x, LANES // 2, 1)
            low = lax.broadcasted_iota(I32, x.shape, 1) < LANES // 2
            dst[:, 0:HEAD // 2, :] = jnp.where(low, x, swapped).reshape(tb, HEAD // 2, LANES)
            dst[:, HEAD // 2:, :] = jnp.where(low, swapped, x).reshape(tb, HEAD // 2, LANES)
        r_ref, w_ref, k_ref, kk_ref, nb_ref = scratch

    def rows(ref, t, k):
        return jnp.broadcast_to(ref[t, pl.ds(k, 1), :], (SUBLANES, LANES))

    def step(t, sa, with_next):
        vt = [v_ref[t, i * SUBLANES:(i + 1) * SUBLANES, :] for i in range(slabs)]
        zero = jnp.zeros((SUBLANES, LANES), F32)

        def key(k, acc):
            y_acc, next_acc = list(acc[0]), list(acc[1])
            wb, nbb, kb, rb = rows(w_ref, t, k), rows(nb_ref, t, k), rows(k_ref, t, k), rows(r_ref, t, k)
            kkb = rows(kk_ref, t + 1, k) if with_next else None
            for i in range(slabs):
                sn = s_ref[k, i] * wb + sa[i] * nbb + vt[i] * kb
                s_ref[k, i] = sn
                y_acc[i] = y_acc[i] + sn * rb
                if with_next:
                    next_acc[i] = next_acc[i] + sn * kkb
            return tuple(y_acc), tuple(next_acc)

        y_acc, next_acc = lax.fori_loop(0, HEAD, key, ((zero,) * slabs, (zero,) * slabs), unroll=key_unroll)
        for i, y in enumerate(y_acc):
            y_ref[t, i * SUBLANES:(i + 1) * SUBLANES, :] = y
        return next_acc

    def first(k, acc):
        kkb = rows(kk_ref, 0, k)
        return tuple(a + s_ref[k, i] * kkb for i, a in enumerate(acc))

    sa0 = lax.fori_loop(0, HEAD, first, (jnp.zeros((SUBLANES, LANES), F32),) * slabs, unroll=key_unroll)
    sa_last = lax.fori_loop(0, tb - 1, lambda t, sa: step(t, sa, True), sa0)
    step(tb - 1, sa_last, False)


def _wkv(scan, s0, tb):
    t, _, rows, l = scan.shape
    dup = rows != HEAD
    assert t % tb == 0 and l % LANES == 0 and rows % SUBLANES == 0 and (not dup or (l == LANES and 2 * rows == HEAD))
    vec = lambda a: pl.BlockSpec((tb, None, rows, LANES), lambda g, i: (i, a, 0, g))
    slabs = rows // SUBLANES
    st = pl.BlockSpec((HEAD, slabs, SUBLANES, LANES), lambda g, i: (0, 0, 0, g))
    s0 = s0.reshape(HEAD, slabs, SUBLANES, l)
    y, s_new = pl.pallas_call(
        functools.partial(_wkv_kernel, tb=tb, dup=dup),
        grid=(l // LANES, t // tb),
        in_specs=[vec(a) for a in range(SCAN_INPUTS)] + [st],
        out_specs=[pl.BlockSpec((tb, rows, LANES), lambda g, i: (i, 0, g)), st],
        out_shape=[jax.ShapeDtypeStruct((t, rows, l), F32), jax.ShapeDtypeStruct(s0.shape, F32)],
        scratch_shapes=[pltpu.VMEM((tb, HEAD, LANES), F32)] * 5 if dup else [],
        compiler_params=_cparams(2),
        name="wkv",
    )(*([scan] * SCAN_INPUTS), s0)
    return y, s_new.reshape(HEAD, rows, l)


def _pool_kernel(prev_ref, zp_ref, pw_ref, ps_ref, o_ref, *, t, pos0):
    bb = zp_ref.shape[0]
    lane = lax.broadcasted_iota(I32, (t, POOL_W), 1)
    window = jnp.where(lane < 64, 2, jnp.where(lane < 128, 4, jnp.where(lane < 192, 8, 16)))
    pos = lax.broadcasted_iota(I32, (t, POOL_W), 0) + (pos0 + 1)
    cnt = jnp.minimum(pos, window).astype(F32)
    diffs = []
    for i in range(bb):
        f = jnp.concatenate([prev_ref[i], zp_ref[i]], axis=0)
        s2 = f + pltpu.roll(f, 1, 0)
        s4 = s2 + pltpu.roll(s2, 2, 0)
        s8 = s4 + pltpu.roll(s4, 4, 0)
        s16 = s8 + pltpu.roll(s8, 8, 0)
        wsum = jnp.where(lane < 64, s2[POOL_PAD:], jnp.where(lane < 128, s4[POOL_PAD:],
                                                            jnp.where(lane < 192, s8[POOL_PAD:], s16[POOL_PAD:])))
        diffs.append(wsum / cnt - f[POOL_PAD:])
    diff = diffs[0] if bb == 1 else jnp.concatenate(diffs, axis=0)
    out = jnp.dot(diff.astype(BF16), pw_ref[...], preferred_element_type=F32) * ps_ref[...]
    for i in range(bb):
        o_ref[i] = out[i * t:(i + 1) * t]


def _pool(prev, z3, pw_bd, pscale, bb, pos0):
    b, t, _ = z3.shape
    assert b % bb == 0
    return pl.pallas_call(
        functools.partial(_pool_kernel, t=t, pos0=pos0),
        grid=(b // bb,),
        in_specs=[pl.BlockSpec((bb, POOL_PAD, POOL_W), lambda i: (i, 0, 0)),
                  pl.BlockSpec((bb, t, POOL_W), lambda i: (i, 0, ZS_W // POOL_W)),
                  pl.BlockSpec((POOL_W, POOL_W), lambda i: (0, 0)), pl.BlockSpec((1, POOL_W), lambda i: (0, 0))],
        out_specs=pl.BlockSpec((bb, t, POOL_W), lambda i: (i, 0, 0)),
        out_shape=jax.ShapeDtypeStruct((b, t, POOL_W), F32),
        compiler_params=_cparams(1),
        name="pool",
    )(prev, z3, pw_bd, pscale)


def _memattn_kernel(q_ref, k_ref, v_ref, o_ref):
    for i in range(q_ref.shape[0]):
        q = q_ref[i]
        kb = k_ref[i].astype(BF16)
        vb = v_ref[i].astype(BF16)
        head_of_lane = lax.broadcasted_iota(I32, q.shape, 1) // (MEM_W // MEM_HEADS)
        out = jnp.zeros(q.shape, F32)
        for h in range(MEM_HEADS):
            qh = jnp.where(head_of_lane == h, q, 0.0).astype(BF16)
            s = lax.dot_general(qh, kb, (((1,), (1,)), ((), ())), preferred_element_type=F32) * (64 ** -0.5)
            e = jnp.exp(s - jnp.max(s, axis=-1, keepdims=True))
            p = e / jnp.sum(e, axis=-1, keepdims=True)
            o = jnp.dot(p.astype(BF16), vb, preferred_element_type=F32)
            out = jnp.where(head_of_lane == h, o, out)
        o_ref[i] = out


def _memattn(z3, mk, mv, tt, bb):
    b, t, _ = z3.shape
    assert t % tt == 0 and b % bb == 0
    qcol = (Z_W - MEM_W) // MEM_W
    kv = pl.BlockSpec((bb, MEM_TOKENS, MEM_W), lambda i, j: (i, 0, 0))
    return pl.pallas_call(
        _memattn_kernel,
        grid=(b // bb, t // tt),
        in_specs=[pl.BlockSpec((bb, tt, MEM_W), lambda i, j: (i, j, qcol)), kv, kv],
        out_specs=pl.BlockSpec((bb, tt, MEM_W), lambda i, j: (i, j, 0)),
        out_shape=jax.ShapeDtypeStruct((b, t, MEM_W), F32),
        compiler_params=_cparams(2),
        name="mem_attn",
    )(z3, mk, mv)


def _layer_norm(x, g, b):
    mu = jnp.mean(x, axis=-1, keepdims=True)
    xc = x - mu
    var = jnp.mean(xc * xc, axis=-1, keepdims=True)
    return xc * lax.rsqrt(var + LN_EPS) * g + b


def _post_kernel(y_ref, bonus_ref, g_ref, op_ref, om_ref, x_ref, wo_ref, lxw_ref, lxb_ref, l1g_ref, l1b_ref, bd_ref,
                 *rest, alpha):
    h_ref = rest[-1]
    y = y_ref[...]
    bd = bd_ref[...]
    mu = _segsum(y, bd) * (1.0 / HEAD)
    yc = y - mu
    var = _segsum(yc * yc, bd) * (1.0 / HEAD)
    yn = yc * lax.rsqrt(var + GN_EPS) * lxw_ref[...] + lxb_ref[...]
    o_rwkv = (yn + bonus_ref[...]) * g_ref[...]
    mixed = (jnp.dot(o_rwkv.astype(BF16), wo_ref[0:RWKV_W, :], preferred_element_type=F32)
             + jnp.dot(op_ref[...].astype(BF16), wo_ref[RWKV_W:RWKV_W + POOL_W, :], preferred_element_type=F32)
             + jnp.dot(om_ref[...].astype(BF16), wo_ref[RWKV_W + POOL_W:, :], preferred_element_type=F32))
    h_ref[...] = _layer_norm(alpha * x_ref[...] + mixed, l1g_ref[...], l1b_ref[...])


def _post(y, bonus, g, o_pool, o_mem, x, wo, lxw, lxb, l1g, l1b, bd, tt, alpha, h_all):
    n = y.shape[0]
    assert n % tt == 0
    tok = lambda w: pl.BlockSpec((tt, w), lambda i: (i, 0))
    row = lambda w: pl.BlockSpec((1, w), lambda i: (0, 0))
    full = lambda a: pl.BlockSpec(a.shape, lambda i: (0, 0))
    in_specs = [tok(RWKV_W), tok(RWKV_W), tok(RWKV_W), tok(POOL_W), tok(MEM_W), tok(D_MODEL), full(wo),
                row(RWKV_W), row(RWKV_W), row(D_MODEL), row(D_MODEL), full(bd)]
    args = [y, bonus, g, o_pool, o_mem, x, wo, lxw, lxb, l1g, l1b, bd]
    if isinstance(h_all, tuple):
        n_total, first = h_all
        aliases = {}
    else:
        n_total, first = h_all.shape[0], h_all.shape[0] - n
        in_specs.append(pl.BlockSpec(memory_space=pl.ANY))
        args.append(h_all)
        aliases = {len(args) - 1: 0}
    assert first % tt == 0
    return pl.pallas_call(
        functools.partial(_post_kernel, alpha=alpha),
        grid=(n // tt,),
        in_specs=in_specs,
        out_specs=pl.BlockSpec((tt, D_MODEL), lambda i: (i + first // tt, 0)),
        out_shape=jax.ShapeDtypeStruct((n_total, D_MODEL), F32),
        input_output_aliases=aliases,
        compiler_params=_cparams(1),
        name="post",
    )(*args)


def _router_kernel(h_ref, rwt_ref, bias_ref, tri_ref, below_ref, slot_o, gate_o, run_o, cnt_o, carry_ref):
    @pl.when(pl.program_id(0) == 0)
    def _():
        carry_ref[...] = jnp.zeros_like(carry_ref)

    neg = -jnp.inf
    logits = lax.dot_general(rwt_ref[...], h_ref[...], (((1,), (1,)), ((), ())),
                             precision=HIGHEST, preferred_element_type=F32)
    scores = _sigmoid(logits)
    sel = scores + bias_ref[...]
    tt = sel.shape[1]
    gio = lax.broadcasted_iota(I32, (GROUP_SIZE, tt), 0).astype(F32)
    blocks, gscore = [], []
    for g in range(N_GROUPS):
        blk = sel[g * GROUP_SIZE:(g + 1) * GROUP_SIZE, :]
        m1 = jnp.max(blk, axis=0, keepdims=True)
        first = jnp.min(jnp.where(blk == m1, gio, float(GROUP_SIZE)), axis=0, keepdims=True)
        m2 = jnp.max(jnp.where(gio == first, neg, blk), axis=0, keepdims=True)
        blocks.append(blk)
        gscore.append(m1 + m2)
    masked = []
    for g in range(N_GROUPS):
        beaten_by = jnp.zeros((1, tt), F32)
        for g2 in range(N_GROUPS):
            if g2 != g:
                wins = (gscore[g2] >= gscore[g]) if g2 < g else (gscore[g2] > gscore[g])
                beaten_by = beaten_by + jnp.where(wins, 1.0, 0.0)
        masked.append(jnp.where(beaten_by < TOPK_GROUPS, blocks[g], neg))
    msel = jnp.concatenate(masked, axis=0)
    eio = lax.broadcasted_iota(I32, msel.shape, 0).astype(F32)
    chosen = jnp.zeros(msel.shape, F32)
    idxs, scs = [], []
    for _ in range(TOP_K):
        m = jnp.max(msel, axis=0, keepdims=True)
        first = jnp.min(jnp.where(msel == m, eio, float(N_EXPERTS)), axis=0, keepdims=True)
        hit = eio == first
        scs.append(jnp.sum(jnp.where(hit, scores, 0.0), axis=0, keepdims=True))
        msel = jnp.where(hit, neg, msel)
        chosen = jnp.where(hit, 1.0, chosen)
        idxs.append(first)
    total = scs[0]
    for s in scs[1:]:
        total = total + s
    chosen_b = chosen.astype(BF16)
    earlier = jnp.dot(chosen_b, tri_ref[...], preferred_element_type=F32)
    smaller = jnp.dot(below_ref[...], chosen_b, preferred_element_type=F32)
    run_len = jnp.sum(chosen, axis=1, keepdims=True)
    run_off = jnp.sum(smaller, axis=1, keepdims=True)
    slot_of = run_off + earlier
    slots = [jnp.sum(jnp.where(eio == i, slot_of, 0.0), axis=0, keepdims=True) for i in idxs]
    slot_o[...] = jnp.concatenate(slots, axis=0).astype(I32)
    gate_o[...] = jnp.concatenate([s / total * ROUTED_SCALE for s in scs], axis=0)
    lane = lax.broadcasted_iota(I32, run_o.shape, 1)
    run_o[...] = jnp.where(lane == 0, run_len, jnp.where(lane == 1, run_off, jnp.where(lane == 2, carry_ref[...], 0.0)))
    carry_ref[...] = carry_ref[...] + run_len
    cnt_o[...] = jnp.broadcast_to(carry_ref[...], cnt_o.shape)


def _router(h, rwt, bias, tt):
    n = h.shape[0]
    assert n % tt == 0
    tri = (lax.broadcasted_iota(I32, (tt, tt), 0) < lax.broadcasted_iota(I32, (tt, tt), 1)).astype(BF16)
    below = (lax.broadcasted_iota(I32, (N_EXPERTS, N_EXPERTS), 1)
             < lax.broadcasted_iota(I32, (N_EXPERTS, N_EXPERTS), 0)).astype(BF16)
    tokT = pl.BlockSpec((TOP_K, tt), lambda i: (0, i))
    return pl.pallas_call(
        _router_kernel,
        grid=(n // tt,),
        in_specs=[pl.BlockSpec((tt, D_MODEL), lambda i: (i, 0)), pl.BlockSpec((N_EXPERTS, D_MODEL), lambda i: (0, 0)),
                  pl.BlockSpec((N_EXPERTS, 1), lambda i: (0, 0)), pl.BlockSpec((tt, tt), lambda i: (0, 0)),
                  pl.BlockSpec((N_EXPERTS, N_EXPERTS), lambda i: (0, 0))],
        out_specs=[tokT, tokT, pl.BlockSpec((N_EXPERTS, LANES), lambda i: (i, 0)),
                   pl.BlockSpec((N_EXPERTS, LANES), lambda i: (0, 0))],
        out_shape=[jax.ShapeDtypeStruct((TOP_K, n), I32), jax.ShapeDtypeStruct((TOP_K, n), F32),
                   jax.ShapeDtypeStruct((n // tt * N_EXPERTS, LANES), F32),
                   jax.ShapeDtypeStruct((N_EXPERTS, LANES), F32)],
        scratch_shapes=[pltpu.VMEM((N_EXPERTS, 1), F32)],
        compiler_params=_cparams(1),
        name="router",
    )(h, rwt, bias, tri, below)


ROW_SUB = D_MODEL // LANES


def _store_row_tiles(ref, x):
    rows = x.shape[0]
    for c in range(ROW_SUB):
        ref[pl.ds(c, rows, stride=ROW_SUB), :] = x[:, c * LANES:(c + 1) * LANES]


def _load_row_tiles(ref):
    rows = ref.shape[0] // ROW_SUB
    return jnp.concatenate([ref[pl.ds(c, rows, stride=ROW_SUB), :] for c in range(ROW_SUB)], axis=-1)


RUN_FIELDS = 3
RUN_UNROLL = 8


def _start_run_copies(runs_ref, tile, local_ref, sorted_ref, sem, to_sorted):
    def body(e, c):
        size = pl.multiple_of(runs_ref[tile * RUN_FIELDS + 2, e], ROW_SUB)
        local = local_ref.at[pl.ds(pl.multiple_of(runs_ref[tile * RUN_FIELDS, e], ROW_SUB), size), :]
        remote = sorted_ref.at[pl.ds(pl.multiple_of(runs_ref[tile * RUN_FIELDS + 1, e], ROW_SUB), size), :]
        (pltpu.make_async_copy(local, remote, sem) if to_sorted else pltpu.make_async_copy(remote, local, sem)).start()
        return c

    lax.fori_loop(0, N_EXPERTS, body, 0, unroll=RUN_UNROLL)


def _wait_run_copies(local_ref, sorted_ref, sem):
    pltpu.make_async_copy(sorted_ref.at[pl.ds(0, local_ref.shape[0]), :], local_ref, sem).wait()


def _dispatch_kernel(zpos_ref, runs_ref, slot_ref, h_ref, xs_hbm, zero_buf, rows_buf, sem_zero, sem_rows, *, td):
    i = pl.program_id(0)

    def zero_copy(e):
        start = pl.multiple_of(zpos_ref[e] * ROW_SUB, CHUNK * ROW_SUB)
        return pltpu.make_async_copy(zero_buf, xs_hbm.at[pl.ds(start, CHUNK * ROW_SUB), :], sem_zero)

    @pl.when(i == 0)
    def _():
        zero_buf[...] = jnp.zeros_like(zero_buf)

        def start(e, c):
            @pl.when(zpos_ref[e] >= 0)
            def _():
                zero_copy(e).start()
            return c

        def wait(e, c):
            @pl.when(zpos_ref[e] >= 0)
            def _():
                zero_copy(e).wait()
            return c

        lax.fori_loop(0, N_EXPERTS, start, 0)
        lax.fori_loop(0, N_EXPERTS, wait, 0)

    slots = slot_ref[...]
    slot_iota = lax.broadcasted_iota(I32, (TOP_K * td, td), 0)
    select = jnp.zeros((TOP_K * td, td), F32)
    for j in range(TOP_K):
        select = jnp.where(slot_iota == slots[j:j + 1, :], 1.0, select)
    local = jnp.dot(select.astype(BF16), h_ref[...].astype(BF16), preferred_element_type=F32)
    _store_row_tiles(rows_buf, local)
    _start_run_copies(runs_ref, i, rows_buf, xs_hbm, sem_rows, True)
    _wait_run_copies(rows_buf, xs_hbm, sem_rows)


def _dispatch(zpos, runs, slots, h, n_rows, td):
    n = h.shape[0]
    assert n % td == 0
    return pl.pallas_call(
        functools.partial(_dispatch_kernel, td=td),
        grid_spec=pltpu.PrefetchScalarGridSpec(
            num_scalar_prefetch=2,
            grid=(n // td,),
            in_specs=[pl.BlockSpec((TOP_K, td), lambda i, z, r: (0, i)),
                      pl.BlockSpec((td, D_MODEL), lambda i, z, r: (i, 0))],
            out_specs=pl.BlockSpec(memory_space=pl.ANY),
            scratch_shapes=[pltpu.VMEM((CHUNK * ROW_SUB, LANES), F32), pltpu.VMEM((TOP_K * td * ROW_SUB, LANES), F32),
                            pltpu.SemaphoreType.DMA, pltpu.SemaphoreType.DMA],
        ),
        out_shape=jax.ShapeDtypeStruct((n_rows * ROW_SUB, LANES), F32),
        compiler_params=_cparams(1),
        name="dispatch",
    )(zpos, runs, slots, h)


CHUNK_ROWS = CHUNK * ROW_SUB
RING = 4


def _ffn_kernel(first_ref, count_ref, nu_ref, xs_hbm, wg_ref, wu_ref, wd_ref, ys_hbm, xbuf, obuf, xsem, osem):
    e = pl.program_id(0)
    n_used = nu_ref[0]

    def x_copy(g, slot):
        src = xs_hbm.at[pl.ds(pl.multiple_of(g * CHUNK_ROWS, CHUNK_ROWS), CHUNK_ROWS), :]
        return pltpu.make_async_copy(src, xbuf.at[slot], xsem.at[slot])

    def o_copy(g, slot):
        dst = ys_hbm.at[pl.ds(pl.multiple_of(g * CHUNK_ROWS, CHUNK_ROWS), CHUNK_ROWS), :]
        return pltpu.make_async_copy(obuf.at[slot], dst, osem.at[slot])

    @pl.when(e == 0)
    def _():
        for g in range(RING - 1):
            @pl.when(g < n_used)
            def _():
                x_copy(g, g).start()

    wg = wg_ref[0].astype(BF16)
    wu = wu_ref[0].astype(BF16)
    wd = wd_ref[0].astype(BF16)

    def chunk(c, carry):
        g = first_ref[e] + c
        slot = g & (RING - 1)
        x_copy(g, slot).wait()

        @pl.when(g + RING - 1 < n_used)
        def _():
            x_copy(g + RING - 1, (g + RING - 1) & (RING - 1)).start()

        @pl.when(g >= RING)
        def _():
            o_copy(g - RING, slot).wait()

        x = _load_row_tiles(xbuf.at[slot]).astype(BF16)
        gate = jnp.dot(x, wg, preferred_element_type=F32)
        up = jnp.dot(x, wu, preferred_element_type=F32)
        act = (gate * _sigmoid(gate)) * up
        _store_row_tiles(obuf.at[slot], jnp.dot(act.astype(BF16), wd, preferred_element_type=F32))
        o_copy(g, slot).start()
        return carry

    lax.fori_loop(0, count_ref[e], chunk, 0)

    @pl.when(e == N_EXPERTS - 1)
    def _():
        for back in range(RING, 0, -1):
            @pl.when(n_used >= back)
            def _():
                o_copy(n_used - back, (n_used - back) & (RING - 1)).wait()


def _ffn(first_chunk, chunk_count, n_used, xs, wg, wu, wd):
    weights = lambda a, b: pl.BlockSpec((1, a, b), lambda e, f, c, nu: (e, 0, 0))
    return pl.pallas_call(
        _ffn_kernel,
        grid_spec=pltpu.PrefetchScalarGridSpec(
            num_scalar_prefetch=3,
            grid=(N_EXPERTS,),
            in_specs=[pl.BlockSpec(memory_space=pl.ANY), weights(D_MODEL, EXPERT_FF), weights(D_MODEL, EXPERT_FF),
                      weights(EXPERT_FF, D_MODEL)],
            out_specs=pl.BlockSpec(memory_space=pl.ANY),
            scratch_shapes=[pltpu.VMEM((RING, CHUNK_ROWS, LANES), F32), pltpu.VMEM((RING, CHUNK_ROWS, LANES), F32),
                            pltpu.SemaphoreType.DMA((RING,)), pltpu.SemaphoreType.DMA((RING,))],
        ),
        out_shape=jax.ShapeDtypeStruct(xs.shape, F32),
        compiler_params=_cparams(1),
        name="expert_ffn",
    )(first_chunk, chunk_count, n_used, xs, wg, wu, wd)


def _combine_kernel(runs_ref, slot_ref, gate_ref, h_ref, ys_hbm, sg_ref, su_ref, sd_ref, l2g_ref, l2b_ref,
                    o1_ref, o2_ref, rows_buf, sem_rows, *, tc, alpha, first_tiles):
    i = pl.program_id(0)
    slot = i & 1
    buf = rows_buf.at[slot]

    @pl.when(i == 0)
    def _():
        _start_run_copies(runs_ref, i, buf, ys_hbm, sem_rows.at[slot], False)

    @pl.when(i + 1 < pl.num_programs(0))
    def _():
        _start_run_copies(runs_ref, i + 1, rows_buf.at[1 - slot], ys_hbm, sem_rows.at[1 - slot], False)

    h = h_ref[...]
    hb = h.astype(BF16)
    sgate = jnp.dot(hb, sg_ref[...], preferred_element_type=F32)
    sup = jnp.dot(hb, su_ref[...], preferred_element_type=F32)
    shared = jnp.dot(((sgate * _sigmoid(sgate)) * sup).astype(BF16), sd_ref[...], preferred_element_type=F32)
    slots = slot_ref[...]
    gate = gate_ref[...]
    slot_iota = lax.broadcasted_iota(I32, (tc, TOP_K * tc), 1)
    weights = jnp.zeros((tc, TOP_K * tc), F32)
    for j in range(TOP_K):
        weights = jnp.where(slot_iota == slots[:, j:j + 1], gate[:, j:j + 1], weights)
    w_hi = weights.astype(BF16)
    w_lo = (weights - w_hi.astype(F32)).astype(BF16)
    _wait_run_copies(buf, ys_hbm, sem_rows.at[slot])
    local = _load_row_tiles(buf).astype(BF16)
    routed = (jnp.dot(w_hi, local, preferred_element_type=F32) + jnp.dot(w_lo, local, preferred_element_type=F32))
    y = _layer_norm(alpha * h + (routed + shared), l2g_ref[...], l2b_ref[...])

    @pl.when(i < first_tiles)
    def _():
        o1_ref[...] = y

    @pl.when(i >= first_tiles)
    def _():
        o2_ref[...] = y


def _combine(runs, slots, gate, h, ys, sg, su, sd, l2g, l2b, tc, alpha, n_first):
    n = h.shape[0]
    assert n % tc == 0 and n_first % tc == 0 and 0 < n_first < n
    first_tiles = n_first // tc
    tok = lambda w: pl.BlockSpec((tc, w), lambda i, r: (i, 0))
    full = lambda a: pl.BlockSpec(a.shape, lambda i, r: (0, 0))
    return pl.pallas_call(
        functools.partial(_combine_kernel, tc=tc, alpha=alpha, first_tiles=first_tiles),
        grid_spec=pltpu.PrefetchScalarGridSpec(
            num_scalar_prefetch=1,
            grid=(n // tc,),
            in_specs=[tok(TOP_K), tok(TOP_K), tok(D_MODEL), pl.BlockSpec(memory_space=pl.ANY),
                      full(sg), full(su), full(sd), full(l2g), full(l2b)],
            out_specs=[pl.BlockSpec((tc, D_MODEL), lambda i, r: (jnp.minimum(i, first_tiles - 1), 0)),
                       pl.BlockSpec((tc, D_MODEL), lambda i, r: (jnp.maximum(i - first_tiles, 0), 0))],
            scratch_shapes=[pltpu.VMEM((2, TOP_K * tc * ROW_SUB, LANES), F32), pltpu.SemaphoreType.DMA((2,))],
        ),
        out_shape=[jax.ShapeDtypeStruct((n_first, D_MODEL), F32), jax.ShapeDtypeStruct((n - n_first, D_MODEL), F32)],
        compiler_params=_cparams(1),
        name="combine",
    )(runs, slots, gate, h, ys, sg, su, sd, l2g, l2b)


def _pad_cols(a, width):
    return jnp.pad(a, ((0, 0), (0, width - a.shape[1])))


def _pack_shift_cols(a):
    c = 3 * RWKV_W
    return jnp.concatenate([a[:, :c], _pad_cols(a[:, c:c + DECAY_LORA], LORA_PAD),
                            _pad_cols(a[:, c + DECAY_LORA:c + DECAY_LORA + AAA_LORA], LORA_PAD),
                            _pad_cols(a[:, c + DECAY_LORA + AAA_LORA:], GATE_PAD)], axis=1)


def _scan_to_lanes(a, b, t, split):
    a = a.reshape(b, t, SCAN_INPUTS, N_HEADS, HEAD).transpose(1, 2, 4, 0, 3)
    return a.reshape(t, SCAN_INPUTS, HEAD // split, split * b * N_HEADS)


def _values_from_lanes(y, b, t):
    return y.reshape(t, HEAD, b, N_HEADS).transpose(2, 0, 3, 1).reshape(b * t, RWKV_W)


def _state_to_lanes(s, b, split):
    s = s.reshape(b, N_HEADS, HEAD // split, split, HEAD).transpose(4, 2, 3, 0, 1)
    return s.reshape(HEAD, HEAD // split, split * b * N_HEADS)


def _state_from_lanes(s, b, split):
    s = s.reshape(HEAD, HEAD // split, split, b, N_HEADS).transpose(3, 4, 1, 2, 0)
    return s.reshape(b, N_HEADS, HEAD, HEAD)


def _mixer(z, zprev, x2, b, t, wkv0, pool_prev, mk, mv, pos0, wts, tiles, h_all):
    n = b * t
    tt, tb, seqs, att_tt = tiles
    z3 = z[:n].reshape(b, t, Z_W)
    scan, g, bonus = _rwkv_prep(z, zprev, b, t, tt, wts["prep"])

    split = 1 if (b * N_HEADS) % LANES == 0 else LANES // (b * N_HEADS)
    key_order = jnp.concatenate([jnp.arange(i, HEAD, split) for i in range(split)])
    s_in = wkv0[..., key_order] if split > 1 else wkv0
    y_l, s_l = _wkv(_scan_to_lanes(scan, b, t, split), _state_to_lanes(s_in, b, split), tb)
    y = _values_from_lanes(y_l, b, t)
    wkv_new = _state_from_lanes(s_l, b, split)
    if split > 1:
        wkv_new = wkv_new[..., jnp.argsort(key_order)]

    prev = jnp.concatenate([jnp.zeros((b, POOL_PAD - POOL_BUF, POOL_W), F32), pool_prev], axis=1)
    o_pool = _pool(prev, z3, wts["pool_w"], wts["pool_scale"], seqs, pos0).reshape(n, POOL_W)
    tail = z3[:, max(t - POOL_BUF, 0):, ZS_W:ZS_W + POOL_W]
    pool_new = jnp.concatenate([pool_prev, tail], axis=1)[:, -POOL_BUF:]

    o_mem = _memattn(z3, mk, mv, att_tt, seqs).reshape(n, MEM_W)
    h = _post(y, bonus, g, o_pool, o_mem, x2, *wts["post"], tt, wts["alpha"], h_all)
    return h, wkv_new, pool_new


def kernel(x_prompt, x_sample, mem_prompt, state_wkv, state_shift, state_pool, cache_mem_k, cache_mem_v, w_in, mu_shift, w0, w_up_decay, a0, w_up_aaa, w_up_gate, k_k, k_a, r_k, ln_x_w, ln_x_b, pool_w, pool_scale, mem_wk, mem_wv, w_out, ln1_g, ln1_b, router_w, router_b, exp_gate, exp_up, exp_down, sh_gate, sh_up, sh_down, ln2_g, ln2_b):
    depth = w_in.shape[0]
    assert depth == 1
    l = 0
    alpha = (2.0 * depth) ** 0.25
    bp, tp, d = x_prompt.shape
    bs, ts, _ = x_sample.shape
    n_p, n_s = bp * tp, bs * ts
    n = n_p + n_s

    w_in_p = jnp.concatenate([_pack_shift_cols(w_in[l][:, :1824]), w_in[l][:, 1824:]], axis=1).astype(BF16)
    row = lambda a: a.reshape(1, -1)
    pad_rows = lambda a, rows: jnp.pad(a, ((0, rows - a.shape[0]), (0, 0)))
    bd = jnp.kron(jnp.eye(N_HEADS, dtype=F32), jnp.ones((HEAD, HEAD), F32)).astype(BF16)
    pw = pool_w[l]
    pw_bd = jnp.zeros((POOL_W, POOL_W), F32)
    for gi in range(4):
        pw_bd = pw_bd.at[gi * 64:(gi + 1) * 64, gi * 64:(gi + 1) * 64].set(pw[gi])
    wts = {
        "prep": (_pack_shift_cols(row(mu_shift[l])), row(w0[l]), row(a0[l]), pad_rows(w_up_decay[l], LORA_PAD),
                 pad_rows(w_up_aaa[l], LORA_PAD), pad_rows(w_up_gate[l], GATE_PAD), row(k_k[l]), row(k_a[l]),
                 row(r_k[l]), bd),
        "pool_w": pw_bd.astype(BF16), "pool_scale": row(pool_scale[l]),
        "post": (w_out[l].astype(BF16), row(ln_x_w[l]), row(ln_x_b[l]), row(ln1_g[l]), row(ln1_b[l]), bd),
        "alpha": alpha,
    }

    xp2 = x_prompt.reshape(n_p, d)
    xs2 = x_sample.reshape(n_s, d)
    z_p = _matmul(xp2, w_in_p, _tile(n_p, 512))
    z_s = _matmul(jnp.concatenate([xs2, state_shift[l]], axis=0), w_in_p, _tile(n_s + bs, 512))
    mkv = _matmul(mem_prompt.reshape(bp * MEM_TOKENS, d),
                  jnp.concatenate([mem_wk[l], mem_wv[l]], axis=1).astype(BF16), _tile(bp * MEM_TOKENS, 512))
    mk_p = mkv[:, :MEM_W].reshape(bp, MEM_TOKENS, MEM_W)
    mv_p = mkv[:, MEM_W:].reshape(bp, MEM_TOKENS, MEM_W)

    h_p, wkv_p, pool_p = _mixer(z_p, jnp.zeros((bp, ZS_W), F32), xp2, bp, tp,
                                jnp.zeros((bp, N_HEADS, HEAD, HEAD), F32), jnp.zeros((bp, POOL_BUF, POOL_W), F32),
                                mk_p, mv_p, 0, wts,
                                (_tile(n_p, 256), _tile(tp, 32, 1), 1, _tile(tp, 256)), (n, 0))
    h, wkv_s, pool_s = _mixer(z_s, z_s[n_s:, :ZS_W], xs2, bs, ts, state_wkv[l], state_pool[l],
                              cache_mem_k[l].reshape(bs, MEM_TOKENS, MEM_W),
                              cache_mem_v[l].reshape(bs, MEM_TOKENS, MEM_W), PAST_LEN, wts,
                              (_tile(n_s, 256), _tile(ts, 32, 1), _tile(bs, 16, 1), _tile(ts, 256)), h_p)

    td = _tile(n, 256, LANES)
    slots, gate, run_tab, cnt = _router(h, router_w[l].T, router_b[l].reshape(N_EXPERTS, 1), td)
    counts = cnt[:, 0].astype(I32)
    padded = (counts + CHUNK - 1) // CHUNK * CHUNK
    pad_end = jnp.cumsum(padded)
    pad_start = pad_end - padded
    n_chunks = (n * TOP_K + N_EXPERTS * (CHUNK - 1) + CHUNK - 1) // CHUNK
    run_tab = run_tab.reshape(n // td, N_EXPERTS, LANES)[:, :, :RUN_FIELDS].astype(I32)
    runs = jnp.stack([run_tab[:, :, 1], pad_start[None, :] + run_tab[:, :, 2], run_tab[:, :, 0]], axis=1)
    runs = runs.reshape(n // td * RUN_FIELDS, N_EXPERTS) * ROW_SUB
    zpos = jnp.where(padded > 0, pad_end - CHUNK, -1).astype(I32)
    n_used = (pad_end[-1:] // CHUNK).astype(I32)
    xs_sorted = _dispatch(zpos, runs, slots, h, n_chunks * CHUNK, td)
    ys_sorted = _ffn((pad_start // CHUNK).astype(I32), (padded // CHUNK).astype(I32), n_used, xs_sorted,
                     exp_gate[l], exp_up[l], exp_down[l])
    y_p, y_s = _combine(runs, slots.T, gate.T, h, ys_sorted, sh_gate[l].astype(BF16), sh_up[l].astype(BF16),
                        sh_down[l].astype(BF16), row(ln2_g[l]), row(ln2_b[l]), td, alpha, n_p)

    return (y_p.reshape(bp, tp, d), y_s.reshape(bs, ts, d),
            wkv_p[None], x_prompt[:, -1][None], pool_p[None],
            mk_p.reshape(bp, MEM_TOKENS, MEM_HEADS, HEAD)[None], mv_p.reshape(bp, MEM_TOKENS, MEM_HEADS, HEAD)[None],
            wkv_s[None], x_sample[:, -1][None], pool_s[None])
```

```python
import functools

import jax
import jax.numpy as jnp
from jax import lax
from jax.experimental import pallas as pl
from jax.experimental.pallas import tpu as pltpu

F32 = jnp.float32
BF16 = jnp.bfloat16
I32 = jnp.int32
HIGHEST = lax.Precision.HIGHEST

D_MODEL = 1024
HEAD = 64
N_HEADS = 8
RWKV_W = 512
POOL_W = 256
MEM_W = 256
MEM_HEADS = 4
MEM_TOKENS = 256
POOL_BUF = 15
POOL_PAD = 16
DECAY_LORA = 64
AAA_LORA = 64
GATE_LORA = 160
LORA_PAD = 128
GATE_PAD = 256
ZS_W = 3 * RWKV_W + 2 * LORA_PAD + GATE_PAD
Z_W = ZS_W + POOL_W + MEM_W
N_EXPERTS = 256
TOP_K = 8
N_GROUPS = 8
GROUP_SIZE = N_EXPERTS // N_GROUPS
TOPK_GROUPS = 4
EXPERT_FF = 256
ROUTED_SCALE = 2.5
CHUNK = 256
PAST_LEN = 16384
LN_EPS = 1e-5
GN_EPS = 64e-5
LANES = 128
SUBLANES = 8
SCAN_INPUTS = 6
VMEM_LIMIT = 56 * 1024 * 1024


def _cparams(n_axes):
    return pltpu.CompilerParams(dimension_semantics=("arbitrary",) * n_axes, vmem_limit_bytes=VMEM_LIMIT)


def _tile(n, preferred, multiple=8):
    best = None
    for c in range(multiple, min(n, preferred) + 1, multiple):
        if n % c == 0:
            best = c
    assert best is not None, (n, preferred, multiple)
    return best


def _sigmoid(x):
    return 1.0 / (1.0 + jnp.exp(-x))


def _matmul_kernel(x_ref, w_ref, o_ref):
    o_ref[...] = jnp.dot(x_ref[...].astype(BF16), w_ref[...], preferred_element_type=F32)


def _matmul(x, w, tm):
    m, k = x.shape
    n = w.shape[1]
    assert m % tm == 0
    return pl.pallas_call(
        _matmul_kernel,
        grid=(m // tm,),
        in_specs=[pl.BlockSpec((tm, k), lambda i: (i, 0)), pl.BlockSpec((k, n), lambda i: (0, 0))],
        out_specs=pl.BlockSpec((tm, n), lambda i: (i, 0)),
        out_shape=jax.ShapeDtypeStruct((m, n), F32),
        compiler_params=_cparams(1),
        name="matmul",
    )(x, w)


def _segsum(x, bd):
    hi = x.astype(BF16)
    lo = (x - hi.astype(F32)).astype(BF16)
    return jnp.dot(hi, bd, preferred_element_type=F32) + jnp.dot(lo, bd, preferred_element_type=F32)


def _prep_kernel(zs_ref, zp_ref, mu_ref, w0_ref, a0_ref, wd_ref, wa_ref, wg_ref, kk_ref, ka_ref, rk_ref, bd_ref,
                 scan_o, g_o, bonus_o, carry_ref, *, tiles_per_seq, seq_len):
    zs = zs_ref[...]
    rolled = pltpu.roll(zs, 1, 0)
    row = lax.broadcasted_iota(I32, zs.shape, 0)
    if tiles_per_seq:
        i = pl.program_id(0)

        @pl.when(i == 0)
        def _():
            carry_ref[...] = jnp.zeros_like(carry_ref)

        prev = jnp.where(i % tiles_per_seq == 0, zp_ref[0], carry_ref[...])
        shifted = jnp.where(row == 0, prev, rolled)
        carry_ref[...] = zs[zs.shape[0] - 1:, :]
    else:
        shifted = jnp.where((row & (seq_len - 1)) == 0, zp_ref[...], rolled)
    zm = zs + mu_ref[...] * (shifted - zs)
    r = zm[:, 0:RWKV_W]
    k = zm[:, RWKV_W:2 * RWKV_W]
    v = zm[:, 2 * RWKV_W:3 * RWKV_W]
    c0 = 3 * RWKV_W
    xw = zm[:, c0:c0 + LORA_PAD]
    xa = zm[:, c0 + LORA_PAD:c0 + 2 * LORA_PAD]
    xg = zm[:, c0 + 2 * LORA_PAD:c0 + 2 * LORA_PAD + GATE_PAD]
    bd = bd_ref[...]
    u = -(w0_ref[...] + jnp.dot(jnp.tanh(xw), wd_ref[...], precision=HIGHEST, preferred_element_type=F32))
    softplus = jnp.maximum(u, 0.0) + jnp.log(1.0 + jnp.exp(-jnp.abs(u)))
    decay = jnp.exp(-jnp.exp(-softplus - 0.5))
    a = _sigmoid(a0_ref[...] + jnp.dot(xa, wa_ref[...], precision=HIGHEST, preferred_element_type=F32))
    g = jnp.dot(_sigmoid(xg), wg_ref[...], precision=HIGHEST, preferred_element_type=F32)
    kk = k * kk_ref[...]
    kk = kk / jnp.maximum(jnp.sqrt(_segsum(kk * kk, bd)), 1e-12)
    kp = k * (1.0 + (a - 1.0) * ka_ref[...])
    for slot, value in enumerate((r, decay, kp, kk, -(kk * a), v)):
        scan_o[:, slot * RWKV_W:(slot + 1) * RWKV_W] = value
    g_o[...] = g
    bonus_o[...] = _segsum(r * kp * rk_ref[...], bd) * v


def _rwkv_prep(z, zprev, b, t, tt, params):
    n_tok = b * t
    assert n_tok % tt == 0
    row = lambda w: pl.BlockSpec((1, w), lambda i: (0, 0))
    full = lambda a: pl.BlockSpec(a.shape, lambda i: (0, 0))
    tok = pl.BlockSpec((tt, RWKV_W), lambda i: (i, 0))
    mu, w0, a0, wd, wa, wg, k_k, k_a, r_k, bd = params
    if t % tt == 0:
        tiles_per_seq = t // tt
        zp = zprev.reshape(b, 1, ZS_W)
        zp_spec = pl.BlockSpec((1, 1, ZS_W), lambda i: (i // tiles_per_seq, 0, 0))
    else:
        assert tt % t == 0 and t & (t - 1) == 0
        tiles_per_seq = 0
        zp = jnp.repeat(zprev, t, axis=0)
        zp_spec = pl.BlockSpec((tt, ZS_W), lambda i: (i, 0))
    return pl.pallas_call(
        functools.partial(_prep_kernel, tiles_per_seq=tiles_per_seq, seq_len=t),
        grid=(n_tok // tt,),
        in_specs=[pl.BlockSpec((tt, ZS_W), lambda i: (i, 0)), zp_spec,
                  row(ZS_W), row(RWKV_W), row(RWKV_W), full(wd), full(wa), full(wg),
                  row(RWKV_W), row(RWKV_W), row(RWKV_W), full(bd)],
        out_specs=[pl.BlockSpec((tt, SCAN_INPUTS * RWKV_W), lambda i: (i, 0)), tok, tok],
        out_shape=[jax.ShapeDtypeStruct((n_tok, SCAN_INPUTS * RWKV_W), F32),
                   jax.ShapeDtypeStruct((n_tok, RWKV_W), F32), jax.ShapeDtypeStruct((n_tok, RWKV_W), F32)],
        scratch_shapes=[pltpu.VMEM((1, ZS_W), F32)],
        compiler_params=_cparams(1),
        name="rwkv_prep",
    )(z, zp, mu, w0, a0, wd, wa, wg, k_k, k_a, r_k, bd)


def _wkv_kernel(r_ref, w_ref, k_ref, kk_ref, nb_ref, v_ref, s0_ref, y_ref, s_ref, *scratch, tb, dup):
    slabs = s_ref.shape[1]
    key_unroll = 16

    @pl.when(pl.program_id(1) == 0)
    def _():
        s_ref[...] = s0_ref[...]

    if dup:
        for src, dst in zip((r_ref, w_ref, k_ref, kk_ref, nb_ref), scratch):
            x = src[...].reshape(tb * (HEAD // 2), LANES)
            swapped = pltpu.roll(x, LANES // 2, 1)
            low = lax.broadcasted_iota(I32, x.shape, 1) < LANES // 2
            dst[:, 0:HEAD // 2, :] = jnp.where(low, x, swapped).reshape(tb, HEAD // 2, LANES)
            dst[:, HEAD // 2:, :] = jnp.where(low, swapped, x).reshape(tb, HEAD // 2, LANES)
        r_ref, w_ref, k_ref, kk_ref, nb_ref = scratch

    def rows(ref, t, k):
        return jnp.broadcast_to(ref[t, pl.ds(k, 1), :], (SUBLANES, LANES))

    def step(t, sa, with_next):
        vt = [v_ref[t, i * SUBLANES:(i + 1) * SUBLANES, :] for i in range(slabs)]
        zero = jnp.zeros((SUBLANES, LANES), F32)

        def key(k, acc):
            y_acc, next_acc = list(acc[0]), list(acc[1])
            wb, nbb, kb, rb = rows(w_ref, t, k), rows(nb_ref, t, k), rows(k_ref, t, k), rows(r_ref, t, k)
            kkb = rows(kk_ref, t + 1, k) if with_next else None
            for i in range(slabs):
                sn = s_ref[k, i] * wb + sa[i] * nbb + vt[i] * kb
                s_ref[k, i] = sn
                y_acc[i] = y_acc[i] + sn * rb
                if with_next:
                    next_acc[i] = next_acc[i] + sn * kkb
            return tuple(y_acc), tuple(next_acc)

        y_acc, next_acc = lax.fori_loop(0, HEAD, key, ((zero,) * slabs, (zero,) * slabs), unroll=key_unroll)
        for i, y in enumerate(y_acc):
            y_ref[t, i * SUBLANES:(i + 1) * SUBLANES, :] = y
        return next_acc

    def first(k, acc):
        kkb = rows(kk_ref, 0, k)
        return tuple(a + s_ref[k, i] * kkb for i, a in enumerate(acc))

    sa0 = lax.fori_loop(0, HEAD, first, (jnp.zeros((SUBLANES, LANES), F32),) * slabs, unroll=key_unroll)
    sa_last = lax.fori_loop(0, tb - 1, lambda t, sa: step(t, sa, True), sa0)
    step(tb - 1, sa_last, False)


def _wkv(scan, s0, tb):
    t, _, rows, l = scan.shape
    dup = rows != HEAD
    assert t % tb == 0 and l % LANES == 0 and rows % SUBLANES == 0 and (not dup or (l == LANES and 2 * rows == HEAD))
    vec = lambda a: pl.BlockSpec((tb, None, rows, LANES), lambda g, i: (i, a, 0, g))
    slabs = rows // SUBLANES
    st = pl.BlockSpec((HEAD, slabs, SUBLANES, LANES), lambda g, i: (0, 0, 0, g))
    s0 = s0.reshape(HEAD, slabs, SUBLANES, l)
    y, s_new = pl.pallas_call(
        functools.partial(_wkv_kernel, tb=tb, dup=dup),
        grid=(l // LANES, t // tb),
        in_specs=[vec(a) for a in range(SCAN_INPUTS)] + [st],
        out_specs=[pl.BlockSpec((tb, rows, LANES), lambda g, i: (i, 0, g)), st],
        out_shape=[jax.ShapeDtypeStruct((t, rows, l), F32), jax.ShapeDtypeStruct(s0.shape, F32)],
        scratch_shapes=[pltpu.VMEM((tb, HEAD, LANES), F32)] * 5 if dup else [],
        compiler_params=_cparams(2),
        name="wkv",
    )(*([scan] * SCAN_INPUTS), s0)
    return y, s_new.reshape(HEAD, rows, l)


def _pool_kernel(prev_ref, zp_ref, pw_ref, ps_ref, o_ref, *, t, pos0):
    bb = zp_ref.shape[0]
    lane = lax.broadcasted_iota(I32, (t, POOL_W), 1)
    window = jnp.where(lane < 64, 2, jnp.where(lane < 128, 4, jnp.where(lane < 192, 8, 16)))
    pos = lax.broadcasted_iota(I32, (t, POOL_W), 0) + (pos0 + 1)
    cnt = jnp.minimum(pos, window).astype(F32)
    diffs = []
    for i in range(bb):
        f = jnp.concatenate([prev_ref[i], zp_ref[i]], axis=0)
        s2 = f + pltpu.roll(f, 1, 0)
        s4 = s2 + pltpu.roll(s2, 2, 0)
        s8 = s4 + pltpu.roll(s4, 4, 0)
        s16 = s8 + pltpu.roll(s8, 8, 0)
        wsum = jnp.where(lane < 64, s2[POOL_PAD:], jnp.where(lane < 128, s4[POOL_PAD:],
                                                            jnp.where(lane < 192, s8[POOL_PAD:], s16[POOL_PAD:])))
        diffs.append(wsum / cnt - f[POOL_PAD:])
    diff = diffs[0] if bb == 1 else jnp.concatenate(diffs, axis=0)
    out = jnp.dot(diff.astype(BF16), pw_ref[...], preferred_element_type=F32) * ps_ref[...]
    for i in range(bb):
        o_ref[i] = out[i * t:(i + 1) * t]


def _pool(prev, z3, pw_bd, pscale, bb, pos0):
    b, t, _ = z3.shape
    assert b % bb == 0
    return pl.pallas_call(
        functools.partial(_pool_kernel, t=t, pos0=pos0),
        grid=(b // bb,),
        in_specs=[pl.BlockSpec((bb, POOL_PAD, POOL_W), lambda i: (i, 0, 0)),
                  pl.BlockSpec((bb, t, POOL_W), lambda i: (i, 0, ZS_W // POOL_W)),
                  pl.BlockSpec((POOL_W, POOL_W), lambda i: (0, 0)), pl.BlockSpec((1, POOL_W), lambda i: (0, 0))],
        out_specs=pl.BlockSpec((bb, t, POOL_W), lambda i: (i, 0, 0)),
        out_shape=jax.ShapeDtypeStruct((b, t, POOL_W), F32),
        compiler_params=_cparams(1),
        name="pool",
    )(prev, z3, pw_bd, pscale)


def _memattn_kernel(q_ref, k_ref, v_ref, o_ref):
    for i in range(q_ref.shape[0]):
        q = q_ref[i].astype(BF16)
        kf = k_ref[i]
        vf = v_ref[i]
        head_of_lane = lax.broadcasted_iota(I32, kf.shape, 1) // (MEM_W // MEM_HEADS)
        out = None
        for h in range(MEM_HEADS):
            kh = jnp.where(head_of_lane == h, kf, 0.0).astype(BF16)
            vh = jnp.where(head_of_lane == h, vf, 0.0).astype(BF16)
            s = lax.dot_general(q, kh, (((1,), (1,)), ((), ())), preferred_element_type=F32) * (64 ** -0.5)
            e = jnp.exp(s - jnp.max(s, axis=-1, keepdims=True))
            p = e / jnp.sum(e, axis=-1, keepdims=True)
            o = jnp.dot(p.astype(BF16), vh, preferred_element_type=F32)
            out = o if out is None else out + o
        o_ref[i] = out


def _memattn(z3, mk, mv, tt, bb):
    b, t, _ = z3.shape
    assert t % tt == 0 and b % bb == 0
    qcol = (Z_W - MEM_W) // MEM_W
    kv = pl.BlockSpec((bb, MEM_TOKENS, MEM_W), lambda i, j: (i, 0, 0))
    return pl.pallas_call(
        _memattn_kernel,
        grid=(b // bb, t // tt),
        in_specs=[pl.BlockSpec((bb, tt, MEM_W), lambda i, j: (i, j, qcol)), kv, kv],
        out_specs=pl.BlockSpec((bb, tt, MEM_W), lambda i, j: (i, j, 0)),
        out_shape=jax.ShapeDtypeStruct((b, t, MEM_W), F32),
        compiler_params=_cparams(2),
        name="mem_attn",
    )(z3, mk, mv)


def _layer_norm(x, g, b):
    mu = jnp.mean(x, axis=-1, keepdims=True)
    xc = x - mu
    var = jnp.mean(xc * xc, axis=-1, keepdims=True)
    return xc * lax.rsqrt(var + LN_EPS) * g + b


def _post_kernel(y_ref, bonus_ref, g_ref, op_ref, om_ref, x_ref, wo_ref, lxw_ref, lxb_ref, l1g_ref, l1b_ref, bd_ref,
                 *rest, alpha):
    h_ref = rest[-1]
    y = y_ref[...]
    bd = bd_ref[...]
    mu = _segsum(y, bd) * (1.0 / HEAD)
    yc = y - mu
    var = _segsum(yc * yc, bd) * (1.0 / HEAD)
    yn = yc * lax.rsqrt(var + GN_EPS) * lxw_ref[...] + lxb_ref[...]
    o_rwkv = (yn + bonus_ref[...]) * g_ref[...]
    mixed = (jnp.dot(o_rwkv.astype(BF16), wo_ref[0:RWKV_W, :], preferred_element_type=F32)
             + jnp.dot(op_ref[...].astype(BF16), wo_ref[RWKV_W:RWKV_W + POOL_W, :], preferred_element_type=F32)
             + jnp.dot(om_ref[...].astype(BF16), wo_ref[RWKV_W + POOL_W:, :], preferred_element_type=F32))
    h_ref[...] = _layer_norm(alpha * x_ref[...] + mixed, l1g_ref[...], l1b_ref[...])


def _post(y, bonus, g, o_pool, o_mem, x, wo, lxw, lxb, l1g, l1b, bd, tt, alpha, h_all):
    n = y.shape[0]
    assert n % tt == 0
    tok = lambda w: pl.BlockSpec((tt, w), lambda i: (i, 0))
    row = lambda w: pl.BlockSpec((1, w), lambda i: (0, 0))
    full = lambda a: pl.BlockSpec(a.shape, lambda i: (0, 0))
    in_specs = [tok(RWKV_W), tok(RWKV_W), tok(RWKV_W), tok(POOL_W), tok(MEM_W), tok(D_MODEL), full(wo),
                row(RWKV_W), row(RWKV_W), row(D_MODEL), row(D_MODEL), full(bd)]
    args = [y, bonus, g, o_pool, o_mem, x, wo, lxw, lxb, l1g, l1b, bd]
    if isinstance(h_all, tuple):
        n_total, first = h_all
        aliases = {}
    else:
        n_total, first = h_all.shape[0], h_all.shape[0] - n
        in_specs.append(pl.BlockSpec(memory_space=pl.ANY))
        args.append(h_all)
        aliases = {len(args) - 1: 0}
    assert first % tt == 0
    return pl.pallas_call(
        functools.partial(_post_kernel, alpha=alpha),
        grid=(n // tt,),
        in_specs=in_specs,
        out_specs=pl.BlockSpec((tt, D_MODEL), lambda i: (i + first // tt, 0)),
        out_shape=jax.ShapeDtypeStruct((n_total, D_MODEL), F32),
        input_output_aliases=aliases,
        compiler_params=_cparams(1),
        name="post",
    )(*args)


def _router_kernel(h_ref, rwt_ref, bias_ref, tri_ref, below_ref, slot_o, gate_o, run_o, cnt_o, carry_ref):
    @pl.when(pl.program_id(0) == 0)
    def _():
        carry_ref[...] = jnp.zeros_like(carry_ref)

    neg = -jnp.inf
    logits = lax.dot_general(rwt_ref[...], h_ref[...], (((1,), (1,)), ((), ())),
                             precision=HIGHEST, preferred_element_type=F32)
    scores = _sigmoid(logits)
    sel = scores + bias_ref[...]
    tt = sel.shape[1]
    gio = lax.broadcasted_iota(I32, (GROUP_SIZE, tt), 0).astype(F32)
    blocks, gscore = [], []
    for g in range(N_GROUPS):
        blk = sel[g * GROUP_SIZE:(g + 1) * GROUP_SIZE, :]
        m1 = jnp.max(blk, axis=0, keepdims=True)
        first = jnp.min(jnp.where(blk == m1, gio, float(GROUP_SIZE)), axis=0, keepdims=True)
        m2 = jnp.max(jnp.where(gio == first, neg, blk), axis=0, keepdims=True)
        blocks.append(blk)
        gscore.append(m1 + m2)
    masked = []
    for g in range(N_GROUPS):
        beaten_by = jnp.zeros((1, tt), F32)
        for g2 in range(N_GROUPS):
            if g2 != g:
                wins = (gscore[g2] >= gscore[g]) if g2 < g else (gscore[g2] > gscore[g])
                beaten_by = beaten_by + jnp.where(wins, 1.0, 0.0)
        masked.append(jnp.where(beaten_by < TOPK_GROUPS, blocks[g], neg))
    msel = jnp.concatenate(masked, axis=0)
    eio = lax.broadcasted_iota(I32, msel.shape, 0).astype(F32)
    chosen = jnp.zeros(msel.shape, F32)
    idxs, scs = [], []
    for _ in range(TOP_K):
        m = jnp.max(msel, axis=0, keepdims=True)
        first = jnp.min(jnp.where(msel == m, eio, float(N_EXPERTS)), axis=0, keepdims=True)
        hit = eio == first
        scs.append(jnp.sum(jnp.where(hit, scores, 0.0), axis=0, keepdims=True))
        msel = jnp.where(hit, neg, msel)
        chosen = jnp.where(hit, 1.0, chosen)
        idxs.append(first)
    total = scs[0]
    for s in scs[1:]:
        total = total + s
    chosen_b = chosen.astype(BF16)
    earlier = jnp.dot(chosen_b, tri_ref[...], preferred_element_type=F32)
    smaller = jnp.dot(below_ref[...], chosen_b, preferred_element_type=F32)
    run_len = jnp.sum(chosen, axis=1, keepdims=True)
    run_off = jnp.sum(smaller, axis=1, keepdims=True)
    slot_of = run_off + earlier
    slots = [jnp.sum(jnp.where(eio == i, slot_of, 0.0), axis=0, keepdims=True) for i in idxs]
    slot_o[...] = jnp.concatenate(slots, axis=0).astype(I32)
    gate_o[...] = jnp.concatenate([s / total * ROUTED_SCALE for s in scs], axis=0)
    lane = lax.broadcasted_iota(I32, run_o.shape, 1)
    run_o[...] = jnp.where(lane == 0, run_len, jnp.where(lane == 1, run_off, jnp.where(lane == 2, carry_ref[...], 0.0)))
    carry_ref[...] = carry_ref[...] + run_len
    cnt_o[...] = jnp.broadcast_to(carry_ref[...], cnt_o.shape)


def _router(h, rwt, bias, tt):
    n = h.shape[0]
    assert n % tt == 0
    tri = (lax.broadcasted_iota(I32, (tt, tt), 0) < lax.broadcasted_iota(I32, (tt, tt), 1)).astype(BF16)
    below = (lax.broadcasted_iota(I32, (N_EXPERTS, N_EXPERTS), 1)
             < lax.broadcasted_iota(I32, (N_EXPERTS, N_EXPERTS), 0)).astype(BF16)
    tokT = pl.BlockSpec((TOP_K, tt), lambda i: (0, i))
    return pl.pallas_call(
        _router_kernel,
        grid=(n // tt,),
        in_specs=[pl.BlockSpec((tt, D_MODEL), lambda i: (i, 0)), pl.BlockSpec((N_EXPERTS, D_MODEL), lambda i: (0, 0)),
                  pl.BlockSpec((N_EXPERTS, 1), lambda i: (0, 0)), pl.BlockSpec((tt, tt), lambda i: (0, 0)),
                  pl.BlockSpec((N_EXPERTS, N_EXPERTS), lambda i: (0, 0))],
        out_specs=[tokT, tokT, pl.BlockSpec((N_EXPERTS, LANES), lambda i: (i, 0)),
                   pl.BlockSpec((N_EXPERTS, LANES), lambda i: (0, 0))],
        out_shape=[jax.ShapeDtypeStruct((TOP_K, n), I32), jax.ShapeDtypeStruct((TOP_K, n), F32),
                   jax.ShapeDtypeStruct((n // tt * N_EXPERTS, LANES), F32),
                   jax.ShapeDtypeStruct((N_EXPERTS, LANES), F32)],
        scratch_shapes=[pltpu.VMEM((N_EXPERTS, 1), F32)],
        compiler_params=_cparams(1),
        name="router",
    )(h, rwt, bias, tri, below)


ROW_SUB = D_MODEL // LANES


def _store_row_tiles(ref, x):
    rows = x.shape[0]
    for c in range(ROW_SUB):
        ref[pl.ds(c, rows, stride=ROW_SUB), :] = x[:, c * LANES:(c + 1) * LANES]


def _load_row_tiles(ref):
    rows = ref.shape[0] // ROW_SUB
    return jnp.concatenate([ref[pl.ds(c, rows, stride=ROW_SUB), :] for c in range(ROW_SUB)], axis=-1)


RUN_FIELDS = 3
RUN_UNROLL = 8


def _start_run_copies(runs_ref, tile, local_ref, sorted_ref, sem, to_sorted):
    def body(e, c):
        size = pl.multiple_of(runs_ref[tile * RUN_FIELDS + 2, e], ROW_SUB)
        local = local_ref.at[pl.ds(pl.multiple_of(runs_ref[tile * RUN_FIELDS, e], ROW_SUB), size), :]
        remote = sorted_ref.at[pl.ds(pl.multiple_of(runs_ref[tile * RUN_FIELDS + 1, e], ROW_SUB), size), :]
        (pltpu.make_async_copy(local, remote, sem) if to_sorted else pltpu.make_async_copy(remote, local, sem)).start()
        return c

    lax.fori_loop(0, N_EXPERTS, body, 0, unroll=RUN_UNROLL)


def _wait_run_copies(local_ref, sorted_ref, sem):
    pltpu.make_async_copy(sorted_ref.at[pl.ds(0, local_ref.shape[0]), :], local_ref, sem).wait()


def _dispatch_kernel(zpos_ref, runs_ref, slot_ref, h_ref, xs_hbm, zero_buf, rows_buf, sem_zero, sem_rows, *, td):
    i = pl.program_id(0)

    def zero_copy(e):
        start = pl.multiple_of(zpos_ref[e] * ROW_SUB, CHUNK * ROW_SUB)
        return pltpu.make_async_copy(zero_buf, xs_hbm.at[pl.ds(start, CHUNK * ROW_SUB), :], sem_zero)

    @pl.when(i == 0)
    def _():
        zero_buf[...] = jnp.zeros_like(zero_buf)

        def start(e, c):
            @pl.when(zpos_ref[e] >= 0)
            def _():
                zero_copy(e).start()
            return c

        def wait(e, c):
            @pl.when(zpos_ref[e] >= 0)
            def _():
                zero_copy(e).wait()
            return c

        lax.fori_loop(0, N_EXPERTS, start, 0)
        lax.fori_loop(0, N_EXPERTS, wait, 0)

    slots = slot_ref[...]
    slot_iota = lax.broadcasted_iota(I32, (TOP_K * td, td), 0)
    select = jnp.zeros((TOP_K * td, td), F32)
    for j in range(TOP_K):
        select = jnp.where(slot_iota == slots[j:j + 1, :], 1.0, select)
    local = jnp.dot(select.astype(BF16), h_ref[...].astype(BF16), preferred_element_type=F32)
    _store_row_tiles(rows_buf, local)
    _start_run_copies(runs_ref, i, rows_buf, xs_hbm, sem_rows, True)
    _wait_run_copies(rows_buf, xs_hbm, sem_rows)


def _dispatch(zpos, runs, slots, h, n_rows, td):
    n = h.shape[0]
    assert n % td == 0
    return pl.pallas_call(
        functools.partial(_dispatch_kernel, td=td),
        grid_spec=pltpu.PrefetchScalarGridSpec(
            num_scalar_prefetch=2,
            grid=(n // td,),
            in_specs=[pl.BlockSpec((TOP_K, td), lambda i, z, r: (0, i)),
                      pl.BlockSpec((td, D_MODEL), lambda i, z, r: (i, 0))],
            out_specs=pl.BlockSpec(memory_space=pl.ANY),
            scratch_shapes=[pltpu.VMEM((CHUNK * ROW_SUB, LANES), F32), pltpu.VMEM((TOP_K * td * ROW_SUB, LANES), F32),
                            pltpu.SemaphoreType.DMA, pltpu.SemaphoreType.DMA],
        ),
        out_shape=jax.ShapeDtypeStruct((n_rows * ROW_SUB, LANES), F32),
        compiler_params=_cparams(1),
        name="dispatch",
    )(zpos, runs, slots, h)


CHUNK_ROWS = CHUNK * ROW_SUB
RING = 4


def _ffn_kernel(first_ref, count_ref, nu_ref, xs_hbm, wg_ref, wu_ref, wd_ref, ys_hbm, xbuf, obuf, xsem, osem):
    e = pl.program_id(0)
    n_used = nu_ref[0]

    def x_copy(g, slot):
        src = xs_hbm.at[pl.ds(pl.multiple_of(g * CHUNK_ROWS, CHUNK_ROWS), CHUNK_ROWS), :]
        return pltpu.make_async_copy(src, xbuf.at[slot], xsem.at[slot])

    def o_copy(g, slot):
        dst = ys_hbm.at[pl.ds(pl.multiple_of(g * CHUNK_ROWS, CHUNK_ROWS), CHUNK_ROWS), :]
        return pltpu.make_async_copy(obuf.at[slot], dst, osem.at[slot])

    @pl.when(e == 0)
    def _():
        for g in range(RING - 1):
            @pl.when(g < n_used)
            def _():
                x_copy(g, g).start()

    wg = wg_ref[0].astype(BF16)
    wu = wu_ref[0].astype(BF16)
    wd = wd_ref[0].astype(BF16)

    def chunk(c, carry):
        g = first_ref[e] + c
        slot = g & (RING - 1)
        x_copy(g, slot).wait()

        @pl.when(g + RING - 1 < n_used)
        def _():
            x_copy(g + RING - 1, (g + RING - 1) & (RING - 1)).start()

        @pl.when(g >= RING)
        def _():
            o_copy(g - RING, slot).wait()

        x = _load_row_tiles(xbuf.at[slot]).astype(BF16)
        gate = jnp.dot(x, wg, preferred_element_type=F32)
        up = jnp.dot(x, wu, preferred_element_type=F32)
        act = (gate * _sigmoid(gate)) * up
        _store_row_tiles(obuf.at[slot], jnp.dot(act.astype(BF16), wd, preferred_element_type=F32))
        o_copy(g, slot).start()
        return carry

    lax.fori_loop(0, count_ref[e], chunk, 0)

    @pl.when(e == N_EXPERTS - 1)
    def _():
        for back in range(RING, 0, -1):
            @pl.when(n_used >= back)
            def _():
                o_copy(n_used - back, (n_used - back) & (RING - 1)).wait()


def _ffn(first_chunk, chunk_count, n_used, xs, wg, wu, wd):
    weights = lambda a, b: pl.BlockSpec((1, a, b), lambda e, f, c, nu: (e, 0, 0))
    return pl.pallas_call(
        _ffn_kernel,
        grid_spec=pltpu.PrefetchScalarGridSpec(
            num_scalar_prefetch=3,
            grid=(N_EXPERTS,),
            in_specs=[pl.BlockSpec(memory_space=pl.ANY), weights(D_MODEL, EXPERT_FF), weights(D_MODEL, EXPERT_FF),
                      weights(EXPERT_FF, D_MODEL)],
            out_specs=pl.BlockSpec(memory_space=pl.ANY),
            scratch_shapes=[pltpu.VMEM((RING, CHUNK_ROWS, LANES), F32), pltpu.VMEM((RING, CHUNK_ROWS, LANES), F32),
                            pltpu.SemaphoreType.DMA((RING,)), pltpu.SemaphoreType.DMA((RING,))],
        ),
        out_shape=jax.ShapeDtypeStruct(xs.shape, F32),
        compiler_params=_cparams(1),
        name="expert_ffn",
    )(first_chunk, chunk_count, n_used, xs, wg, wu, wd)


def _combine_kernel(runs_ref, slot_ref, gate_ref, h_ref, ys_hbm, sg_ref, su_ref, sd_ref, l2g_ref, l2b_ref,
                    o1_ref, o2_ref, rows_buf, sem_rows, *, tc, alpha, first_tiles):
    i = pl.program_id(0)
    slot = i & 1
    buf = rows_buf.at[slot]

    @pl.when(i == 0)
    def _():
        _start_run_copies(runs_ref, i, buf, ys_hbm, sem_rows.at[slot], False)

    @pl.when(i + 1 < pl.num_programs(0))
    def _():
        _start_run_copies(runs_ref, i + 1, rows_buf.at[1 - slot], ys_hbm, sem_rows.at[1 - slot], False)

    h = h_ref[...]
    hb = h.astype(BF16)
    sgate = jnp.dot(hb, sg_ref[...], preferred_element_type=F32)
    sup = jnp.dot(hb, su_ref[...], preferred_element_type=F32)
    shared = jnp.dot(((sgate * _sigmoid(sgate)) * sup).astype(BF16), sd_ref[...], preferred_element_type=F32)
    slots = slot_ref[...]
    gate = gate_ref[...]
    slot_iota = lax.broadcasted_iota(I32, (tc, TOP_K * tc), 1)
    weights = jnp.zeros((tc, TOP_K * tc), F32)
    for j in range(TOP_K):
        weights = jnp.where(slot_iota == slots[:, j:j + 1], gate[:, j:j + 1], weights)
    w_hi = weights.astype(BF16)
    w_lo = (weights - w_hi.astype(F32)).astype(BF16)
    _wait_run_copies(buf, ys_hbm, sem_rows.at[slot])
    local = _load_row_tiles(buf).astype(BF16)
    routed = (jnp.dot(w_hi, local, preferred_element_type=F32) + jnp.dot(w_lo, local, preferred_element_type=F32))
    y = _layer_norm(alpha * h + (routed + shared), l2g_ref[...], l2b_ref[...])

    @pl.when(i < first_tiles)
    def _():
        o1_ref[...] = y

    @pl.when(i >= first_tiles)
    def _():
        o2_ref[...] = y


def _combine(runs, slots, gate, h, ys, sg, su, sd, l2g, l2b, tc, alpha, n_first):
    n = h.shape[0]
    assert n % tc == 0 and n_first % tc == 0 and 0 < n_first < n
    first_tiles = n_first // tc
    tok = lambda w: pl.BlockSpec((tc, w), lambda i, r: (i, 0))
    full = lambda a: pl.BlockSpec(a.shape, lambda i, r: (0, 0))
    return pl.pallas_call(
        functools.partial(_combine_kernel, tc=tc, alpha=alpha, first_tiles=first_tiles),
        grid_spec=pltpu.PrefetchScalarGridSpec(
            num_scalar_prefetch=1,
            grid=(n // tc,),
            in_specs=[tok(TOP_K), tok(TOP_K), tok(D_MODEL), pl.BlockSpec(memory_space=pl.ANY),
                      full(sg), full(su), full(sd), full(l2g), full(l2b)],
            out_specs=[pl.BlockSpec((tc, D_MODEL), lambda i, r: (jnp.minimum(i, first_tiles - 1), 0)),
                       pl.BlockSpec((tc, D_MODEL), lambda i, r: (jnp.maximum(i - first_tiles, 0), 0))],
            scratch_shapes=[pltpu.VMEM((2, TOP_K * tc * ROW_SUB, LANES), F32), pltpu.SemaphoreType.DMA((2,))],
        ),
        out_shape=[jax.ShapeDtypeStruct((n_first, D_MODEL), F32), jax.ShapeDtypeStruct((n - n_first, D_MODEL), F32)],
        compiler_params=_cparams(1),
        name="combine",
    )(runs, slots, gate, h, ys, sg, su, sd, l2g, l2b)


def _pad_cols(a, width):
    return jnp.pad(a, ((0, 0), (0, width - a.shape[1])))


def _pack_shift_cols(a):
    c = 3 * RWKV_W
    return jnp.concatenate([a[:, :c], _pad_cols(a[:, c:c + DECAY_LORA], LORA_PAD),
                            _pad_cols(a[:, c + DECAY_LORA:c + DECAY_LORA + AAA_LORA], LORA_PAD),
                            _pad_cols(a[:, c + DECAY_LORA + AAA_LORA:], GATE_PAD)], axis=1)


def _scan_to_lanes(a, b, t, split):
    a = a.reshape(b, t, SCAN_INPUTS, N_HEADS, HEAD).transpose(1, 2, 4, 0, 3)
    return a.reshape(t, SCAN_INPUTS, HEAD // split, split * b * N_HEADS)


def _values_from_lanes(y, b, t):
    return y.reshape(t, HEAD, b, N_HEADS).transpose(2, 0, 3, 1).reshape(b * t, RWKV_W)


def _state_to_lanes(s, b, split):
    s = s.reshape(b, N_HEADS, HEAD // split, split, HEAD).transpose(4, 2, 3, 0, 1)
    return s.reshape(HEAD, HEAD // split, split * b * N_HEADS)


def _state_from_lanes(s, b, split):
    s = s.reshape(HEAD, HEAD // split, split, b, N_HEADS).transpose(3, 4, 1, 2, 0)
    return s.reshape(b, N_HEADS, HEAD, HEAD)


def _mixer(z, zprev, x2, b, t, wkv0, pool_prev, mk, mv, pos0, wts, tiles, h_all):
    n = b * t
    tt, tb, seqs, att_tt = tiles
    z3 = z[:n].reshape(b, t, Z_W)
    scan, g, bonus = _rwkv_prep(z, zprev, b, t, tt, wts["prep"])

    split = 1 if (b * N_HEADS) % LANES == 0 else LANES // (b * N_HEADS)
    key_order = jnp.concatenate([jnp.arange(i, HEAD, split) for i in range(split)])
    s_in = wkv0[..., key_order] if split > 1 else wkv0
    y_l, s_l = _wkv(_scan_to_lanes(scan, b, t, split), _state_to_lanes(s_in, b, split), tb)
    y = _values_from_lanes(y_l, b, t)
    wkv_new = _state_from_lanes(s_l, b, split)
    if split > 1:
        wkv_new = wkv_new[..., jnp.argsort(key_order)]

    prev = jnp.concatenate([jnp.zeros((b, POOL_PAD - POOL_BUF, POOL_W), F32), pool_prev], axis=1)
    o_pool = _pool(prev, z3, wts["pool_w"], wts["pool_scale"], seqs, pos0).reshape(n, POOL_W)
    tail = z3[:, max(t - POOL_BUF, 0):, ZS_W:ZS_W + POOL_W]
    pool_new = jnp.concatenate([pool_prev, tail], axis=1)[:, -POOL_BUF:]

    o_mem = _memattn(z3, mk, mv, att_tt, seqs).reshape(n, MEM_W)
    h = _post(y, bonus, g, o_pool, o_mem, x2, *wts["post"], tt, wts["alpha"], h_all)
    return h, wkv_new, pool_new


def kernel(x_prompt, x_sample, mem_prompt, state_wkv, state_shift, state_pool, cache_mem_k, cache_mem_v, w_in, mu_shift, w0, w_up_decay, a0, w_up_aaa, w_up_gate, k_k, k_a, r_k, ln_x_w, ln_x_b, pool_w, pool_scale, mem_wk, mem_wv, w_out, ln1_g, ln1_b, router_w, router_b, exp_gate, exp_up, exp_down, sh_gate, sh_up, sh_down, ln2_g, ln2_b):
    depth = w_in.shape[0]
    assert depth == 1
    l = 0
    alpha = (2.0 * depth) ** 0.25
    bp, tp, d = x_prompt.shape
    bs, ts, _ = x_sample.shape
    n_p, n_s = bp * tp, bs * ts
    n = n_p + n_s

    w_in_p = jnp.concatenate([_pack_shift_cols(w_in[l][:, :1824]), w_in[l][:, 1824:]], axis=1).astype(BF16)
    row = lambda a: a.reshape(1, -1)
    pad_rows = lambda a, rows: jnp.pad(a, ((0, rows - a.shape[0]), (0, 0)))
    bd = jnp.kron(jnp.eye(N_HEADS, dtype=F32), jnp.ones((HEAD, HEAD), F32)).astype(BF16)
    pw = pool_w[l]
    pw_bd = jnp.zeros((POOL_W, POOL_W), F32)
    for gi in range(4):
        pw_bd = pw_bd.at[gi * 64:(gi + 1) * 64, gi * 64:(gi + 1) * 64].set(pw[gi])
    wts = {
        "prep": (_pack_shift_cols(row(mu_shift[l])), row(w0[l]), row(a0[l]), pad_rows(w_up_decay[l], LORA_PAD),
                 pad_rows(w_up_aaa[l], LORA_PAD), pad_rows(w_up_gate[l], GATE_PAD), row(k_k[l]), row(k_a[l]),
                 row(r_k[l]), bd),
        "pool_w": pw_bd.astype(BF16), "pool_scale": row(pool_scale[l]),
        "post": (w_out[l].astype(BF16), row(ln_x_w[l]), row(ln_x_b[l]), row(ln1_g[l]), row(ln1_b[l]), bd),
        "alpha": alpha,
    }

    xp2 = x_prompt.reshape(n_p, d)
    xs2 = x_sample.reshape(n_s, d)
    z_p = _matmul(xp2, w_in_p, _tile(n_p, 512))
    z_s = _matmul(jnp.concatenate([xs2, state_shift[l]], axis=0), w_in_p, _tile(n_s + bs, 512))
    mkv = _matmul(mem_prompt.reshape(bp * MEM_TOKENS, d),
                  jnp.concatenate([mem_wk[l], mem_wv[l]], axis=1).astype(BF16), _tile(bp * MEM_TOKENS, 512))
    mk_p = mkv[:, :MEM_W].reshape(bp, MEM_TOKENS, MEM_W)
    mv_p = mkv[:, MEM_W:].reshape(bp, MEM_TOKENS, MEM_W)

    h_p, wkv_p, pool_p = _mixer(z_p, jnp.zeros((bp, ZS_W), F32), xp2, bp, tp,
                                jnp.zeros((bp, N_HEADS, HEAD, HEAD), F32), jnp.zeros((bp, POOL_BUF, POOL_W), F32),
                                mk_p, mv_p, 0, wts,
                                (_tile(n_p, 256), _tile(tp, 32, 1), 1, _tile(tp, 256)), (n, 0))
    h, wkv_s, pool_s = _mixer(z_s, z_s[n_s:, :ZS_W], xs2, bs, ts, state_wkv[l], state_pool[l],
                              cache_mem_k[l].reshape(bs, MEM_TOKENS, MEM_W),
                              cache_mem_v[l].reshape(bs, MEM_TOKENS, MEM_W), PAST_LEN, wts,
                              (_tile(n_s, 256), _tile(ts, 32, 1), _tile(bs, 16, 1), _tile(ts, 256)), h_p)

    td = _tile(n, 256, LANES)
    slots, gate, run_tab, cnt = _router(h, router_w[l].T, router_b[l].reshape(N_EXPERTS, 1), td)
    counts = cnt[:, 0].astype(I32)
    padded = (counts + CHUNK - 1) // CHUNK * CHUNK
    pad_end = jnp.cumsum(padded)
    pad_start = pad_end - padded
    n_chunks = (n * TOP_K + N_EXPERTS * (CHUNK - 1) + CHUNK - 1) // CHUNK
    run_tab = run_tab.reshape(n // td, N_EXPERTS, LANES)[:, :, :RUN_FIELDS].astype(I32)
    runs = jnp.stack([run_tab[:, :, 1], pad_start[None, :] + run_tab[:, :, 2], run_tab[:, :, 0]], axis=1)
    runs = runs.reshape(n // td * RUN_FIELDS, N_EXPERTS) * ROW_SUB
    zpos = jnp.where(padded > 0, pad_end - CHUNK, -1).astype(I32)
    n_used = (pad_end[-1:] // CHUNK).astype(I32)
    xs_sorted = _dispatch(zpos, runs, slots, h, n_chunks * CHUNK, td)
    ys_sorted = _ffn((pad_start // CHUNK).astype(I32), (padded // CHUNK).astype(I32), n_used, xs_sorted,
                     exp_gate[l], exp_up[l], exp_down[l])
    y_p, y_s = _combine(runs, slots.T, gate.T, h, ys_sorted, sh_gate[l].astype(BF16), sh_up[l].astype(BF16),
                        sh_down[l].astype(BF16), row(ln2_g[l]), row(ln2_b[l]), td, alpha, n_p)

    return (y_p.reshape(bp, tp, d), y_s.reshape(bs, ts, d),
            wkv_p[None], x_prompt[:, -1][None], pool_p[None],
            mk_p.reshape(bp, MEM_TOKENS, MEM_HEADS, HEAD)[None], mv_p.reshape(bp, MEM_TOKENS, MEM_HEADS, HEAD)[None],
            wkv_s[None], x_sample[:, -1][None], pool_s[None])
```

```python
import functools

import jax
import jax.numpy as jnp
from jax import lax
from jax.experimental import pallas as pl
from jax.experimental.pallas import tpu as pltpu

F32 = jnp.float32
BF16 = jnp.bfloat16
I32 = jnp.int32

D_MODEL = 1024
HEAD = 64
N_HEADS = 8
RWKV_W = 512
POOL_W = 256
MEM_W = 256
MEM_HEADS = 4
MEM_TOKENS = 256
POOL_BUF = 15
POOL_PAD = 16
DECAY_LORA = 64
AAA_LORA = 64
GATE_LORA = 160
LORA_PAD = 128
GATE_PAD = 256
ZS_W = 3 * RWKV_W + 2 * LORA_PAD + GATE_PAD
Z_W = ZS_W + POOL_W + MEM_W
N_EXPERTS = 256
TOP_K = 8
N_GROUPS = 8
GROUP_SIZE = N_EXPERTS // N_GROUPS
TOPK_GROUPS = 4
EXPERT_FF = 256
ROUTED_SCALE = 2.5
CHUNK = 256
PAST_LEN = 16384
LN_EPS = 1e-5
GN_EPS = 64e-5
LANES = 128
SUBLANES = 8
SCAN_INPUTS = 6
VMEM_LIMIT = 56 * 1024 * 1024


def _cparams(n_axes):
    return pltpu.CompilerParams(dimension_semantics=("arbitrary",) * n_axes, vmem_limit_bytes=VMEM_LIMIT)


def _tile(n, preferred, multiple=8):
    best = None
    for c in range(multiple, min(n, preferred) + 1, multiple):
        if n % c == 0:
            best = c
    assert best is not None, (n, preferred, multiple)
    return best


def _sigmoid(x):
    return 1.0 / (1.0 + jnp.exp(-x))


def _matmul_kernel(x_ref, w_ref, o_ref):
    o_ref[...] = jnp.dot(x_ref[...].astype(BF16), w_ref[...], preferred_element_type=F32)


def _matmul(x, w, tm):
    m, k = x.shape
    n = w.shape[1]
    assert m % tm == 0
    return pl.pallas_call(
        _matmul_kernel,
        grid=(m // tm,),
        in_specs=[pl.BlockSpec((tm, k), lambda i: (i, 0)), pl.BlockSpec((k, n), lambda i: (0, 0))],
        out_specs=pl.BlockSpec((tm, n), lambda i: (i, 0)),
        out_shape=jax.ShapeDtypeStruct((m, n), F32),
        compiler_params=_cparams(1),
        name="matmul",
    )(x, w)


def _split(x):
    hi = x.astype(BF16)
    return hi, (x - hi.astype(F32)).astype(BF16)


def _dot_f32(a, b, contract=((1,), (0,))):
    a_hi, a_lo = _split(a)
    b_hi, b_lo = _split(b)
    dot = lambda x, y: lax.dot_general(x, y, (contract, ((), ())), preferred_element_type=F32)
    return dot(a_hi, b_hi) + (dot(a_hi, b_lo) + dot(a_lo, b_hi))


def _segsum(x, bd):
    hi = x.astype(BF16)
    lo = (x - hi.astype(F32)).astype(BF16)
    return jnp.dot(hi, bd, preferred_element_type=F32) + jnp.dot(lo, bd, preferred_element_type=F32)


def _prep_kernel(zs_ref, zp_ref, mu_ref, w0_ref, a0_ref, wd_ref, wa_ref, wg_ref, kk_ref, ka_ref, rk_ref, bd_ref,
                 scan_o, g_o, bonus_o, carry_ref, *, tiles_per_seq, seq_len):
    zs = zs_ref[...]
    rolled = pltpu.roll(zs, 1, 0)
    row = lax.broadcasted_iota(I32, zs.shape, 0)
    if tiles_per_seq:
        i = pl.program_id(0)

        @pl.when(i == 0)
        def _():
            carry_ref[...] = jnp.zeros_like(carry_ref)

        prev = jnp.where(i % tiles_per_seq == 0, zp_ref[0], carry_ref[...])
        shifted = jnp.where(row == 0, prev, rolled)
        carry_ref[...] = zs[zs.shape[0] - 1:, :]
    else:
        shifted = jnp.where((row & (seq_len - 1)) == 0, zp_ref[...], rolled)
    zm = zs + mu_ref[...] * (shifted - zs)
    r = zm[:, 0:RWKV_W]
    k = zm[:, RWKV_W:2 * RWKV_W]
    v = zm[:, 2 * RWKV_W:3 * RWKV_W]
    c0 = 3 * RWKV_W
    xw = zm[:, c0:c0 + LORA_PAD]
    xa = zm[:, c0 + LORA_PAD:c0 + 2 * LORA_PAD]
    xg = zm[:, c0 + 2 * LORA_PAD:c0 + 2 * LORA_PAD + GATE_PAD]
    bd = bd_ref[...]
    u = -(w0_ref[...] + _dot_f32(jnp.tanh(xw), wd_ref[...]))
    softplus = jnp.maximum(u, 0.0) + jnp.log(1.0 + jnp.exp(-jnp.abs(u)))
    decay = jnp.exp(-jnp.exp(-softplus - 0.5))
    a = _sigmoid(a0_ref[...] + _dot_f32(xa, wa_ref[...]))
    g = _dot_f32(_sigmoid(xg), wg_ref[...])
    kk = k * kk_ref[...]
    kk = kk / jnp.maximum(jnp.sqrt(_segsum(kk * kk, bd)), 1e-12)
    kp = k * (1.0 + (a - 1.0) * ka_ref[...])
    for slot, value in enumerate((r, decay, kp, kk, -(kk * a), v)):
        scan_o[:, slot * RWKV_W:(slot + 1) * RWKV_W] = value
    g_o[...] = g
    bonus_o[...] = _segsum(r * kp * rk_ref[...], bd) * v


def _rwkv_prep(z, zprev, b, t, tt, params):
    n_tok = b * t
    assert n_tok % tt == 0
    row = lambda w: pl.BlockSpec((1, w), lambda i: (0, 0))
    full = lambda a: pl.BlockSpec(a.shape, lambda i: (0, 0))
    tok = pl.BlockSpec((tt, RWKV_W), lambda i: (i, 0))
    mu, w0, a0, wd, wa, wg, k_k, k_a, r_k, bd = params
    if t % tt == 0:
        tiles_per_seq = t // tt
        zp = zprev.reshape(b, 1, ZS_W)
        zp_spec = pl.BlockSpec((1, 1, ZS_W), lambda i: (i // tiles_per_seq, 0, 0))
    else:
        assert tt % t == 0 and t & (t - 1) == 0
        tiles_per_seq = 0
        zp = jnp.repeat(zprev, t, axis=0)
        zp_spec = pl.BlockSpec((tt, ZS_W), lambda i: (i, 0))
    return pl.pallas_call(
        functools.partial(_prep_kernel, tiles_per_seq=tiles_per_seq, seq_len=t),
        grid=(n_tok // tt,),
        in_specs=[pl.BlockSpec((tt, ZS_W), lambda i: (i, 0)), zp_spec,
                  row(ZS_W), row(RWKV_W), row(RWKV_W), full(wd), full(wa), full(wg),
                  row(RWKV_W), row(RWKV_W), row(RWKV_W), full(bd)],
        out_specs=[pl.BlockSpec((tt, SCAN_INPUTS * RWKV_W), lambda i: (i, 0)), tok, tok],
        out_shape=[jax.ShapeDtypeStruct((n_tok, SCAN_INPUTS * RWKV_W), F32),
                   jax.ShapeDtypeStruct((n_tok, RWKV_W), F32), jax.ShapeDtypeStruct((n_tok, RWKV_W), F32)],
        scratch_shapes=[pltpu.VMEM((1, ZS_W), F32)],
        compiler_params=_cparams(1),
        name="rwkv_prep",
    )(z, zp, mu, w0, a0, wd, wa, wg, k_k, k_a, r_k, bd)


def _wkv_kernel(r_ref, w_ref, k_ref, kk_ref, nb_ref, v_ref, s0_ref, y_ref, s_ref, *scratch, tb, dup):
    slabs = s_ref.shape[1]
    key_unroll = 16

    @pl.when(pl.program_id(1) == 0)
    def _():
        s_ref[...] = s0_ref[...]

    if dup:
        for src, dst in zip((r_ref, w_ref, k_ref, kk_ref, nb_ref), scratch):
            x = src[...].reshape(tb * (HEAD // 2), LANES)
            swapped = pltpu.roll(x, LANES // 2, 1)
            low = lax.broadcasted_iota(I32, x.shape, 1) < LANES // 2
            dst[:, 0:HEAD // 2, :] = jnp.where(low, x, swapped).reshape(tb, HEAD // 2, LANES)
            dst[:, HEAD // 2:, :] = jnp.where(low, swapped, x).reshape(tb, HEAD // 2, LANES)
        r_ref, w_ref, k_ref, kk_ref, nb_ref = scratch

    def rows(ref, t, k):
        return jnp.broadcast_to(ref[t, pl.ds(k, 1), :], (SUBLANES, LANES))

    def step(t, sa, with_next):
        vt = [v_ref[t, i * SUBLANES:(i + 1) * SUBLANES, :] for i in range(slabs)]
        zero = jnp.zeros((SUBLANES, LANES), F32)

        def key(k, acc):
            y_acc, next_acc = list(acc[0]), list(acc[1])
            wb, nbb, kb, rb = rows(w_ref, t, k), rows(nb_ref, t, k), rows(k_ref, t, k), rows(r_ref, t, k)
            kkb = rows(kk_ref, t + 1, k) if with_next else None
            for i in range(slabs):
                sn = s_ref[k, i] * wb + sa[i] * nbb + vt[i] * kb
                s_ref[k, i] = sn
                y_acc[i] = y_acc[i] + sn * rb
                if with_next:
                    next_acc[i] = next_acc[i] + sn * kkb
            return tuple(y_acc), tuple(next_acc)

        y_acc, next_acc = lax.fori_loop(0, HEAD, key, ((zero,) * slabs, (zero,) * slabs), unroll=key_unroll)
        for i, y in enumerate(y_acc):
            y_ref[t, i * SUBLANES:(i + 1) * SUBLANES, :] = y
        return next_acc

    def first(k, acc):
        kkb = rows(kk_ref, 0, k)
        return tuple(a + s_ref[k, i] * kkb for i, a in enumerate(acc))

    sa0 = lax.fori_loop(0, HEAD, first, (jnp.zeros((SUBLANES, LANES), F32),) * slabs, unroll=key_unroll)
    sa_last = lax.fori_loop(0, tb - 1, lambda t, sa: step(t, sa, True), sa0)
    step(tb - 1, sa_last, False)


def _wkv(scan, s0, tb):
    t, _, rows, l = scan.shape
    dup = rows != HEAD
    assert t % tb == 0 and l % LANES == 0 and rows % SUBLANES == 0 and (not dup or (l == LANES and 2 * rows == HEAD))
    vec = lambda a: pl.BlockSpec((tb, None, rows, LANES), lambda g, i: (i, a, 0, g))
    slabs = rows // SUBLANES
    st = pl.BlockSpec((HEAD, slabs, SUBLANES, LANES), lambda g, i: (0, 0, 0, g))
    s0 = s0.reshape(HEAD, slabs, SUBLANES, l)
    y, s_new = pl.pallas_call(
        functools.partial(_wkv_kernel, tb=tb, dup=dup),
        grid=(l // LANES, t // tb),
        in_specs=[vec(a) for a in range(SCAN_INPUTS)] + [st],
        out_specs=[pl.BlockSpec((tb, rows, LANES), lambda g, i: (i, 0, g)), st],
        out_shape=[jax.ShapeDtypeStruct((t, rows, l), F32), jax.ShapeDtypeStruct(s0.shape, F32)],
        scratch_shapes=[pltpu.VMEM((tb, HEAD, LANES), F32)] * 5 if dup else [],
        compiler_params=_cparams(2),
        name="wkv",
    )(*([scan] * SCAN_INPUTS), s0)
    return y, s_new.reshape(HEAD, rows, l)


def _pool_kernel(prev_ref, zp_ref, pw_ref, ps_ref, o_ref, *, t, pos0):
    bb = zp_ref.shape[0]
    lane = lax.broadcasted_iota(I32, (t, POOL_W), 1)
    window = jnp.where(lane < 64, 2, jnp.where(lane < 128, 4, jnp.where(lane < 192, 8, 16)))
    pos = lax.broadcasted_iota(I32, (t, POOL_W), 0) + (pos0 + 1)
    cnt = jnp.minimum(pos, window).astype(F32)
    diffs = []
    for i in range(bb):
        f = jnp.concatenate([prev_ref[i], zp_ref[i]], axis=0)
        s2 = f + pltpu.roll(f, 1, 0)
        s4 = s2 + pltpu.roll(s2, 2, 0)
        s8 = s4 + pltpu.roll(s4, 4, 0)
        s16 = s8 + pltpu.roll(s8, 8, 0)
        wsum = jnp.where(lane < 64, s2[POOL_PAD:], jnp.where(lane < 128, s4[POOL_PAD:],
                                                            jnp.where(lane < 192, s8[POOL_PAD:], s16[POOL_PAD:])))
        diffs.append(wsum / cnt - f[POOL_PAD:])
    diff = diffs[0] if bb == 1 else jnp.concatenate(diffs, axis=0)
    out = jnp.dot(diff.astype(BF16), pw_ref[...], preferred_element_type=F32) * ps_ref[...]
    for i in range(bb):
        o_ref[i] = out[i * t:(i + 1) * t]


def _pool(prev, z3, pw_bd, pscale, bb, pos0):
    b, t, _ = z3.shape
    assert b % bb == 0
    return pl.pallas_call(
        functools.partial(_pool_kernel, t=t, pos0=pos0),
        grid=(b // bb,),
        in_specs=[pl.BlockSpec((bb, POOL_PAD, POOL_W), lambda i: (i, 0, 0)),
                  pl.BlockSpec((bb, t, POOL_W), lambda i: (i, 0, ZS_W // POOL_W)),
                  pl.BlockSpec((POOL_W, POOL_W), lambda i: (0, 0)), pl.BlockSpec((1, POOL_W), lambda i: (0, 0))],
        out_specs=pl.BlockSpec((bb, t, POOL_W), lambda i: (i, 0, 0)),
        out_shape=jax.ShapeDtypeStruct((b, t, POOL_W), F32),
        compiler_params=_cparams(1),
        name="pool",
    )(prev, z3, pw_bd, pscale)


def _memattn_kernel(q_ref, k_ref, v_ref, o_ref):
    for i in range(q_ref.shape[0]):
        q = q_ref[i].astype(BF16)
        kf = k_ref[i]
        vf = v_ref[i]
        head_of_lane = lax.broadcasted_iota(I32, kf.shape, 1) // (MEM_W // MEM_HEADS)
        out = None
        for h in range(MEM_HEADS):
            kh = jnp.where(head_of_lane == h, kf, 0.0).astype(BF16)
            vh = jnp.where(head_of_lane == h, vf, 0.0).astype(BF16)
            s = lax.dot_general(q, kh, (((1,), (1,)), ((), ())), preferred_element_type=F32) * (64 ** -0.5)
            e = jnp.exp(s - jnp.max(s, axis=-1, keepdims=True))
            p = e / jnp.sum(e, axis=-1, keepdims=True)
            o = jnp.dot(p.astype(BF16), vh, preferred_element_type=F32)
            out = o if out is None else out + o
        o_ref[i] = out


def _memattn(z3, mk, mv, tt, bb):
    b, t, _ = z3.shape
    assert t % tt == 0 and b % bb == 0
    qcol = (Z_W - MEM_W) // MEM_W
    kv = pl.BlockSpec((bb, MEM_TOKENS, MEM_W), lambda i, j: (i, 0, 0))
    return pl.pallas_call(
        _memattn_kernel,
        grid=(b // bb, t // tt),
        in_specs=[pl.BlockSpec((bb, tt, MEM_W), lambda i, j: (i, j, qcol)), kv, kv],
        out_specs=pl.BlockSpec((bb, tt, MEM_W), lambda i, j: (i, j, 0)),
        out_shape=jax.ShapeDtypeStruct((b, t, MEM_W), F32),
        compiler_params=_cparams(2),
        name="mem_attn",
    )(z3, mk, mv)


def _layer_norm(x, g, b):
    mu = jnp.mean(x, axis=-1, keepdims=True)
    xc = x - mu
    var = jnp.mean(xc * xc, axis=-1, keepdims=True)
    return xc * lax.rsqrt(var + LN_EPS) * g + b


def _post_kernel(y_ref, bonus_ref, g_ref, op_ref, om_ref, x_ref, wo_ref, lxw_ref, lxb_ref, l1g_ref, l1b_ref, bd_ref,
                 *rest, alpha):
    h_ref = rest[-1]
    y = y_ref[...]
    bd = bd_ref[...]
    mu = _segsum(y, bd) * (1.0 / HEAD)
    yc = y - mu
    var = _segsum(yc * yc, bd) * (1.0 / HEAD)
    yn = yc * lax.rsqrt(var + GN_EPS) * lxw_ref[...] + lxb_ref[...]
    o_rwkv = (yn + bonus_ref[...]) * g_ref[...]
    mixed = (jnp.dot(o_rwkv.astype(BF16), wo_ref[0:RWKV_W, :], preferred_element_type=F32)
             + jnp.dot(op_ref[...].astype(BF16), wo_ref[RWKV_W:RWKV_W + POOL_W, :], preferred_element_type=F32)
             + jnp.dot(om_ref[...].astype(BF16), wo_ref[RWKV_W + POOL_W:, :], preferred_element_type=F32))
    h_ref[...] = _layer_norm(alpha * x_ref[...] + mixed, l1g_ref[...], l1b_ref[...])


def _post(y, bonus, g, o_pool, o_mem, x, wo, lxw, lxb, l1g, l1b, bd, tt, alpha, h_all):
    n = y.shape[0]
    assert n % tt == 0
    tok = lambda w: pl.BlockSpec((tt, w), lambda i: (i, 0))
    row = lambda w: pl.BlockSpec((1, w), lambda i: (0, 0))
    full = lambda a: pl.BlockSpec(a.shape, lambda i: (0, 0))
    in_specs = [tok(RWKV_W), tok(RWKV_W), tok(RWKV_W), tok(POOL_W), tok(MEM_W), tok(D_MODEL), full(wo),
                row(RWKV_W), row(RWKV_W), row(D_MODEL), row(D_MODEL), full(bd)]
    args = [y, bonus, g, o_pool, o_mem, x, wo, lxw, lxb, l1g, l1b, bd]
    if isinstance(h_all, tuple):
        n_total, first = h_all
        aliases = {}
    else:
        n_total, first = h_all.shape[0], h_all.shape[0] - n
        in_specs.append(pl.BlockSpec(memory_space=pl.ANY))
        args.append(h_all)
        aliases = {len(args) - 1: 0}
    assert first % tt == 0
    return pl.pallas_call(
        functools.partial(_post_kernel, alpha=alpha),
        grid=(n // tt,),
        in_specs=in_specs,
        out_specs=pl.BlockSpec((tt, D_MODEL), lambda i: (i + first // tt, 0)),
        out_shape=jax.ShapeDtypeStruct((n_total, D_MODEL), F32),
        input_output_aliases=aliases,
        compiler_params=_cparams(1),
        name="post",
    )(*args)


def _router_kernel(h_ref, rwt_ref, bias_ref, tri_ref, below_ref, slot_o, gate_o, run_o, cnt_o, carry_ref):
    @pl.when(pl.program_id(0) == 0)
    def _():
        carry_ref[...] = jnp.zeros_like(carry_ref)

    neg = -jnp.inf
    logits = _dot_f32(rwt_ref[...], h_ref[...], ((1,), (1,)))
    scores = _sigmoid(logits)
    sel = scores + bias_ref[...]
    tt = sel.shape[1]
    gio = lax.broadcasted_iota(I32, (GROUP_SIZE, tt), 0).astype(F32)
    blocks, gscore = [], []
    for g in range(N_GROUPS):
        blk = sel[g * GROUP_SIZE:(g + 1) * GROUP_SIZE, :]
        m1 = jnp.max(blk, axis=0, keepdims=True)
        first = jnp.min(jnp.where(blk == m1, gio, float(GROUP_SIZE)), axis=0, keepdims=True)
        m2 = jnp.max(jnp.where(gio == first, neg, blk), axis=0, keepdims=True)
        blocks.append(blk)
        gscore.append(m1 + m2)
    masked = []
    for g in range(N_GROUPS):
        beaten_by = jnp.zeros((1, tt), F32)
        for g2 in range(N_GROUPS):
            if g2 != g:
                wins = (gscore[g2] >= gscore[g]) if g2 < g else (gscore[g2] > gscore[g])
                beaten_by = beaten_by + jnp.where(wins, 1.0, 0.0)
        masked.append(jnp.where(beaten_by < TOPK_GROUPS, blocks[g], neg))
    msel = jnp.concatenate(masked, axis=0)
    eio = lax.broadcasted_iota(I32, msel.shape, 0).astype(F32)
    chosen = jnp.zeros(msel.shape, F32)
    idxs, scs = [], []
    for _ in range(TOP_K):
        m = jnp.max(msel, axis=0, keepdims=True)
        first = jnp.min(jnp.where(msel == m, eio, float(N_EXPERTS)), axis=0, keepdims=True)
        hit = eio == first
        scs.append(jnp.sum(jnp.where(hit, scores, 0.0), axis=0, keepdims=True))
        msel = jnp.where(hit, neg, msel)
        chosen = jnp.where(hit, 1.0, chosen)
        idxs.append(first)
    total = scs[0]
    for s in scs[1:]:
        total = total + s
    chosen_b = chosen.astype(BF16)
    earlier = jnp.dot(chosen_b, tri_ref[...], preferred_element_type=F32)
    smaller = jnp.dot(below_ref[...], chosen_b, preferred_element_type=F32)
    run_len = jnp.sum(chosen, axis=1, keepdims=True)
    run_off = jnp.sum(smaller, axis=1, keepdims=True)
    slot_of = run_off + earlier
    slots = [jnp.sum(jnp.where(eio == i, slot_of, 0.0), axis=0, keepdims=True) for i in idxs]
    slot_o[...] = jnp.concatenate(slots, axis=0).astype(I32)
    gate_o[...] = jnp.concatenate([s / total * ROUTED_SCALE for s in scs], axis=0)
    lane = lax.broadcasted_iota(I32, run_o.shape, 1)
    run_o[...] = jnp.where(lane == 0, run_len, jnp.where(lane == 1, run_off, jnp.where(lane == 2, carry_ref[...], 0.0)))
    carry_ref[...] = carry_ref[...] + run_len
    cnt_o[...] = jnp.broadcast_to(carry_ref[...], cnt_o.shape)


def _router(h, rwt, bias, tt):
    n = h.shape[0]
    assert n % tt == 0
    tri = (lax.broadcasted_iota(I32, (tt, tt), 0) < lax.broadcasted_iota(I32, (tt, tt), 1)).astype(BF16)
    below = (lax.broadcasted_iota(I32, (N_EXPERTS, N_EXPERTS), 1)
             < lax.broadcasted_iota(I32, (N_EXPERTS, N_EXPERTS), 0)).astype(BF16)
    tokT = pl.BlockSpec((TOP_K, tt), lambda i: (0, i))
    return pl.pallas_call(
        _router_kernel,
        grid=(n // tt,),
        in_specs=[pl.BlockSpec((tt, D_MODEL), lambda i: (i, 0)), pl.BlockSpec((N_EXPERTS, D_MODEL), lambda i: (0, 0)),
                  pl.BlockSpec((N_EXPERTS, 1), lambda i: (0, 0)), pl.BlockSpec((tt, tt), lambda i: (0, 0)),
                  pl.BlockSpec((N_EXPERTS, N_EXPERTS), lambda i: (0, 0))],
        out_specs=[tokT, tokT, pl.BlockSpec((N_EXPERTS, LANES), lambda i: (i, 0)),
                   pl.BlockSpec((N_EXPERTS, LANES), lambda i: (0, 0))],
        out_shape=[jax.ShapeDtypeStruct((TOP_K, n), I32), jax.ShapeDtypeStruct((TOP_K, n), F32),
                   jax.ShapeDtypeStruct((n // tt * N_EXPERTS, LANES), F32),
                   jax.ShapeDtypeStruct((N_EXPERTS, LANES), F32)],
        scratch_shapes=[pltpu.VMEM((N_EXPERTS, 1), F32)],
        compiler_params=_cparams(1),
        name="router",
    )(h, rwt, bias, tri, below)


ROW_SUB = D_MODEL // LANES


def _store_row_tiles(ref, x):
    rows = x.shape[0]
    for c in range(ROW_SUB):
        ref[pl.ds(c, rows, stride=ROW_SUB), :] = x[:, c * LANES:(c + 1) * LANES]


def _load_row_tiles(ref):
    rows = ref.shape[0] // ROW_SUB
    return jnp.concatenate([ref[pl.ds(c, rows, stride=ROW_SUB), :] for c in range(ROW_SUB)], axis=-1)


RUN_FIELDS = 3
RUN_UNROLL = 8


def _start_run_copies(runs_ref, tile, local_ref, sorted_ref, sem, to_sorted):
    def body(e, c):
        size = pl.multiple_of(runs_ref[tile * RUN_FIELDS + 2, e], ROW_SUB)
        local = local_ref.at[pl.ds(pl.multiple_of(runs_ref[tile * RUN_FIELDS, e], ROW_SUB), size), :]
        remote = sorted_ref.at[pl.ds(pl.multiple_of(runs_ref[tile * RUN_FIELDS + 1, e], ROW_SUB), size), :]
        (pltpu.make_async_copy(local, remote, sem) if to_sorted else pltpu.make_async_copy(remote, local, sem)).start()
        return c

    lax.fori_loop(0, N_EXPERTS, body, 0, unroll=RUN_UNROLL)


def _wait_run_copies(local_ref, sorted_ref, sem):
    pltpu.make_async_copy(sorted_ref.at[pl.ds(0, local_ref.shape[0]), :], local_ref, sem).wait()


def _dispatch_kernel(zpos_ref, runs_ref, slot_ref, h_ref, xs_hbm, zero_buf, rows_buf, sem_zero, sem_rows, *, td):
    i = pl.program_id(0)

    def zero_copy(e):
        start = pl.multiple_of(zpos_ref[e] * ROW_SUB, CHUNK * ROW_SUB)
        return pltpu.make_async_copy(zero_buf, xs_hbm.at[pl.ds(start, CHUNK * ROW_SUB), :], sem_zero)

    @pl.when(i == 0)
    def _():
        zero_buf[...] = jnp.zeros_like(zero_buf)

        def start(e, c):
            @pl.when(zpos_ref[e] >= 0)
            def _():
                zero_copy(e).start()
            return c

        def wait(e, c):
            @pl.when(zpos_ref[e] >= 0)
            def _():
                zero_copy(e).wait()
            return c

        lax.fori_loop(0, N_EXPERTS, start, 0)
        lax.fori_loop(0, N_EXPERTS, wait, 0)

    slots = slot_ref[...]
    slot_iota = lax.broadcasted_iota(I32, (TOP_K * td, td), 0)
    select = jnp.zeros((TOP_K * td, td), F32)
    for j in range(TOP_K):
        select = jnp.where(slot_iota == slots[j:j + 1, :], 1.0, select)
    local = jnp.dot(select.astype(BF16), h_ref[...].astype(BF16), preferred_element_type=F32)
    _store_row_tiles(rows_buf, local)
    _start_run_copies(runs_ref, i, rows_buf, xs_hbm, sem_rows, True)
    _wait_run_copies(rows_buf, xs_hbm, sem_rows)


def _dispatch(zpos, runs, slots, h, n_rows, td):
    n = h.shape[0]
    assert n % td == 0
    return pl.pallas_call(
        functools.partial(_dispatch_kernel, td=td),
        grid_spec=pltpu.PrefetchScalarGridSpec(
            num_scalar_prefetch=2,
            grid=(n // td,),
            in_specs=[pl.BlockSpec((TOP_K, td), lambda i, z, r: (0, i)),
                      pl.BlockSpec((td, D_MODEL), lambda i, z, r: (i, 0))],
            out_specs=pl.BlockSpec(memory_space=pl.ANY),
            scratch_shapes=[pltpu.VMEM((CHUNK * ROW_SUB, LANES), F32), pltpu.VMEM((TOP_K * td * ROW_SUB, LANES), F32),
                            pltpu.SemaphoreType.DMA, pltpu.SemaphoreType.DMA],
        ),
        out_shape=jax.ShapeDtypeStruct((n_rows * ROW_SUB, LANES), F32),
        compiler_params=_cparams(1),
        name="dispatch",
    )(zpos, runs, slots, h)


CHUNK_ROWS = CHUNK * ROW_SUB
RING = 4


def _ffn_kernel(first_ref, count_ref, nu_ref, xs_hbm, wg_ref, wu_ref, wd_ref, ys_hbm, xbuf, obuf, xsem, osem):
    e = pl.program_id(0)
    n_used = nu_ref[0]

    def x_copy(g, slot):
        src = xs_hbm.at[pl.ds(pl.multiple_of(g * CHUNK_ROWS, CHUNK_ROWS), CHUNK_ROWS), :]
        return pltpu.make_async_copy(src, xbuf.at[slot], xsem.at[slot])

    def o_copy(g, slot):
        dst = ys_hbm.at[pl.ds(pl.multiple_of(g * CHUNK_ROWS, CHUNK_ROWS), CHUNK_ROWS), :]
        return pltpu.make_async_copy(obuf.at[slot], dst, osem.at[slot])

    @pl.when(e == 0)
    def _():
        for g in range(RING - 1):
            @pl.when(g < n_used)
            def _():
                x_copy(g, g).start()

    wg = wg_ref[0].astype(BF16)
    wu = wu_ref[0].astype(BF16)
    wd = wd_ref[0].astype(BF16)

    def chunk(c, carry):
        g = first_ref[e] + c
        slot = g & (RING - 1)
        x_copy(g, slot).wait()

        @pl.when(g + RING - 1 < n_used)
        def _():
            x_copy(g + RING - 1, (g + RING - 1) & (RING - 1)).start()

        @pl.when(g >= RING)
        def _():
            o_copy(g - RING, slot).wait()

        x = _load_row_tiles(xbuf.at[slot]).astype(BF16)
        gate = jnp.dot(x, wg, preferred_element_type=F32)
        up = jnp.dot(x, wu, preferred_element_type=F32)
        act = (gate * _sigmoid(gate)) * up
        _store_row_tiles(obuf.at[slot], jnp.dot(act.astype(BF16), wd, preferred_element_type=F32))
        o_copy(g, slot).start()
        return carry

    lax.fori_loop(0, count_ref[e], chunk, 0)

    @pl.when(e == N_EXPERTS - 1)
    def _():
        for back in range(RING, 0, -1):
            @pl.when(n_used >= back)
            def _():
                o_copy(n_used - back, (n_used - back) & (RING - 1)).wait()


def _ffn(first_chunk, chunk_count, n_used, xs, wg, wu, wd):
    weights = lambda a, b: pl.BlockSpec((1, a, b), lambda e, f, c, nu: (e, 0, 0))
    return pl.pallas_call(
        _ffn_kernel,
        grid_spec=pltpu.PrefetchScalarGridSpec(
            num_scalar_prefetch=3,
            grid=(N_EXPERTS,),
            in_specs=[pl.BlockSpec(memory_space=pl.ANY), weights(D_MODEL, EXPERT_FF), weights(D_MODEL, EXPERT_FF),
                      weights(EXPERT_FF, D_MODEL)],
            out_specs=pl.BlockSpec(memory_space=pl.ANY),
            scratch_shapes=[pltpu.VMEM((RING, CHUNK_ROWS, LANES), F32), pltpu.VMEM((RING, CHUNK_ROWS, LANES), F32),
                            pltpu.SemaphoreType.DMA((RING,)), pltpu.SemaphoreType.DMA((RING,))],
        ),
        out_shape=jax.ShapeDtypeStruct(xs.shape, F32),
        compiler_params=_cparams(1),
        name="expert_ffn",
    )(first_chunk, chunk_count, n_used, xs, wg, wu, wd)


def _combine_kernel(runs_ref, slot_ref, gate_ref, h_ref, ys_hbm, sg_ref, su_ref, sd_ref, l2g_ref, l2b_ref,
                    o1_ref, o2_ref, rows_buf, sem_rows, *, tc, alpha, first_tiles):
    i = pl.program_id(0)
    slot = i & 1
    buf = rows_buf.at[slot]

    @pl.when(i == 0)
    def _():
        _start_run_copies(runs_ref, i, buf, ys_hbm, sem_rows.at[slot], False)

    @pl.when(i + 1 < pl.num_programs(0))
    def _():
        _start_run_copies(runs_ref, i + 1, rows_buf.at[1 - slot], ys_hbm, sem_rows.at[1 - slot], False)

    h = h_ref[...]
    hb = h.astype(BF16)
    sgate = jnp.dot(hb, sg_ref[...], preferred_element_type=F32)
    sup = jnp.dot(hb, su_ref[...], preferred_element_type=F32)
    shared = jnp.dot(((sgate * _sigmoid(sgate)) * sup).astype(BF16), sd_ref[...], preferred_element_type=F32)
    slots = slot_ref[...]
    gate = gate_ref[...]
    slot_iota = lax.broadcasted_iota(I32, (tc, TOP_K * tc), 1)
    weights = jnp.zeros((tc, TOP_K * tc), F32)
    for j in range(TOP_K):
        weights = jnp.where(slot_iota == slots[:, j:j + 1], gate[:, j:j + 1], weights)
    w_hi = weights.astype(BF16)
    w_lo = (weights - w_hi.astype(F32)).astype(BF16)
    _wait_run_copies(buf, ys_hbm, sem_rows.at[slot])
    local = _load_row_tiles(buf).astype(BF16)
    routed = (jnp.dot(w_hi, local, preferred_element_type=F32) + jnp.dot(w_lo, local, preferred_element_type=F32))
    y = _layer_norm(alpha * h + (routed + shared), l2g_ref[...], l2b_ref[...])

    @pl.when(i < first_tiles)
    def _():
        o1_ref[...] = y

    @pl.when(i >= first_tiles)
    def _():
        o2_ref[...] = y


def _combine(runs, slots, gate, h, ys, sg, su, sd, l2g, l2b, tc, alpha, n_first):
    n = h.shape[0]
    assert n % tc == 0 and n_first % tc == 0 and 0 < n_first < n
    first_tiles = n_first // tc
    tok = lambda w: pl.BlockSpec((tc, w), lambda i, r: (i, 0))
    full = lambda a: pl.BlockSpec(a.shape, lambda i, r: (0, 0))
    return pl.pallas_call(
        functools.partial(_combine_kernel, tc=tc, alpha=alpha, first_tiles=first_tiles),
        grid_spec=pltpu.PrefetchScalarGridSpec(
            num_scalar_prefetch=1,
            grid=(n // tc,),
            in_specs=[tok(TOP_K), tok(TOP_K), tok(D_MODEL), pl.BlockSpec(memory_space=pl.ANY),
                      full(sg), full(su), full(sd), full(l2g), full(l2b)],
            out_specs=[pl.BlockSpec((tc, D_MODEL), lambda i, r: (jnp.minimum(i, first_tiles - 1), 0)),
                       pl.BlockSpec((tc, D_MODEL), lambda i, r: (jnp.maximum(i - first_tiles, 0), 0))],
            scratch_shapes=[pltpu.VMEM((2, TOP_K * tc * ROW_SUB, LANES), F32), pltpu.SemaphoreType.DMA((2,))],
        ),
        out_shape=[jax.ShapeDtypeStruct((n_first, D_MODEL), F32), jax.ShapeDtypeStruct((n - n_first, D_MODEL), F32)],
        compiler_params=_cparams(1),
        name="combine",
    )(runs, slots, gate, h, ys, sg, su, sd, l2g, l2b)


def _pad_cols(a, width):
    return jnp.pad(a, ((0, 0), (0, width - a.shape[1])))


def _pack_shift_cols(a):
    c = 3 * RWKV_W
    return jnp.concatenate([a[:, :c], _pad_cols(a[:, c:c + DECAY_LORA], LORA_PAD),
                            _pad_cols(a[:, c + DECAY_LORA:c + DECAY_LORA + AAA_LORA], LORA_PAD),
                            _pad_cols(a[:, c + DECAY_LORA + AAA_LORA:], GATE_PAD)], axis=1)


def _scan_to_lanes(a, b, t, split):
    a = a.reshape(b, t, SCAN_INPUTS, N_HEADS, HEAD).transpose(1, 2, 4, 0, 3)
    return a.reshape(t, SCAN_INPUTS, HEAD // split, split * b * N_HEADS)


def _values_from_lanes(y, b, t):
    return y.reshape(t, HEAD, b, N_HEADS).transpose(2, 0, 3, 1).reshape(b * t, RWKV_W)


def _state_to_lanes(s, b, split):
    s = s.reshape(b, N_HEADS, HEAD // split, split, HEAD).transpose(4, 2, 3, 0, 1)
    return s.reshape(HEAD, HEAD // split, split * b * N_HEADS)


def _state_from_lanes(s, b, split):
    s = s.reshape(HEAD, HEAD // split, split, b, N_HEADS).transpose(3, 4, 1, 2, 0)
    return s.reshape(b, N_HEADS, HEAD, HEAD)


def _mixer(z, zprev, x2, b, t, wkv0, pool_prev, mk, mv, pos0, wts, tiles, h_all):
    n = b * t
    tt, tb, seqs, att_tt = tiles
    z3 = z[:n].reshape(b, t, Z_W)
    scan, g, bonus = _rwkv_prep(z, zprev, b, t, tt, wts["prep"])

    split = 1 if (b * N_HEADS) % LANES == 0 else LANES // (b * N_HEADS)
    key_order = jnp.concatenate([jnp.arange(i, HEAD, split) for i in range(split)])
    s_in = wkv0[..., key_order] if split > 1 else wkv0
    y_l, s_l = _wkv(_scan_to_lanes(scan, b, t, split), _state_to_lanes(s_in, b, split), tb)
    y = _values_from_lanes(y_l, b, t)
    wkv_new = _state_from_lanes(s_l, b, split)
    if split > 1:
        wkv_new = wkv_new[..., jnp.argsort(key_order)]

    prev = jnp.concatenate([jnp.zeros((b, POOL_PAD - POOL_BUF, POOL_W), F32), pool_prev], axis=1)
    o_pool = _pool(prev, z3, wts["pool_w"], wts["pool_scale"], seqs, pos0).reshape(n, POOL_W)
    tail = z3[:, max(t - POOL_BUF, 0):, ZS_W:ZS_W + POOL_W]
    pool_new = jnp.concatenate([pool_prev, tail], axis=1)[:, -POOL_BUF:]

    o_mem = _memattn(z3, mk, mv, att_tt, seqs).reshape(n, MEM_W)
    h = _post(y, bonus, g, o_pool, o_mem, x2, *wts["post"], tt, wts["alpha"], h_all)
    return h, wkv_new, pool_new


def kernel(x_prompt, x_sample, mem_prompt, state_wkv, state_shift, state_pool, cache_mem_k, cache_mem_v, w_in, mu_shift, w0, w_up_decay, a0, w_up_aaa, w_up_gate, k_k, k_a, r_k, ln_x_w, ln_x_b, pool_w, pool_scale, mem_wk, mem_wv, w_out, ln1_g, ln1_b, router_w, router_b, exp_gate, exp_up, exp_down, sh_gate, sh_up, sh_down, ln2_g, ln2_b):
    depth = w_in.shape[0]
    assert depth == 1
    l = 0
    alpha = (2.0 * depth) ** 0.25
    bp, tp, d = x_prompt.shape
    bs, ts, _ = x_sample.shape
    n_p, n_s = bp * tp, bs * ts
    n = n_p + n_s

    w_in_p = jnp.concatenate([_pack_shift_cols(w_in[l][:, :1824]), w_in[l][:, 1824:]], axis=1).astype(BF16)
    row = lambda a: a.reshape(1, -1)
    pad_rows = lambda a, rows: jnp.pad(a, ((0, rows - a.shape[0]), (0, 0)))
    bd = jnp.kron(jnp.eye(N_HEADS, dtype=F32), jnp.ones((HEAD, HEAD), F32)).astype(BF16)
    pw = pool_w[l]
    pw_bd = jnp.zeros((POOL_W, POOL_W), F32)
    for gi in range(4):
        pw_bd = pw_bd.at[gi * 64:(gi + 1) * 64, gi * 64:(gi + 1) * 64].set(pw[gi])
    wts = {
        "prep": (_pack_shift_cols(row(mu_shift[l])), row(w0[l]), row(a0[l]), pad_rows(w_up_decay[l], LORA_PAD),
                 pad_rows(w_up_aaa[l], LORA_PAD), pad_rows(w_up_gate[l], GATE_PAD), row(k_k[l]), row(k_a[l]),
                 row(r_k[l]), bd),
        "pool_w": pw_bd.astype(BF16), "pool_scale": row(pool_scale[l]),
        "post": (w_out[l].astype(BF16), row(ln_x_w[l]), row(ln_x_b[l]), row(ln1_g[l]), row(ln1_b[l]), bd),
        "alpha": alpha,
    }

    xp2 = x_prompt.reshape(n_p, d)
    xs2 = x_sample.reshape(n_s, d)
    z_p = _matmul(xp2, w_in_p, _tile(n_p, 512))
    z_s = _matmul(jnp.concatenate([xs2, state_shift[l]], axis=0), w_in_p, _tile(n_s + bs, 512))
    mkv = _matmul(mem_prompt.reshape(bp * MEM_TOKENS, d),
                  jnp.concatenate([mem_wk[l], mem_wv[l]], axis=1).astype(BF16), _tile(bp * MEM_TOKENS, 512))
    mk_p = mkv[:, :MEM_W].reshape(bp, MEM_TOKENS, MEM_W)
    mv_p = mkv[:, MEM_W:].reshape(bp, MEM_TOKENS, MEM_W)

    h_p, wkv_p, pool_p = _mixer(z_p, jnp.zeros((bp, ZS_W), F32), xp2, bp, tp,
                                jnp.zeros((bp, N_HEADS, HEAD, HEAD), F32), jnp.zeros((bp, POOL_BUF, POOL_W), F32),
                                mk_p, mv_p, 0, wts,
                                (_tile(n_p, 256), _tile(tp, 32, 1), 1, _tile(tp, 256)), (n, 0))
    h, wkv_s, pool_s = _mixer(z_s, z_s[n_s:, :ZS_W], xs2, bs, ts, state_wkv[l], state_pool[l],
                              cache_mem_k[l].reshape(bs, MEM_TOKENS, MEM_W),
                              cache_mem_v[l].reshape(bs, MEM_TOKENS, MEM_W), PAST_LEN, wts,
                              (_tile(n_s, 256), _tile(ts, 32, 1), _tile(bs, 16, 1), _tile(ts, 256)), h_p)

    td = _tile(n, 256, LANES)
    slots, gate, run_tab, cnt = _router(h, router_w[l].T, router_b[l].reshape(N_EXPERTS, 1), td)
    counts = cnt[:, 0].astype(I32)
    padded = (counts + CHUNK - 1) // CHUNK * CHUNK
    pad_end = jnp.cumsum(padded)
    pad_start = pad_end - padded
    n_chunks = (n * TOP_K + N_EXPERTS * (CHUNK - 1) + CHUNK - 1) // CHUNK
    run_tab = run_tab.reshape(n // td, N_EXPERTS, LANES)[:, :, :RUN_FIELDS].astype(I32)
    runs = jnp.stack([run_tab[:, :, 1], pad_start[None, :] + run_tab[:, :, 2], run_tab[:, :, 0]], axis=1)
    runs = runs.reshape(n // td * RUN_FIELDS, N_EXPERTS) * ROW_SUB
    zpos = jnp.where(padded > 0, pad_end - CHUNK, -1).astype(I32)
    n_used = (pad_end[-1:] // CHUNK).astype(I32)
    xs_sorted = _dispatch(zpos, runs, slots, h, n_chunks * CHUNK, td)
    ys_sorted = _ffn((pad_start // CHUNK).astype(I32), (padded // CHUNK).astype(I32), n_used, xs_sorted,
                     exp_gate[l], exp_up[l], exp_down[l])
    y_p, y_s = _combine(runs, slots.T, gate.T, h, ys_sorted, sh_gate[l].astype(BF16), sh_up[l].astype(BF16),
                        sh_down[l].astype(BF16), row(ln2_g[l]), row(ln2_b[l]), td, alpha, n_p)

    return (y_p.reshape(bp, tp, d), y_s.reshape(bs, ts, d),
            wkv_p[None], x_prompt[:, -1][None], pool_p[None],
            mk_p.reshape(bp, MEM_TOKENS, MEM_HEADS, HEAD)[None], mv_p.reshape(bp, MEM_TOKENS, MEM_HEADS, HEAD)[None],
            wkv_s[None], x_sample[:, -1][None], pool_s[None])
```

```python
import functools

import jax
import jax.numpy as jnp
from jax import lax
from jax.experimental import pallas as pl
from jax.experimental.pallas import tpu as pltpu

F32 = jnp.float32
BF16 = jnp.bfloat16
I32 = jnp.int32

D_MODEL = 1024
HEAD = 64
N_HEADS = 8
RWKV_W = 512
POOL_W = 256
MEM_W = 256
MEM_HEADS = 4
MEM_TOKENS = 256
POOL_BUF = 15
POOL_PAD = 16
DECAY_LORA = 64
AAA_LORA = 64
GATE_LORA = 160
LORA_PAD = 128
GATE_PAD = 256
ZS_W = 3 * RWKV_W + 2 * LORA_PAD + GATE_PAD
Z_W = ZS_W + POOL_W + MEM_W
N_EXPERTS = 256
TOP_K = 8
N_GROUPS = 8
GROUP_SIZE = N_EXPERTS // N_GROUPS
TOPK_GROUPS = 4
EXPERT_FF = 256
ROUTED_SCALE = 2.5
CHUNK = 256
PAST_LEN = 16384
LN_EPS = 1e-5
GN_EPS = 64e-5
LANES = 128
SUBLANES = 8
SCAN_INPUTS = 6
VMEM_LIMIT = 56 * 1024 * 1024


def _cparams(n_axes):
    return pltpu.CompilerParams(dimension_semantics=("arbitrary",) * n_axes, vmem_limit_bytes=VMEM_LIMIT)


def _tile(n, preferred, multiple=8):
    best = None
    for c in range(multiple, min(n, preferred) + 1, multiple):
        if n % c == 0:
            best = c
    assert best is not None, (n, preferred, multiple)
    return best


def _sigmoid(x):
    return 1.0 / (1.0 + jnp.exp(-x))


def _matmul_kernel(x_ref, w_ref, o_ref):
    o_ref[...] = jnp.dot(x_ref[...].astype(BF16), w_ref[...], preferred_element_type=F32)


def _matmul(x, w, tm):
    m, k = x.shape
    n = w.shape[1]
    assert m % tm == 0
    return pl.pallas_call(
        _matmul_kernel,
        grid=(m // tm,),
        in_specs=[pl.BlockSpec((tm, k), lambda i: (i, 0)), pl.BlockSpec((k, n), lambda i: (0, 0))],
        out_specs=pl.BlockSpec((tm, n), lambda i: (i, 0)),
        out_shape=jax.ShapeDtypeStruct((m, n), F32),
        compiler_params=_cparams(1),
        name="matmul",
    )(x, w)


def _split(x):
    hi = x.astype(BF16)
    return hi, (x - hi.astype(F32)).astype(BF16)


def _dot_f32(a, b, contract=((1,), (0,))):
    a_hi, a_lo = _split(a)
    b_hi, b_lo = _split(b)
    dot = lambda x, y: lax.dot_general(x, y, (contract, ((), ())), preferred_element_type=F32)
    return dot(a_hi, b_hi) + (dot(a_hi, b_lo) + dot(a_lo, b_hi))


def _segsum(x, bd):
    hi = x.astype(BF16)
    lo = (x - hi.astype(F32)).astype(BF16)
    return jnp.dot(hi, bd, preferred_element_type=F32) + jnp.dot(lo, bd, preferred_element_type=F32)


def _prep_kernel(zs_ref, zp_ref, mu_ref, w0_ref, a0_ref, wd_ref, wa_ref, wg_ref, kk_ref, ka_ref, rk_ref, bd_ref,
                 scan_o, g_o, bonus_o, carry_ref, *, tiles_per_seq, seq_len):
    zs = zs_ref[...]
    rolled = pltpu.roll(zs, 1, 0)
    row = lax.broadcasted_iota(I32, zs.shape, 0)
    if tiles_per_seq:
        i = pl.program_id(0)

        @pl.when(i == 0)
        def _():
            carry_ref[...] = jnp.zeros_like(carry_ref)

        prev = jnp.where(i % tiles_per_seq == 0, zp_ref[0], carry_ref[...])
        shifted = jnp.where(row == 0, prev, rolled)
        carry_ref[...] = zs[zs.shape[0] - 1:, :]
    else:
        shifted = jnp.where((row & (seq_len - 1)) == 0, zp_ref[...], rolled)
    zm = zs + mu_ref[...] * (shifted - zs)
    r = zm[:, 0:RWKV_W]
    k = zm[:, RWKV_W:2 * RWKV_W]
    v = zm[:, 2 * RWKV_W:3 * RWKV_W]
    c0 = 3 * RWKV_W
    xw = zm[:, c0:c0 + LORA_PAD]
    xa = zm[:, c0 + LORA_PAD:c0 + 2 * LORA_PAD]
    xg = zm[:, c0 + 2 * LORA_PAD:c0 + 2 * LORA_PAD + GATE_PAD]
    bd = bd_ref[...]
    u = -(w0_ref[...] + _dot_f32(jnp.tanh(xw), wd_ref[...]))
    softplus = jnp.maximum(u, 0.0) + jnp.log(1.0 + jnp.exp(-jnp.abs(u)))
    decay = jnp.exp(-jnp.exp(-softplus - 0.5))
    a = _sigmoid(a0_ref[...] + _dot_f32(xa, wa_ref[...]))
    g = _dot_f32(_sigmoid(xg), wg_ref[...])
    kk = k * kk_ref[...]
    kk = kk / jnp.maximum(jnp.sqrt(_segsum(kk * kk, bd)), 1e-12)
    kp = k * (1.0 + (a - 1.0) * ka_ref[...])
    for slot, value in enumerate((r, decay, kp, kk, -(kk * a), v)):
        scan_o[:, slot * RWKV_W:(slot + 1) * RWKV_W] = value
    g_o[...] = g
    bonus_o[...] = _segsum(r * kp * rk_ref[...], bd) * v


def _rwkv_prep(z, zprev, b, t, tt, params):
    n_tok = b * t
    assert n_tok % tt == 0
    row = lambda w: pl.BlockSpec((1, w), lambda i: (0, 0))
    full = lambda a: pl.BlockSpec(a.shape, lambda i: (0, 0))
    tok = pl.BlockSpec((tt, RWKV_W), lambda i: (i, 0))
    mu, w0, a0, wd, wa, wg, k_k, k_a, r_k, bd = params
    if t % tt == 0:
        tiles_per_seq = t // tt
        zp = zprev.reshape(b, 1, ZS_W)
        zp_spec = pl.BlockSpec((1, 1, ZS_W), lambda i: (i // tiles_per_seq, 0, 0))
    else:
        assert tt % t == 0 and t & (t - 1) == 0
        tiles_per_seq = 0
        zp = jnp.repeat(zprev, t, axis=0)
        zp_spec = pl.BlockSpec((tt, ZS_W), lambda i: (i, 0))
    return pl.pallas_call(
        functools.partial(_prep_kernel, tiles_per_seq=tiles_per_seq, seq_len=t),
        grid=(n_tok // tt,),
        in_specs=[pl.BlockSpec((tt, ZS_W), lambda i: (i, 0)), zp_spec,
                  row(ZS_W), row(RWKV_W), row(RWKV_W), full(wd), full(wa), full(wg),
                  row(RWKV_W), row(RWKV_W), row(RWKV_W), full(bd)],
        out_specs=[pl.BlockSpec((tt, SCAN_INPUTS * RWKV_W), lambda i: (i, 0)), tok, tok],
        out_shape=[jax.ShapeDtypeStruct((n_tok, SCAN_INPUTS * RWKV_W), F32),
                   jax.ShapeDtypeStruct((n_tok, RWKV_W), F32), jax.ShapeDtypeStruct((n_tok, RWKV_W), F32)],
        scratch_shapes=[pltpu.VMEM((1, ZS_W), F32)],
        compiler_params=_cparams(1),
        name="rwkv_prep",
    )(z, zp, mu, w0, a0, wd, wa, wg, k_k, k_a, r_k, bd)


def _wkv_kernel(r_ref, w_ref, k_ref, kk_ref, nb_ref, v_ref, s0_ref, y_ref, s_ref, *scratch, tb, dup):
    slabs = s_ref.shape[1]
    key_unroll = 16

    @pl.when(pl.program_id(1) == 0)
    def _():
        s_ref[...] = s0_ref[...]

    if dup:
        for src, dst in zip((r_ref, w_ref, k_ref, kk_ref, nb_ref), scratch):
            x = src[...].reshape(tb * (HEAD // 2), LANES)
            swapped = pltpu.roll(x, LANES // 2, 1)
            low = lax.broadcasted_iota(I32, x.shape, 1) < LANES // 2
            dst[:, 0:HEAD // 2, :] = jnp.where(low, x, swapped).reshape(tb, HEAD // 2, LANES)
            dst[:, HEAD // 2:, :] = jnp.where(low, swapped, x).reshape(tb, HEAD // 2, LANES)
        r_ref, w_ref, k_ref, kk_ref, nb_ref = scratch

    def rows(ref, t, k):
        return jnp.broadcast_to(ref[t, pl.ds(k, 1), :], (SUBLANES, LANES))

    def step(t, sa, with_next):
        vt = [v_ref[t, i * SUBLANES:(i + 1) * SUBLANES, :] for i in range(slabs)]
        zero = jnp.zeros((SUBLANES, LANES), F32)

        def key(k, acc):
            y_acc, next_acc = list(acc[0]), list(acc[1])
            wb, nbb, kb, rb = rows(w_ref, t, k), rows(nb_ref, t, k), rows(k_ref, t, k), rows(r_ref, t, k)
            kkb = rows(kk_ref, t + 1, k) if with_next else None
            for i in range(slabs):
                sn = s_ref[k, i] * wb + sa[i] * nbb + vt[i] * kb
                s_ref[k, i] = sn
                y_acc[i] = y_acc[i] + sn * rb
                if with_next:
                    next_acc[i] = next_acc[i] + sn * kkb
            return tuple(y_acc), tuple(next_acc)

        y_acc, next_acc = lax.fori_loop(0, HEAD, key, ((zero,) * slabs, (zero,) * slabs), unroll=key_unroll)
        for i, y in enumerate(y_acc):
            y_ref[t, i * SUBLANES:(i + 1) * SUBLANES, :] = y
        return next_acc

    def first(k, acc):
        kkb = rows(kk_ref, 0, k)
        return tuple(a + s_ref[k, i] * kkb for i, a in enumerate(acc))

    sa0 = lax.fori_loop(0, HEAD, first, (jnp.zeros((SUBLANES, LANES), F32),) * slabs, unroll=key_unroll)
    sa_last = lax.fori_loop(0, tb - 1, lambda t, sa: step(t, sa, True), sa0)
    step(tb - 1, sa_last, False)


def _wkv(scan, s0, tb):
    t, _, rows, l = scan.shape
    dup = rows != HEAD
    assert t % tb == 0 and l % LANES == 0 and rows % SUBLANES == 0 and (not dup or (l == LANES and 2 * rows == HEAD))
    vec = lambda a: pl.BlockSpec((tb, None, rows, LANES), lambda g, i: (i, a, 0, g))
    slabs = rows // SUBLANES
    st = pl.BlockSpec((HEAD, slabs, SUBLANES, LANES), lambda g, i: (0, 0, 0, g))
    s0 = s0.reshape(HEAD, slabs, SUBLANES, l)
    y, s_new = pl.pallas_call(
        functools.partial(_wkv_kernel, tb=tb, dup=dup),
        grid=(l // LANES, t // tb),
        in_specs=[vec(a) for a in range(SCAN_INPUTS)] + [st],
        out_specs=[pl.BlockSpec((tb, rows, LANES), lambda g, i: (i, 0, g)), st],
        out_shape=[jax.ShapeDtypeStruct((t, rows, l), F32), jax.ShapeDtypeStruct(s0.shape, F32)],
        scratch_shapes=[pltpu.VMEM((tb, HEAD, LANES), F32)] * 5 if dup else [],
        compiler_params=_cparams(2),
        name="wkv",
    )(*([scan] * SCAN_INPUTS), s0)
    return y, s_new.reshape(HEAD, rows, l)


def _pool_kernel(prev_ref, zp_ref, pw_ref, ps_ref, o_ref, *, t, pos0):
    bb = zp_ref.shape[0]
    lane = lax.broadcasted_iota(I32, (t, POOL_W), 1)
    window = jnp.where(lane < 64, 2, jnp.where(lane < 128, 4, jnp.where(lane < 192, 8, 16)))
    pos = lax.broadcasted_iota(I32, (t, POOL_W), 0) + (pos0 + 1)
    cnt = jnp.minimum(pos, window).astype(F32)
    diffs = []
    for i in range(bb):
        f = jnp.concatenate([prev_ref[i], zp_ref[i]], axis=0)
        s2 = f + pltpu.roll(f, 1, 0)
        s4 = s2 + pltpu.roll(s2, 2, 0)
        s8 = s4 + pltpu.roll(s4, 4, 0)
        s16 = s8 + pltpu.roll(s8, 8, 0)
        wsum = jnp.where(lane < 64, s2[POOL_PAD:], jnp.where(lane < 128, s4[POOL_PAD:],
                                                            jnp.where(lane < 192, s8[POOL_PAD:], s16[POOL_PAD:])))
        diffs.append(wsum / cnt - f[POOL_PAD:])
    diff = diffs[0] if bb == 1 else jnp.concatenate(diffs, axis=0)
    out = jnp.dot(diff.astype(BF16), pw_ref[...], preferred_element_type=F32) * ps_ref[...]
    for i in range(bb):
        o_ref[i] = out[i * t:(i + 1) * t]


def _pool(prev, z3, pw_bd, pscale, bb, pos0):
    b, t, _ = z3.shape
    assert b % bb == 0
    return pl.pallas_call(
        functools.partial(_pool_kernel, t=t, pos0=pos0),
        grid=(b // bb,),
        in_specs=[pl.BlockSpec((bb, POOL_PAD, POOL_W), lambda i: (i, 0, 0)),
                  pl.BlockSpec((bb, t, POOL_W), lambda i: (i, 0, ZS_W // POOL_W)),
                  pl.BlockSpec((POOL_W, POOL_W), lambda i: (0, 0)), pl.BlockSpec((1, POOL_W), lambda i: (0, 0))],
        out_specs=pl.BlockSpec((bb, t, POOL_W), lambda i: (i, 0, 0)),
        out_shape=jax.ShapeDtypeStruct((b, t, POOL_W), F32),
        compiler_params=_cparams(1),
        name="pool",
    )(prev, z3, pw_bd, pscale)


def _memattn_kernel(q_ref, k_ref, v_ref, o_ref):
    for i in range(q_ref.shape[0]):
        q = q_ref[i].astype(BF16)
        kf = k_ref[i]
        vf = v_ref[i]
        head_of_lane = lax.broadcasted_iota(I32, kf.shape, 1) // (MEM_W // MEM_HEADS)
        out = None
        for h in range(MEM_HEADS):
            kh = jnp.where(head_of_lane == h, kf, 0.0).astype(BF16)
            vh = jnp.where(head_of_lane == h, vf, 0.0).astype(BF16)
            s = lax.dot_general(q, kh, (((1,), (1,)), ((), ())), preferred_element_type=F32) * (64 ** -0.5)
            e = jnp.exp(s - jnp.max(s, axis=-1, keepdims=True))
            p = e / jnp.sum(e, axis=-1, keepdims=True)
            o = jnp.dot(p.astype(BF16), vh, preferred_element_type=F32)
            out = o if out is None else out + o
        o_ref[i] = out


def _memattn(z3, mk, mv, tt, bb):
    b, t, _ = z3.shape
    assert t % tt == 0 and b % bb == 0
    qcol = (Z_W - MEM_W) // MEM_W
    kv = pl.BlockSpec((bb, MEM_TOKENS, MEM_W), lambda i, j: (i, 0, 0))
    return pl.pallas_call(
        _memattn_kernel,
        grid=(b // bb, t // tt),
        in_specs=[pl.BlockSpec((bb, tt, MEM_W), lambda i, j: (i, j, qcol)), kv, kv],
        out_specs=pl.BlockSpec((bb, tt, MEM_W), lambda i, j: (i, j, 0)),
        out_shape=jax.ShapeDtypeStruct((b, t, MEM_W), F32),
        compiler_params=_cparams(2),
        name="mem_attn",
    )(z3, mk, mv)


def _layer_norm(x, g, b):
    mu = jnp.mean(x, axis=-1, keepdims=True)
    xc = x - mu
    var = jnp.mean(xc * xc, axis=-1, keepdims=True)
    return xc * lax.rsqrt(var + LN_EPS) * g + b


def _post_kernel(y_ref, bonus_ref, g_ref, op_ref, om_ref, x_ref, wo_ref, lxw_ref, lxb_ref, l1g_ref, l1b_ref, bd_ref,
                 *rest, alpha):
    h_ref = rest[-1]
    y = y_ref[...]
    bd = bd_ref[...]
    mu = _segsum(y, bd) * (1.0 / HEAD)
    yc = y - mu
    var = _segsum(yc * yc, bd) * (1.0 / HEAD)
    yn = yc * lax.rsqrt(var + GN_EPS) * lxw_ref[...] + lxb_ref[...]
    o_rwkv = (yn + bonus_ref[...]) * g_ref[...]
    mixed = (jnp.dot(o_rwkv.astype(BF16), wo_ref[0:RWKV_W, :], preferred_element_type=F32)
             + jnp.dot(op_ref[...].astype(BF16), wo_ref[RWKV_W:RWKV_W + POOL_W, :], preferred_element_type=F32)
             + jnp.dot(om_ref[...].astype(BF16), wo_ref[RWKV_W + POOL_W:, :], preferred_element_type=F32))
    h_ref[...] = _layer_norm(alpha * x_ref[...] + mixed, l1g_ref[...], l1b_ref[...])


def _post(y, bonus, g, o_pool, o_mem, x, wo, lxw, lxb, l1g, l1b, bd, tt, alpha, h_all):
    n = y.shape[0]
    assert n % tt == 0
    tok = lambda w: pl.BlockSpec((tt, w), lambda i: (i, 0))
    row = lambda w: pl.BlockSpec((1, w), lambda i: (0, 0))
    full = lambda a: pl.BlockSpec(a.shape, lambda i: (0, 0))
    in_specs = [tok(RWKV_W), tok(RWKV_W), tok(RWKV_W), tok(POOL_W), tok(MEM_W), tok(D_MODEL), full(wo),
                row(RWKV_W), row(RWKV_W), row(D_MODEL), row(D_MODEL), full(bd)]
    args = [y, bonus, g, o_pool, o_mem, x, wo, lxw, lxb, l1g, l1b, bd]
    if isinstance(h_all, tuple):
        n_total, first = h_all
        aliases = {}
    else:
        n_total, first = h_all.shape[0], h_all.shape[0] - n
        in_specs.append(pl.BlockSpec(memory_space=pl.ANY))
        args.append(h_all)
        aliases = {len(args) - 1: 0}
    assert first % tt == 0
    return pl.pallas_call(
        functools.partial(_post_kernel, alpha=alpha),
        grid=(n // tt,),
        in_specs=in_specs,
        out_specs=pl.BlockSpec((tt, D_MODEL), lambda i: (i + first // tt, 0)),
        out_shape=jax.ShapeDtypeStruct((n_total, D_MODEL), F32),
        input_output_aliases=aliases,
        compiler_params=_cparams(1),
        name="post",
    )(*args)


def _router_kernel(h_ref, rwt_ref, bias_ref, tri_ref, below_ref, slot_o, gate_o, run_o, cnt_o, carry_ref):
    @pl.when(pl.program_id(0) == 0)
    def _():
        carry_ref[...] = jnp.zeros_like(carry_ref)

    neg = -jnp.inf
    logits = _dot_f32(rwt_ref[...], h_ref[...], ((1,), (1,)))
    scores = _sigmoid(logits)
    sel = scores + bias_ref[...]
    tt = sel.shape[1]
    gio = lax.broadcasted_iota(I32, (GROUP_SIZE, tt), 0).astype(F32)
    blocks, gscore = [], []
    for g in range(N_GROUPS):
        blk = sel[g * GROUP_SIZE:(g + 1) * GROUP_SIZE, :]
        m1 = jnp.max(blk, axis=0, keepdims=True)
        first = jnp.min(jnp.where(blk == m1, gio, float(GROUP_SIZE)), axis=0, keepdims=True)
        m2 = jnp.max(jnp.where(gio == first, neg, blk), axis=0, keepdims=True)
        blocks.append(blk)
        gscore.append(m1 + m2)
    masked = []
    for g in range(N_GROUPS):
        beaten_by = jnp.zeros((1, tt), F32)
        for g2 in range(N_GROUPS):
            if g2 != g:
                wins = (gscore[g2] >= gscore[g]) if g2 < g else (gscore[g2] > gscore[g])
                beaten_by = beaten_by + jnp.where(wins, 1.0, 0.0)
        masked.append(jnp.where(beaten_by < TOPK_GROUPS, blocks[g], neg))
    msel = jnp.concatenate(masked, axis=0)
    eio = lax.broadcasted_iota(I32, msel.shape, 0).astype(F32)
    chosen = jnp.zeros(msel.shape, F32)
    idxs, scs = [], []
    for _ in range(TOP_K):
        m = jnp.max(msel, axis=0, keepdims=True)
        first = jnp.min(jnp.where(msel == m, eio, float(N_EXPERTS)), axis=0, keepdims=True)
        hit = eio == first
        scs.append(jnp.sum(jnp.where(hit, scores, 0.0), axis=0, keepdims=True))
        msel = jnp.where(hit, neg, msel)
        chosen = jnp.where(hit, 1.0, chosen)
        idxs.append(first)
    total = scs[0]
    for s in scs[1:]:
        total = total + s
    chosen_b = chosen.astype(BF16)
    earlier = jnp.dot(chosen_b, tri_ref[...], preferred_element_type=F32)
    smaller = jnp.dot(below_ref[...], chosen_b, preferred_element_type=F32)
    run_len = jnp.sum(chosen, axis=1, keepdims=True)
    run_off = jnp.sum(smaller, axis=1, keepdims=True)
    slot_of = run_off + earlier
    slots = [jnp.sum(jnp.where(eio == i, slot_of, 0.0), axis=0, keepdims=True) for i in idxs]
    slot_o[...] = jnp.concatenate(slots, axis=0).astype(I32)
    gate_o[...] = jnp.concatenate([s / total * ROUTED_SCALE for s in scs], axis=0)
    lane = lax.broadcasted_iota(I32, run_o.shape, 1)
    run_o[...] = jnp.where(lane == 0, run_len, jnp.where(lane == 1, run_off, jnp.where(lane == 2, carry_ref[...], 0.0)))
    carry_ref[...] = carry_ref[...] + run_len
    cnt_o[...] = jnp.broadcast_to(carry_ref[...], cnt_o.shape)


def _router(h, rwt, bias, tt):
    n = h.shape[0]
    assert n % tt == 0
    tri = (lax.broadcasted_iota(I32, (tt, tt), 0) < lax.broadcasted_iota(I32, (tt, tt), 1)).astype(BF16)
    below = (lax.broadcasted_iota(I32, (N_EXPERTS, N_EXPERTS), 1)
             < lax.broadcasted_iota(I32, (N_EXPERTS, N_EXPERTS), 0)).astype(BF16)
    tokT = pl.BlockSpec((TOP_K, tt), lambda i: (0, i))
    return pl.pallas_call(
        _router_kernel,
        grid=(n // tt,),
        in_specs=[pl.BlockSpec((tt, D_MODEL), lambda i: (i, 0)), pl.BlockSpec((N_EXPERTS, D_MODEL), lambda i: (0, 0)),
                  pl.BlockSpec((N_EXPERTS, 1), lambda i: (0, 0)), pl.BlockSpec((tt, tt), lambda i: (0, 0)),
                  pl.BlockSpec((N_EXPERTS, N_EXPERTS), lambda i: (0, 0))],
        out_specs=[tokT, tokT, pl.BlockSpec((N_EXPERTS, LANES), lambda i: (i, 0)),
                   pl.BlockSpec((N_EXPERTS, LANES), lambda i: (0, 0))],
        out_shape=[jax.ShapeDtypeStruct((TOP_K, n), I32), jax.ShapeDtypeStruct((TOP_K, n), F32),
                   jax.ShapeDtypeStruct((n // tt * N_EXPERTS, LANES), F32),
                   jax.ShapeDtypeStruct((N_EXPERTS, LANES), F32)],
        scratch_shapes=[pltpu.VMEM((N_EXPERTS, 1), F32)],
        compiler_params=_cparams(1),
        name="router",
    )(h, rwt, bias, tri, below)


ROW_SUB = D_MODEL // LANES


def _store_row_tiles(ref, x):
    rows = x.shape[0]
    for c in range(ROW_SUB):
        ref[pl.ds(c, rows, stride=ROW_SUB), :] = x[:, c * LANES:(c + 1) * LANES]


def _load_row_tiles(ref):
    rows = ref.shape[0] // ROW_SUB
    return jnp.concatenate([ref[pl.ds(c, rows, stride=ROW_SUB), :] for c in range(ROW_SUB)], axis=-1)


RUN_FIELDS = 3
RUN_UNROLL = 8


def _start_run_copies(runs_ref, tile, local_ref, sorted_ref, sem, to_sorted):
    def body(e, c):
        size = pl.multiple_of(runs_ref[tile * RUN_FIELDS + 2, e], ROW_SUB)
        local = local_ref.at[pl.ds(pl.multiple_of(runs_ref[tile * RUN_FIELDS, e], ROW_SUB), size), :]
        remote = sorted_ref.at[pl.ds(pl.multiple_of(runs_ref[tile * RUN_FIELDS + 1, e], ROW_SUB), size), :]
        (pltpu.make_async_copy(local, remote, sem) if to_sorted else pltpu.make_async_copy(remote, local, sem)).start()
        return c

    lax.fori_loop(0, N_EXPERTS, body, 0, unroll=RUN_UNROLL)


def _wait_run_copies(local_ref, sorted_ref, sem):
    pltpu.make_async_copy(sorted_ref.at[pl.ds(0, local_ref.shape[0]), :], local_ref, sem).wait()


def _dispatch_kernel(zpos_ref, runs_ref, slot_ref, h_ref, xs_hbm, zero_buf, rows_buf, sem_zero, sem_rows, *, td):
    i = pl.program_id(0)

    def zero_copy(e):
        start = pl.multiple_of(zpos_ref[e] * ROW_SUB, CHUNK * ROW_SUB)
        return pltpu.make_async_copy(zero_buf, xs_hbm.at[pl.ds(start, CHUNK * ROW_SUB), :], sem_zero)

    @pl.when(i == 0)
    def _():
        zero_buf[...] = jnp.zeros_like(zero_buf)

        def start(e, c):
            @pl.when(zpos_ref[e] >= 0)
            def _():
                zero_copy(e).start()
            return c

        def wait(e, c):
            @pl.when(zpos_ref[e] >= 0)
            def _():
                zero_copy(e).wait()
            return c

        lax.fori_loop(0, N_EXPERTS, start, 0)
        lax.fori_loop(0, N_EXPERTS, wait, 0)

    slots = slot_ref[...]
    slot_iota = lax.broadcasted_iota(I32, (TOP_K * td, td), 0)
    select = jnp.zeros((TOP_K * td, td), F32)
    for j in range(TOP_K):
        select = jnp.where(slot_iota == slots[j:j + 1, :], 1.0, select)
    local = jnp.dot(select.astype(BF16), h_ref[...].astype(BF16), preferred_element_type=F32)
    _store_row_tiles(rows_buf, local)
    _start_run_copies(runs_ref, i, rows_buf, xs_hbm, sem_rows, True)
    _wait_run_copies(rows_buf, xs_hbm, sem_rows)


def _dispatch(zpos, runs, slots, h, n_rows, td):
    n = h.shape[0]
    assert n % td == 0
    return pl.pallas_call(
        functools.partial(_dispatch_kernel, td=td),
        grid_spec=pltpu.PrefetchScalarGridSpec(
            num_scalar_prefetch=2,
            grid=(n // td,),
            in_specs=[pl.BlockSpec((TOP_K, td), lambda i, z, r: (0, i)),
                      pl.BlockSpec((td, D_MODEL), lambda i, z, r: (i, 0))],
            out_specs=pl.BlockSpec(memory_space=pl.ANY),
            scratch_shapes=[pltpu.VMEM((CHUNK * ROW_SUB, LANES), F32), pltpu.VMEM((TOP_K * td * ROW_SUB, LANES), F32),
                            pltpu.SemaphoreType.DMA, pltpu.SemaphoreType.DMA],
        ),
        out_shape=jax.ShapeDtypeStruct((n_rows * ROW_SUB, LANES), F32),
        compiler_params=_cparams(1),
        name="dispatch",
    )(zpos, runs, slots, h)


CHUNK_ROWS = CHUNK * ROW_SUB
RING = 8


def _ffn_kernel(first_ref, count_ref, nu_ref, xs_hbm, wg_ref, wu_ref, wd_ref, ys_hbm, xbuf, obuf, xsem, osem):
    e = pl.program_id(0)
    n_used = nu_ref[0]

    def x_copy(g, slot):
        src = xs_hbm.at[pl.ds(pl.multiple_of(g * CHUNK_ROWS, CHUNK_ROWS), CHUNK_ROWS), :]
        return pltpu.make_async_copy(src, xbuf.at[slot], xsem.at[slot])

    def o_copy(g, slot):
        dst = ys_hbm.at[pl.ds(pl.multiple_of(g * CHUNK_ROWS, CHUNK_ROWS), CHUNK_ROWS), :]
        return pltpu.make_async_copy(obuf.at[slot], dst, osem.at[slot])

    @pl.when(e == 0)
    def _():
        for g in range(RING - 1):
            @pl.when(g < n_used)
            def _():
                x_copy(g, g).start()

    wg = wg_ref[0].astype(BF16)
    wu = wu_ref[0].astype(BF16)
    wd = wd_ref[0].astype(BF16)

    def chunk(c, carry):
        g = first_ref[e] + c
        slot = g & (RING - 1)
        x_copy(g, slot).wait()

        @pl.when(g + RING - 1 < n_used)
        def _():
            x_copy(g + RING - 1, (g + RING - 1) & (RING - 1)).start()

        @pl.when(g >= RING)
        def _():
            o_copy(g - RING, slot).wait()

        x = _load_row_tiles(xbuf.at[slot]).astype(BF16)
        gate = jnp.dot(x, wg, preferred_element_type=F32)
        up = jnp.dot(x, wu, preferred_element_type=F32)
        act = (gate * _sigmoid(gate)) * up
        _store_row_tiles(obuf.at[slot], jnp.dot(act.astype(BF16), wd, preferred_element_type=F32))
        o_copy(g, slot).start()
        return carry

    lax.fori_loop(0, count_ref[e], chunk, 0)

    @pl.when(e == N_EXPERTS - 1)
    def _():
        for back in range(RING, 0, -1):
            @pl.when(n_used >= back)
            def _():
                o_copy(n_used - back, (n_used - back) & (RING - 1)).wait()


def _ffn(first_chunk, chunk_count, n_used, xs, wg, wu, wd):
    weights = lambda a, b: pl.BlockSpec((1, a, b), lambda e, f, c, nu: (e, 0, 0))
    return pl.pallas_call(
        _ffn_kernel,
        grid_spec=pltpu.PrefetchScalarGridSpec(
            num_scalar_prefetch=3,
            grid=(N_EXPERTS,),
            in_specs=[pl.BlockSpec(memory_space=pl.ANY), weights(D_MODEL, EXPERT_FF), weights(D_MODEL, EXPERT_FF),
                      weights(EXPERT_FF, D_MODEL)],
            out_specs=pl.BlockSpec(memory_space=pl.ANY),
            scratch_shapes=[pltpu.VMEM((RING, CHUNK_ROWS, LANES), F32), pltpu.VMEM((RING, CHUNK_ROWS, LANES), F32),
                            pltpu.SemaphoreType.DMA((RING,)), pltpu.SemaphoreType.DMA((RING,))],
        ),
        out_shape=jax.ShapeDtypeStruct(xs.shape, F32),
        compiler_params=_cparams(1),
        name="expert_ffn",
    )(first_chunk, chunk_count, n_used, xs, wg, wu, wd)


def _combine_kernel(runs_ref, slot_ref, gate_ref, h_ref, ys_hbm, sg_ref, su_ref, sd_ref, l2g_ref, l2b_ref,
                    o1_ref, o2_ref, rows_buf, sem_rows, *, tc, alpha, first_tiles):
    i = pl.program_id(0)
    slot = i & 1
    buf = rows_buf.at[slot]

    @pl.when(i == 0)
    def _():
        _start_run_copies(runs_ref, i, buf, ys_hbm, sem_rows.at[slot], False)

    @pl.when(i + 1 < pl.num_programs(0))
    def _():
        _start_run_copies(runs_ref, i + 1, rows_buf.at[1 - slot], ys_hbm, sem_rows.at[1 - slot], False)

    h = h_ref[...]
    hb = h.astype(BF16)
    sgate = jnp.dot(hb, sg_ref[...], preferred_element_type=F32)
    sup = jnp.dot(hb, su_ref[...], preferred_element_type=F32)
    shared = jnp.dot(((sgate * _sigmoid(sgate)) * sup).astype(BF16), sd_ref[...], preferred_element_type=F32)
    slots = slot_ref[...]
    gate = gate_ref[...]
    slot_iota = lax.broadcasted_iota(I32, (tc, TOP_K * tc), 1)
    weights = jnp.zeros((tc, TOP_K * tc), F32)
    for j in range(TOP_K):
        weights = jnp.where(slot_iota == slots[:, j:j + 1], gate[:, j:j + 1], weights)
    w_hi = weights.astype(BF16)
    w_lo = (weights - w_hi.astype(F32)).astype(BF16)
    _wait_run_copies(buf, ys_hbm, sem_rows.at[slot])
    local = _load_row_tiles(buf).astype(BF16)
    routed = (jnp.dot(w_hi, local, preferred_element_type=F32) + jnp.dot(w_lo, local, preferred_element_type=F32))
    y = _layer_norm(alpha * h + (routed + shared), l2g_ref[...], l2b_ref[...])

    @pl.when(i < first_tiles)
    def _():
        o1_ref[...] = y

    @pl.when(i >= first_tiles)
    def _():
        o2_ref[...] = y


def _combine(runs, slots, gate, h, ys, sg, su, sd, l2g, l2b, tc, alpha, n_first):
    n = h.shape[0]
    assert n % tc == 0 and n_first % tc == 0 and 0 < n_first < n
    first_tiles = n_first // tc
    tok = lambda w: pl.BlockSpec((tc, w), lambda i, r: (i, 0))
    full = lambda a: pl.BlockSpec(a.shape, lambda i, r: (0, 0))
    return pl.pallas_call(
        functools.partial(_combine_kernel, tc=tc, alpha=alpha, first_tiles=first_tiles),
        grid_spec=pltpu.PrefetchScalarGridSpec(
            num_scalar_prefetch=1,
            grid=(n // tc,),
            in_specs=[tok(TOP_K), tok(TOP_K), tok(D_MODEL), pl.BlockSpec(memory_space=pl.ANY),
                      full(sg), full(su), full(sd), full(l2g), full(l2b)],
            out_specs=[pl.BlockSpec((tc, D_MODEL), lambda i, r: (jnp.minimum(i, first_tiles - 1), 0)),
                       pl.BlockSpec((tc, D_MODEL), lambda i, r: (jnp.maximum(i - first_tiles, 0), 0))],
            scratch_shapes=[pltpu.VMEM((2, TOP_K * tc * ROW_SUB, LANES), F32), pltpu.SemaphoreType.DMA((2,))],
        ),
        out_shape=[jax.ShapeDtypeStruct((n_first, D_MODEL), F32), jax.ShapeDtypeStruct((n - n_first, D_MODEL), F32)],
        compiler_params=_cparams(1),
        name="combine",
    )(runs, slots, gate, h, ys, sg, su, sd, l2g, l2b)


def _pad_cols(a, width):
    return jnp.pad(a, ((0, 0), (0, width - a.shape[1])))


def _pack_shift_cols(a):
    c = 3 * RWKV_W
    return jnp.concatenate([a[:, :c], _pad_cols(a[:, c:c + DECAY_LORA], LORA_PAD),
                            _pad_cols(a[:, c + DECAY_LORA:c + DECAY_LORA + AAA_LORA], LORA_PAD),
                            _pad_cols(a[:, c + DECAY_LORA + AAA_LORA:], GATE_PAD)], axis=1)


def _scan_to_lanes(a, b, t, split):
    a = a.reshape(b, t, SCAN_INPUTS, N_HEADS, HEAD).transpose(1, 2, 4, 0, 3)
    return a.reshape(t, SCAN_INPUTS, HEAD // split, split * b * N_HEADS)


def _values_from_lanes(y, b, t):
    return y.reshape(t, HEAD, b, N_HEADS).transpose(2, 0, 3, 1).reshape(b * t, RWKV_W)


def _state_to_lanes(s, b, split):
    s = s.reshape(b, N_HEADS, HEAD // split, split, HEAD).transpose(4, 2, 3, 0, 1)
    return s.reshape(HEAD, HEAD // split, split * b * N_HEADS)


def _state_from_lanes(s, b, split):
    s = s.reshape(HEAD, HEAD // split, split, b, N_HEADS).transpose(3, 4, 1, 2, 0)
    return s.reshape(b, N_HEADS, HEAD, HEAD)


def _mixer(z, zprev, x2, b, t, wkv0, pool_prev, mk, mv, pos0, wts, tiles, h_all):
    n = b * t
    tt, tb, seqs, att_tt = tiles
    z3 = z[:n].reshape(b, t, Z_W)
    scan, g, bonus = _rwkv_prep(z, zprev, b, t, tt, wts["prep"])

    split = 1 if (b * N_HEADS) % LANES == 0 else LANES // (b * N_HEADS)
    key_order = jnp.concatenate([jnp.arange(i, HEAD, split) for i in range(split)])
    s_in = wkv0[..., key_order] if split > 1 else wkv0
    y_l, s_l = _wkv(_scan_to_lanes(scan, b, t, split), _state_to_lanes(s_in, b, split), tb)
    y = _values_from_lanes(y_l, b, t)
    wkv_new = _state_from_lanes(s_l, b, split)
    if split > 1:
        wkv_new = wkv_new[..., jnp.argsort(key_order)]

    prev = jnp.concatenate([jnp.zeros((b, POOL_PAD - POOL_BUF, POOL_W), F32), pool_prev], axis=1)
    o_pool = _pool(prev, z3, wts["pool_w"], wts["pool_scale"], seqs, pos0).reshape(n, POOL_W)
    tail = z3[:, max(t - POOL_BUF, 0):, ZS_W:ZS_W + POOL_W]
    pool_new = jnp.concatenate([pool_prev, tail], axis=1)[:, -POOL_BUF:]

    o_mem = _memattn(z3, mk, mv, att_tt, seqs).reshape(n, MEM_W)
    h = _post(y, bonus, g, o_pool, o_mem, x2, *wts["post"], tt, wts["alpha"], h_all)
    return h, wkv_new, pool_new


def kernel(x_prompt, x_sample, mem_prompt, state_wkv, state_shift, state_pool, cache_mem_k, cache_mem_v, w_in, mu_shift, w0, w_up_decay, a0, w_up_aaa, w_up_gate, k_k, k_a, r_k, ln_x_w, ln_x_b, pool_w, pool_scale, mem_wk, mem_wv, w_out, ln1_g, ln1_b, router_w, router_b, exp_gate, exp_up, exp_down, sh_gate, sh_up, sh_down, ln2_g, ln2_b):
    depth = w_in.shape[0]
    assert depth == 1
    l = 0
    alpha = (2.0 * depth) ** 0.25
    bp, tp, d = x_prompt.shape
    bs, ts, _ = x_sample.shape
    n_p, n_s = bp * tp, bs * ts
    n = n_p + n_s

    w_in_p = jnp.concatenate([_pack_shift_cols(w_in[l][:, :1824]), w_in[l][:, 1824:]], axis=1).astype(BF16)
    row = lambda a: a.reshape(1, -1)
    pad_rows = lambda a, rows: jnp.pad(a, ((0, rows - a.shape[0]), (0, 0)))
    bd = jnp.kron(jnp.eye(N_HEADS, dtype=F32), jnp.ones((HEAD, HEAD), F32)).astype(BF16)
    pw = pool_w[l]
    pw_bd = jnp.zeros((POOL_W, POOL_W), F32)
    for gi in range(4):
        pw_bd = pw_bd.at[gi * 64:(gi + 1) * 64, gi * 64:(gi + 1) * 64].set(pw[gi])
    wts = {
        "prep": (_pack_shift_cols(row(mu_shift[l])), row(w0[l]), row(a0[l]), pad_rows(w_up_decay[l], LORA_PAD),
                 pad_rows(w_up_aaa[l], LORA_PAD), pad_rows(w_up_gate[l], GATE_PAD), row(k_k[l]), row(k_a[l]),
                 row(r_k[l]), bd),
        "pool_w": pw_bd.astype(BF16), "pool_scale": row(pool_scale[l]),
        "post": (w_out[l].astype(BF16), row(ln_x_w[l]), row(ln_x_b[l]), row(ln1_g[l]), row(ln1_b[l]), bd),
        "alpha": alpha,
    }

    xp2 = x_prompt.reshape(n_p, d)
    xs2 = x_sample.reshape(n_s, d)
    z_p = _matmul(xp2, w_in_p, _tile(n_p, 512))
    z_s = _matmul(jnp.concatenate([xs2, state_shift[l]], axis=0), w_in_p, _tile(n_s + bs, 512))
    mkv = _matmul(mem_prompt.reshape(bp * MEM_TOKENS, d),
                  jnp.concatenate([mem_wk[l], mem_wv[l]], axis=1).astype(BF16), _tile(bp * MEM_TOKENS, 512))
    mk_p = mkv[:, :MEM_W].reshape(bp, MEM_TOKENS, MEM_W)
    mv_p = mkv[:, MEM_W:].reshape(bp, MEM_TOKENS, MEM_W)

    h_p, wkv_p, pool_p = _mixer(z_p, jnp.zeros((bp, ZS_W), F32), xp2, bp, tp,
                                jnp.zeros((bp, N_HEADS, HEAD, HEAD), F32), jnp.zeros((bp, POOL_BUF, POOL_W), F32),
                                mk_p, mv_p, 0, wts,
                                (_tile(n_p, 256), _tile(tp, 32, 1), 1, _tile(tp, 256)), (n, 0))
    h, wkv_s, pool_s = _mixer(z_s, z_s[n_s:, :ZS_W], xs2, bs, ts, state_wkv[l], state_pool[l],
                              cache_mem_k[l].reshape(bs, MEM_TOKENS, MEM_W),
                              cache_mem_v[l].reshape(bs, MEM_TOKENS, MEM_W), PAST_LEN, wts,
                              (_tile(n_s, 256), _tile(ts, 32, 1), _tile(bs, 16, 1), _tile(ts, 256)), h_p)

    td = _tile(n, 256, LANES)
    slots, gate, run_tab, cnt = _router(h, router_w[l].T, router_b[l].reshape(N_EXPERTS, 1), td)
    counts = cnt[:, 0].astype(I32)
    padded = (counts + CHUNK - 1) // CHUNK * CHUNK
    pad_end = jnp.cumsum(padded)
    pad_start = pad_end - padded
    n_chunks = (n * TOP_K + N_EXPERTS * (CHUNK - 1) + CHUNK - 1) // CHUNK
    run_tab = run_tab.reshape(n // td, N_EXPERTS, LANES)[:, :, :RUN_FIELDS].astype(I32)
    runs = jnp.stack([run_tab[:, :, 1], pad_start[None, :] + run_tab[:, :, 2], run_tab[:, :, 0]], axis=1)
    runs = runs.reshape(n // td * RUN_FIELDS, N_EXPERTS) * ROW_SUB
    zpos = jnp.where(padded > 0, pad_end - CHUNK, -1).astype(I32)
    n_used = (pad_end[-1:] // CHUNK).astype(I32)
    xs_sorted = _dispatch(zpos, runs, slots, h, n_chunks * CHUNK, td)
    ys_sorted = _ffn((pad_start // CHUNK).astype(I32), (padded // CHUNK).astype(I32), n_used, xs_sorted,
                     exp_gate[l], exp_up[l], exp_down[l])
    y_p, y_s = _combine(runs, slots.T, gate.T, h, ys_sorted, sh_gate[l].astype(BF16), sh_up[l].astype(BF16),
                        sh_down[l].astype(BF16), row(ln2_g[l]), row(ln2_b[l]), td, alpha, n_p)

    return (y_p.reshape(bp, tp, d), y_s.reshape(bs, ts, d),
            wkv_p[None], x_prompt[:, -1][None], pool_p[None],
            mk_p.reshape(bp, MEM_TOKENS, MEM_HEADS, HEAD)[None], mv_p.reshape(bp, MEM_TOKENS, MEM_HEADS, HEAD)[None],
            wkv_s[None], x_sample[:, -1][None], pool_s[None])
```

```python
import functools

import jax
import jax.numpy as jnp
from jax import lax
from jax.experimental import pallas as pl
from jax.experimental.pallas import tpu as pltpu

F32 = jnp.float32
BF16 = jnp.bfloat16
I32 = jnp.int32

D_MODEL = 1024
HEAD = 64
N_HEADS = 8
RWKV_W = 512
POOL_W = 256
MEM_W = 256
MEM_HEADS = 4
MEM_TOKENS = 256
POOL_BUF = 15
POOL_PAD = 16
DECAY_LORA = 64
AAA_LORA = 64
GATE_LORA = 160
LORA_PAD = 128
GATE_PAD = 256
ZS_W = 3 * RWKV_W + 2 * LORA_PAD + GATE_PAD
Z_W = ZS_W + POOL_W + MEM_W
N_EXPERTS = 256
TOP_K = 8
N_GROUPS = 8
GROUP_SIZE = N_EXPERTS // N_GROUPS
TOPK_GROUPS = 4
EXPERT_FF = 256
ROUTED_SCALE = 2.5
CHUNK = 256
PAST_LEN = 16384
LN_EPS = 1e-5
GN_EPS = 64e-5
LANES = 128
SUBLANES = 8
SCAN_INPUTS = 6
VMEM_LIMIT = 56 * 1024 * 1024


def _cparams(n_axes):
    return pltpu.CompilerParams(dimension_semantics=("arbitrary",) * n_axes, vmem_limit_bytes=VMEM_LIMIT)


def _tile(n, preferred, multiple=8):
    best = None
    for c in range(multiple, min(n, preferred) + 1, multiple):
        if n % c == 0:
            best = c
    assert best is not None, (n, preferred, multiple)
    return best


def _sigmoid(x):
    return 1.0 / (1.0 + jnp.exp(-x))


def _matmul_kernel(x_ref, w_ref, o_ref):
    o_ref[...] = jnp.dot(x_ref[...].astype(BF16), w_ref[...], preferred_element_type=F32)


def _matmul(x, w, tm):
    m, k = x.shape
    n = w.shape[1]
    assert m % tm == 0
    return pl.pallas_call(
        _matmul_kernel,
        grid=(m // tm,),
        in_specs=[pl.BlockSpec((tm, k), lambda i: (i, 0)), pl.BlockSpec((k, n), lambda i: (0, 0))],
        out_specs=pl.BlockSpec((tm, n), lambda i: (i, 0)),
        out_shape=jax.ShapeDtypeStruct((m, n), F32),
        compiler_params=_cparams(1),
        name="matmul",
    )(x, w)


def _split(x):
    hi = x.astype(BF16)
    return hi, (x - hi.astype(F32)).astype(BF16)


def _dot_f32(a, b, contract=((1,), (0,))):
    a_hi, a_lo = _split(a)
    b_hi, b_lo = _split(b)
    dot = lambda x, y: lax.dot_general(x, y, (contract, ((), ())), preferred_element_type=F32)
    return dot(a_hi, b_hi) + (dot(a_hi, b_lo) + dot(a_lo, b_hi))


def _segsum(x, bd):
    hi = x.astype(BF16)
    lo = (x - hi.astype(F32)).astype(BF16)
    return jnp.dot(hi, bd, preferred_element_type=F32) + jnp.dot(lo, bd, preferred_element_type=F32)


def _prep_kernel(zs_ref, zp_ref, mu_ref, w0_ref, a0_ref, wd_ref, wa_ref, wg_ref, kk_ref, ka_ref, rk_ref, bd_ref,
                 scan_o, g_o, bonus_o, carry_ref, *, tiles_per_seq, seq_len):
    zs = zs_ref[...]
    rolled = pltpu.roll(zs, 1, 0)
    row = lax.broadcasted_iota(I32, zs.shape, 0)
    if tiles_per_seq:
        i = pl.program_id(0)

        @pl.when(i == 0)
        def _():
            carry_ref[...] = jnp.zeros_like(carry_ref)

        prev = jnp.where(i % tiles_per_seq == 0, zp_ref[0], carry_ref[...])
        shifted = jnp.where(row == 0, prev, rolled)
        carry_ref[...] = zs[zs.shape[0] - 1:, :]
    else:
        shifted = jnp.where((row & (seq_len - 1)) == 0, zp_ref[...], rolled)
    zm = zs + mu_ref[...] * (shifted - zs)
    r = zm[:, 0:RWKV_W]
    k = zm[:, RWKV_W:2 * RWKV_W]
    v = zm[:, 2 * RWKV_W:3 * RWKV_W]
    c0 = 3 * RWKV_W
    xw = zm[:, c0:c0 + LORA_PAD]
    xa = zm[:, c0 + LORA_PAD:c0 + 2 * LORA_PAD]
    xg = zm[:, c0 + 2 * LORA_PAD:c0 + 2 * LORA_PAD + GATE_PAD]
    bd = bd_ref[...]
    u = -(w0_ref[...] + _dot_f32(jnp.tanh(xw), wd_ref[...]))
    softplus = jnp.maximum(u, 0.0) + jnp.log(1.0 + jnp.exp(-jnp.abs(u)))
    decay = jnp.exp(-jnp.exp(-softplus - 0.5))
    a = _sigmoid(a0_ref[...] + _dot_f32(xa, wa_ref[...]))
    g = _dot_f32(_sigmoid(xg), wg_ref[...])
    kk = k * kk_ref[...]
    kk = kk / jnp.maximum(jnp.sqrt(_segsum(kk * kk, bd)), 1e-12)
    kp = k * (1.0 + (a - 1.0) * ka_ref[...])
    for slot, value in enumerate((r, decay, kp, kk, -(kk * a), v)):
        scan_o[:, slot * RWKV_W:(slot + 1) * RWKV_W] = value
    g_o[...] = g
    bonus_o[...] = _segsum(r * kp * rk_ref[...], bd) * v


def _rwkv_prep(z, zprev, b, t, tt, params):
    n_tok = b * t
    assert n_tok % tt == 0
    row = lambda w: pl.BlockSpec((1, w), lambda i: (0, 0))
    full = lambda a: pl.BlockSpec(a.shape, lambda i: (0, 0))
    tok = pl.BlockSpec((tt, RWKV_W), lambda i: (i, 0))
    mu, w0, a0, wd, wa, wg, k_k, k_a, r_k, bd = params
    if t % tt == 0:
        tiles_per_seq = t // tt
        zp = zprev.reshape(b, 1, ZS_W)
        zp_spec = pl.BlockSpec((1, 1, ZS_W), lambda i: (i // tiles_per_seq, 0, 0))
    else:
        assert tt % t == 0 and t & (t - 1) == 0
        tiles_per_seq = 0
        zp = jnp.repeat(zprev, t, axis=0)
        zp_spec = pl.BlockSpec((tt, ZS_W), lambda i: (i, 0))
    return pl.pallas_call(
        functools.partial(_prep_kernel, tiles_per_seq=tiles_per_seq, seq_len=t),
        grid=(n_tok // tt,),
        in_specs=[pl.BlockSpec((tt, ZS_W), lambda i: (i, 0)), zp_spec,
                  row(ZS_W), row(RWKV_W), row(RWKV_W), full(wd), full(wa), full(wg),
                  row(RWKV_W), row(RWKV_W), row(RWKV_W), full(bd)],
        out_specs=[pl.BlockSpec((tt, SCAN_INPUTS * RWKV_W), lambda i: (i, 0)), tok, tok],
        out_shape=[jax.ShapeDtypeStruct((n_tok, SCAN_INPUTS * RWKV_W), F32),
                   jax.ShapeDtypeStruct((n_tok, RWKV_W), F32), jax.ShapeDtypeStruct((n_tok, RWKV_W), F32)],
        scratch_shapes=[pltpu.VMEM((1, ZS_W), F32)],
        compiler_params=_cparams(1),
        name="rwkv_prep",
    )(z, zp, mu, w0, a0, wd, wa, wg, k_k, k_a, r_k, bd)


def _wkv_kernel(r_ref, w_ref, k_ref, kk_ref, nb_ref, v_ref, s0_ref, y_ref, s_ref, *scratch, tb, dup):
    slabs = s_ref.shape[1]
    key_unroll = 16

    @pl.when(pl.program_id(1) == 0)
    def _():
        s_ref[...] = s0_ref[...]

    if dup:
        for src, dst in zip((r_ref, w_ref, k_ref, kk_ref, nb_ref), scratch):
            x = src[...].reshape(tb * (HEAD // 2), LANES)
            swapped = pltpu.roll(x, LANES // 2, 1)
            low = lax.broadcasted_iota(I32, x.shape, 1) < LANES // 2
            dst[:, 0:HEAD // 2, :] = jnp.where(low, x, swapped).reshape(tb, HEAD // 2, LANES)
            dst[:, HEAD // 2:, :] = jnp.where(low, swapped, x).reshape(tb, HEAD // 2, LANES)
        r_ref, w_ref, k_ref, kk_ref, nb_ref = scratch

    def rows(ref, t, k):
        return jnp.broadcast_to(ref[t, pl.ds(k, 1), :], (SUBLANES, LANES))

    def step(t, sa, with_next):
        vt = [v_ref[t, i * SUBLANES:(i + 1) * SUBLANES, :] for i in range(slabs)]
        zero = jnp.zeros((SUBLANES, LANES), F32)

        def key(k, acc):
            y_acc, next_acc = list(acc[0]), list(acc[1])
            wb, nbb, kb, rb = rows(w_ref, t, k), rows(nb_ref, t, k), rows(k_ref, t, k), rows(r_ref, t, k)
            kkb = rows(kk_ref, t + 1, k) if with_next else None
            for i in range(slabs):
                sn = s_ref[k, i] * wb + sa[i] * nbb + vt[i] * kb
                s_ref[k, i] = sn
                y_acc[i] = y_acc[i] + sn * rb
                if with_next:
                    next_acc[i] = next_acc[i] + sn * kkb
            return tuple(y_acc), tuple(next_acc)

        y_acc, next_acc = lax.fori_loop(0, HEAD, key, ((zero,) * slabs, (zero,) * slabs), unroll=key_unroll)
        for i, y in enumerate(y_acc):
            y_ref[t, i * SUBLANES:(i + 1) * SUBLANES, :] = y
        return next_acc

    def first(k, acc):
        kkb = rows(kk_ref, 0, k)
        return tuple(a + s_ref[k, i] * kkb for i, a in enumerate(acc))

    sa0 = lax.fori_loop(0, HEAD, first, (jnp.zeros((SUBLANES, LANES), F32),) * slabs, unroll=key_unroll)
    sa_last = lax.fori_loop(0, tb - 1, lambda t, sa: step(t, sa, True), sa0)
    step(tb - 1, sa_last, False)


def _wkv(scan, s0, tb):
    t, _, rows, l = scan.shape
    dup = rows != HEAD
    assert t % tb == 0 and l % LANES == 0 and rows % SUBLANES == 0 and (not dup or (l == LANES and 2 * rows == HEAD))
    vec = lambda a: pl.BlockSpec((tb, None, rows, LANES), lambda g, i: (i, a, 0, g))
    slabs = rows // SUBLANES
    st = pl.BlockSpec((HEAD, slabs, SUBLANES, LANES), lambda g, i: (0, 0, 0, g))
    s0 = s0.reshape(HEAD, slabs, SUBLANES, l)
    y, s_new = pl.pallas_call(
        functools.partial(_wkv_kernel, tb=tb, dup=dup),
        grid=(l // LANES, t // tb),
        in_specs=[vec(a) for a in range(SCAN_INPUTS)] + [st],
        out_specs=[pl.BlockSpec((tb, rows, LANES), lambda g, i: (i, 0, g)), st],
        out_shape=[jax.ShapeDtypeStruct((t, rows, l), F32), jax.ShapeDtypeStruct(s0.shape, F32)],
        scratch_shapes=[pltpu.VMEM((tb, HEAD, LANES), F32)] * 5 if dup else [],
        compiler_params=_cparams(2),
        name="wkv",
    )(*([scan] * SCAN_INPUTS), s0)
    return y, s_new.reshape(HEAD, rows, l)


def _pool_kernel(prev_ref, zp_ref, pw_ref, ps_ref, o_ref, *, t, pos0):
    bb = zp_ref.shape[0]
    lane = lax.broadcasted_iota(I32, (t, POOL_W), 1)
    window = jnp.where(lane < 64, 2, jnp.where(lane < 128, 4, jnp.where(lane < 192, 8, 16)))
    pos = lax.broadcasted_iota(I32, (t, POOL_W), 0) + (pos0 + 1)
    cnt = jnp.minimum(pos, window).astype(F32)
    diffs = []
    for i in range(bb):
        f = jnp.concatenate([prev_ref[i], zp_ref[i]], axis=0)
        s2 = f + pltpu.roll(f, 1, 0)
        s4 = s2 + pltpu.roll(s2, 2, 0)
        s8 = s4 + pltpu.roll(s4, 4, 0)
        s16 = s8 + pltpu.roll(s8, 8, 0)
        wsum = jnp.where(lane < 64, s2[POOL_PAD:], jnp.where(lane < 128, s4[POOL_PAD:],
                                                            jnp.where(lane < 192, s8[POOL_PAD:], s16[POOL_PAD:])))
        diffs.append(wsum / cnt - f[POOL_PAD:])
    diff = diffs[0] if bb == 1 else jnp.concatenate(diffs, axis=0)
    out = jnp.dot(diff.astype(BF16), pw_ref[...], preferred_element_type=F32) * ps_ref[...]
    for i in range(bb):
        o_ref[i] = out[i * t:(i + 1) * t]


def _pool(prev, z3, pw_bd, pscale, bb, pos0):
    b, t, _ = z3.shape
    assert b % bb == 0
    return pl.pallas_call(
        functools.partial(_pool_kernel, t=t, pos0=pos0),
        grid=(b // bb,),
        in_specs=[pl.BlockSpec((bb, POOL_PAD, POOL_W), lambda i: (i, 0, 0)),
                  pl.BlockSpec((bb, t, POOL_W), lambda i: (i, 0, ZS_W // POOL_W)),
                  pl.BlockSpec((POOL_W, POOL_W), lambda i: (0, 0)), pl.BlockSpec((1, POOL_W), lambda i: (0, 0))],
        out_specs=pl.BlockSpec((bb, t, POOL_W), lambda i: (i, 0, 0)),
        out_shape=jax.ShapeDtypeStruct((b, t, POOL_W), F32),
        compiler_params=_cparams(1),
        name="pool",
    )(prev, z3, pw_bd, pscale)


def _memattn_kernel(q_ref, k_ref, v_ref, o_ref):
    for i in range(q_ref.shape[0]):
        q = q_ref[i].astype(BF16)
        kf = k_ref[i]
        vf = v_ref[i]
        head_of_lane = lax.broadcasted_iota(I32, kf.shape, 1) // (MEM_W // MEM_HEADS)
        out = None
        for h in range(MEM_HEADS):
            kh = jnp.where(head_of_lane == h, kf, 0.0).astype(BF16)
            vh = jnp.where(head_of_lane == h, vf, 0.0).astype(BF16)
            s = lax.dot_general(q, kh, (((1,), (1,)), ((), ())), preferred_element_type=F32) * (64 ** -0.5)
            e = jnp.exp(s - jnp.max(s, axis=-1, keepdims=True))
            p = e / jnp.sum(e, axis=-1, keepdims=True)
            o = jnp.dot(p.astype(BF16), vh, preferred_element_type=F32)
            out = o if out is None else out + o
        o_ref[i] = out


def _memattn(z3, mk, mv, tt, bb):
    b, t, _ = z3.shape
    assert t % tt == 0 and b % bb == 0
    qcol = (Z_W - MEM_W) // MEM_W
    kv = pl.BlockSpec((bb, MEM_TOKENS, MEM_W), lambda i, j: (i, 0, 0))
    return pl.pallas_call(
        _memattn_kernel,
        grid=(b // bb, t // tt),
        in_specs=[pl.BlockSpec((bb, tt, MEM_W), lambda i, j: (i, j, qcol)), kv, kv],
        out_specs=pl.BlockSpec((bb, tt, MEM_W), lambda i, j: (i, j, 0)),
        out_shape=jax.ShapeDtypeStruct((b, t, MEM_W), F32),
        compiler_params=_cparams(2),
        name="mem_attn",
    )(z3, mk, mv)


def _layer_norm(x, g, b):
    mu = jnp.mean(x, axis=-1, keepdims=True)
    xc = x - mu
    var = jnp.mean(xc * xc, axis=-1, keepdims=True)
    return xc * lax.rsqrt(var + LN_EPS) * g + b


def _post_kernel(y_ref, bonus_ref, g_ref, op_ref, om_ref, x_ref, wo_ref, lxw_ref, lxb_ref, l1g_ref, l1b_ref, bd_ref,
                 *rest, alpha):
    h_ref = rest[-1]
    y = y_ref[...]
    bd = bd_ref[...]
    mu = _segsum(y, bd) * (1.0 / HEAD)
    yc = y - mu
    var = _segsum(yc * yc, bd) * (1.0 / HEAD)
    yn = yc * lax.rsqrt(var + GN_EPS) * lxw_ref[...] + lxb_ref[...]
    o_rwkv = (yn + bonus_ref[...]) * g_ref[...]
    mixed = (jnp.dot(o_rwkv.astype(BF16), wo_ref[0:RWKV_W, :], preferred_element_type=F32)
             + jnp.dot(op_ref[...].astype(BF16), wo_ref[RWKV_W:RWKV_W + POOL_W, :], preferred_element_type=F32)
             + jnp.dot(om_ref[...].astype(BF16), wo_ref[RWKV_W + POOL_W:, :], preferred_element_type=F32))
    h_ref[...] = _layer_norm(alpha * x_ref[...] + mixed, l1g_ref[...], l1b_ref[...])


def _post(y, bonus, g, o_pool, o_mem, x, wo, lxw, lxb, l1g, l1b, bd, tt, alpha, h_all):
    n = y.shape[0]
    assert n % tt == 0
    tok = lambda w: pl.BlockSpec((tt, w), lambda i: (i, 0))
    row = lambda w: pl.BlockSpec((1, w), lambda i: (0, 0))
    full = lambda a: pl.BlockSpec(a.shape, lambda i: (0, 0))
    in_specs = [tok(RWKV_W), tok(RWKV_W), tok(RWKV_W), tok(POOL_W), tok(MEM_W), tok(D_MODEL), full(wo),
                row(RWKV_W), row(RWKV_W), row(D_MODEL), row(D_MODEL), full(bd)]
    args = [y, bonus, g, o_pool, o_mem, x, wo, lxw, lxb, l1g, l1b, bd]
    if isinstance(h_all, tuple):
        n_total, first = h_all
        aliases = {}
    else:
        n_total, first = h_all.shape[0], h_all.shape[0] - n
        in_specs.append(pl.BlockSpec(memory_space=pl.ANY))
        args.append(h_all)
        aliases = {len(args) - 1: 0}
    assert first % tt == 0
    return pl.pallas_call(
        functools.partial(_post_kernel, alpha=alpha),
        grid=(n // tt,),
        in_specs=in_specs,
        out_specs=pl.BlockSpec((tt, D_MODEL), lambda i: (i + first // tt, 0)),
        out_shape=jax.ShapeDtypeStruct((n_total, D_MODEL), F32),
        input_output_aliases=aliases,
        compiler_params=_cparams(1),
        name="post",
    )(*args)


def _router_kernel(h_ref, rwt_ref, bias_ref, tri_ref, below_ref, slot_o, gate_o, run_o, cnt_o, carry_ref):
    @pl.when(pl.program_id(0) == 0)
    def _():
        carry_ref[...] = jnp.zeros_like(carry_ref)

    neg = -jnp.inf
    logits = _dot_f32(rwt_ref[...], h_ref[...], ((1,), (1,)))
    scores = _sigmoid(logits)
    sel = scores + bias_ref[...]
    tt = sel.shape[1]
    gio = lax.broadcasted_iota(I32, (GROUP_SIZE, tt), 0).astype(F32)
    blocks, gscore = [], []
    for g in range(N_GROUPS):
        blk = sel[g * GROUP_SIZE:(g + 1) * GROUP_SIZE, :]
        m1 = jnp.max(blk, axis=0, keepdims=True)
        first = jnp.min(jnp.where(blk == m1, gio, float(GROUP_SIZE)), axis=0, keepdims=True)
        m2 = jnp.max(jnp.where(gio == first, neg, blk), axis=0, keepdims=True)
        blocks.append(blk)
        gscore.append(m1 + m2)
    masked = []
    for g in range(N_GROUPS):
        beaten_by = jnp.zeros((1, tt), F32)
        for g2 in range(N_GROUPS):
            if g2 != g:
                wins = (gscore[g2] >= gscore[g]) if g2 < g else (gscore[g2] > gscore[g])
                beaten_by = beaten_by + jnp.where(wins, 1.0, 0.0)
        masked.append(jnp.where(beaten_by < TOPK_GROUPS, blocks[g], neg))
    msel = jnp.concatenate(masked, axis=0)
    eio = lax.broadcasted_iota(I32, msel.shape, 0).astype(F32)
    chosen = jnp.zeros(msel.shape, F32)
    idxs, scs = [], []
    for _ in range(TOP_K):
        m = jnp.max(msel, axis=0, keepdims=True)
        first = jnp.min(jnp.where(msel == m, eio, float(N_EXPERTS)), axis=0, keepdims=True)
        hit = eio == first
        scs.append(jnp.sum(jnp.where(hit, scores, 0.0), axis=0, keepdims=True))
        msel = jnp.where(hit, neg, msel)
        chosen = jnp.where(hit, 1.0, chosen)
        idxs.append(first)
    total = scs[0]
    for s in scs[1:]:
        total = total + s
    chosen_b = chosen.astype(BF16)
    earlier = jnp.dot(chosen_b, tri_ref[...], preferred_element_type=F32)
    smaller = jnp.dot(below_ref[...], chosen_b, preferred_element_type=F32)
    run_len = jnp.sum(chosen, axis=1, keepdims=True)
    run_off = jnp.sum(smaller, axis=1, keepdims=True)
    slot_of = run_off + earlier
    slots = [jnp.sum(jnp.where(eio == i, slot_of, 0.0), axis=0, keepdims=True) for i in idxs]
    slot_o[...] = jnp.concatenate(slots, axis=0).astype(I32)
    gate_o[...] = jnp.concatenate([s / total * ROUTED_SCALE for s in scs], axis=0)
    lane = lax.broadcasted_iota(I32, run_o.shape, 1)
    run_o[...] = jnp.where(lane == 0, run_len, jnp.where(lane == 1, run_off, jnp.where(lane == 2, carry_ref[...], 0.0)))
    carry_ref[...] = carry_ref[...] + run_len
    cnt_o[...] = jnp.broadcast_to(carry_ref[...], cnt_o.shape)


def _router(h, rwt, bias, tt):
    n = h.shape[0]
    assert n % tt == 0
    tri = (lax.broadcasted_iota(I32, (tt, tt), 0) < lax.broadcasted_iota(I32, (tt, tt), 1)).astype(BF16)
    below = (lax.broadcasted_iota(I32, (N_EXPERTS, N_EXPERTS), 1)
             < lax.broadcasted_iota(I32, (N_EXPERTS, N_EXPERTS), 0)).astype(BF16)
    tokT = pl.BlockSpec((TOP_K, tt), lambda i: (0, i))
    return pl.pallas_call(
        _router_kernel,
        grid=(n // tt,),
        in_specs=[pl.BlockSpec((tt, D_MODEL), lambda i: (i, 0)), pl.BlockSpec((N_EXPERTS, D_MODEL), lambda i: (0, 0)),
                  pl.BlockSpec((N_EXPERTS, 1), lambda i: (0, 0)), pl.BlockSpec((tt, tt), lambda i: (0, 0)),
                  pl.BlockSpec((N_EXPERTS, N_EXPERTS), lambda i: (0, 0))],
        out_specs=[tokT, tokT, pl.BlockSpec((N_EXPERTS, LANES), lambda i: (i, 0)),
                   pl.BlockSpec((N_EXPERTS, LANES), lambda i: (0, 0))],
        out_shape=[jax.ShapeDtypeStruct((TOP_K, n), I32), jax.ShapeDtypeStruct((TOP_K, n), F32),
                   jax.ShapeDtypeStruct((n // tt * N_EXPERTS, LANES), F32),
                   jax.ShapeDtypeStruct((N_EXPERTS, LANES), F32)],
        scratch_shapes=[pltpu.VMEM((N_EXPERTS, 1), F32)],
        compiler_params=_cparams(1),
        name="router",
    )(h, rwt, bias, tri, below)


ROW_SUB = D_MODEL // LANES


def _store_row_tiles(ref, x):
    rows = x.shape[0]
    for c in range(ROW_SUB):
        ref[pl.ds(c, rows, stride=ROW_SUB), :] = x[:, c * LANES:(c + 1) * LANES]


def _load_row_tiles(ref):
    rows = ref.shape[0] // ROW_SUB
    return jnp.concatenate([ref[pl.ds(c, rows, stride=ROW_SUB), :] for c in range(ROW_SUB)], axis=-1)


RUN_FIELDS = 3
RUN_UNROLL = 8


def _start_run_copies(runs_ref, tile, local_ref, sorted_ref, sem, to_sorted):
    def body(pair, c):
        for priority in range(2):
            e = 2 * pair + priority
            size = pl.multiple_of(runs_ref[tile * RUN_FIELDS + 2, e], ROW_SUB)
            local = local_ref.at[pl.ds(pl.multiple_of(runs_ref[tile * RUN_FIELDS, e], ROW_SUB), size), :]
            remote = sorted_ref.at[pl.ds(pl.multiple_of(runs_ref[tile * RUN_FIELDS + 1, e], ROW_SUB), size), :]
            copy = pltpu.make_async_copy(local, remote, sem) if to_sorted else pltpu.make_async_copy(remote, local, sem)
            copy.start(priority=priority)
        return c

    lax.fori_loop(0, N_EXPERTS // 2, body, 0, unroll=RUN_UNROLL // 2)


def _wait_run_copies(local_ref, sorted_ref, sem):
    pltpu.make_async_copy(sorted_ref.at[pl.ds(0, local_ref.shape[0]), :], local_ref, sem).wait()


def _dispatch_kernel(zpos_ref, runs_ref, slot_ref, h_ref, xs_hbm, zero_buf, rows_buf, sem_zero, sem_rows, *, td):
    i = pl.program_id(0)

    def zero_copy(e):
        start = pl.multiple_of(zpos_ref[e] * ROW_SUB, CHUNK * ROW_SUB)
        return pltpu.make_async_copy(zero_buf, xs_hbm.at[pl.ds(start, CHUNK * ROW_SUB), :], sem_zero)

    @pl.when(i == 0)
    def _():
        zero_buf[...] = jnp.zeros_like(zero_buf)

        def start(e, c):
            @pl.when(zpos_ref[e] >= 0)
            def _():
                zero_copy(e).start()
            return c

        def wait(e, c):
            @pl.when(zpos_ref[e] >= 0)
            def _():
                zero_copy(e).wait()
            return c

        lax.fori_loop(0, N_EXPERTS, start, 0)
        lax.fori_loop(0, N_EXPERTS, wait, 0)

    slots = slot_ref[...]
    slot_iota = lax.broadcasted_iota(I32, (TOP_K * td, td), 0)
    select = jnp.zeros((TOP_K * td, td), F32)
    for j in range(TOP_K):
        select = jnp.where(slot_iota == slots[j:j + 1, :], 1.0, select)
    local = jnp.dot(select.astype(BF16), h_ref[...].astype(BF16), preferred_element_type=F32)
    _store_row_tiles(rows_buf, local)
    _start_run_copies(runs_ref, i, rows_buf, xs_hbm, sem_rows, True)
    _wait_run_copies(rows_buf, xs_hbm, sem_rows)


def _dispatch(zpos, runs, slots, h, n_rows, td):
    n = h.shape[0]
    assert n % td == 0
    return pl.pallas_call(
        functools.partial(_dispatch_kernel, td=td),
        grid_spec=pltpu.PrefetchScalarGridSpec(
            num_scalar_prefetch=2,
            grid=(n // td,),
            in_specs=[pl.BlockSpec((TOP_K, td), lambda i, z, r: (0, i)),
                      pl.BlockSpec((td, D_MODEL), lambda i, z, r: (i, 0))],
            out_specs=pl.BlockSpec(memory_space=pl.ANY),
            scratch_shapes=[pltpu.VMEM((CHUNK * ROW_SUB, LANES), F32), pltpu.VMEM((TOP_K * td * ROW_SUB, LANES), F32),
                            pltpu.SemaphoreType.DMA, pltpu.SemaphoreType.DMA],
        ),
        out_shape=jax.ShapeDtypeStruct((n_rows * ROW_SUB, LANES), F32),
        compiler_params=_cparams(1),
        name="dispatch",
    )(zpos, runs, slots, h)


CHUNK_ROWS = CHUNK * ROW_SUB
RING = 4


def _ffn_kernel(first_ref, count_ref, nu_ref, xs_hbm, wg_ref, wu_ref, wd_ref, ys_hbm, xbuf, obuf, xsem, osem):
    e = pl.program_id(0)
    n_used = nu_ref[0]

    def x_copy(g, slot):
        src = xs_hbm.at[pl.ds(pl.multiple_of(g * CHUNK_ROWS, CHUNK_ROWS), CHUNK_ROWS), :]
        return pltpu.make_async_copy(src, xbuf.at[slot], xsem.at[slot])

    def o_copy(g, slot):
        dst = ys_hbm.at[pl.ds(pl.multiple_of(g * CHUNK_ROWS, CHUNK_ROWS), CHUNK_ROWS), :]
        return pltpu.make_async_copy(obuf.at[slot], dst, osem.at[slot])

    @pl.when(e == 0)
    def _():
        for g in range(RING - 1):
            @pl.when(g < n_used)
            def _():
                x_copy(g, g).start()

    wg = wg_ref[0].astype(BF16)
    wu = wu_ref[0].astype(BF16)
    wd = wd_ref[0].astype(BF16)

    def chunk(c, carry):
        g = first_ref[e] + c
        slot = g & (RING - 1)
        x_copy(g, slot).wait()

        @pl.when(g + RING - 1 < n_used)
        def _():
            x_copy(g + RING - 1, (g + RING - 1) & (RING - 1)).start()

        @pl.when(g >= RING)
        def _():
            o_copy(g - RING, slot).wait()

        x = _load_row_tiles(xbuf.at[slot]).astype(BF16)
        gate = jnp.dot(x, wg, preferred_element_type=F32)
        up = jnp.dot(x, wu, preferred_element_type=F32)
        act = (gate * _sigmoid(gate)) * up
        _store_row_tiles(obuf.at[slot], jnp.dot(act.astype(BF16), wd, preferred_element_type=F32))
        o_copy(g, slot).start()
        return carry

    lax.fori_loop(0, count_ref[e], chunk, 0)

    @pl.when(e == N_EXPERTS - 1)
    def _():
        for back in range(RING, 0, -1):
            @pl.when(n_used >= back)
            def _():
                o_copy(n_used - back, (n_used - back) & (RING - 1)).wait()


def _ffn(first_chunk, chunk_count, n_used, xs, wg, wu, wd):
    weights = lambda a, b: pl.BlockSpec((1, a, b), lambda e, f, c, nu: (e, 0, 0))
    return pl.pallas_call(
        _ffn_kernel,
        grid_spec=pltpu.PrefetchScalarGridSpec(
            num_scalar_prefetch=3,
            grid=(N_EXPERTS,),
            in_specs=[pl.BlockSpec(memory_space=pl.ANY), weights(D_MODEL, EXPERT_FF), weights(D_MODEL, EXPERT_FF),
                      weights(EXPERT_FF, D_MODEL)],
            out_specs=pl.BlockSpec(memory_space=pl.ANY),
            scratch_shapes=[pltpu.VMEM((RING, CHUNK_ROWS, LANES), F32), pltpu.VMEM((RING, CHUNK_ROWS, LANES), F32),
                            pltpu.SemaphoreType.DMA((RING,)), pltpu.SemaphoreType.DMA((RING,))],
        ),
        out_shape=jax.ShapeDtypeStruct(xs.shape, F32),
        compiler_params=_cparams(1),
        name="expert_ffn",
    )(first_chunk, chunk_count, n_used, xs, wg, wu, wd)


def _combine_kernel(runs_ref, slot_ref, gate_ref, h_ref, ys_hbm, sg_ref, su_ref, sd_ref, l2g_ref, l2b_ref,
                    o1_ref, o2_ref, rows_buf, sem_rows, *, tc, alpha, first_tiles):
    i = pl.program_id(0)
    slot = i & 1
    buf = rows_buf.at[slot]

    @pl.when(i == 0)
    def _():
        _start_run_copies(runs_ref, i, buf, ys_hbm, sem_rows.at[slot], False)

    @pl.when(i + 1 < pl.num_programs(0))
    def _():
        _start_run_copies(runs_ref, i + 1, rows_buf.at[1 - slot], ys_hbm, sem_rows.at[1 - slot], False)

    h = h_ref[...]
    hb = h.astype(BF16)
    sgate = jnp.dot(hb, sg_ref[...], preferred_element_type=F32)
    sup = jnp.dot(hb, su_ref[...], preferred_element_type=F32)
    shared = jnp.dot(((sgate * _sigmoid(sgate)) * sup).astype(BF16), sd_ref[...], preferred_element_type=F32)
    slots = slot_ref[...]
    gate = gate_ref[...]
    slot_iota = lax.broadcasted_iota(I32, (tc, TOP_K * tc), 1)
    weights = jnp.zeros((tc, TOP_K * tc), F32)
    for j in range(TOP_K):
        weights = jnp.where(slot_iota == slots[:, j:j + 1], gate[:, j:j + 1], weights)
    w_hi = weights.astype(BF16)
    w_lo = (weights - w_hi.astype(F32)).astype(BF16)
    _wait_run_copies(buf, ys_hbm, sem_rows.at[slot])
    local = _load_row_tiles(buf).astype(BF16)
    routed = (jnp.dot(w_hi, local, preferred_element_type=F32) + jnp.dot(w_lo, local, preferred_element_type=F32))
    y = _layer_norm(alpha * h + (routed + shared), l2g_ref[...], l2b_ref[...])

    @pl.when(i < first_tiles)
    def _():
        o1_ref[...] = y

    @pl.when(i >= first_tiles)
    def _():
        o2_ref[...] = y


def _combine(runs, slots, gate, h, ys, sg, su, sd, l2g, l2b, tc, alpha, n_first):
    n = h.shape[0]
    assert n % tc == 0 and n_first % tc == 0 and 0 < n_first < n
    first_tiles = n_first // tc
    tok = lambda w: pl.BlockSpec((tc, w), lambda i, r: (i, 0))
    full = lambda a: pl.BlockSpec(a.shape, lambda i, r: (0, 0))
    return pl.pallas_call(
        functools.partial(_combine_kernel, tc=tc, alpha=alpha, first_tiles=first_tiles),
        grid_spec=pltpu.PrefetchScalarGridSpec(
            num_scalar_prefetch=1,
            grid=(n // tc,),
            in_specs=[tok(TOP_K), tok(TOP_K), tok(D_MODEL), pl.BlockSpec(memory_space=pl.ANY),
                      full(sg), full(su), full(sd), full(l2g), full(l2b)],
            out_specs=[pl.BlockSpec((tc, D_MODEL), lambda i, r: (jnp.minimum(i, first_tiles - 1), 0)),
                       pl.BlockSpec((tc, D_MODEL), lambda i, r: (jnp.maximum(i - first_tiles, 0), 0))],
            scratch_shapes=[pltpu.VMEM((2, TOP_K * tc * ROW_SUB, LANES), F32), pltpu.SemaphoreType.DMA((2,))],
        ),
        out_shape=[jax.ShapeDtypeStruct((n_first, D_MODEL), F32), jax.ShapeDtypeStruct((n - n_first, D_MODEL), F32)],
        compiler_params=_cparams(1),
        name="combine",
    )(runs, slots, gate, h, ys, sg, su, sd, l2g, l2b)


def _pad_cols(a, width):
    return jnp.pad(a, ((0, 0), (0, width - a.shape[1])))


def _pack_shift_cols(a):
    c = 3 * RWKV_W
    return jnp.concatenate([a[:, :c], _pad_cols(a[:, c:c + DECAY_LORA], LORA_PAD),
                            _pad_cols(a[:, c + DECAY_LORA:c + DECAY_LORA + AAA_LORA], LORA_PAD),
                            _pad_cols(a[:, c + DECAY_LORA + AAA_LORA:], GATE_PAD)], axis=1)


def _scan_to_lanes(a, b, t, split):
    a = a.reshape(b, t, SCAN_INPUTS, N_HEADS, HEAD).transpose(1, 2, 4, 0, 3)
    return a.reshape(t, SCAN_INPUTS, HEAD // split, split * b * N_HEADS)


def _values_from_lanes(y, b, t):
    return y.reshape(t, HEAD, b, N_HEADS).transpose(2, 0, 3, 1).reshape(b * t, RWKV_W)


def _state_to_lanes(s, b, split):
    s = s.reshape(b, N_HEADS, HEAD // split, split, HEAD).transpose(4, 2, 3, 0, 1)
    return s.reshape(HEAD, HEAD // split, split * b * N_HEADS)


def _state_from_lanes(s, b, split):
    s = s.reshape(HEAD, HEAD // split, split, b, N_HEADS).transpose(3, 4, 1, 2, 0)
    return s.reshape(b, N_HEADS, HEAD, HEAD)


def _mixer(z, zprev, x2, b, t, wkv0, pool_prev, mk, mv, pos0, wts, tiles, h_all):
    n = b * t
    tt, tb, seqs, att_tt = tiles
    z3 = z[:n].reshape(b, t, Z_W)
    scan, g, bonus = _rwkv_prep(z, zprev, b, t, tt, wts["prep"])

    split = 1 if (b * N_HEADS) % LANES == 0 else LANES // (b * N_HEADS)
    key_order = jnp.concatenate([jnp.arange(i, HEAD, split) for i in range(split)])
    s_in = wkv0[..., key_order] if split > 1 else wkv0
    y_l, s_l = _wkv(_scan_to_lanes(scan, b, t, split), _state_to_lanes(s_in, b, split), tb)
    y = _values_from_lanes(y_l, b, t)
    wkv_new = _state_from_lanes(s_l, b, split)
    if split > 1:
        wkv_new = wkv_new[..., jnp.argsort(key_order)]

    prev = jnp.concatenate([jnp.zeros((b, POOL_PAD - POOL_BUF, POOL_W), F32), pool_prev], axis=1)
    o_pool = _pool(prev, z3, wts["pool_w"], wts["pool_scale"], seqs, pos0).reshape(n, POOL_W)
    tail = z3[:, max(t - POOL_BUF, 0):, ZS_W:ZS_W + POOL_W]
    pool_new = jnp.concatenate([pool_prev, tail], axis=1)[:, -POOL_BUF:]

    o_mem = _memattn(z3, mk, mv, att_tt, seqs).reshape(n, MEM_W)
    h = _post(y, bonus, g, o_pool, o_mem, x2, *wts["post"], tt, wts["alpha"], h_all)
    return h, wkv_new, pool_new


def kernel(x_prompt, x_sample, mem_prompt, state_wkv, state_shift, state_pool, cache_mem_k, cache_mem_v, w_in, mu_shift, w0, w_up_decay, a0, w_up_aaa, w_up_gate, k_k, k_a, r_k, ln_x_w, ln_x_b, pool_w, pool_scale, mem_wk, mem_wv, w_out, ln1_g, ln1_b, router_w, router_b, exp_gate, exp_up, exp_down, sh_gate, sh_up, sh_down, ln2_g, ln2_b):
    depth = w_in.shape[0]
    assert depth == 1
    l = 0
    alpha = (2.0 * depth) ** 0.25
    bp, tp, d = x_prompt.shape
    bs, ts, _ = x_sample.shape
    n_p, n_s = bp * tp, bs * ts
    n = n_p + n_s

    w_in_p = jnp.concatenate([_pack_shift_cols(w_in[l][:, :1824]), w_in[l][:, 1824:]], axis=1).astype(BF16)
    row = lambda a: a.reshape(1, -1)
    pad_rows = lambda a, rows: jnp.pad(a, ((0, rows - a.shape[0]), (0, 0)))
    bd = jnp.kron(jnp.eye(N_HEADS, dtype=F32), jnp.ones((HEAD, HEAD), F32)).astype(BF16)
    pw = pool_w[l]
    pw_bd = jnp.zeros((POOL_W, POOL_W), F32)
    for gi in range(4):
        pw_bd = pw_bd.at[gi * 64:(gi + 1) * 64, gi * 64:(gi + 1) * 64].set(pw[gi])
    wts = {
        "prep": (_pack_shift_cols(row(mu_shift[l])), row(w0[l]), row(a0[l]), pad_rows(w_up_decay[l], LORA_PAD),
                 pad_rows(w_up_aaa[l], LORA_PAD), pad_rows(w_up_gate[l], GATE_PAD), row(k_k[l]), row(k_a[l]),
                 row(r_k[l]), bd),
        "pool_w": pw_bd.astype(BF16), "pool_scale": row(pool_scale[l]),
        "post": (w_out[l].astype(BF16), row(ln_x_w[l]), row(ln_x_b[l]), row(ln1_g[l]), row(ln1_b[l]), bd),
        "alpha": alpha,
    }

    xp2 = x_prompt.reshape(n_p, d)
    xs2 = x_sample.reshape(n_s, d)
    z_p = _matmul(xp2, w_in_p, _tile(n_p, 512))
    z_s = _matmul(jnp.concatenate([xs2, state_shift[l]], axis=0), w_in_p, _tile(n_s + bs, 512))
    mkv = _matmul(mem_prompt.reshape(bp * MEM_TOKENS, d),
                  jnp.concatenate([mem_wk[l], mem_wv[l]], axis=1).astype(BF16), _tile(bp * MEM_TOKENS, 512))
    mk_p = mkv[:, :MEM_W].reshape(bp, MEM_TOKENS, MEM_W)
    mv_p = mkv[:, MEM_W:].reshape(bp, MEM_TOKENS, MEM_W)

    h_p, wkv_p, pool_p = _mixer(z_p, jnp.zeros((bp, ZS_W), F32), xp2, bp, tp,
                                jnp.zeros((bp, N_HEADS, HEAD, HEAD), F32), jnp.zeros((bp, POOL_BUF, POOL_W), F32),
                                mk_p, mv_p, 0, wts,
                                (_tile(n_p, 256), _tile(tp, 32, 1), 1, _tile(tp, 256)), (n, 0))
    h, wkv_s, pool_s = _mixer(z_s, z_s[n_s:, :ZS_W], xs2, bs, ts, state_wkv[l], state_pool[l],
                              cache_mem_k[l].reshape(bs, MEM_TOKENS, MEM_W),
                              cache_mem_v[l].reshape(bs, MEM_TOKENS, MEM_W), PAST_LEN, wts,
                              (_tile(n_s, 256), _tile(ts, 32, 1), _tile(bs, 16, 1), _tile(ts, 256)), h_p)

    td = _tile(n, 256, LANES)
    slots, gate, run_tab, cnt = _router(h, router_w[l].T, router_b[l].reshape(N_EXPERTS, 1), td)
    counts = cnt[:, 0].astype(I32)
    padded = (counts + CHUNK - 1) // CHUNK * CHUNK
    pad_end = jnp.cumsum(padded)
    pad_start = pad_end - padded
    n_chunks = (n * TOP_K + N_EXPERTS * (CHUNK - 1) + CHUNK - 1) // CHUNK
    run_tab = run_tab.reshape(n // td, N_EXPERTS, LANES)[:, :, :RUN_FIELDS].astype(I32)
    runs = jnp.stack([run_tab[:, :, 1], pad_start[None, :] + run_tab[:, :, 2], run_tab[:, :, 0]], axis=1)
    runs = runs.reshape(n // td * RUN_FIELDS, N_EXPERTS) * ROW_SUB
    zpos = jnp.where(padded > 0, pad_end - CHUNK, -1).astype(I32)
    n_used = (pad_end[-1:] // CHUNK).astype(I32)
    xs_sorted = _dispatch(zpos, runs, slots, h, n_chunks * CHUNK, td)
    ys_sorted = _ffn((pad_start // CHUNK).astype(I32), (padded // CHUNK).astype(I32), n_used, xs_sorted,
                     exp_gate[l], exp_up[l], exp_down[l])
    y_p, y_s = _combine(runs, slots.T, gate.T, h, ys_sorted, sh_gate[l].astype(BF16), sh_up[l].astype(BF16),
                        sh_down[l].astype(BF16), row(ln2_g[l]), row(ln2_b[l]), td, alpha, n_p)

    return (y_p.reshape(bp, tp, d), y_s.reshape(bs, ts, d),
            wkv_p[None], x_prompt[:, -1][None], pool_p[None],
            mk_p.reshape(bp, MEM_TOKENS, MEM_HEADS, HEAD)[None], mv_p.reshape(bp, MEM_TOKENS, MEM_HEADS, HEAD)[None],
            wkv_s[None], x_sample[:, -1][None], pool_s[None])
```
